```python
import math
import jax, jax.numpy as jnp
from jax import lax
import numpy as np

D_MODEL = 1024
BATCH = 8
SEQ = 8192
DEPTH = 4

N_MIXERS = 4
MIX_WIDTH = D_MODEL
GROUP_WIDTH = MIX_WIDTH // N_MIXERS
SC_HEAD_DIM = 64
SC_TAPS = 3
POOL_WINDOWS = (2, 4, 8, 16)
POOL_GROUP = GROUP_WIDTH // len(POOL_WINDOWS)
CF_TAPS = 31
SSM_CH = 16
SSM_GROUPS = GROUP_WIDTH // SSM_CH
SSM_STATE = 64
SSM_DT_MIN = 1e-3
SSM_DT_MAX = 1e-1
D_FF = -(-(8 * D_MODEL) // (3 * 256)) * 256
IN_SIZES = (GROUP_WIDTH, GROUP_WIDTH, GROUP_WIDTH, GROUP_WIDTH, GROUP_WIDTH, GROUP_WIDTH, GROUP_WIDTH)
IN_WIDTH = sum(IN_SIZES)
IN_SPLITS = tuple(int(v) for v in np.cumsum(IN_SIZES)[:-1])
ALPHA = (2 * DEPTH) ** 0.25
BETA = (8 * DEPTH) ** -0.25
LN_EPS = 1e-5

kernel_name = "hybrid_parallel_mixer_trunk"


def layer_norm(x, g, b):
    xf = x.astype(jnp.float32)
    mu = jnp.mean(xf, axis=-1, keepdims=True)
    var = jnp.mean(jnp.square(xf - mu), axis=-1, keepdims=True)
    y = (xf - mu) * lax.rsqrt(var + LN_EPS) * g.astype(jnp.float32) + b.astype(jnp.float32)
    return y.astype(x.dtype)


def causal_dwconv(x, w):
    k, ch = w.shape
    return lax.conv_general_dilated(
        x, w[:, None, :].astype(x.dtype), window_strides=(1,), padding=[(k - 1, 0)],
        dimension_numbers=('NWC', 'WIO', 'NWC'), feature_group_count=ch)


def pool_mixer(z, w_pool, scale):
    zf = z.astype(jnp.float32)
    s = z.shape[1]
    pos = jnp.arange(1, s + 1, dtype=jnp.float32)[None, :, None]
    outs = []
    for i, w in enumerate(POOL_WINDOWS):
        zg = zf[..., i * POOL_GROUP:(i + 1) * POOL_GROUP]
        cs = jnp.cumsum(zg, axis=1)
        lagged = jnp.pad(cs, ((0, 0), (w, 0), (0, 0)))[:, :s]
        mean = (cs - lagged) / jnp.minimum(pos, float(w))
        outs.append(jnp.einsum('bsc,cd->bsd', mean - zg, w_pool[i].astype(jnp.float32)))
    return (jnp.concatenate(outs, axis=-1) * scale.astype(jnp.float32)).astype(z.dtype)


def conformer_conv(z, dw_w, dw_b, ln_g, ln_b):
    val, gate = jnp.split(z, 2, axis=-1)
    h = val * jax.nn.sigmoid(gate)
    h = causal_dwconv(h, dw_w) + dw_b
    h = layer_norm(h, ln_g, ln_b)
    return jax.nn.silu(h)


def ssm_mixer(u, lam_re, lam_im, log_dt, b_re, b_im, c_re, c_im, d, w_glu, b_glu):
    bsz, s, _ = u.shape
    f32 = jnp.float32
    uf = u.astype(f32).reshape(bsz, s, SSM_GROUPS, SSM_CH)
    lam = lax.complex(lam_re.astype(f32), lam_im.astype(f32))
    dt = jnp.exp(log_dt.astype(f32))[:, None]
    lam_bar = jnp.exp(lam * dt)
    bmat = lax.complex(b_re.astype(f32), b_im.astype(f32))
    b_bar = ((lam_bar - 1.0) / lam)[..., None] * bmat
    bu = jnp.einsum('gph,bsgh->bsgp', b_bar, uf)
    a = jnp.broadcast_to(lam_bar, (1, s) + lam_bar.shape)

    def combine(e1, e2):
        a1, b1 = e1
        a2, b2 = e2
        return a1 * a2, a2 * b1 + b2

    _, states = lax.associative_scan(combine, (a, bu), axis=1)
    cmat = lax.complex(c_re.astype(f32), c_im.astype(f32))
    y = jnp.real(jnp.einsum('ghp,bsgp->bsgh', cmat, states)) + d.astype(f32).reshape(SSM_GROUPS, SSM_CH) * uf
    y = y.reshape(bsz, s, GROUP_WIDTH)
    yg = jax.nn.gelu(y)
    out = yg * jax.nn.sigmoid(yg @ w_glu.astype(f32) + b_glu.astype(f32))
    return out.astype(u.dtype)


def _fwd_setup_inputs(seed: int = 0) -> dict:
    key = jax.random.key(seed)
    ks = jax.random.split(key, 32)
    L = DEPTH
    f32 = jnp.float32

    def nrm(k, shape, scale):
        return jax.random.normal(k, shape, f32) * scale

    lam_im_base = jnp.pi * jnp.arange(SSM_STATE, dtype=f32)
    return {
        "x": nrm(ks[0], (BATCH, SEQ, D_MODEL), 1.0),
        "c": nrm(ks[1], (BATCH, D_MODEL), 1.0),
        "w_ada": nrm(ks[2], (L, D_MODEL, 6 * D_MODEL), 0.1 * D_MODEL ** -0.5),
        "b_ada": nrm(ks[3], (L, 6 * D_MODEL), 0.01),
        "w_in": nrm(ks[4], (L, D_MODEL, IN_WIDTH), D_MODEL ** -0.5),
        "b_in": nrm(ks[5], (L, IN_WIDTH), 0.01),
        "sc_w": nrm(ks[6], (L, SC_TAPS, GROUP_WIDTH), SC_TAPS ** -0.5),
        "pool_w": nrm(ks[7], (L, len(POOL_WINDOWS), POOL_GROUP, POOL_GROUP), POOL_GROUP ** -0.5),
        "pool_scale": 1.0 + nrm(ks[8], (L, GROUP_WIDTH), 0.02),
        "cf_dw_w": nrm(ks[9], (L, CF_TAPS, GROUP_WIDTH), CF_TAPS ** -0.5),
        "cf_dw_b": nrm(ks[10], (L, GROUP_WIDTH), 0.01),
        "cf_ln_g": 1.0 + nrm(ks[11], (L, GROUP_WIDTH), 0.02),
        "cf_ln_b": nrm(ks[12], (L, GROUP_WIDTH), 0.01),
        "ssm_lam_re": -0.5 + nrm(ks[13], (L, SSM_GROUPS, SSM_STATE), 0.01),
        "ssm_lam_im": lam_im_base + nrm(ks[14], (L, SSM_GROUPS, SSM_STATE), 0.01),
        "ssm_log_dt": jax.random.uniform(ks[15], (L, SSM_GROUPS), f32, math.log(SSM_DT_MIN), math.log(SSM_DT_MAX)),
        "ssm_b_re": nrm(ks[16], (L, SSM_GROUPS, SSM_STATE, SSM_CH), (2 * SSM_CH) ** -0.5),
        "ssm_b_im": nrm(ks[17], (L, SSM_GROUPS, SSM_STATE, SSM_CH), (2 * SSM_CH) ** -0.5),
        "ssm_c_re": nrm(ks[18], (L, SSM_GROUPS, SSM_CH, SSM_STATE), SSM_STATE ** -0.5),
        "ssm_c_im": nrm(ks[19], (L, SSM_GROUPS, SSM_CH, SSM_STATE), SSM_STATE ** -0.5),
        "ssm_d": nrm(ks[20], (L, GROUP_WIDTH), 1.0),
        "ssm_w_glu": nrm(ks[21], (L, GROUP_WIDTH, GROUP_WIDTH), GROUP_WIDTH ** -0.5),
        "ssm_b_glu": nrm(ks[22], (L, GROUP_WIDTH), 0.01),
        "w_o": nrm(ks[23], (L, MIX_WIDTH, D_MODEL), BETA * MIX_WIDTH ** -0.5),
        "ln1_g": 1.0 + nrm(ks[24], (L, D_MODEL), 0.02),
        "ln1_b": nrm(ks[25], (L, D_MODEL), 0.01),
        "w_gate": nrm(ks[26], (L, D_MODEL, D_FF), D_MODEL ** -0.5),
        "w_up": nrm(ks[27], (L, D_MODEL, D_FF), D_MODEL ** -0.5),
        "w_down": nrm(ks[28], (L, D_FF, D_MODEL), BETA * D_FF ** -0.5),
        "ln2_g": 1.0 + nrm(ks[29], (L, D_MODEL), 0.02),
        "ln2_b": nrm(ks[30], (L, D_MODEL), 0.01),
    }


def _fwd_reference(x, c, w_ada, b_ada, w_in, b_in, sc_w, pool_w, pool_scale, cf_dw_w, cf_dw_b, cf_ln_g, cf_ln_b,
              ssm_lam_re, ssm_lam_im, ssm_log_dt, ssm_b_re, ssm_b_im, ssm_c_re, ssm_c_im, ssm_d,
              ssm_w_glu, ssm_b_glu, w_o, ln1_g, ln1_b, w_gate, w_up, w_down, ln2_g, ln2_b):
    cond = jax.nn.silu(c)
    for l in range(DEPTH):
        mod = cond @ w_ada[l] + b_ada[l]
        sh1, sc1, g1, sh2, sc2, g2 = [m[:, None, :] for m in jnp.split(mod, 6, axis=-1)]

        h = x * (1 + sc1) + sh1
        z = h @ w_in[l] + b_in[l]
        z_h, z_b, z_c, z_p, z_cv, z_cg, z_s = jnp.split(z, IN_SPLITS, axis=-1)
        y_a = z_b * causal_dwconv(z_c * z_h, sc_w[l])
        y_b = pool_mixer(z_p, pool_w[l], pool_scale[l])
        y_c = conformer_conv(jnp.concatenate([z_cv, z_cg], axis=-1),
                             cf_dw_w[l], cf_dw_b[l], cf_ln_g[l], cf_ln_b[l])
        y_d = ssm_mixer(z_s, ssm_lam_re[l], ssm_lam_im[l], ssm_log_dt[l], ssm_b_re[l], ssm_b_im[l],
                        ssm_c_re[l], ssm_c_im[l], ssm_d[l], ssm_w_glu[l], ssm_b_glu[l])
        y = jnp.concatenate([y_a, y_b, y_c, y_d], axis=-1) @ w_o[l]
        x = layer_norm(ALPHA * x + (1 + g1) * y, ln1_g[l], ln1_b[l])

        h = x * (1 + sc2) + sh2
        f = (jax.nn.silu(h @ w_gate[l]) * (h @ w_up[l])) @ w_down[l]
        x = layer_norm(ALPHA * x + (1 + g2) * f, ln2_g[l], ln2_b[l])
    return x


import jax as _jax
import jax.numpy as _jnp

TWIN_FORMAT = 'train_step'
FWD_PARAMS = ['x', 'c', 'w_ada', 'b_ada', 'w_in', 'b_in', 'sc_w', 'pool_w', 'pool_scale', 'cf_dw_w', 'cf_dw_b', 'cf_ln_g', 'cf_ln_b', 'ssm_lam_re', 'ssm_lam_im', 'ssm_log_dt', 'ssm_b_re', 'ssm_b_im', 'ssm_c_re', 'ssm_c_im', 'ssm_d', 'ssm_w_glu', 'ssm_b_glu', 'w_o', 'ln1_g', 'ln1_b', 'w_gate', 'w_up', 'w_down', 'ln2_g', 'ln2_b']
TWIN_WEIGHTS = ['w_ada', 'b_ada', 'w_in', 'b_in', 'sc_w', 'pool_w', 'pool_scale', 'cf_dw_w', 'cf_dw_b', 'cf_ln_g', 'cf_ln_b', 'ssm_lam_re', 'ssm_lam_im', 'ssm_log_dt', 'ssm_b_re', 'ssm_b_im', 'ssm_c_re', 'ssm_c_im', 'ssm_d', 'ssm_w_glu', 'ssm_b_glu', 'w_o', 'ln1_g', 'ln1_b', 'w_gate', 'w_up', 'w_down', 'ln2_g', 'ln2_b']
TWIN_DIFF_INPUT = 'x'
TWIN_INPUTS = ['x', 'c', 'w_ada', 'b_ada', 'w_in', 'b_in', 'sc_w', 'pool_w', 'pool_scale', 'cf_dw_w', 'cf_dw_b', 'cf_ln_g', 'cf_ln_b', 'ssm_lam_re', 'ssm_lam_im', 'ssm_log_dt', 'ssm_b_re', 'ssm_b_im', 'ssm_c_re', 'ssm_c_im', 'ssm_d', 'ssm_w_glu', 'ssm_b_glu', 'w_o', 'ln1_g', 'ln1_b', 'w_gate', 'w_up', 'w_down', 'ln2_g', 'ln2_b', 'loss_target', 'm_w_ada', 'm_b_ada', 'm_w_in', 'm_b_in', 'm_sc_w', 'm_pool_w', 'm_pool_scale', 'm_cf_dw_w', 'm_cf_dw_b', 'm_cf_ln_g', 'm_cf_ln_b', 'm_ssm_lam_re', 'm_ssm_lam_im', 'm_ssm_log_dt', 'm_ssm_b_re', 'm_ssm_b_im', 'm_ssm_c_re', 'm_ssm_c_im', 'm_ssm_d', 'm_ssm_w_glu', 'm_ssm_b_glu', 'm_w_o', 'm_ln1_g', 'm_ln1_b', 'm_w_gate', 'm_w_up', 'm_w_down', 'm_ln2_g', 'm_ln2_b', 'v_w_ada', 'v_b_ada', 'v_w_in', 'v_b_in', 'v_sc_w', 'v_pool_w', 'v_pool_scale', 'v_cf_dw_w', 'v_cf_dw_b', 'v_cf_ln_g', 'v_cf_ln_b', 'v_ssm_lam_re', 'v_ssm_lam_im', 'v_ssm_log_dt', 'v_ssm_b_re', 'v_ssm_b_im', 'v_ssm_c_re', 'v_ssm_c_im', 'v_ssm_d', 'v_ssm_w_glu', 'v_ssm_b_glu', 'v_w_o', 'v_ln1_g', 'v_ln1_b', 'v_w_gate', 'v_w_up', 'v_w_down', 'v_ln2_g', 'v_ln2_b']
TWIN_OUTPUTS = ['loss', 'grad_x', 'grad_w_ada', 'grad_b_ada', 'grad_w_in', 'grad_b_in', 'grad_sc_w', 'grad_pool_w', 'grad_pool_scale', 'grad_cf_dw_w', 'grad_cf_dw_b', 'grad_cf_ln_g', 'grad_cf_ln_b', 'grad_ssm_lam_re', 'grad_ssm_lam_im', 'grad_ssm_log_dt', 'grad_ssm_b_re', 'grad_ssm_b_im', 'grad_ssm_c_re', 'grad_ssm_c_im', 'grad_ssm_d', 'grad_ssm_w_glu', 'grad_ssm_b_glu', 'grad_w_o', 'grad_ln1_g', 'grad_ln1_b', 'grad_w_gate', 'grad_w_up', 'grad_w_down', 'grad_ln2_g', 'grad_ln2_b', 'delta_w_ada', 'delta_b_ada', 'delta_w_in', 'delta_b_in', 'delta_sc_w', 'delta_pool_w', 'delta_pool_scale', 'delta_cf_dw_w', 'delta_cf_dw_b', 'delta_cf_ln_g', 'delta_cf_ln_b', 'delta_ssm_lam_re', 'delta_ssm_lam_im', 'delta_ssm_log_dt', 'delta_ssm_b_re', 'delta_ssm_b_im', 'delta_ssm_c_re', 'delta_ssm_c_im', 'delta_ssm_d', 'delta_ssm_w_glu', 'delta_ssm_b_glu', 'delta_w_o', 'delta_ln1_g', 'delta_ln1_b', 'delta_w_gate', 'delta_w_up', 'delta_w_down', 'delta_ln2_g', 'delta_ln2_b', 'new_m_w_ada', 'new_m_b_ada', 'new_m_w_in', 'new_m_b_in', 'new_m_sc_w', 'new_m_pool_w', 'new_m_pool_scale', 'new_m_cf_dw_w', 'new_m_cf_dw_b', 'new_m_cf_ln_g', 'new_m_cf_ln_b', 'new_m_ssm_lam_re', 'new_m_ssm_lam_im', 'new_m_ssm_log_dt', 'new_m_ssm_b_re', 'new_m_ssm_b_im', 'new_m_ssm_c_re', 'new_m_ssm_c_im', 'new_m_ssm_d', 'new_m_ssm_w_glu', 'new_m_ssm_b_glu', 'new_m_w_o', 'new_m_ln1_g', 'new_m_ln1_b', 'new_m_w_gate', 'new_m_w_up', 'new_m_w_down', 'new_m_ln2_g', 'new_m_ln2_b', 'new_v_w_ada', 'new_v_b_ada', 'new_v_w_in', 'new_v_b_in', 'new_v_sc_w', 'new_v_pool_w', 'new_v_pool_scale', 'new_v_cf_dw_w', 'new_v_cf_dw_b', 'new_v_cf_ln_g', 'new_v_cf_ln_b', 'new_v_ssm_lam_re', 'new_v_ssm_lam_im', 'new_v_ssm_log_dt', 'new_v_ssm_b_re', 'new_v_ssm_b_im', 'new_v_ssm_c_re', 'new_v_ssm_c_im', 'new_v_ssm_d', 'new_v_ssm_w_glu', 'new_v_ssm_b_glu', 'new_v_w_o', 'new_v_ln1_g', 'new_v_ln1_b', 'new_v_w_gate', 'new_v_w_up', 'new_v_w_down', 'new_v_ln2_g', 'new_v_ln2_b']
TWIN_LEAF_KINDS = {'loss': 'loss', 'grad_x': 'grad_x', 'grad_w_ada': 'grad_w', 'grad_b_ada': 'grad_w', 'grad_w_in': 'grad_w', 'grad_b_in': 'grad_w', 'grad_sc_w': 'grad_w', 'grad_pool_w': 'grad_w', 'grad_pool_scale': 'grad_w', 'grad_cf_dw_w': 'grad_w', 'grad_cf_dw_b': 'grad_w', 'grad_cf_ln_g': 'grad_w', 'grad_cf_ln_b': 'grad_w', 'grad_ssm_lam_re': 'grad_w', 'grad_ssm_lam_im': 'grad_w', 'grad_ssm_log_dt': 'grad_w', 'grad_ssm_b_re': 'grad_w', 'grad_ssm_b_im': 'grad_w', 'grad_ssm_c_re': 'grad_w', 'grad_ssm_c_im': 'grad_w', 'grad_ssm_d': 'grad_w', 'grad_ssm_w_glu': 'grad_w', 'grad_ssm_b_glu': 'grad_w', 'grad_w_o': 'grad_w', 'grad_ln1_g': 'grad_w', 'grad_ln1_b': 'grad_w', 'grad_w_gate': 'grad_w', 'grad_w_up': 'grad_w', 'grad_w_down': 'grad_w', 'grad_ln2_g': 'grad_w', 'grad_ln2_b': 'grad_w', 'delta_w_ada': 'delta_w', 'delta_b_ada': 'delta_w', 'delta_w_in': 'delta_w', 'delta_b_in': 'delta_w', 'delta_sc_w': 'delta_w', 'delta_pool_w': 'delta_w', 'delta_pool_scale': 'delta_w', 'delta_cf_dw_w': 'delta_w', 'delta_cf_dw_b': 'delta_w', 'delta_cf_ln_g': 'delta_w', 'delta_cf_ln_b': 'delta_w', 'delta_ssm_lam_re': 'delta_w', 'delta_ssm_lam_im': 'delta_w', 'delta_ssm_log_dt': 'delta_w', 'delta_ssm_b_re': 'delta_w', 'delta_ssm_b_im': 'delta_w', 'delta_ssm_c_re': 'delta_w', 'delta_ssm_c_im': 'delta_w', 'delta_ssm_d': 'delta_w', 'delta_ssm_w_glu': 'delta_w', 'delta_ssm_b_glu': 'delta_w', 'delta_w_o': 'delta_w', 'delta_ln1_g': 'delta_w', 'delta_ln1_b': 'delta_w', 'delta_w_gate': 'delta_w', 'delta_w_up': 'delta_w', 'delta_w_down': 'delta_w', 'delta_ln2_g': 'delta_w', 'delta_ln2_b': 'delta_w', 'new_m_w_ada': 'new_m', 'new_m_b_ada': 'new_m', 'new_m_w_in': 'new_m', 'new_m_b_in': 'new_m', 'new_m_sc_w': 'new_m', 'new_m_pool_w': 'new_m', 'new_m_pool_scale': 'new_m', 'new_m_cf_dw_w': 'new_m', 'new_m_cf_dw_b': 'new_m', 'new_m_cf_ln_g': 'new_m', 'new_m_cf_ln_b': 'new_m', 'new_m_ssm_lam_re': 'new_m', 'new_m_ssm_lam_im': 'new_m', 'new_m_ssm_log_dt': 'new_m', 'new_m_ssm_b_re': 'new_m', 'new_m_ssm_b_im': 'new_m', 'new_m_ssm_c_re': 'new_m', 'new_m_ssm_c_im': 'new_m', 'new_m_ssm_d': 'new_m', 'new_m_ssm_w_glu': 'new_m', 'new_m_ssm_b_glu': 'new_m', 'new_m_w_o': 'new_m', 'new_m_ln1_g': 'new_m', 'new_m_ln1_b': 'new_m', 'new_m_w_gate': 'new_m', 'new_m_w_up': 'new_m', 'new_m_w_down': 'new_m', 'new_m_ln2_g': 'new_m', 'new_m_ln2_b': 'new_m', 'new_v_w_ada': 'new_v', 'new_v_b_ada': 'new_v', 'new_v_w_in': 'new_v', 'new_v_b_in': 'new_v', 'new_v_sc_w': 'new_v', 'new_v_pool_w': 'new_v', 'new_v_pool_scale': 'new_v', 'new_v_cf_dw_w': 'new_v', 'new_v_cf_dw_b': 'new_v', 'new_v_cf_ln_g': 'new_v', 'new_v_cf_ln_b': 'new_v', 'new_v_ssm_lam_re': 'new_v', 'new_v_ssm_lam_im': 'new_v', 'new_v_ssm_log_dt': 'new_v', 'new_v_ssm_b_re': 'new_v', 'new_v_ssm_b_im': 'new_v', 'new_v_ssm_c_re': 'new_v', 'new_v_ssm_c_im': 'new_v', 'new_v_ssm_d': 'new_v', 'new_v_ssm_w_glu': 'new_v', 'new_v_ssm_b_glu': 'new_v', 'new_v_w_o': 'new_v', 'new_v_ln1_g': 'new_v', 'new_v_ln1_b': 'new_v', 'new_v_w_gate': 'new_v', 'new_v_w_up': 'new_v', 'new_v_w_down': 'new_v', 'new_v_ln2_g': 'new_v', 'new_v_ln2_b': 'new_v'}


def _forward(args):
    return _fwd_reference(*[args[k] for k in FWD_PARAMS])


def _output_shape():
    out = _jax.eval_shape(lambda: _forward(_fwd_setup_inputs(0)))
    return out.shape, out.dtype

N_MICROBATCH = 1
ADAM_LR = 0.001
ADAM_B1 = 0.9
ADAM_B2 = 0.999
ADAM_EPS = 1e-08
ADAM_WD = 0.01
ADAM_STEP = 10
PER_EXAMPLE_BATCH_AXIS = {'x': 0, 'c': 0, 'loss_target': 0}
SHARED_INPUTS = []
_WEIGHT_DTYPES = {'w_ada': _jnp.float32, 'b_ada': _jnp.float32, 'w_in': _jnp.float32, 'b_in': _jnp.float32, 'sc_w': _jnp.float32, 'pool_w': _jnp.float32, 'pool_scale': _jnp.float32, 'cf_dw_w': _jnp.float32, 'cf_dw_b': _jnp.float32, 'cf_ln_g': _jnp.float32, 'cf_ln_b': _jnp.float32, 'ssm_lam_re': _jnp.float32, 'ssm_lam_im': _jnp.float32, 'ssm_log_dt': _jnp.float32, 'ssm_b_re': _jnp.float32, 'ssm_b_im': _jnp.float32, 'ssm_c_re': _jnp.float32, 'ssm_c_im': _jnp.float32, 'ssm_d': _jnp.float32, 'ssm_w_glu': _jnp.float32, 'ssm_b_glu': _jnp.float32, 'w_o': _jnp.float32, 'ln1_g': _jnp.float32, 'ln1_b': _jnp.float32, 'w_gate': _jnp.float32, 'w_up': _jnp.float32, 'w_down': _jnp.float32, 'ln2_g': _jnp.float32, 'ln2_b': _jnp.float32}
MOMENT_SCALE = {'w_ada': 3.484166e-02, 'b_ada': 5.793282e-02, 'w_in': 5.140974e-02, 'b_in': 4.802931e-02, 'sc_w': 6.599906e-02, 'pool_w': 5.720648e-02, 'pool_scale': 5.927986e-02, 'cf_dw_w': 4.075398e-02, 'cf_dw_b': 9.016886e-02, 'cf_ln_g': 5.030374e-02, 'cf_ln_b': 4.886541e-02, 'ssm_lam_re': 2.514601e-03, 'ssm_lam_im': 2.370657e-03, 'ssm_log_dt': 1.165214e+00, 'ssm_b_re': 1.281365e-03, 'ssm_b_im': 1.275432e-03, 'ssm_c_re': 1.798637e-03, 'ssm_c_im': 1.808695e-03, 'ssm_d': 3.272510e-02, 'ssm_w_glu': 6.929209e-03, 'ssm_b_glu': 1.198661e-02, 'w_o': 1.182966e-01, 'ln1_g': 1.650248e+00, 'ln1_b': 3.935779e-01, 'w_gate': 2.427532e-02, 'w_up': 2.350826e-02, 'w_down': 9.266053e-02, 'ln2_g': 3.202503e+01, 'ln2_b': 1.455663e+00}


def _to_microbatches(a, axis):
    t = _jnp.moveaxis(a, axis, 0)
    t = t.reshape((N_MICROBATCH, t.shape[0] // N_MICROBATCH) + t.shape[1:])
    return _jnp.moveaxis(t, 1, axis + 1)


def setup_inputs(seed: int = 0) -> dict:
    inp = _fwd_setup_inputs(seed)
    key = _jax.random.fold_in(_jax.random.key(seed), 7919)
    shape, _ = _output_shape()
    out = dict(inp)
    out["loss_target"] = _jax.random.normal(_jax.random.fold_in(key, 0), shape, _jnp.float32)
    for i, name in enumerate(TWIN_WEIGHTS):
        w = inp[name].astype(_jnp.float32)
        if MOMENT_SCALE is None:
            s = _jnp.sqrt(_jnp.mean(_jnp.square(w)) + 1e-30)
        else:
            s = MOMENT_SCALE[name]
        km, kv = _jax.random.split(_jax.random.fold_in(key, i + 1))
        out[name] = w
        out["m_" + name] = s * _jax.random.normal(km, w.shape, _jnp.float32)
        out["v_" + name] = (s * s) * _jax.random.uniform(kv, w.shape, _jnp.float32, 0.5, 1.5)
    if N_MICROBATCH > 1:
        for name, axis in PER_EXAMPLE_BATCH_AXIS.items():
            out[name] = _to_microbatches(out[name], axis)
    return {'x': out['x'], 'c': out['c'], 'w_ada': out['w_ada'], 'b_ada': out['b_ada'], 'w_in': out['w_in'], 'b_in': out['b_in'], 'sc_w': out['sc_w'], 'pool_w': out['pool_w'], 'pool_scale': out['pool_scale'], 'cf_dw_w': out['cf_dw_w'], 'cf_dw_b': out['cf_dw_b'], 'cf_ln_g': out['cf_ln_g'], 'cf_ln_b': out['cf_ln_b'], 'ssm_lam_re': out['ssm_lam_re'], 'ssm_lam_im': out['ssm_lam_im'], 'ssm_log_dt': out['ssm_log_dt'], 'ssm_b_re': out['ssm_b_re'], 'ssm_b_im': out['ssm_b_im'], 'ssm_c_re': out['ssm_c_re'], 'ssm_c_im': out['ssm_c_im'], 'ssm_d': out['ssm_d'], 'ssm_w_glu': out['ssm_w_glu'], 'ssm_b_glu': out['ssm_b_glu'], 'w_o': out['w_o'], 'ln1_g': out['ln1_g'], 'ln1_b': out['ln1_b'], 'w_gate': out['w_gate'], 'w_up': out['w_up'], 'w_down': out['w_down'], 'ln2_g': out['ln2_g'], 'ln2_b': out['ln2_b'], 'loss_target': out['loss_target'], 'm_w_ada': out['m_w_ada'], 'm_b_ada': out['m_b_ada'], 'm_w_in': out['m_w_in'], 'm_b_in': out['m_b_in'], 'm_sc_w': out['m_sc_w'], 'm_pool_w': out['m_pool_w'], 'm_pool_scale': out['m_pool_scale'], 'm_cf_dw_w': out['m_cf_dw_w'], 'm_cf_dw_b': out['m_cf_dw_b'], 'm_cf_ln_g': out['m_cf_ln_g'], 'm_cf_ln_b': out['m_cf_ln_b'], 'm_ssm_lam_re': out['m_ssm_lam_re'], 'm_ssm_lam_im': out['m_ssm_lam_im'], 'm_ssm_log_dt': out['m_ssm_log_dt'], 'm_ssm_b_re': out['m_ssm_b_re'], 'm_ssm_b_im': out['m_ssm_b_im'], 'm_ssm_c_re': out['m_ssm_c_re'], 'm_ssm_c_im': out['m_ssm_c_im'], 'm_ssm_d': out['m_ssm_d'], 'm_ssm_w_glu': out['m_ssm_w_glu'], 'm_ssm_b_glu': out['m_ssm_b_glu'], 'm_w_o': out['m_w_o'], 'm_ln1_g': out['m_ln1_g'], 'm_ln1_b': out['m_ln1_b'], 'm_w_gate': out['m_w_gate'], 'm_w_up': out['m_w_up'], 'm_w_down': out['m_w_down'], 'm_ln2_g': out['m_ln2_g'], 'm_ln2_b': out['m_ln2_b'], 'v_w_ada': out['v_w_ada'], 'v_b_ada': out['v_b_ada'], 'v_w_in': out['v_w_in'], 'v_b_in': out['v_b_in'], 'v_sc_w': out['v_sc_w'], 'v_pool_w': out['v_pool_w'], 'v_pool_scale': out['v_pool_scale'], 'v_cf_dw_w': out['v_cf_dw_w'], 'v_cf_dw_b': out['v_cf_dw_b'], 'v_cf_ln_g': out['v_cf_ln_g'], 'v_cf_ln_b': out['v_cf_ln_b'], 'v_ssm_lam_re': out['v_ssm_lam_re'], 'v_ssm_lam_im': out['v_ssm_lam_im'], 'v_ssm_log_dt': out['v_ssm_log_dt'], 'v_ssm_b_re': out['v_ssm_b_re'], 'v_ssm_b_im': out['v_ssm_b_im'], 'v_ssm_c_re': out['v_ssm_c_re'], 'v_ssm_c_im': out['v_ssm_c_im'], 'v_ssm_d': out['v_ssm_d'], 'v_ssm_w_glu': out['v_ssm_w_glu'], 'v_ssm_b_glu': out['v_ssm_b_glu'], 'v_w_o': out['v_w_o'], 'v_ln1_g': out['v_ln1_g'], 'v_ln1_b': out['v_ln1_b'], 'v_w_gate': out['v_w_gate'], 'v_w_up': out['v_w_up'], 'v_w_down': out['v_w_down'], 'v_ln2_g': out['v_ln2_g'], 'v_ln2_b': out['v_ln2_b']}


def _loss(weights, diff, rest, loss_target):
    with _jax.named_scope("forward"):
        args = {**rest, TWIN_DIFF_INPUT: diff, **{k: w.astype(_WEIGHT_DTYPES[k]) for k, w in weights.items()}}
        y = _forward(args)
    with _jax.named_scope("loss_head"):
        err = _jnp.square(y.astype(_jnp.float32) - loss_target)
        return 0.5 * _jnp.sum(_jnp.mean(err, axis=-1)) if err.ndim else 0.5 * err


def _adamw(w, g, m, v):
    m = ADAM_B1 * m + (1.0 - ADAM_B1) * g
    v = ADAM_B2 * v + (1.0 - ADAM_B2) * _jnp.square(g)
    m_hat = m / (1.0 - ADAM_B1 ** ADAM_STEP)
    v_hat = v / (1.0 - ADAM_B2 ** ADAM_STEP)
    delta = -ADAM_LR * (m_hat / (_jnp.sqrt(v_hat) + ADAM_EPS) + ADAM_WD * w)
    return delta, m, v


def reference(x, c, w_ada, b_ada, w_in, b_in, sc_w, pool_w, pool_scale, cf_dw_w, cf_dw_b, cf_ln_g, cf_ln_b, ssm_lam_re, ssm_lam_im, ssm_log_dt, ssm_b_re, ssm_b_im, ssm_c_re, ssm_c_im, ssm_d, ssm_w_glu, ssm_b_glu, w_o, ln1_g, ln1_b, w_gate, w_up, w_down, ln2_g, ln2_b, loss_target, m_w_ada, m_b_ada, m_w_in, m_b_in, m_sc_w, m_pool_w, m_pool_scale, m_cf_dw_w, m_cf_dw_b, m_cf_ln_g, m_cf_ln_b, m_ssm_lam_re, m_ssm_lam_im, m_ssm_log_dt, m_ssm_b_re, m_ssm_b_im, m_ssm_c_re, m_ssm_c_im, m_ssm_d, m_ssm_w_glu, m_ssm_b_glu, m_w_o, m_ln1_g, m_ln1_b, m_w_gate, m_w_up, m_w_down, m_ln2_g, m_ln2_b, v_w_ada, v_b_ada, v_w_in, v_b_in, v_sc_w, v_pool_w, v_pool_scale, v_cf_dw_w, v_cf_dw_b, v_cf_ln_g, v_cf_ln_b, v_ssm_lam_re, v_ssm_lam_im, v_ssm_log_dt, v_ssm_b_re, v_ssm_b_im, v_ssm_c_re, v_ssm_c_im, v_ssm_d, v_ssm_w_glu, v_ssm_b_glu, v_w_o, v_ln1_g, v_ln1_b, v_w_gate, v_w_up, v_w_down, v_ln2_g, v_ln2_b):
    given = dict(x=x, c=c, w_ada=w_ada, b_ada=b_ada, w_in=w_in, b_in=b_in, sc_w=sc_w, pool_w=pool_w, pool_scale=pool_scale, cf_dw_w=cf_dw_w, cf_dw_b=cf_dw_b, cf_ln_g=cf_ln_g, cf_ln_b=cf_ln_b, ssm_lam_re=ssm_lam_re, ssm_lam_im=ssm_lam_im, ssm_log_dt=ssm_log_dt, ssm_b_re=ssm_b_re, ssm_b_im=ssm_b_im, ssm_c_re=ssm_c_re, ssm_c_im=ssm_c_im, ssm_d=ssm_d, ssm_w_glu=ssm_w_glu, ssm_b_glu=ssm_b_glu, w_o=w_o, ln1_g=ln1_g, ln1_b=ln1_b, w_gate=w_gate, w_up=w_up, w_down=w_down, ln2_g=ln2_g, ln2_b=ln2_b, loss_target=loss_target, m_w_ada=m_w_ada, m_b_ada=m_b_ada, m_w_in=m_w_in, m_b_in=m_b_in, m_sc_w=m_sc_w, m_pool_w=m_pool_w, m_pool_scale=m_pool_scale, m_cf_dw_w=m_cf_dw_w, m_cf_dw_b=m_cf_dw_b, m_cf_ln_g=m_cf_ln_g, m_cf_ln_b=m_cf_ln_b, m_ssm_lam_re=m_ssm_lam_re, m_ssm_lam_im=m_ssm_lam_im, m_ssm_log_dt=m_ssm_log_dt, m_ssm_b_re=m_ssm_b_re, m_ssm_b_im=m_ssm_b_im, m_ssm_c_re=m_ssm_c_re, m_ssm_c_im=m_ssm_c_im, m_ssm_d=m_ssm_d, m_ssm_w_glu=m_ssm_w_glu, m_ssm_b_glu=m_ssm_b_glu, m_w_o=m_w_o, m_ln1_g=m_ln1_g, m_ln1_b=m_ln1_b, m_w_gate=m_w_gate, m_w_up=m_w_up, m_w_down=m_w_down, m_ln2_g=m_ln2_g, m_ln2_b=m_ln2_b, v_w_ada=v_w_ada, v_b_ada=v_b_ada, v_w_in=v_w_in, v_b_in=v_b_in, v_sc_w=v_sc_w, v_pool_w=v_pool_w, v_pool_scale=v_pool_scale, v_cf_dw_w=v_cf_dw_w, v_cf_dw_b=v_cf_dw_b, v_cf_ln_g=v_cf_ln_g, v_cf_ln_b=v_cf_ln_b, v_ssm_lam_re=v_ssm_lam_re, v_ssm_lam_im=v_ssm_lam_im, v_ssm_log_dt=v_ssm_log_dt, v_ssm_b_re=v_ssm_b_re, v_ssm_b_im=v_ssm_b_im, v_ssm_c_re=v_ssm_c_re, v_ssm_c_im=v_ssm_c_im, v_ssm_d=v_ssm_d, v_ssm_w_glu=v_ssm_w_glu, v_ssm_b_glu=v_ssm_b_glu, v_w_o=v_w_o, v_ln1_g=v_ln1_g, v_ln1_b=v_ln1_b, v_w_gate=v_w_gate, v_w_up=v_w_up, v_w_down=v_w_down, v_ln2_g=v_ln2_g, v_ln2_b=v_ln2_b)
    weights = {n: given[n] for n in TWIN_WEIGHTS}
    shared = {n: given[n] for n in SHARED_INPUTS}
    per_example = {n: given[n] for n in ['x', 'c']}
    grad_fn = _jax.value_and_grad(_loss, argnums=(0, 1))

    def one_microbatch(ex, loss_target):
        ex = dict(ex)
        diff = ex.pop(TWIN_DIFF_INPUT)
        return grad_fn(weights, diff, {**shared, **ex}, loss_target)

    if N_MICROBATCH == 1:
        loss, (grad_w, grad_x) = one_microbatch(per_example, given["loss_target"])
    else:
        def body(carry, xs):
            loss_sum, grad_sum = carry
            l_k, (gw_k, gx_k) = one_microbatch(xs[0], xs[1])
            with _jax.named_scope("update"):
                return (loss_sum + l_k, _jax.tree.map(_jnp.add, grad_sum, gw_k)), gx_k

        init = (_jnp.zeros((), _jnp.float32), _jax.tree.map(_jnp.zeros_like, weights))
        (loss, grad_w), grad_x = _jax.lax.scan(body, init, (per_example, given["loss_target"]))
    with _jax.named_scope("update"):
        delta_w, new_m, new_v = {}, {}, {}
        for n in TWIN_WEIGHTS:
            delta_w[n], new_m[n], new_v[n] = _adamw(weights[n], grad_w[n], given["m_" + n], given["v_" + n])
    return (loss, grad_x, *[grad_w[n] for n in TWIN_WEIGHTS], *[delta_w[n] for n in TWIN_WEIGHTS],
            *[new_m[n] for n in TWIN_WEIGHTS], *[new_v[n] for n in TWIN_WEIGHTS])
```

```python
import functools
import math

import jax
import jax.numpy as jnp
from jax import lax
from jax.experimental import pallas as pl
from jax.experimental.pallas import tpu as pltpu

F32 = jnp.float32
BF16 = jnp.bfloat16
_MXU = jnp.bfloat16

D = 1024
GW = 256
INW = 7 * GW
DFF = 2816
DEPTH = 4
NDEV = 8
SC_TAPS = 3
CF_TAPS = 31
NGRP = 16
NCH = 16
NSTATE = 64
XH = NGRP * NSTATE
XW = 2 * XH
ALPHA = (2 * DEPTH) ** 0.25
LN_EPS = 1e-5
T = 128
LOGT = 7
HALO = 32
TM = 512
VMEM_LIMIT = 56 * 1024 * 1024

ADAM_LR = 0.001
ADAM_B1 = 0.9
ADAM_B2 = 0.999
ADAM_EPS = 1e-08
ADAM_WD = 0.01
ADAM_STEP = 10

MESH = pl.DeviceIdType.MESH
ANY = pl.BlockSpec(memory_space=pl.ANY)
AXES = ("x", "y", "c")


def _cparams(sem=None):
    return pltpu.CompilerParams(dimension_semantics=sem, vmem_limit_bytes=VMEM_LIMIT)


def _full(shape):
    nd = len(shape)
    return pl.BlockSpec(shape, lambda *_: (0,) * nd)


def _dot(a, b):
    return jnp.dot(a.astype(_MXU), b.astype(_MXU), preferred_element_type=F32)


def _dot_nt(a, b):
    return lax.dot_general(a.astype(_MXU), b.astype(_MXU), (((1,), (1,)), ((), ())), preferred_element_type=F32)


def _dot_tn(a, b):
    return lax.dot_general(a.astype(_MXU), b.astype(_MXU), (((0,), (0,)), ((), ())), preferred_element_type=F32)


def _sigmoid(x):
    return 1.0 / (1.0 + jnp.exp(-x))


def _colsum(x):
    return jnp.sum(x, axis=0, keepdims=True)


def _me():
    return lax.axis_index("x"), lax.axis_index("y"), lax.axis_index("c")


def _flat(p):
    return 4 * p[0] + 2 * p[1] + p[2]


def _allgather(arrays, name):
    n = len(arrays)

    def body(*refs):
        ins, outs = refs[:n], refs[n:2 * n]
        send_sems, recv_sems, local_sems = refs[2 * n:]
        x, y, c = _me()
        me, sibling = (x, y, c), (x, y, 1 - c)
        chips = [(1 - x, y), (x, 1 - y), (1 - x, 1 - y)]

        def copy(a, k, block, to, src=None):
            dst = outs[a].at[_flat(block)]
            return pltpu.make_async_remote_copy(
                src_ref=dst if src is None else src, dst_ref=dst,
                send_sem=send_sems.at[a, k], recv_sem=recv_sems.at[a, k],
                device_id=to, device_id_type=MESH)

        started = []
        for a in range(n):
            mine = pltpu.make_async_copy(ins[a], outs[a].at[_flat(me)], local_sems.at[a])
            mine.start()
            started.append(mine)
        first = []
        for a in range(n):
            first.append(copy(a, 0, me, sibling, src=ins[a]))
            first += [copy(a, 1 + j, me, (*chip, c), src=ins[a]) for j, chip in enumerate(chips)]
        for cp in first:
            cp.start()
        passed = []
        for a in range(n):
            for j, chip in enumerate(chips):
                copy(a, 1 + j, (*chip, c), me).wait_recv()
                fwd = copy(a, 4 + j, (*chip, c), sibling)
                fwd.start()
                passed.append(fwd)
        for a in range(n):
            copy(a, 0, sibling, me).wait_recv()
            for j, chip in enumerate(chips):
                copy(a, 4 + j, (*chip, 1 - c), me).wait_recv()
        for cp in first + passed:
            cp.wait_send()
        for cp in started:
            cp.wait()

    return pl.pallas_call(
        body, name=name,
        out_shape=[jax.ShapeDtypeStruct((NDEV,) + a.shape, a.dtype) for a in arrays],
        in_specs=[ANY] * n, out_specs=[ANY] * n,
        scratch_shapes=[pltpu.SemaphoreType.DMA((n, 7)), pltpu.SemaphoreType.DMA((n, 7)),
                        pltpu.SemaphoreType.DMA((n,))],
    )(*arrays)


def _alltoall(arrays, name):
    n = len(arrays)

    def body(*refs):
        ins, outs = refs[:n], refs[n:2 * n]
        send_sems, recv_sems, local_sems = refs[2 * n:]
        x, y, c = _me()
        mine = _flat((x, y, c))
        copies = []
        for a in range(n):
            loc = pltpu.make_async_copy(ins[a].at[mine], outs[a].at[mine], local_sems.at[a])
            loc.start()
            copies.append(loc)
        remote = []
        for a in range(n):
            for k in range(1, NDEV):
                peer = (lax.rem(x + ((k >> 2) & 1), 2), lax.rem(y + ((k >> 1) & 1), 2), lax.rem(c + (k & 1), 2))
                cp = pltpu.make_async_remote_copy(
                    src_ref=ins[a].at[_flat(peer)], dst_ref=outs[a].at[mine],
                    send_sem=send_sems.at[a, k - 1], recv_sem=recv_sems.at[a, k - 1],
                    device_id=peer, device_id_type=MESH)
                cp.start()
                remote.append(cp)
        for cp in remote:
            cp.wait_send()
            cp.wait_recv()
        for cp in copies:
            cp.wait()

    return pl.pallas_call(
        body, name=name,
        out_shape=[jax.ShapeDtypeStruct(a.shape, a.dtype) for a in arrays],
        in_specs=[ANY] * n, out_specs=[ANY] * n,
        scratch_shapes=[pltpu.SemaphoreType.DMA((n, 7)), pltpu.SemaphoreType.DMA((n, 7)),
                        pltpu.SemaphoreType.DMA((n,))],
    )(*arrays)


def _modulated_matmul(x, sc, sh, ws, bias, tn, out_dtypes, epilogue, name):
    s, k = x.shape
    nw = len(ws)
    n = ws[0].shape[1]
    nb = 0 if bias is None else 1

    def body(*refs):
        x_ref, sc_ref, sh_ref = refs[:3]
        w_refs = refs[3:3 + nw]
        b_refs = refs[3 + nw:3 + nw + nb]
        o_refs = refs[3 + nw + nb:-1]
        h_scr = refs[-1]

        @pl.when(pl.program_id(1) == 0)
        def _():
            h_scr[...] = (x_ref[...] * (1.0 + sc_ref[...]) + sh_ref[...]).astype(_MXU)

        h = h_scr[...]
        prods = [jnp.dot(h, w[...], preferred_element_type=F32) for w in w_refs]
        if nb:
            prods[0] = prods[0] + b_refs[0][...]
        for o, v in zip(o_refs, epilogue(*prods)):
            o[...] = v.astype(o.dtype)

    vec = pl.BlockSpec((1, k), lambda i, j: (0, 0))
    return pl.pallas_call(
        body, name=name, grid=(s // TM, n // tn),
        out_shape=[jax.ShapeDtypeStruct((s, n), dt) for dt in out_dtypes],
        in_specs=[pl.BlockSpec((TM, k), lambda i, j: (i, 0)), vec, vec]
        + [pl.BlockSpec((k, tn), lambda i, j: (0, j))] * nw
        + [pl.BlockSpec((1, tn), lambda i, j: (0, j))] * nb,
        out_specs=[pl.BlockSpec((TM, tn), lambda i, j: (i, j))] * len(out_dtypes),
        scratch_shapes=[pltpu.VMEM((TM, k), _MXU)],
        compiler_params=_cparams(("arbitrary", "arbitrary")),
    )(x, sc, sh, *ws, *([bias] if nb else []))


def _swiglu(gate, up):
    return gate, up, gate * _sigmoid(gate) * up


def _proj_residual_ln(a, w, xres, gvec, lng, lnb, name):
    s, k = a.shape
    n = w.shape[1]

    def body(a_ref, w_ref, x_ref, g_ref, lg_ref, lb_ref, r_ref, u_ref, o_ref):
        r = jnp.dot(a_ref[...], w_ref[...], preferred_element_type=F32)
        u = ALPHA * x_ref[...] + (1.0 + g_ref[...]) * r
        mu = jnp.mean(u, axis=-1, keepdims=True)
        xc = u - mu
        var = jnp.mean(xc * xc, axis=-1, keepdims=True)
        r_ref[...] = r
        u_ref[...] = u
        o_ref[...] = xc * lax.rsqrt(var + LN_EPS) * lg_ref[...] + lb_ref[...]

    row = pl.BlockSpec((TM, n), lambda i: (i, 0))
    vec = pl.BlockSpec((1, n), lambda i: (0, 0))
    return pl.pallas_call(
        body, name=name, grid=(s // TM,),
        out_shape=[jax.ShapeDtypeStruct((s, n), F32)] * 3,
        in_specs=[pl.BlockSpec((TM, k), lambda i: (i, 0)), pl.BlockSpec((k, n), lambda i: (0, 0)), row, vec, vec, vec],
        out_specs=[row, row, row],
        compiler_params=_cparams(("arbitrary",)),
    )(a, w, xres, gvec, lng, lnb)


def _loss_head(xo, target, name):
    s, n = xo.shape

    def body(x_ref, t_ref, dx_ref, acc_ref):
        @pl.when(pl.program_id(0) == 0)
        def _():
            acc_ref[...] = jnp.zeros_like(acc_ref)

        err = x_ref[...] - t_ref[...]
        dx_ref[...] = err * (1.0 / n)
        acc_ref[...] += jnp.sum(err * err)

    row = pl.BlockSpec((TM, n), lambda i: (i, 0))
    return pl.pallas_call(
        body, name=name, grid=(s // TM,),
        out_shape=[jax.ShapeDtypeStruct((s, n), F32), jax.ShapeDtypeStruct((8, 128), F32)],
        in_specs=[row, row], out_specs=[row, _full((8, 128))],
        compiler_params=_cparams(("arbitrary",)),
    )(xo, target)


def _ln_bwd(dxo, u, res, lng, gvec, name):
    s, n = u.shape

    def body(dy_ref, u_ref, r_ref, lg_ref, g_ref, du_ref, dr_ref, acc_ref):
        @pl.when(pl.program_id(0) == 0)
        def _():
            acc_ref[...] = jnp.zeros_like(acc_ref)

        uu = u_ref[...]
        mu = jnp.mean(uu, axis=-1, keepdims=True)
        xc = uu - mu
        var = jnp.mean(xc * xc, axis=-1, keepdims=True)
        rstd = lax.rsqrt(var + LN_EPS)
        xh = xc * rstd
        dy = dy_ref[...]
        dxh = dy * lg_ref[...]
        du = rstd * (dxh - jnp.mean(dxh, axis=-1, keepdims=True) - xh * jnp.mean(dxh * xh, axis=-1, keepdims=True))
        du_ref[...] = du
        dr_ref[...] = (du * (1.0 + g_ref[...])).astype(dr_ref.dtype)
        acc_ref[0:1, :] += _colsum(dy * xh)
        acc_ref[1:2, :] += _colsum(dy)
        acc_ref[2:3, :] += _colsum(du * r_ref[...])

    row = pl.BlockSpec((TM, n), lambda i: (i, 0))
    vec = pl.BlockSpec((1, n), lambda i: (0, 0))
    return pl.pallas_call(
        body, name=name, grid=(s // TM,),
        out_shape=[jax.ShapeDtypeStruct((s, n), F32), jax.ShapeDtypeStruct((s, n), _MXU),
                   jax.ShapeDtypeStruct((8, n), F32)],
        in_specs=[row, row, row, vec, vec], out_specs=[row, row, _full((8, n))],
        compiler_params=_cparams(("arbitrary",)),
    )(dxo, u, res, lng, gvec)


def _swiglu_bwd(dr, w_down, gate, up, tn, name):
    s, n = dr.shape
    f = w_down.shape[0]

    def body(dr_ref, w_ref, g_ref, u_ref, dg_ref, du_ref):
        dp = lax.dot_general(dr_ref[...], w_ref[...], (((1,), (1,)), ((), ())), preferred_element_type=F32)
        g = g_ref[...]
        sg = _sigmoid(g)
        dg_ref[...] = (dp * u_ref[...] * (sg * (1.0 + g * (1.0 - sg)))).astype(dg_ref.dtype)
        du_ref[...] = (dp * (g * sg)).astype(du_ref.dtype)

    tile = pl.BlockSpec((TM, tn), lambda i, j: (i, j))
    return pl.pallas_call(
        body, name=name, grid=(s // TM, f // tn),
        out_shape=[jax.ShapeDtypeStruct((s, f), _MXU)] * 2,
        in_specs=[pl.BlockSpec((TM, n), lambda i, j: (i, 0)), pl.BlockSpec((tn, n), lambda i, j: (j, 0)), tile, tile],
        out_specs=[tile, tile],
        compiler_params=_cparams(("arbitrary", "arbitrary")),
    )(dr, w_down, gate, up)


def _input_grad(acts, ws, du, xin, sc, tm, name):
    s = acts[0].shape[0]
    n = ws[0].shape[0]
    na = len(acts)

    def body(*refs):
        a_refs, w_refs = refs[:na], refs[na:2 * na]
        du_ref, x_ref, sc_ref, dx_ref, acc_ref = refs[2 * na:]

        @pl.when(pl.program_id(0) == 0)
        def _():
            acc_ref[...] = jnp.zeros_like(acc_ref)

        dh = None
        for a, w in zip(a_refs, w_refs):
            t = lax.dot_general(a[...], w[...], (((1,), (1,)), ((), ())), preferred_element_type=F32)
            dh = t if dh is None else dh + t
        dx_ref[...] = ALPHA * du_ref[...] + dh * (1.0 + sc_ref[...])
        acc_ref[0:1, :] += _colsum(dh)
        acc_ref[1:2, :] += _colsum(dh * x_ref[...])

    row = pl.BlockSpec((tm, n), lambda i: (i, 0))
    return pl.pallas_call(
        body, name=name, grid=(s // tm,),
        out_shape=[jax.ShapeDtypeStruct((s, n), F32), jax.ShapeDtypeStruct((8, n), F32)],
        in_specs=[pl.BlockSpec((tm, a.shape[1]), lambda i: (i, 0)) for a in acts]
        + [pl.BlockSpec(w.shape, lambda i: (0, 0)) for w in ws]
        + [row, row, pl.BlockSpec((1, n), lambda i: (0, 0))],
        out_specs=[row, _full((8, n))],
        compiler_params=_cparams(("arbitrary",)),
    )(*acts, *ws, du, xin, sc)


def _matmul_nt(a, w, name):
    s, k = a.shape
    n = w.shape[0]

    def body(a_ref, w_ref, o_ref):
        o_ref[...] = lax.dot_general(a_ref[...], w_ref[...], (((1,), (1,)), ((), ())), preferred_element_type=F32)

    return pl.pallas_call(
        body, name=name, grid=(s // TM,),
        out_shape=jax.ShapeDtypeStruct((s, n), F32),
        in_specs=[pl.BlockSpec((TM, k), lambda i: (i, 0)), pl.BlockSpec((n, k), lambda i: (0, 0))],
        out_specs=pl.BlockSpec((TM, n), lambda i: (i, 0)),
        compiler_params=_cparams(("arbitrary",)),
    )(a, w)


def _weight_grad(a, b, tb, ts, name, mod=None):
    s, ka = a.shape
    nbc = b.shape[1]
    nm = 0 if mod is None else 2

    def body(*refs):
        a_ref = refs[0]
        m_refs = refs[1:1 + nm]
        b_ref, o_ref = refs[1 + nm:]

        @pl.when(pl.program_id(1) == 0)
        def _():
            o_ref[...] = jnp.zeros_like(o_ref)

        av = a_ref[...]
        if nm:
            av = av * (1.0 + m_refs[0][...]) + m_refs[1][...]
        o_ref[...] += lax.dot_general(av.astype(_MXU), b_ref[...], (((0,), (0,)), ((), ())),
                                      preferred_element_type=F32)

    vec = pl.BlockSpec((1, ka), lambda j, t: (0, 0))
    return pl.pallas_call(
        body, name=name, grid=(nbc // tb, s // ts),
        out_shape=jax.ShapeDtypeStruct((ka, nbc), F32),
        in_specs=[pl.BlockSpec((ts, ka), lambda j, t: (t, 0))] + [vec] * nm
        + [pl.BlockSpec((ts, tb), lambda j, t: (t, j))],
        out_specs=pl.BlockSpec((ka, tb), lambda j, t: (0, j)),
        compiler_params=_cparams(("arbitrary", "arbitrary")),
    )(a, *(mod or ()), b)


def _ada_fwd(c_all, w_ada, b_cols, name):
    cols = w_ada.shape[2]

    def body(c_ref, w_ref, b_ref, cond_ref, o_ref):
        cv = c_ref[...]
        cond = cv * _sigmoid(cv)

        @pl.when(pl.program_id(0) == 0)
        def _():
            cond_ref[...] = cond

        o_ref[...] = _dot(cond, w_ref[...]) + b_ref[...]

    return pl.pallas_call(
        body, name=name, grid=(DEPTH,),
        out_shape=[jax.ShapeDtypeStruct((NDEV, D), F32), jax.ShapeDtypeStruct((DEPTH, NDEV, cols), F32)],
        in_specs=[_full((NDEV, D)), pl.BlockSpec((None, D, cols), lambda l: (l, 0, 0)),
                  pl.BlockSpec((None, 1, cols), lambda l: (l, 0, 0))],
        out_specs=[_full((NDEV, D)), pl.BlockSpec((None, NDEV, cols), lambda l: (l, 0, 0))],
        compiler_params=_cparams(("arbitrary",)),
    )(c_all, w_ada, b_cols)


def _ada_bwd(cond, dmod_cols, name):
    cols = dmod_cols.shape[2]

    def body(c_ref, d_ref, o_ref):
        o_ref[...] = _dot_tn(c_ref[...], d_ref[...])

    return pl.pallas_call(
        body, name=name, grid=(DEPTH,),
        out_shape=jax.ShapeDtypeStruct((DEPTH, D, cols), F32),
        in_specs=[_full((NDEV, D)), pl.BlockSpec((None, NDEV, cols), lambda l: (l, 0, 0))],
        out_specs=pl.BlockSpec((None, D, cols), lambda l: (l, 0, 0)),
        compiler_params=_cparams(("arbitrary",)),
    )(cond, dmod_cols)


def _adam_math(w, g, m, v):
    m = ADAM_B1 * m + (1.0 - ADAM_B1) * g
    v = ADAM_B2 * v + (1.0 - ADAM_B2) * (g * g)
    m_hat = m / (1.0 - ADAM_B1 ** ADAM_STEP)
    v_hat = v / (1.0 - ADAM_B2 ** ADAM_STEP)
    delta = -ADAM_LR * (m_hat / (jnp.sqrt(v_hat) + ADAM_EPS) + ADAM_WD * w)
    return delta, m, v


def _adamw(parts, w, m, v, tr, name):
    r, c = w.shape
    nparts = parts.shape[0]

    def body(p_ref, w_ref, m_ref, v_ref, g_ref, d_ref, nm_ref, nv_ref):
        g = p_ref[0].astype(F32)
        for i in range(1, nparts):
            g = g + p_ref[i].astype(F32)
        delta, nm, nv = _adam_math(w_ref[...], g, m_ref[...], v_ref[...])
        g_ref[...] = g
        d_ref[...] = delta
        nm_ref[...] = nm
        nv_ref[...] = nv

    blk = pl.BlockSpec((tr, c), lambda i: (i, 0))
    return pl.pallas_call(
        body, name=name, grid=(r // tr,),
        out_shape=[jax.ShapeDtypeStruct((r, c), F32)] * 4,
        in_specs=[pl.BlockSpec((nparts, tr, c), lambda i: (0, i, 0)), blk, blk, blk],
        out_specs=[blk] * 4,
        compiler_params=_cparams(("arbitrary",)),
    )(parts, w, m, v)


def _cmul(ar, ai, br, bi):
    return ar * br - ai * bi, ar * bi + ai * br


def _ssm_prep(lam_re, lam_im, log_dt, b_re, b_im, name):
    def body(lr_ref, li_ref, dt_ref, br_ref, bi_ref, pr_ref, pi_ref, or_ref, oi_ref):
        lr, li = lr_ref[...], li_ref[...]
        dt = jnp.exp(dt_ref[...])
        mag = jnp.exp(lr * dt)
        ar, ai = mag * jnp.cos(li * dt), mag * jnp.sin(li * dt)
        den = lr * lr + li * li
        fr, fi = _cmul(ar - 1.0, ai, lr / den, -li / den)
        br, bi = br_ref[...], bi_ref[...]
        obr, obi = _cmul(fr[:, None, :], fi[:, None, :], br, bi)
        or_ref[...] = obr
        oi_ref[...] = obi
        for k in range(8):
            pr_ref[k] = ar
            pi_ref[k] = ai
            ar, ai = _cmul(ar, ai, ar, ai)

    g, p = lam_re.shape
    h = b_re.shape[1]
    return pl.pallas_call(
        body, name=name,
        out_shape=[jax.ShapeDtypeStruct((8, g, p), F32)] * 2 + [jax.ShapeDtypeStruct((g, h, p), F32)] * 2,
    )(lam_re, lam_im, log_dt, b_re, b_im)


def _ssm_param_bwd(lam_re, lam_im, log_dt, b_re, b_im, dlr, dli, dbr, dbi, name):
    def body(lr_ref, li_ref, dt_ref, br_ref, bi_ref, gar_ref, gai_ref, gbr_ref, gbi_ref,
             olr_ref, oli_ref, odt_ref, obr_ref, obi_ref):
        lr, li = lr_ref[...], li_ref[...]
        dt = jnp.exp(dt_ref[...])
        mag = jnp.exp(lr * dt)
        ar, ai = mag * jnp.cos(li * dt), mag * jnp.sin(li * dt)
        den = lr * lr + li * li
        ir, ii = lr / den, -li / den
        fr, fi = _cmul(ar - 1.0, ai, ir, ii)
        br, bi = br_ref[...], bi_ref[...]
        gbr, gbi = gbr_ref[...], gbi_ref[...]
        obr, obi = _cmul(fr[:, None, :], -fi[:, None, :], gbr, gbi)
        obr_ref[...] = obr
        obi_ref[...] = obi
        pr, pi = _cmul(br, -bi, gbr, gbi)
        gfr, gfi = jnp.sum(pr, axis=1), jnp.sum(pi, axis=1)
        t_r, t_i = _cmul(gfr, gfi, ir, -ii)
        gar, gai = gar_ref[...] + t_r, gai_ref[...] + t_i
        qr, qi = _cmul(fr, fi, ir, ii)
        l1r, l1i = _cmul(gfr, gfi, qr, -qi)
        l2r, l2i = _cmul(gar, gai, dt * ar, -dt * ai)
        olr_ref[...] = l2r - l1r
        oli_ref[...] = l2i - l1i
        mr, mi = _cmul(lr, li, ar, ai)
        dr, _ = _cmul(gar, gai, mr, -mi)
        odt_ref[...] = jnp.sum(dr, axis=1, keepdims=True) * dt

    g, p = lam_re.shape
    h = b_re.shape[1]
    return pl.pallas_call(
        body, name=name,
        out_shape=[jax.ShapeDtypeStruct((g, p), F32)] * 2 + [jax.ShapeDtypeStruct((g, 1), F32)]
        + [jax.ShapeDtypeStruct((g, h, p), F32)] * 2,
    )(lam_re, lam_im, log_dt, b_re, b_im, dlr, dli, dbr, dbi)


def _gelu(x):
    k = math.sqrt(2.0 / math.pi)
    return 0.5 * x * (1.0 + jnp.tanh(k * (x + 0.044715 * x * x * x)))


def _gelu_grad(x):
    k = math.sqrt(2.0 / math.pi)
    th = jnp.tanh(k * (x + 0.044715 * x * x * x))
    return 0.5 * (1.0 + th) + 0.5 * x * (1.0 - th * th) * k * (1.0 + 3 * 0.044715 * x * x)


def _pool_inv_count(chunk):
    t = lax.broadcasted_iota(jnp.int32, (T, GW), 0) + chunk * T + 1
    lane = lax.broadcasted_iota(jnp.int32, (T, GW), 1)
    win = jnp.where(lane < 64, 2, jnp.where(lane < 128, 4, jnp.where(lane < 192, 8, 16)))
    return 1.0 / jnp.minimum(t, win).astype(F32)


def _window_select(acc2, acc4, acc8, acc16):
    lane = lax.broadcasted_iota(jnp.int32, (T, GW), 1)
    return jnp.where(lane < 64, acc2, jnp.where(lane < 128, acc4, jnp.where(lane < 192, acc8, acc16)))


def _window_sums(buf, base, step):
    acc = None
    snaps = []
    for j in range(16):
        v = buf[pl.ds(base + step * j, T), :]
        acc = v if acc is None else acc + v
        if j + 1 in (2, 4, 8, 16):
            snaps.append(acc)
    return _window_select(*snaps)


def _conv_taps(buf, w_ref, taps, base):
    acc = None
    for k in range(taps):
        v = w_ref[k:k + 1, :] * buf[pl.ds(base + k, T), :]
        acc = v if acc is None else acc + v
    return acc


def _scan(x_scr, lp_ref, reverse):
    row = lax.broadcasted_iota(jnp.int32, (T, 128), 0)
    for j in range(XH // 128):
        lo, hi = j * 128, XH + j * 128
        re, im = x_scr[:, lo:lo + 128], x_scr[:, hi:hi + 128]
        for k in range(LOGT):
            d = 1 << k
            ar, ai = lp_ref[k:k + 1, lo:lo + 128], lp_ref[k:k + 1, hi:hi + 128]
            if reverse:
                ai = -ai
                keep = row < T - d
                sre, sim = pltpu.roll(re, T - d, 0), pltpu.roll(im, T - d, 0)
            else:
                keep = row >= d
                sre, sim = pltpu.roll(re, d, 0), pltpu.roll(im, d, 0)
            sre, sim = jnp.where(keep, sre, 0.0), jnp.where(keep, sim, 0.0)
            re, im = re + ar * sre - ai * sim, im + ar * sim + ai * sre
        x_scr[:, lo:lo + 128] = re
        x_scr[:, hi:hi + 128] = im


def _ssm_states(zs, bblk_ref, lp_ref, carry, x_scr):
    x_scr[...] = _dot(zs, bblk_ref[...])
    lr, li = lp_ref[0:1, 0:XH], lp_ref[0:1, XH:XW]
    cr, ci = carry[:, 0:XH], carry[:, XH:XW]
    x_scr[0:1, 0:XH] += lr * cr - li * ci
    x_scr[0:1, XH:XW] += lr * ci + li * cr
    _scan(x_scr, lp_ref, reverse=False)


def _mixer_fwd(z, scw, pblk, pscale, dww, dwb, clg, clb, bblk, cblk, lampow, ssd, wglu, bglu, name):
    s = z.shape[0]
    nch = s // T

    def body(z_ref, scw_ref, pblk_ref, ps_ref, dww_ref, dwb_ref, clg_ref, clb_ref, bblk_ref, cblk_ref, lp_ref,
             ssd_ref, wglu_ref, bglu_ref, y_ref, xb_ref, qbuf, pbuf, hbuf, carry, x_scr):
        i = pl.program_id(0)

        @pl.when(i == 0)
        def _():
            qbuf[0:HALO, :] = jnp.zeros((HALO, GW), F32)
            pbuf[0:HALO, :] = jnp.zeros((HALO, GW), F32)
            hbuf[0:HALO, :] = jnp.zeros((HALO, GW), F32)
            carry[...] = jnp.zeros_like(carry)

        zh, zb, zc = z_ref[:, 0:GW], z_ref[:, GW:2 * GW], z_ref[:, 2 * GW:3 * GW]
        zp, zcv, zcg, zs = (z_ref[:, 3 * GW:4 * GW], z_ref[:, 4 * GW:5 * GW], z_ref[:, 5 * GW:6 * GW],
                            z_ref[:, 6 * GW:7 * GW])
        qbuf[HALO:HALO + T, :] = zc * zh
        y_ref[:, 0:GW] = (zb * _conv_taps(qbuf, scw_ref, SC_TAPS, HALO - (SC_TAPS - 1))).astype(y_ref.dtype)
        pbuf[HALO:HALO + T, :] = zp
        r = _window_sums(pbuf, HALO, -1) * _pool_inv_count(i) - zp
        y_ref[:, GW:2 * GW] = (_dot(r, pblk_ref[...]) * ps_ref[...]).astype(y_ref.dtype)
        hbuf[HALO:HALO + T, :] = zcv * _sigmoid(zcg)
        hc = _conv_taps(hbuf, dww_ref, CF_TAPS, HALO - (CF_TAPS - 1)) + dwb_ref[...]
        mu = jnp.mean(hc, axis=-1, keepdims=True)
        xc = hc - mu
        var = jnp.mean(xc * xc, axis=-1, keepdims=True)
        ln = xc * lax.rsqrt(var + LN_EPS) * clg_ref[...] + clb_ref[...]
        y_ref[:, 2 * GW:3 * GW] = (ln * _sigmoid(ln)).astype(y_ref.dtype)
        xb_ref[...] = carry[...]
        _ssm_states(zs, bblk_ref, lp_ref, carry[...], x_scr)
        carry[...] = x_scr[T - 1:T, :]
        yg = _gelu(_dot(x_scr[...], cblk_ref[...]) + ssd_ref[...] * zs)
        v = _dot(yg, wglu_ref[...]) + bglu_ref[...]
        y_ref[:, 3 * GW:4 * GW] = (yg * _sigmoid(v)).astype(y_ref.dtype)
        qbuf[0:HALO, :] = qbuf[T:T + HALO, :]
        pbuf[0:HALO, :] = pbuf[T:T + HALO, :]
        hbuf[0:HALO, :] = hbuf[T:T + HALO, :]

    params = [scw, pblk, pscale, dww, dwb, clg, clb, bblk, cblk, lampow, ssd, wglu, bglu]
    return pl.pallas_call(
        body, name=name, grid=(nch,),
        out_shape=[jax.ShapeDtypeStruct((s, D), _MXU), jax.ShapeDtypeStruct((nch, 1, XW), F32)],
        in_specs=[pl.BlockSpec((T, INW), lambda i: (i, 0))] + [_full(p.shape) for p in params],
        out_specs=[pl.BlockSpec((T, D), lambda i: (i, 0)), pl.BlockSpec((None, 1, XW), lambda i: (i, 0, 0))],
        scratch_shapes=[pltpu.VMEM((HALO + T, GW), F32)] * 3 + [pltpu.VMEM((1, XW), F32), pltpu.VMEM((T, XW), F32)],
        compiler_params=_cparams(("arbitrary",)),
    )(z, *params)


def _mixer_bwd(z, dy, xbound, scw, pblk, pscale, dww, dwb, clg, clb, bblk, cblk, lampow, ssd, wglu, bglu, name):
    s = z.shape[0]
    nch = s // T
    hb = T // HALO

    def body(z_ref, zprev_ref, dy_ref, xb_ref, scw_ref, pblk_ref, ps_ref, dww_ref, dwb_ref, clg_ref, clb_ref,
             bblk_ref, cblk_ref, lp_ref, ssd_ref, wglu_ref, bglu_ref,
             dz_ref, vec_ref, small_ref, dpblk_ref, dwglu_ref, dbblk_ref, dcblk_ref, dlam_ref,
             qbuf, pbuf, hbuf, dcbuf, ebuf, dhbuf, acarry, x_scr, a_scr):
        i = pl.program_id(0)
        chunk = nch - 1 - i

        @pl.when(i == 0)
        def _():
            for ref in (vec_ref, small_ref, dpblk_ref, dwglu_ref, dbblk_ref, dcblk_ref, dlam_ref, acarry):
                ref[...] = jnp.zeros_like(ref)
            for ref in (dcbuf, ebuf, dhbuf):
                ref[T:T + HALO, :] = jnp.zeros((HALO, GW), F32)

        zh, zb, zc = z_ref[:, 0:GW], z_ref[:, GW:2 * GW], z_ref[:, 2 * GW:3 * GW]
        zp, zcv, zcg, zs = (z_ref[:, 3 * GW:4 * GW], z_ref[:, 4 * GW:5 * GW], z_ref[:, 5 * GW:6 * GW],
                            z_ref[:, 6 * GW:7 * GW])
        has_prev = (chunk > 0).astype(F32)
        qbuf[0:HALO, :] = zprev_ref[:, 2 * GW:3 * GW] * zprev_ref[:, 0:GW] * has_prev
        pbuf[0:HALO, :] = zprev_ref[:, 3 * GW:4 * GW] * has_prev
        hbuf[0:HALO, :] = zprev_ref[:, 4 * GW:5 * GW] * _sigmoid(zprev_ref[:, 5 * GW:6 * GW]) * has_prev

        dya = dy_ref[:, 0:GW]
        q = zc * zh
        qbuf[HALO:HALO + T, :] = q
        conv = _conv_taps(qbuf, scw_ref, SC_TAPS, HALO - (SC_TAPS - 1))
        dconv = dya * zb
        dcbuf[0:T, :] = dconv
        dq = None
        for k in range(SC_TAPS):
            small_ref[k:k + 1, :] += _colsum(dconv * qbuf[pl.ds(HALO - (SC_TAPS - 1) + k, T), :])
            v = scw_ref[k:k + 1, :] * dcbuf[pl.ds(SC_TAPS - 1 - k, T), :]
            dq = v if dq is None else dq + v
        dzh, dzb, dzc = dq * zc, dya * conv, dq * zh

        dyb = dy_ref[:, GW:2 * GW]
        inv = _pool_inv_count(chunk)
        pbuf[HALO:HALO + T, :] = zp
        r = _window_sums(pbuf, HALO, -1) * inv - zp
        o = _dot(r, pblk_ref[...])
        small_ref[3:4, :] += _colsum(dyb * o)
        do = dyb * ps_ref[...]
        dpblk_ref[...] += _dot_tn(r, do)
        dr = _dot_nt(do, pblk_ref[...])
        ebuf[0:T, :] = dr * inv
        dzp = _window_sums(ebuf, 0, 1) - dr

        dyc = dy_ref[:, 2 * GW:3 * GW]
        sgc = _sigmoid(zcg)
        hbuf[HALO:HALO + T, :] = zcv * sgc
        hc = _conv_taps(hbuf, dww_ref, CF_TAPS, HALO - (CF_TAPS - 1)) + dwb_ref[...]
        mu = jnp.mean(hc, axis=-1, keepdims=True)
        xc = hc - mu
        var = jnp.mean(xc * xc, axis=-1, keepdims=True)
        rstd = lax.rsqrt(var + LN_EPS)
        nrm = xc * rstd
        ln = nrm * clg_ref[...] + clb_ref[...]
        sl = _sigmoid(ln)
        dln = dyc * (sl * (1.0 + ln * (1.0 - sl)))
        small_ref[5:6, :] += _colsum(dln * nrm)
        small_ref[6:7, :] += _colsum(dln)
        dn = dln * clg_ref[...]
        dhc = rstd * (dn - jnp.mean(dn, axis=-1, keepdims=True) - nrm * jnp.mean(dn * nrm, axis=-1, keepdims=True))
        small_ref[4:5, :] += _colsum(dhc)
        dhbuf[0:T, :] = dhc
        dhg = None
        for k in range(CF_TAPS):
            small_ref[16 + k:17 + k, :] += _colsum(dhc * hbuf[pl.ds(HALO - (CF_TAPS - 1) + k, T), :])
            v = dww_ref[k:k + 1, :] * dhbuf[pl.ds(CF_TAPS - 1 - k, T), :]
            dhg = v if dhg is None else dhg + v
        dzcv = dhg * sgc
        dzcg = dhg * zcv * sgc * (1.0 - sgc)

        dyd = dy_ref[:, 3 * GW:4 * GW]
        xin = xb_ref[...]
        _ssm_states(zs, bblk_ref, lp_ref, xin, x_scr)
        yy = _dot(x_scr[...], cblk_ref[...]) + ssd_ref[...] * zs
        yg = _gelu(yy)
        v = _dot(yg, wglu_ref[...]) + bglu_ref[...]
        sg = _sigmoid(v)
        dv = dyd * yg * sg * (1.0 - sg)
        small_ref[8:9, :] += _colsum(dv)
        dwglu_ref[...] += _dot_tn(yg, dv)
        g = (dyd * sg + _dot_nt(dv, wglu_ref[...])) * _gelu_grad(yy)
        small_ref[7:8, :] += _colsum(g * zs)
        dcblk_ref[...] += _dot_tn(x_scr[...], g)
        a_scr[...] = _dot_nt(g, cblk_ref[...])
        lr, li = lp_ref[0:1, 0:XH], lp_ref[0:1, XH:XW]
        cr, ci = acarry[:, 0:XH], acarry[:, XH:XW]
        a_scr[T - 1:T, 0:XH] += lr * cr + li * ci
        a_scr[T - 1:T, XH:XW] += lr * ci - li * cr
        _scan(a_scr, lp_ref, reverse=True)
        acarry[...] = a_scr[0:1, :]
        dzs = _dot_nt(a_scr[...], bblk_ref[...]) + ssd_ref[...] * g
        dbblk_ref[...] += _dot_tn(zs, a_scr[...])
        row = lax.broadcasted_iota(jnp.int32, (T, 128), 0)
        for j in range(XH // 128):
            lo, hi = j * 128, XH + j * 128
            pr = jnp.where(row == 0, xin[:, lo:lo + 128], pltpu.roll(x_scr[:, lo:lo + 128], 1, 0))
            pi = jnp.where(row == 0, xin[:, hi:hi + 128], pltpu.roll(x_scr[:, hi:hi + 128], 1, 0))
            ar, ai = a_scr[:, lo:lo + 128], a_scr[:, hi:hi + 128]
            dlam_ref[0:1, lo:lo + 128] += _colsum(ar * pr + ai * pi)
            dlam_ref[0:1, hi:hi + 128] += _colsum(ai * pr - ar * pi)

        parts = (dzh, dzb, dzc, dzp, dzcv, dzcg, dzs)
        for k, part in enumerate(parts):
            dz_ref[:, k * GW:(k + 1) * GW] = part.astype(dz_ref.dtype)
            vec_ref[0:1, k * GW:(k + 1) * GW] += _colsum(part)
        dcbuf[T:T + HALO, :] = dcbuf[0:HALO, :]
        ebuf[T:T + HALO, :] = ebuf[0:HALO, :]
        dhbuf[T:T + HALO, :] = dhbuf[0:HALO, :]

    params = [scw, pblk, pscale, dww, dwb, clg, clb, bblk, cblk, lampow, ssd, wglu, bglu]
    rev = lambda i: (nch - 1 - i, 0)
    outs = [((s, INW), _MXU), ((8, INW), F32), ((48, GW), F32), ((GW, GW), F32), ((GW, GW), F32),
            ((GW, XW), F32), ((XW, GW), F32), ((8, XW), F32)]
    return pl.pallas_call(
        body, name=name, grid=(nch,),
        out_shape=[jax.ShapeDtypeStruct(sh, dt) for sh, dt in outs],
        in_specs=[pl.BlockSpec((T, INW), rev),
                  pl.BlockSpec((HALO, INW), lambda i: (jnp.maximum((nch - 1 - i) * hb - 1, 0), 0)),
                  pl.BlockSpec((T, D), rev),
                  pl.BlockSpec((None, 1, XW), lambda i: (nch - 1 - i, 0, 0))] + [_full(p.shape) for p in params],
        out_specs=[pl.BlockSpec((T, INW), rev)] + [_full(sh) for sh, _ in outs[1:]],
        scratch_shapes=[pltpu.VMEM((HALO + T, GW), F32)] * 6
        + [pltpu.VMEM((1, XW), F32), pltpu.VMEM((T, XW), F32), pltpu.VMEM((T, XW), F32)],
        compiler_params=_cparams(("arbitrary",)),
    )(z, z, dy, xbound, *params)


def _blockdiag(m):
    g, h, p = m.shape
    eye = jnp.eye(g, dtype=m.dtype)
    return (m[:, :, None, :] * eye[:, None, :, None]).reshape(g * h, g * p)


def _blockdiag_extract(f, h, p):
    g = f.shape[0] // h
    eye = jnp.eye(g, dtype=f.dtype)
    return jnp.sum(f.reshape(g, h, g, p) * eye[:, None, :, None], axis=2)


PACK_ROWS = 512


def _pack(arrs):
    flat = jnp.concatenate([a.reshape(-1).astype(F32) for a in arrs])
    quantum = PACK_ROWS * 128
    padded = -(-flat.shape[0] // quantum) * quantum
    return jnp.pad(flat, (0, padded - flat.shape[0])).reshape(-1, 128)


def _unpack(buf, shapes, lead=()):
    flat = buf.reshape(lead + (-1,))
    out, off = [], 0
    for shp in shapes:
        n = math.prod(shp)
        out.append(flat[..., off:off + n].reshape(lead + tuple(shp)))
        off += n
    return out


def _cols_to_full(g):
    n, l, r, c = g.shape
    return g.transpose(1, 2, 0, 3).reshape(l, r, n * c)


def _rows_to_full(g):
    n, l, r, c = g.shape
    return g.transpose(1, 0, 2, 3).reshape(l, n * r, c)


def _full_to_cols(w):
    l, r, c = w.shape
    return w.reshape(l, r, NDEV, c // NDEV).transpose(2, 0, 1, 3)


def _full_to_rows(w):
    l, r, c = w.shape
    return w.reshape(l, NDEV, r // NDEV, c).transpose(1, 0, 2, 3)


REPLICATED = ['b_ada', 'b_in', 'pool_w', 'pool_scale', 'cf_dw_b', 'cf_ln_g', 'cf_ln_b', 'ssm_lam_re', 'ssm_lam_im',
              'ssm_log_dt', 'ssm_b_re', 'ssm_b_im', 'ssm_c_re', 'ssm_c_im', 'ssm_d', 'ssm_b_glu', 'ln1_g', 'ln1_b',
              'ln2_g', 'ln2_b']
SMALL_SHARDED = ['sc_w', 'cf_dw_w', 'ssm_w_glu']
COL_SHARDED = ['w_in', 'w_gate', 'w_up']
ROW_SHARDED = ['w_o', 'w_down']
WEIGHTS = ['w_ada', 'b_ada', 'w_in', 'b_in', 'sc_w', 'pool_w', 'pool_scale', 'cf_dw_w', 'cf_dw_b', 'cf_ln_g',
           'cf_ln_b', 'ssm_lam_re', 'ssm_lam_im', 'ssm_log_dt', 'ssm_b_re', 'ssm_b_im', 'ssm_c_re', 'ssm_c_im',
           'ssm_d', 'ssm_w_glu', 'ssm_b_glu', 'w_o', 'ln1_g', 'ln1_b', 'w_gate', 'w_up', 'w_down', 'ln2_g', 'ln2_b']


def _train_step(p):
    xi, yi, ci = _me()
    me = _flat((xi, yi, ci))
    s = p['x'].shape[1]
    xs = p['x'].reshape(s, D)
    target = p['loss_target'].reshape(s, D)
    row = lambda a: a.reshape(1, -1)

    small_shapes = [p[n].shape for n in SMALL_SHARDED] + [p['c'].shape]
    gathered = _allgather(
        [_pack([p[n] for n in SMALL_SHARDED] + [p['c']])] + [p[n].astype(_MXU) for n in COL_SHARDED + ROW_SHARDED],
        "gather_weights")
    scw_g, dww_g, wglu_g, c_g = _unpack(gathered[0], small_shapes, lead=(NDEV,))
    full = {'sc_w': _cols_to_full(scw_g), 'cf_dw_w': _cols_to_full(dww_g), 'ssm_w_glu': _rows_to_full(wglu_g)}
    for n, g in zip(COL_SHARDED, gathered[1:4]):
        full[n] = _cols_to_full(g)
    for n, g in zip(ROW_SHARDED, gathered[4:6]):
        full[n] = _rows_to_full(g)
    c_all = c_g.reshape(NDEV, D)

    ncol = p['w_ada'].shape[2]
    b_cols = lax.dynamic_slice_in_dim(p['b_ada'], me * ncol, ncol, axis=1).reshape(DEPTH, 1, ncol)
    cond, mod_cols = _ada_fwd(c_all, p['w_ada'], b_cols, "ada_fwd")
    (mod_x,) = _alltoall([mod_cols.transpose(1, 0, 2)], "mod_exchange")
    mod = mod_x.transpose(1, 0, 2).reshape(DEPTH, 6 * D)

    def mixer_params(l):
        pr, pi, bbr, bbi = _ssm_prep(p['ssm_lam_re'][l], p['ssm_lam_im'][l], p['ssm_log_dt'][l].reshape(NGRP, 1),
                                     p['ssm_b_re'][l].transpose(0, 2, 1), p['ssm_b_im'][l].transpose(0, 2, 1),
                                     "ssm_prep")
        bblk = jnp.concatenate([_blockdiag(bbr), _blockdiag(bbi)], axis=1).astype(_MXU)
        cblk = jnp.concatenate([_blockdiag(p['ssm_c_re'][l].transpose(0, 2, 1)),
                                -_blockdiag(p['ssm_c_im'][l].transpose(0, 2, 1))], axis=0).astype(_MXU)
        lampow = jnp.concatenate([pr.reshape(8, XH), pi.reshape(8, XH)], axis=1)
        return [full['sc_w'][l], _blockdiag(p['pool_w'][l]).astype(_MXU), row(p['pool_scale'][l]),
                full['cf_dw_w'][l], row(p['cf_dw_b'][l]), row(p['cf_ln_g'][l]), row(p['cf_ln_b'][l]),
                bblk, cblk, lampow, row(p['ssm_d'][l]), full['ssm_w_glu'][l].astype(_MXU), row(p['ssm_b_glu'][l])]

    saved = []
    for l in range(DEPTH):
        sh1, sc1, g1, sh2, sc2, g2 = [mod[l, k * D:(k + 1) * D].reshape(1, D) for k in range(6)]
        mp = mixer_params(l)
        (z,) = _modulated_matmul(xs, sc1, sh1, [full['w_in'][l]], row(p['b_in'][l]), INW // 2, [F32],
                                 lambda a: (a,), "in_proj")
        ycat, xbound = _mixer_fwd(z, *mp, "mixer_fwd")
        y, u1, x1 = _proj_residual_ln(ycat, full['w_o'][l], xs, g1, row(p['ln1_g'][l]), row(p['ln1_b'][l]),
                                      "o_proj_ln")
        gate, up, act = _modulated_matmul(x1, sc2, sh2, [full['w_gate'][l], full['w_up'][l]], None, DFF // 2,
                                          [F32, F32, _MXU], _swiglu, "gate_up")
        f, u2, x2 = _proj_residual_ln(act, full['w_down'][l], x1, g2, row(p['ln2_g'][l]), row(p['ln2_b'][l]),
                                      "down_proj_ln")
        saved.append(dict(x=xs, z=z, ycat=ycat, xbound=xbound, y=y, u1=u1, x1=x1, gate=gate, up=up, act=act,
                          f=f, u2=u2, mp=mp, mod=(sh1, sc1, g1, sh2, sc2, g2)))
        xs = x2

    dx, sq = _loss_head(xs, target, "loss_head")
    loss = lax.psum(sq[0, 0] * (0.5 / D), AXES)

    grads = {n: [None] * DEPTH for n in WEIGHTS}
    dmod = [None] * DEPTH
    for l in reversed(range(DEPTH)):
        sv = saved[l]
        sh1, sc1, g1, sh2, sc2, g2 = sv['mod']
        du2, df, acc2 = _ln_bwd(dx, sv['u2'], sv['f'], row(p['ln2_g'][l]), g2, "ln2_bwd")
        dgate, dup = _swiglu_bwd(df, full['w_down'][l], sv['gate'], sv['up'], DFF // 2, "swiglu_bwd")
        grads['w_down'][l] = _weight_grad(sv['act'], df, 512, TM, "dw_down")
        dx1, acc_h2 = _input_grad([dgate, dup], [full['w_gate'][l], full['w_up'][l]], du2, sv['x1'], sc2, 256, "dh2")
        grads['w_gate'][l] = _weight_grad(sv['x1'], dgate, DFF // 2, TM, "dw_gate", mod=(sc2, sh2))
        grads['w_up'][l] = _weight_grad(sv['x1'], dup, DFF // 2, TM, "dw_up", mod=(sc2, sh2))
        du1, dyb, acc1 = _ln_bwd(dx1, sv['u1'], sv['y'], row(p['ln1_g'][l]), g1, "ln1_bwd")
        dycat = _matmul_nt(dyb, full['w_o'][l], "dycat")
        grads['w_o'][l] = _weight_grad(sv['ycat'], dyb, D, TM, "dw_o")
        dz, vec, small, dpblk, dwglu, dbblk, dcblk, dlam = _mixer_bwd(sv['z'], dycat, sv['xbound'], *sv['mp'],
                                                                      "mixer_bwd")
        dx, acc_h1 = _input_grad([dz], [full['w_in'][l]], du1, sv['x'], sc1, TM, "dh1")
        grads['w_in'][l] = _weight_grad(sv['x'], dz, INW // 2, TM, "dw_in", mod=(sc1, sh1))
        dmod[l] = jnp.concatenate([acc_h1[0], acc_h1[1], acc1[2], acc_h2[0], acc_h2[1], acc2[2]])
        grads['ln2_g'][l], grads['ln2_b'][l] = acc2[0], acc2[1]
        grads['ln1_g'][l], grads['ln1_b'][l] = acc1[0], acc1[1]
        grads['b_in'][l] = vec[0]
        grads['sc_w'][l] = small[0:SC_TAPS]
        grads['pool_scale'][l], grads['cf_dw_b'][l], grads['cf_ln_g'][l] = small[3], small[4], small[5]
        grads['cf_ln_b'][l], grads['ssm_d'][l], grads['ssm_b_glu'][l] = small[6], small[7], small[8]
        grads['cf_dw_w'][l] = small[16:16 + CF_TAPS]
        grads['pool_w'][l] = _blockdiag_extract(dpblk, 64, 64)
        grads['ssm_w_glu'][l] = dwglu
        grads['ssm_c_re'][l] = _blockdiag_extract(dcblk[:XH], NSTATE, NCH).transpose(0, 2, 1)
        grads['ssm_c_im'][l] = -_blockdiag_extract(dcblk[XH:], NSTATE, NCH).transpose(0, 2, 1)
        dlr, dli, dlog, dbr, dbi = _ssm_param_bwd(
            p['ssm_lam_re'][l], p['ssm_lam_im'][l], p['ssm_log_dt'][l].reshape(NGRP, 1),
            p['ssm_b_re'][l].transpose(0, 2, 1), p['ssm_b_im'][l].transpose(0, 2, 1),
            dlam[0, :XH].reshape(NGRP, NSTATE), dlam[0, XH:].reshape(NGRP, NSTATE),
            _blockdiag_extract(dbblk[:, :XH], NCH, NSTATE), _blockdiag_extract(dbblk[:, XH:], NCH, NSTATE),
            "ssm_param_bwd")
        grads['ssm_lam_re'][l], grads['ssm_lam_im'][l], grads['ssm_log_dt'][l] = dlr, dli, dlog.reshape(NGRP)
        grads['ssm_b_re'][l], grads['ssm_b_im'][l] = dbr.transpose(0, 2, 1), dbi.transpose(0, 2, 1)
    grads['b_ada'] = dmod
    partial = {n: jnp.stack(grads[n]) for n in WEIGHTS if n != 'w_ada'}

    out = {'loss': loss, 'grad_x': dx.reshape(1, s, D)}

    def emit(n, g, delta, nm, nv):
        shp = p[n].shape
        out['grad_' + n], out['delta_' + n] = g.reshape(shp), delta.reshape(shp)
        out['new_m_' + n], out['new_v_' + n] = nm.reshape(shp), nv.reshape(shp)

    names_a = REPLICATED + SMALL_SHARDED
    shapes_a = [partial[n].shape for n in names_a]
    zeros_like_full = lambda n: jnp.zeros(partial[n].shape, F32)
    (parts_a,) = _allgather([_pack([partial[n] for n in names_a])], "gather_small_grads")
    packs = [_pack([p[pre + n] for n in REPLICATED] + [zeros_like_full(n) for n in SMALL_SHARDED])
             for pre in ('', 'm_', 'v_')]
    res_a = _adamw(parts_a, *packs, PACK_ROWS, "adamw_small")
    res_a = [_unpack(r, shapes_a) for r in res_a]
    for k, n in enumerate(REPLICATED):
        emit(n, *[r[k] for r in res_a])
    reduced = dict(zip(names_a, res_a[0]))

    blocks = []
    for n in SMALL_SHARDED:
        axis = 1 if n == 'ssm_w_glu' else 2
        width = p[n].shape[axis]
        blocks.append(lax.dynamic_slice_in_dim(reduced[n], me * width, width, axis=axis))
    shapes_b = [b.shape for b in blocks]
    packs = [_pack([p[pre + n] for n in SMALL_SHARDED]) for pre in ('', 'm_', 'v_')]
    res_b = _adamw(_pack(blocks)[None], *packs, PACK_ROWS, "adamw_small_sharded")
    res_b = [_unpack(r, shapes_b) for r in res_b]
    for k, n in enumerate(SMALL_SHARDED):
        emit(n, *[r[k] for r in res_b])

    dmod_all = _unpack(parts_a, shapes_a, lead=(NDEV,))[0]
    dmod_cols = lax.dynamic_slice_in_dim(dmod_all, me * ncol, ncol, axis=2).transpose(1, 0, 2)
    g_ada = _ada_bwd(cond, dmod_cols, "ada_bwd")
    flat2 = lambda a: a.reshape(-1, a.shape[-1])
    emit('w_ada', *_adamw(flat2(g_ada)[None], flat2(p['w_ada']), flat2(p['m_w_ada']), flat2(p['v_w_ada']),
                          512, "adamw_w_ada"))

    big = COL_SHARDED + ROW_SHARDED
    sends = [_full_to_cols(partial[n]).astype(_MXU) for n in COL_SHARDED]
    sends += [_full_to_rows(partial[n]).astype(_MXU) for n in ROW_SHARDED]
    recvs = _alltoall(sends, "exchange_grads")
    for n, r in zip(big, recvs):
        parts = r.reshape(NDEV, -1, r.shape[-1])
        tr = 512 if parts.shape[1] % 512 == 0 else 352
        emit(n, *_adamw(parts, flat2(p[n]), flat2(p['m_' + n]), flat2(p['v_' + n]), tr, "adamw_" + n))
    return out


def kernel(x, c, w_ada, b_ada, w_in, b_in, sc_w, pool_w, pool_scale, cf_dw_w, cf_dw_b, cf_ln_g, cf_ln_b, ssm_lam_re, ssm_lam_im, ssm_log_dt, ssm_b_re, ssm_b_im, ssm_c_re, ssm_c_im, ssm_d, ssm_w_glu, ssm_b_glu, w_o, ln1_g, ln1_b, w_gate, w_up, w_down, ln2_g, ln2_b, loss_target, m_w_ada, m_b_ada, m_w_in, m_b_in, m_sc_w, m_pool_w, m_pool_scale, m_cf_dw_w, m_cf_dw_b, m_cf_ln_g, m_cf_ln_b, m_ssm_lam_re, m_ssm_lam_im, m_ssm_log_dt, m_ssm_b_re, m_ssm_b_im, m_ssm_c_re, m_ssm_c_im, m_ssm_d, m_ssm_w_glu, m_ssm_b_glu, m_w_o, m_ln1_g, m_ln1_b, m_w_gate, m_w_up, m_w_down, m_ln2_g, m_ln2_b, v_w_ada, v_b_ada, v_w_in, v_b_in, v_sc_w, v_pool_w, v_pool_scale, v_cf_dw_w, v_cf_dw_b, v_cf_ln_g, v_cf_ln_b, v_ssm_lam_re, v_ssm_lam_im, v_ssm_log_dt, v_ssm_b_re, v_ssm_b_im, v_ssm_c_re, v_ssm_c_im, v_ssm_d, v_ssm_w_glu, v_ssm_b_glu, v_w_o, v_ln1_g, v_ln1_b, v_w_gate, v_w_up, v_w_down, v_ln2_g, v_ln2_b):
    out = _train_step(dict(locals()))
    return (out['loss'], out['grad_x'], *[out[pre + n] for pre in ('grad_', 'delta_', 'new_m_', 'new_v_')
                                          for n in WEIGHTS])
```

```python
import math

import jax
import jax.numpy as jnp
from jax import lax
from jax.experimental import pallas as pl
from jax.experimental.pallas import tpu as pltpu

F32 = jnp.float32
_MXU = jnp.bfloat16

D = 1024
GW = 256
INW = 7 * GW
DFF = 2816
DEPTH = 4
NDEV = 8
SC_TAPS = 3
CF_TAPS = 31
NGRP = 16
NCH = 16
NSTATE = 64
XH = NGRP * NSTATE
XW = 2 * XH
ALPHA = (2 * DEPTH) ** 0.25
LN_EPS = 1e-5
T = 128
SUB = 8
HALO = 32
TM = 512
TM_WIDE = 256
VMEM_LIMIT = 56 * 1024 * 1024

ADAM_LR = 0.001
ADAM_B1 = 0.9
ADAM_B2 = 0.999
ADAM_EPS = 1e-08
ADAM_WD = 0.01
ADAM_STEP = 10

MESH = pl.DeviceIdType.MESH
ANY = pl.BlockSpec(memory_space=pl.ANY)
AXES = ("x", "y", "c")


def _cparams(ngrid):
    return pltpu.CompilerParams(dimension_semantics=("arbitrary",) * ngrid, vmem_limit_bytes=VMEM_LIMIT)


def _full(shape):
    nd = len(shape)
    return pl.BlockSpec(shape, lambda *_: (0,) * nd)


def _sel(arr, *idx):
    tail = arr.shape[len(idx):]
    return pl.BlockSpec((None,) * len(idx) + tail, lambda *_: idx + (0,) * len(tail))


def _dot(a, b):
    return jnp.dot(a.astype(_MXU), b.astype(_MXU), preferred_element_type=F32)


def _dot_nt(a, b):
    return lax.dot_general(a.astype(_MXU), b.astype(_MXU), (((1,), (1,)), ((), ())), preferred_element_type=F32)


def _dot_tn(a, b):
    return lax.dot_general(a.astype(_MXU), b.astype(_MXU), (((0,), (0,)), ((), ())), preferred_element_type=F32)


def _sigmoid(x):
    return 1.0 / (1.0 + jnp.exp(-x))


def _colsum(x):
    return jnp.sum(x, axis=0, keepdims=True)


def _me():
    return lax.axis_index("x"), lax.axis_index("y"), lax.axis_index("c")


def _flat(p):
    return 4 * p[0] + 2 * p[1] + p[2]


def _allgather(arrays, name):
    n = len(arrays)

    def body(*refs):
        ins, outs = refs[:n], refs[n:2 * n]
        send_sems, recv_sems, local_sems = refs[2 * n:]
        x, y, c = _me()
        me, sibling = (x, y, c), (x, y, 1 - c)
        chips = [(1 - x, y), (x, 1 - y), (1 - x, 1 - y)]

        def copy(a, k, block, to, src=None):
            dst = outs[a].at[_flat(block)]
            return pltpu.make_async_remote_copy(
                src_ref=dst if src is None else src, dst_ref=dst,
                send_sem=send_sems.at[a, k], recv_sem=recv_sems.at[a, k],
                device_id=to, device_id_type=MESH)

        started = []
        for a in range(n):
            mine = pltpu.make_async_copy(ins[a], outs[a].at[_flat(me)], local_sems.at[a])
            mine.start()
            started.append(mine)
        first = []
        for a in range(n):
            first.append(copy(a, 0, me, sibling, src=ins[a]))
            first += [copy(a, 1 + j, me, (*chip, c), src=ins[a]) for j, chip in enumerate(chips)]
        for cp in first:
            cp.start()
        passed = []
        for a in range(n):
            for j, chip in enumerate(chips):
                copy(a, 1 + j, (*chip, c), me).wait_recv()
                fwd = copy(a, 4 + j, (*chip, c), sibling)
                fwd.start()
                passed.append(fwd)
        for a in range(n):
            copy(a, 0, sibling, me).wait_recv()
            for j, chip in enumerate(chips):
                copy(a, 4 + j, (*chip, 1 - c), me).wait_recv()
        for cp in first + passed:
            cp.wait_send()
        for cp in started:
            cp.wait()

    return pl.pallas_call(
        body, name=name,
        out_shape=[jax.ShapeDtypeStruct((NDEV,) + a.shape, a.dtype) for a in arrays],
        in_specs=[ANY] * n, out_specs=[ANY] * n,
        scratch_shapes=[pltpu.SemaphoreType.DMA((n, 7)), pltpu.SemaphoreType.DMA((n, 7)),
                        pltpu.SemaphoreType.DMA((n,))],
    )(*arrays)


def _alltoall(arrays, name):
    n = len(arrays)

    def body(*refs):
        ins, outs = refs[:n], refs[n:2 * n]
        send_sems, recv_sems, local_sems = refs[2 * n:]
        x, y, c = _me()
        mine = _flat((x, y, c))
        copies = []
        for a in range(n):
            loc = pltpu.make_async_copy(ins[a].at[mine], outs[a].at[mine], local_sems.at[a])
            loc.start()
            copies.append(loc)
        remote = []
        for a in range(n):
            for k in range(1, NDEV):
                peer = (lax.rem(x + ((k >> 2) & 1), 2), lax.rem(y + ((k >> 1) & 1), 2), lax.rem(c + (k & 1), 2))
                cp = pltpu.make_async_remote_copy(
                    src_ref=ins[a].at[_flat(peer)], dst_ref=outs[a].at[mine],
                    send_sem=send_sems.at[a, k - 1], recv_sem=recv_sems.at[a, k - 1],
                    device_id=peer, device_id_type=MESH)
                cp.start()
                remote.append(cp)
        for cp in remote:
            cp.wait_send()
            cp.wait_recv()
        for cp in copies:
            cp.wait()

    return pl.pallas_call(
        body, name=name,
        out_shape=[jax.ShapeDtypeStruct(a.shape, a.dtype) for a in arrays],
        in_specs=[ANY] * n, out_specs=[ANY] * n,
        scratch_shapes=[pltpu.SemaphoreType.DMA((n, 7)), pltpu.SemaphoreType.DMA((n, 7)),
                        pltpu.SemaphoreType.DMA((n,))],
    )(*arrays)


def _modulated_matmul(x, mod, l, ksc, ksh, ws, bias, tm, out_dtypes, epilogue, name):
    s, k = x.shape
    nw = len(ws)
    n = ws[0].shape[2]
    nb = 0 if bias is None else 1

    def body(*refs):
        x_ref, sc_ref, sh_ref = refs[:3]
        w_refs = refs[3:3 + nw]
        b_refs = refs[3 + nw:3 + nw + nb]
        o_refs = refs[3 + nw + nb:]
        h = (x_ref[...] * (1.0 + sc_ref[...]) + sh_ref[...]).astype(_MXU)
        prods = [jnp.dot(h, w[...], preferred_element_type=F32) for w in w_refs]
        if nb:
            prods[0] = prods[0] + b_refs[0][...]
        for o, v in zip(o_refs, epilogue(*prods)):
            o[...] = v.astype(o.dtype)

    return pl.pallas_call(
        body, name=name, grid=(s // tm,),
        out_shape=[jax.ShapeDtypeStruct((s, n), dt) for dt in out_dtypes],
        in_specs=[pl.BlockSpec((tm, k), lambda i: (i, 0)), _sel(mod, l, ksc), _sel(mod, l, ksh)]
        + [_sel(w, l) for w in ws] + ([_sel(bias, l)] if nb else []),
        out_specs=[pl.BlockSpec((tm, n), lambda i: (i, 0))] * len(out_dtypes),
        compiler_params=_cparams(1),
    )(x, mod, mod, *ws, *([bias] if nb else []))


def _swiglu(gate, up):
    return gate, up, gate * _sigmoid(gate) * up


def _proj_residual_ln(a, w, l, xres, mod, kg, lng, lnb, name):
    s, k = a.shape
    n = w.shape[2]

    def body(a_ref, w_ref, x_ref, g_ref, lg_ref, lb_ref, r_ref, u_ref, o_ref):
        r = jnp.dot(a_ref[...], w_ref[...], preferred_element_type=F32)
        u = ALPHA * x_ref[...] + (1.0 + g_ref[...]) * r
        mu = jnp.mean(u, axis=-1, keepdims=True)
        xc = u - mu
        var = jnp.mean(xc * xc, axis=-1, keepdims=True)
        r_ref[...] = r
        u_ref[...] = u
        o_ref[...] = xc * lax.rsqrt(var + LN_EPS) * lg_ref[...] + lb_ref[...]

    row = pl.BlockSpec((TM, n), lambda i: (i, 0))
    return pl.pallas_call(
        body, name=name, grid=(s // TM,),
        out_shape=[jax.ShapeDtypeStruct((s, n), F32)] * 3,
        in_specs=[pl.BlockSpec((TM, k), lambda i: (i, 0)), _sel(w, l), row, _sel(mod, l, kg), _sel(lng, l),
                  _sel(lnb, l)],
        out_specs=[row, row, row],
        compiler_params=_cparams(1),
    )(a, w, xres, mod, lng, lnb)


def _loss_head(xo, target, name):
    s, n = xo.shape

    def body(x_ref, t_ref, dx_ref, acc_ref):
        @pl.when(pl.program_id(0) == 0)
        def _():
            acc_ref[...] = jnp.zeros_like(acc_ref)

        err = x_ref[...] - t_ref[...]
        dx_ref[...] = err * (1.0 / n)
        acc_ref[...] += jnp.sum(err * err)

    row = pl.BlockSpec((TM, n), lambda i: (i, 0))
    return pl.pallas_call(
        body, name=name, grid=(s // TM,),
        out_shape=[jax.ShapeDtypeStruct((s, n), F32), jax.ShapeDtypeStruct((8, 128), F32)],
        in_specs=[row, row], out_specs=[row, _full((8, 128))],
        compiler_params=_cparams(1),
    )(xo, target)


def _ln_bwd(dxo, u, res, lng, l, mod, kg, name):
    s, n = u.shape

    def body(dy_ref, u_ref, r_ref, lg_ref, g_ref, du_ref, dr_ref, acc_ref):
        @pl.when(pl.program_id(0) == 0)
        def _():
            acc_ref[...] = jnp.zeros_like(acc_ref)

        uu = u_ref[...]
        mu = jnp.mean(uu, axis=-1, keepdims=True)
        xc = uu - mu
        var = jnp.mean(xc * xc, axis=-1, keepdims=True)
        rstd = lax.rsqrt(var + LN_EPS)
        xh = xc * rstd
        dy = dy_ref[...]
        dxh = dy * lg_ref[...]
        du = rstd * (dxh - jnp.mean(dxh, axis=-1, keepdims=True) - xh * jnp.mean(dxh * xh, axis=-1, keepdims=True))
        du_ref[...] = du
        dr_ref[...] = (du * (1.0 + g_ref[...])).astype(dr_ref.dtype)
        acc_ref[0:1, :] += _colsum(dy * xh)
        acc_ref[1:2, :] += _colsum(dy)
        acc_ref[2:3, :] += _colsum(du * r_ref[...])

    row = pl.BlockSpec((TM, n), lambda i: (i, 0))
    return pl.pallas_call(
        body, name=name, grid=(s // TM,),
        out_shape=[jax.ShapeDtypeStruct((s, n), F32), jax.ShapeDtypeStruct((s, n), _MXU),
                   jax.ShapeDtypeStruct((8, n), F32)],
        in_specs=[row, row, row, _sel(lng, l), _sel(mod, l, kg)], out_specs=[row, row, _full((8, n))],
        compiler_params=_cparams(1),
    )(dxo, u, res, lng, mod)


def _swiglu_bwd(dr, w_down, l, gate, up, name):
    s, n = dr.shape
    f = w_down.shape[1]

    def body(dr_ref, w_ref, g_ref, u_ref, dg_ref, du_ref):
        dp = lax.dot_general(dr_ref[...], w_ref[...], (((1,), (1,)), ((), ())), preferred_element_type=F32)
        g = g_ref[...].astype(F32)
        sg = _sigmoid(g)
        dg_ref[...] = (dp * u_ref[...].astype(F32) * (sg * (1.0 + g * (1.0 - sg)))).astype(dg_ref.dtype)
        du_ref[...] = (dp * (g * sg)).astype(du_ref.dtype)

    tile = pl.BlockSpec((TM_WIDE, f), lambda i: (i, 0))
    return pl.pallas_call(
        body, name=name, grid=(s // TM_WIDE,),
        out_shape=[jax.ShapeDtypeStruct((s, f), _MXU)] * 2,
        in_specs=[pl.BlockSpec((TM_WIDE, n), lambda i: (i, 0)), _sel(w_down, l), tile, tile],
        out_specs=[tile, tile],
        compiler_params=_cparams(1),
    )(dr, w_down, gate, up)


def _input_grad(acts, ws, l, du, xin, mod, ksc, tm, name):
    s = acts[0].shape[0]
    n = ws[0].shape[1]
    na = len(acts)

    def body(*refs):
        a_refs, w_refs = refs[:na], refs[na:2 * na]
        du_ref, x_ref, sc_ref, dx_ref, acc_ref = refs[2 * na:]

        @pl.when(pl.program_id(0) == 0)
        def _():
            acc_ref[...] = jnp.zeros_like(acc_ref)

        dh = None
        for a, w in zip(a_refs, w_refs):
            t = lax.dot_general(a[...], w[...], (((1,), (1,)), ((), ())), preferred_element_type=F32)
            dh = t if dh is None else dh + t
        dx_ref[...] = ALPHA * du_ref[...] + dh * (1.0 + sc_ref[...])
        acc_ref[0:1, :] += _colsum(dh)
        acc_ref[1:2, :] += _colsum(dh * x_ref[...])

    row = pl.BlockSpec((tm, n), lambda i: (i, 0))
    return pl.pallas_call(
        body, name=name, grid=(s // tm,),
        out_shape=[jax.ShapeDtypeStruct((s, n), F32), jax.ShapeDtypeStruct((8, n), F32)],
        in_specs=[pl.BlockSpec((tm, a.shape[1]), lambda i: (i, 0)) for a in acts]
        + [_sel(w, l) for w in ws] + [row, row, _sel(mod, l, ksc)],
        out_specs=[row, _full((8, n))],
        compiler_params=_cparams(1),
    )(*acts, *ws, du, xin, mod)


def _matmul_nt(a, w, l, name):
    s, k = a.shape
    n = w.shape[1]

    def body(a_ref, w_ref, o_ref):
        o_ref[...] = lax.dot_general(a_ref[...], w_ref[...], (((1,), (1,)), ((), ())), preferred_element_type=F32)

    return pl.pallas_call(
        body, name=name, grid=(s // TM,),
        out_shape=jax.ShapeDtypeStruct((s, n), F32),
        in_specs=[pl.BlockSpec((TM, k), lambda i: (i, 0)), _sel(w, l)],
        out_specs=pl.BlockSpec((TM, n), lambda i: (i, 0)),
        compiler_params=_cparams(1),
    )(a, w)


def _weight_grad(a, b, tb, ts, name, mod=None):
    s, ka = a.shape
    nbc = b.shape[1]
    nm = 0 if mod is None else 2

    def body(*refs):
        a_ref = refs[0]
        m_refs = refs[1:1 + nm]
        b_ref, o_ref = refs[1 + nm:]

        @pl.when(pl.program_id(1) == 0)
        def _():
            o_ref[...] = jnp.zeros_like(o_ref)

        av = a_ref[...]
        if nm:
            av = av * (1.0 + m_refs[0][...]) + m_refs[1][...]
        o_ref[...] += lax.dot_general(av.astype(_MXU), b_ref[...], (((0,), (0,)), ((), ())),
                                      preferred_element_type=F32)

    mspecs, margs = [], []
    if nm:
        marr, l, ksc, ksh = mod
        mspecs, margs = [_sel(marr, l, ksc), _sel(marr, l, ksh)], [marr, marr]
    return pl.pallas_call(
        body, name=name, grid=(nbc // tb, s // ts),
        out_shape=jax.ShapeDtypeStruct((ka, nbc), F32),
        in_specs=[pl.BlockSpec((ts, ka), lambda j, t: (t, 0))] + mspecs
        + [pl.BlockSpec((ts, tb), lambda j, t: (t, j))],
        out_specs=pl.BlockSpec((ka, tb), lambda j, t: (0, j)),
        compiler_params=_cparams(2),
    )(a, *margs, b)


def _ada_fwd(c_all, w_ada, b_cols, name):
    cols = w_ada.shape[2]

    def body(c_ref, w_ref, b_ref, cond_ref, o_ref):
        cv = c_ref[...]
        cond = cv * _sigmoid(cv)

        @pl.when(pl.program_id(0) == 0)
        def _():
            cond_ref[...] = cond

        o_ref[...] = _dot(cond, w_ref[...]) + b_ref[...]

    return pl.pallas_call(
        body, name=name, grid=(DEPTH,),
        out_shape=[jax.ShapeDtypeStruct((NDEV, D), F32), jax.ShapeDtypeStruct((DEPTH, NDEV, cols), F32)],
        in_specs=[_full((NDEV, D)), pl.BlockSpec((None, D, cols), lambda l: (l, 0, 0)),
                  pl.BlockSpec((None, 1, cols), lambda l: (l, 0, 0))],
        out_specs=[_full((NDEV, D)), pl.BlockSpec((None, NDEV, cols), lambda l: (l, 0, 0))],
        compiler_params=_cparams(1),
    )(c_all, w_ada, b_cols)


def _ada_bwd(cond, dmod_cols, name):
    cols = dmod_cols.shape[2]

    def body(c_ref, d_ref, o_ref):
        o_ref[...] = _dot_tn(c_ref[...], d_ref[...])

    return pl.pallas_call(
        body, name=name, grid=(DEPTH,),
        out_shape=jax.ShapeDtypeStruct((DEPTH, D, cols), F32),
        in_specs=[_full((NDEV, D)), pl.BlockSpec((None, NDEV, cols), lambda l: (l, 0, 0))],
        out_specs=pl.BlockSpec((None, D, cols), lambda l: (l, 0, 0)),
        compiler_params=_cparams(1),
    )(cond, dmod_cols)


def _adam_math(w, g, m, v):
    m = ADAM_B1 * m + (1.0 - ADAM_B1) * g
    v = ADAM_B2 * v + (1.0 - ADAM_B2) * (g * g)
    m_hat = m / (1.0 - ADAM_B1 ** ADAM_STEP)
    v_hat = v / (1.0 - ADAM_B2 ** ADAM_STEP)
    delta = -ADAM_LR * (m_hat / (jnp.sqrt(v_hat) + ADAM_EPS) + ADAM_WD * w)
    return delta, m, v


def _adamw(parts, w, m, v, tr, name):
    r, c = w.shape
    nparts = parts.shape[0]

    def body(p_ref, w_ref, m_ref, v_ref, g_ref, d_ref, nm_ref, nv_ref):
        g = p_ref[0].astype(F32)
        for i in range(1, nparts):
            g = g + p_ref[i].astype(F32)
        delta, nm, nv = _adam_math(w_ref[...], g, m_ref[...], v_ref[...])
        g_ref[...] = g
        d_ref[...] = delta
        nm_ref[...] = nm
        nv_ref[...] = nv

    blk = pl.BlockSpec((tr, c), lambda i: (i, 0))
    return pl.pallas_call(
        body, name=name, grid=(r // tr,),
        out_shape=[jax.ShapeDtypeStruct((r, c), F32)] * 4,
        in_specs=[pl.BlockSpec((nparts, tr, c), lambda i: (0, i, 0)), blk, blk, blk],
        out_specs=[blk] * 4,
        compiler_params=_cparams(1),
    )(parts, w, m, v)


def _cmul(ar, ai, br, bi):
    return ar * br - ai * bi, ar * bi + ai * br


def _lam_bar(lr, li, log_dt):
    dt = jnp.exp(log_dt)
    mag = jnp.exp(lr * dt)
    return dt, mag * jnp.cos(li * dt), mag * jnp.sin(li * dt)


def _layer_spec(arr):
    tail = arr.shape[1:]
    return pl.BlockSpec((None,) + tail, lambda l: (l,) + (0,) * len(tail))


def _ssm_prep(lam_re, lam_im, log_dt, b_re, b_im, name):
    def body(lr_ref, li_ref, dt_ref, br_ref, bi_ref, pr_ref, pi_ref, or_ref, oi_ref):
        lr, li = lr_ref[...], li_ref[...]
        _, ar, ai = _lam_bar(lr, li, dt_ref[...])
        den = lr * lr + li * li
        fr, fi = _cmul(ar - 1.0, ai, lr / den, -li / den)
        obr, obi = _cmul(fr[:, None, :], fi[:, None, :], br_ref[...], bi_ref[...])
        or_ref[...] = obr
        oi_ref[...] = obi
        qr, qi = ar, ai
        for k in range(SUB):
            pr_ref[k] = qr
            pi_ref[k] = qi
            qr, qi = _cmul(qr, qi, ar, ai)

    nl, g, p = lam_re.shape
    h = b_re.shape[2]
    outs = [jax.ShapeDtypeStruct((nl, SUB, g, p), F32)] * 2 + [jax.ShapeDtypeStruct((nl, g, h, p), F32)] * 2
    args = (lam_re, lam_im, log_dt, b_re, b_im)
    return pl.pallas_call(
        body, name=name, grid=(nl,), out_shape=outs,
        in_specs=[_layer_spec(a) for a in args], out_specs=[_layer_spec(o) for o in outs],
        compiler_params=_cparams(1),
    )(*args)


def _ssm_param_bwd(lam_re, lam_im, log_dt, b_re, b_im, dlr, dli, dbr, dbi, name):
    def body(lr_ref, li_ref, dt_ref, br_ref, bi_ref, gar_ref, gai_ref, gbr_ref, gbi_ref,
             olr_ref, oli_ref, odt_ref, obr_ref, obi_ref):
        lr, li = lr_ref[...], li_ref[...]
        dt, ar, ai = _lam_bar(lr, li, dt_ref[...])
        den = lr * lr + li * li
        ir, ii = lr / den, -li / den
        fr, fi = _cmul(ar - 1.0, ai, ir, ii)
        br, bi = br_ref[...], bi_ref[...]
        gbr, gbi = gbr_ref[...], gbi_ref[...]
        obr, obi = _cmul(fr[:, None, :], -fi[:, None, :], gbr, gbi)
        obr_ref[...] = obr
        obi_ref[...] = obi
        pr, pi = _cmul(br, -bi, gbr, gbi)
        gfr, gfi = jnp.sum(pr, axis=1), jnp.sum(pi, axis=1)
        t_r, t_i = _cmul(gfr, gfi, ir, -ii)
        gar, gai = gar_ref[...] + t_r, gai_ref[...] + t_i
        qr, qi = _cmul(fr, fi, ir, ii)
        l1r, l1i = _cmul(gfr, gfi, qr, -qi)
        l2r, l2i = _cmul(gar, gai, dt * ar, -dt * ai)
        olr_ref[...] = l2r - l1r
        oli_ref[...] = l2i - l1i
        mr, mi = _cmul(lr, li, ar, ai)
        dr, _ = _cmul(gar, gai, mr, -mi)
        odt_ref[...] = jnp.sum(dr, axis=1, keepdims=True) * dt

    nl, g, p = lam_re.shape
    h = b_re.shape[2]
    outs = ([jax.ShapeDtypeStruct((nl, g, p), F32)] * 2 + [jax.ShapeDtypeStruct((nl, g, 1), F32)]
            + [jax.ShapeDtypeStruct((nl, g, h, p), F32)] * 2)
    args = (lam_re, lam_im, log_dt, b_re, b_im, dlr, dli, dbr, dbi)
    return pl.pallas_call(
        body, name=name, grid=(nl,), out_shape=outs,
        in_specs=[_layer_spec(a) for a in args], out_specs=[_layer_spec(o) for o in outs],
        compiler_params=_cparams(1),
    )(*args)


def _gelu(x):
    k = math.sqrt(2.0 / math.pi)
    return 0.5 * x * (1.0 + jnp.tanh(k * (x + 0.044715 * x * x * x)))


def _gelu_grad(x):
    k = math.sqrt(2.0 / math.pi)
    th = jnp.tanh(k * (x + 0.044715 * x * x * x))
    return 0.5 * (1.0 + th) + 0.5 * x * (1.0 - th * th) * k * (1.0 + 3 * 0.044715 * x * x)


def _pool_inv_count(chunk):
    t = lax.broadcasted_iota(jnp.int32, (T, GW), 0) + chunk * T + 1
    lane = lax.broadcasted_iota(jnp.int32, (T, GW), 1)
    win = jnp.where(lane < 64, 2, jnp.where(lane < 128, 4, jnp.where(lane < 192, 8, 16)))
    return 1.0 / jnp.minimum(t, win).astype(F32)


def _window_sums(v, bufs, base, step):
    sums = []
    for k, buf in enumerate(bufs):
        buf[base:base + T, :] = v
        v = v + buf[pl.ds(base + step * (1 << k), T), :]
        sums.append(v)
    lane = lax.broadcasted_iota(jnp.int32, (T, GW), 1)
    return jnp.where(lane < 64, sums[0], jnp.where(lane < 128, sums[1], jnp.where(lane < 192, sums[2], sums[3])))


def _phase_copies(buf, rot):
    n = HALO + T - SUB
    for p in range(1, SUB):
        rot[p - 1, 0:n, :] = buf[pl.ds(p, n), :]


def _tap(buf, rot, off):
    q, p = divmod(off, SUB)
    return buf[off:off + T, :] if p == 0 else rot[p - 1, SUB * q:SUB * q + T, :]


def _scan(x_scr, pw_ref, pwrev_ref, carry, reverse):
    row8 = lax.broadcasted_iota(jnp.int32, (T, 128), 0) % SUB
    nb = T // SUB
    for j in range(XH // 128):
        lo, hi = j * 128, XH + j * 128
        re, im = x_scr[:, lo:lo + 128], x_scr[:, hi:hi + 128]
        for d in (1, 2, 4):
            ar, ai = pw_ref[d - 1:d, lo:lo + 128], pw_ref[d - 1:d, hi:hi + 128]
            if reverse:
                ai = -ai
                keep = row8 < SUB - d
                sre, sim = pltpu.roll(re, T - d, 0), pltpu.roll(im, T - d, 0)
            else:
                keep = row8 >= d
                sre, sim = pltpu.roll(re, d, 0), pltpu.roll(im, d, 0)
            sre, sim = jnp.where(keep, sre, 0.0), jnp.where(keep, sim, 0.0)
            re, im = re + ar * sre - ai * sim, im + ar * sim + ai * sre
        cr, ci = carry[:, lo:lo + 128], carry[:, hi:hi + 128]
        if reverse:
            pr, pi = pwrev_ref[:, lo:lo + 128], -pwrev_ref[:, hi:hi + 128]
            order, edge = reversed(range(nb)), 0
        else:
            pr, pi = pw_ref[:, lo:lo + 128], pw_ref[:, hi:hi + 128]
            order, edge = range(nb), SUB - 1
        for b in order:
            rb, ib = re[SUB * b:SUB * b + SUB], im[SUB * b:SUB * b + SUB]
            crb, cib = jnp.broadcast_to(cr, (SUB, 128)), jnp.broadcast_to(ci, (SUB, 128))
            rb, ib = rb + pr * crb - pi * cib, ib + pr * cib + pi * crb
            x_scr[SUB * b:SUB * b + SUB, lo:lo + 128] = rb
            x_scr[SUB * b:SUB * b + SUB, hi:hi + 128] = ib
            cr, ci = rb[edge:edge + 1], ib[edge:edge + 1]


MIXER_PARAMS = ('scw', 'pblk', 'pscale', 'dww', 'dwb', 'clg', 'clb', 'bblk', 'cblk', 'pw', 'pwrev', 'ssd', 'wglu',
                'bglu')


def _mixer_fwd(z, mp, l, name):
    s = z.shape[0]
    nch = s // T
    params = [mp[k] for k in MIXER_PARAMS]

    def body(z_ref, scw_ref, pblk_ref, ps_ref, dww_ref, dwb_ref, clg_ref, clb_ref, bblk_ref, cblk_ref, pw_ref,
             pwrev_ref, ssd_ref, wglu_ref, bglu_ref, y_ref, xs_ref, aux_ref,
             qbuf, hbuf, pb1, pb2, pb4, pb8, rot, carry):
        i = pl.program_id(0)

        @pl.when(i == 0)
        def _():
            for ref in (qbuf, hbuf, pb1, pb2, pb4, pb8):
                ref[0:HALO, :] = jnp.zeros((HALO, GW), F32)
            carry[...] = jnp.zeros_like(carry)

        zh, zb, zc = z_ref[:, 0:GW], z_ref[:, GW:2 * GW], z_ref[:, 2 * GW:3 * GW]
        zp, zcv, zcg, zs = (z_ref[:, 3 * GW:4 * GW], z_ref[:, 4 * GW:5 * GW], z_ref[:, 5 * GW:6 * GW],
                            z_ref[:, 6 * GW:7 * GW])
        qbuf[HALO:HALO + T, :] = zc * zh
        conv = None
        for k in range(SC_TAPS):
            v = scw_ref[k:k + 1, :] * qbuf[pl.ds(HALO - (SC_TAPS - 1) + k, T), :]
            conv = v if conv is None else conv + v
        aux_ref[:, 0:GW] = conv
        y_ref[:, 0:GW] = (zb * conv).astype(y_ref.dtype)
        r = _window_sums(zp, (pb1, pb2, pb4, pb8), HALO, -1) * _pool_inv_count(i) - zp
        aux_ref[:, GW:2 * GW] = r
        y_ref[:, GW:2 * GW] = (_dot(r, pblk_ref[...]) * ps_ref[...]).astype(y_ref.dtype)
        hbuf[HALO:HALO + T, :] = zcv * _sigmoid(zcg)
        _phase_copies(hbuf, rot)
        hc = None
        for k in range(CF_TAPS):
            v = dww_ref[k:k + 1, :] * _tap(hbuf, rot, HALO - (CF_TAPS - 1) + k)
            hc = v if hc is None else hc + v
        hc = hc + dwb_ref[...]
        aux_ref[:, 2 * GW:3 * GW] = hc
        mu = jnp.mean(hc, axis=-1, keepdims=True)
        xc = hc - mu
        var = jnp.mean(xc * xc, axis=-1, keepdims=True)
        ln = xc * lax.rsqrt(var + LN_EPS) * clg_ref[...] + clb_ref[...]
        y_ref[:, 2 * GW:3 * GW] = (ln * _sigmoid(ln)).astype(y_ref.dtype)
        xs_ref[...] = _dot(zs, bblk_ref[...])
        _scan(xs_ref, pw_ref, pwrev_ref, carry[...], reverse=False)
        carry[...] = xs_ref[T - 1:T, :]
        yy = _dot(xs_ref[...], cblk_ref[...]) + ssd_ref[...] * zs
        aux_ref[:, 3 * GW:4 * GW] = yy
        yg = _gelu(yy)
        v = _dot(yg, wglu_ref[...]) + bglu_ref[...]
        y_ref[:, 3 * GW:4 * GW] = (yg * _sigmoid(v)).astype(y_ref.dtype)
        for ref in (qbuf, hbuf, pb1, pb2, pb4, pb8):
            ref[0:HALO, :] = ref[T:T + HALO, :]

    fwd = lambda i: (i, 0)
    return pl.pallas_call(
        body, name=name, grid=(nch,),
        out_shape=[jax.ShapeDtypeStruct((s, D), _MXU), jax.ShapeDtypeStruct((s, XW), F32),
                   jax.ShapeDtypeStruct((s, 4 * GW), F32)],
        in_specs=[pl.BlockSpec((T, INW), fwd)] + [_sel(p, l) for p in params],
        out_specs=[pl.BlockSpec((T, D), fwd), pl.BlockSpec((T, XW), fwd), pl.BlockSpec((T, 4 * GW), fwd)],
        scratch_shapes=[pltpu.VMEM((HALO + T, GW), F32)] * 6
        + [pltpu.VMEM((SUB - 1, HALO + T, GW), F32), pltpu.VMEM((1, XW), F32)],
        compiler_params=_cparams(1),
    )(z, *params)


def _mixer_bwd(z, dy, xs, aux, mp, l, name):
    s = z.shape[0]
    nch = s // T
    params = [mp[k] for k in MIXER_PARAMS]

    def body(z_ref, zprev_ref, dy_ref, xs_ref, xprev_ref, aux_ref, scw_ref, pblk_ref, ps_ref, dww_ref, dwb_ref,
             clg_ref, clb_ref, bblk_ref, cblk_ref, pw_ref, pwrev_ref, ssd_ref, wglu_ref, bglu_ref,
             dz_ref, vec_ref, small_ref, dpblk_ref, dwglu_ref, dbblk_ref, dcblk_ref, dlam_ref,
             qbuf, hbuf, dcbuf, dhbuf, eb1, eb2, eb4, eb8, rot_h, rot_d, acarry, a_scr):
        i = pl.program_id(0)
        chunk = nch - 1 - i

        @pl.when(i == 0)
        def _():
            for ref in (vec_ref, small_ref, dpblk_ref, dwglu_ref, dbblk_ref, dcblk_ref, dlam_ref, acarry):
                ref[...] = jnp.zeros_like(ref)
            for ref in (dcbuf, dhbuf, eb1, eb2, eb4, eb8):
                ref[T:T + HALO, :] = jnp.zeros((HALO, GW), F32)

        zh, zb, zc = z_ref[:, 0:GW], z_ref[:, GW:2 * GW], z_ref[:, 2 * GW:3 * GW]
        zcv, zcg, zs = z_ref[:, 4 * GW:5 * GW], z_ref[:, 5 * GW:6 * GW], z_ref[:, 6 * GW:7 * GW]
        conv, r, hc, yy = (aux_ref[:, 0:GW], aux_ref[:, GW:2 * GW], aux_ref[:, 2 * GW:3 * GW],
                           aux_ref[:, 3 * GW:4 * GW])
        has_prev = (chunk > 0).astype(F32)
        qbuf[0:HALO, :] = zprev_ref[:, 2 * GW:3 * GW] * zprev_ref[:, 0:GW] * has_prev
        hbuf[0:HALO, :] = zprev_ref[:, 4 * GW:5 * GW] * _sigmoid(zprev_ref[:, 5 * GW:6 * GW]) * has_prev

        dya = dy_ref[:, 0:GW]
        qbuf[HALO:HALO + T, :] = zc * zh
        dconv = dya * zb
        dcbuf[0:T, :] = dconv
        dq = None
        for k in range(SC_TAPS):
            small_ref[k:k + 1, :] += _colsum(dconv * qbuf[pl.ds(HALO - (SC_TAPS - 1) + k, T), :])
            v = scw_ref[k:k + 1, :] * dcbuf[pl.ds(SC_TAPS - 1 - k, T), :]
            dq = v if dq is None else dq + v
        dzh, dzb, dzc = dq * zc, dya * conv, dq * zh

        dyb = dy_ref[:, GW:2 * GW]
        o = _dot(r, pblk_ref[...])
        small_ref[3:4, :] += _colsum(dyb * o)
        do = dyb * ps_ref[...]
        dpblk_ref[...] += _dot_tn(r, do)
        dr = _dot_nt(do, pblk_ref[...])
        dzp = _window_sums(dr * _pool_inv_count(chunk), (eb1, eb2, eb4, eb8), 0, 1) - dr

        dyc = dy_ref[:, 2 * GW:3 * GW]
        sgc = _sigmoid(zcg)
        hbuf[HALO:HALO + T, :] = zcv * sgc
        mu = jnp.mean(hc, axis=-1, keepdims=True)
        xc = hc - mu
        var = jnp.mean(xc * xc, axis=-1, keepdims=True)
        rstd = lax.rsqrt(var + LN_EPS)
        nrm = xc * rstd
        ln = nrm * clg_ref[...] + clb_ref[...]
        sl = _sigmoid(ln)
        dln = dyc * (sl * (1.0 + ln * (1.0 - sl)))
        small_ref[5:6, :] += _colsum(dln * nrm)
        small_ref[6:7, :] += _colsum(dln)
        dn = dln * clg_ref[...]
        dhc = rstd * (dn - jnp.mean(dn, axis=-1, keepdims=True) - nrm * jnp.mean(dn * nrm, axis=-1, keepdims=True))
        small_ref[4:5, :] += _colsum(dhc)
        dhbuf[0:T, :] = dhc
        _phase_copies(hbuf, rot_h)
        _phase_copies(dhbuf, rot_d)
        dhg = None
        for k in range(CF_TAPS):
            small_ref[16 + k:17 + k, :] += _colsum(dhc * _tap(hbuf, rot_h, HALO - (CF_TAPS - 1) + k))
            v = dww_ref[k:k + 1, :] * _tap(dhbuf, rot_d, CF_TAPS - 1 - k)
            dhg = v if dhg is None else dhg + v
        dzcv = dhg * sgc
        dzcg = dhg * zcv * sgc * (1.0 - sgc)

        dyd = dy_ref[:, 3 * GW:4 * GW]
        yg = _gelu(yy)
        v = _dot(yg, wglu_ref[...]) + bglu_ref[...]
        sg = _sigmoid(v)
        dv = dyd * yg * sg * (1.0 - sg)
        small_ref[8:9, :] += _colsum(dv)
        dwglu_ref[...] += _dot_tn(yg, dv)
        g = (dyd * sg + _dot_nt(dv, wglu_ref[...])) * _gelu_grad(yy)
        small_ref[7:8, :] += _colsum(g * zs)
        dcblk_ref[...] += _dot_tn(xs_ref[...], g)
        a_scr[...] = _dot_nt(g, cblk_ref[...])
        _scan(a_scr, pw_ref, pwrev_ref, acarry[...], reverse=True)
        acarry[...] = a_scr[0:1, :]
        dzs = _dot_nt(a_scr[...], bblk_ref[...]) + ssd_ref[...] * g
        dbblk_ref[...] += _dot_tn(zs, a_scr[...])
        row = lax.broadcasted_iota(jnp.int32, (T, 128), 0)
        for j in range(XH // 128):
            lo, hi = j * 128, XH + j * 128
            first_r = xprev_ref[SUB - 1:SUB, lo:lo + 128] * has_prev
            first_i = xprev_ref[SUB - 1:SUB, hi:hi + 128] * has_prev
            pr = jnp.where(row == 0, first_r, pltpu.roll(xs_ref[:, lo:lo + 128], 1, 0))
            pi = jnp.where(row == 0, first_i, pltpu.roll(xs_ref[:, hi:hi + 128], 1, 0))
            ar, ai = a_scr[:, lo:lo + 128], a_scr[:, hi:hi + 128]
            dlam_ref[0:1, lo:lo + 128] += _colsum(ar * pr + ai * pi)
            dlam_ref[0:1, hi:hi + 128] += _colsum(ai * pr - ar * pi)

        parts = (dzh, dzb, dzc, dzp, dzcv, dzcg, dzs)
        for k, part in enumerate(parts):
            dz_ref[:, k * GW:(k + 1) * GW] = part.astype(dz_ref.dtype)
            vec_ref[0:1, k * GW:(k + 1) * GW] += _colsum(part)
        for ref in (dcbuf, dhbuf, eb1, eb2, eb4, eb8):
            ref[T:T + HALO, :] = ref[0:HALO, :]

    rev = lambda i: (nch - 1 - i, 0)

    def before(rows):
        return lambda i: (jnp.maximum((nch - 1 - i) * (T // rows) - 1, 0), 0)

    outs = [((s, INW), _MXU), ((8, INW), F32), ((48, GW), F32), ((GW, GW), F32), ((GW, GW), F32),
            ((GW, XW), F32), ((XW, GW), F32), ((8, XW), F32)]
    return pl.pallas_call(
        body, name=name, grid=(nch,),
        out_shape=[jax.ShapeDtypeStruct(sh, dt) for sh, dt in outs],
        in_specs=[pl.BlockSpec((T, INW), rev), pl.BlockSpec((HALO, INW), before(HALO)), pl.BlockSpec((T, D), rev),
                  pl.BlockSpec((T, XW), rev), pl.BlockSpec((SUB, XW), before(SUB)), pl.BlockSpec((T, 4 * GW), rev)]
        + [_sel(p, l) for p in params],
        out_specs=[pl.BlockSpec((T, INW), rev)] + [_full(sh) for sh, _ in outs[1:]],
        scratch_shapes=[pltpu.VMEM((HALO + T, GW), F32)] * 8
        + [pltpu.VMEM((SUB - 1, HALO + T, GW), F32)] * 2 + [pltpu.VMEM((1, XW), F32), pltpu.VMEM((T, XW), F32)],
        compiler_params=_cparams(1),
    )(z, z, dy, xs, xs, aux, *params)


def _blockdiag(m):
    *lead, g, h, p = m.shape
    eye = jnp.eye(g, dtype=m.dtype)
    return (m[..., :, :, None, :] * eye[:, None, :, None]).reshape(*lead, g * h, g * p)


def _blockdiag_extract(f, h, p):
    *lead, r, _ = f.shape
    g = r // h
    eye = jnp.eye(g, dtype=f.dtype)
    return jnp.sum(f.reshape(*lead, g, h, g, p) * eye[:, None, :, None], axis=-2)


def _t(a):
    return jnp.swapaxes(a, -1, -2)


PACK_ROWS = 512


def _pack(arrs):
    flat = jnp.concatenate([a.reshape(-1).astype(F32) for a in arrs])
    quantum = PACK_ROWS * 128
    padded = -(-flat.shape[0] // quantum) * quantum
    return jnp.pad(flat, (0, padded - flat.shape[0])).reshape(-1, 128)


def _unpack(buf, shapes, lead=()):
    flat = buf.reshape(lead + (-1,))
    out, off = [], 0
    for shp in shapes:
        n = math.prod(shp)
        out.append(flat[..., off:off + n].reshape(lead + tuple(shp)))
        off += n
    return out


def _cols_to_full(g):
    n, l, r, c = g.shape
    return g.transpose(1, 2, 0, 3).reshape(l, r, n * c)


def _rows_to_full(g):
    n, l, r, c = g.shape
    return g.transpose(1, 0, 2, 3).reshape(l, n * r, c)


def _full_to_cols(w):
    l, r, c = w.shape
    return w.reshape(l, r, NDEV, c // NDEV).transpose(2, 0, 1, 3)


def _full_to_rows(w):
    l, r, c = w.shape
    return w.reshape(l, NDEV, r // NDEV, c).transpose(1, 0, 2, 3)


REPLICATED = ['b_ada', 'b_in', 'pool_w', 'pool_scale', 'cf_dw_b', 'cf_ln_g', 'cf_ln_b', 'ssm_lam_re', 'ssm_lam_im',
              'ssm_log_dt', 'ssm_b_re', 'ssm_b_im', 'ssm_c_re', 'ssm_c_im', 'ssm_d', 'ssm_b_glu', 'ln1_g', 'ln1_b',
              'ln2_g', 'ln2_b']
SMALL_SHARDED = ['sc_w', 'cf_dw_w', 'ssm_w_glu']
COL_SHARDED = ['w_in', 'w_gate', 'w_up']
ROW_SHARDED = ['w_o', 'w_down']
WEIGHTS = ['w_ada', 'b_ada', 'w_in', 'b_in', 'sc_w', 'pool_w', 'pool_scale', 'cf_dw_w', 'cf_dw_b', 'cf_ln_g',
           'cf_ln_b', 'ssm_lam_re', 'ssm_lam_im', 'ssm_log_dt', 'ssm_b_re', 'ssm_b_im', 'ssm_c_re', 'ssm_c_im',
           'ssm_d', 'ssm_w_glu', 'ssm_b_glu', 'w_o', 'ln1_g', 'ln1_b', 'w_gate', 'w_up', 'w_down', 'ln2_g', 'ln2_b']
SH1, SC1, G1, SH2, SC2, G2 = range(6)


def _train_step(p):
    xi, yi, ci = _me()
    me = _flat((xi, yi, ci))
    s = p['x'].shape[1]
    xs = p['x'].reshape(s, D)
    target = p['loss_target'].reshape(s, D)
    vec3 = lambda a: a.reshape(DEPTH, 1, -1)

    small_shapes = [p[n].shape for n in SMALL_SHARDED] + [p['c'].shape]
    gathered = _allgather(
        [_pack([p[n] for n in SMALL_SHARDED] + [p['c']])] + [p[n].astype(_MXU) for n in COL_SHARDED + ROW_SHARDED],
        "gather_weights")
    scw_g, dww_g, wglu_g, c_g = _unpack(gathered[0], small_shapes, lead=(NDEV,))
    full = {'sc_w': _cols_to_full(scw_g), 'cf_dw_w': _cols_to_full(dww_g), 'ssm_w_glu': _rows_to_full(wglu_g)}
    for n, g in zip(COL_SHARDED, gathered[1:4]):
        full[n] = _cols_to_full(g)
    for n, g in zip(ROW_SHARDED, gathered[4:6]):
        full[n] = _rows_to_full(g)
    c_all = c_g.reshape(NDEV, D)

    ncol = p['w_ada'].shape[2]
    b_cols = lax.dynamic_slice_in_dim(p['b_ada'], me * ncol, ncol, axis=1).reshape(DEPTH, 1, ncol)
    cond, mod_cols = _ada_fwd(c_all, p['w_ada'], b_cols, "ada_fwd")
    (mod_x,) = _alltoall([mod_cols.transpose(1, 0, 2)], "mod_exchange")
    mod = mod_x.transpose(1, 0, 2).reshape(DEPTH, 6, 1, D)

    ssm_in = (p['ssm_lam_re'], p['ssm_lam_im'], p['ssm_log_dt'].reshape(DEPTH, NGRP, 1),
              _t(p['ssm_b_re']), _t(p['ssm_b_im']))
    pwr, pwi, bbr, bbi = _ssm_prep(*ssm_in, "ssm_prep")
    pw = jnp.concatenate([pwr.reshape(DEPTH, SUB, XH), pwi.reshape(DEPTH, SUB, XH)], axis=2)
    mp = dict(
        scw=full['sc_w'], pblk=_blockdiag(p['pool_w']).astype(_MXU), pscale=vec3(p['pool_scale']),
        dww=full['cf_dw_w'], dwb=vec3(p['cf_dw_b']), clg=vec3(p['cf_ln_g']), clb=vec3(p['cf_ln_b']),
        bblk=jnp.concatenate([_blockdiag(bbr), _blockdiag(bbi)], axis=2).astype(_MXU),
        cblk=jnp.concatenate([_blockdiag(_t(p['ssm_c_re'])), -_blockdiag(_t(p['ssm_c_im']))], axis=1).astype(_MXU),
        pw=pw, pwrev=pw[:, ::-1], ssd=vec3(p['ssm_d']), wglu=full['ssm_w_glu'].astype(_MXU),
        bglu=vec3(p['ssm_b_glu']))
    b_in, ln1_g, ln1_b, ln2_g, ln2_b = (vec3(p[n]) for n in ('b_in', 'ln1_g', 'ln1_b', 'ln2_g', 'ln2_b'))

    saved = []
    for l in range(DEPTH):
        (z,) = _modulated_matmul(xs, mod, l, SC1, SH1, [full['w_in']], b_in, TM, [F32], lambda a: (a,), "in_proj")
        ycat, states, aux = _mixer_fwd(z, mp, l, "mixer_fwd")
        y, u1, x1 = _proj_residual_ln(ycat, full['w_o'], l, xs, mod, G1, ln1_g, ln1_b, "o_proj_ln")
        gate, up, act = _modulated_matmul(x1, mod, l, SC2, SH2, [full['w_gate'], full['w_up']], None, TM_WIDE,
                                          [_MXU, _MXU, _MXU], _swiglu, "gate_up")
        f, u2, x2 = _proj_residual_ln(act, full['w_down'], l, x1, mod, G2, ln2_g, ln2_b, "down_proj_ln")
        saved.append(dict(x=xs, z=z, ycat=ycat, states=states, aux=aux, y=y, u1=u1, x1=x1, gate=gate, up=up,
                          act=act, f=f, u2=u2))
        xs = x2

    dx, sq = _loss_head(xs, target, "loss_head")
    loss = lax.psum(sq[0, 0] * (0.5 / D), AXES)

    grads = {n: [None] * DEPTH for n in WEIGHTS}
    dmod = [None] * DEPTH
    mixer_out = [None] * DEPTH
    for l in reversed(range(DEPTH)):
        sv = saved[l]
        du2, df, acc2 = _ln_bwd(dx, sv['u2'], sv['f'], ln2_g, l, mod, G2, "ln2_bwd")
        dgate, dup = _swiglu_bwd(df, full['w_down'], l, sv['gate'], sv['up'], "swiglu_bwd")
        grads['w_down'][l] = _weight_grad(sv['act'], df, 512, TM, "dw_down")
        dx1, acc_h2 = _input_grad([dgate, dup], [full['w_gate'], full['w_up']], l, du2, sv['x1'], mod, SC2,
                                  TM_WIDE, "dh2")
        grads['w_gate'][l] = _weight_grad(sv['x1'], dgate, DFF // 2, TM, "dw_gate", mod=(mod, l, SC2, SH2))
        grads['w_up'][l] = _weight_grad(sv['x1'], dup, DFF // 2, TM, "dw_up", mod=(mod, l, SC2, SH2))
        du1, dyb, acc1 = _ln_bwd(dx1, sv['u1'], sv['y'], ln1_g, l, mod, G1, "ln1_bwd")
        dycat = _matmul_nt(dyb, full['w_o'], l, "dycat")
        grads['w_o'][l] = _weight_grad(sv['ycat'], dyb, D, TM, "dw_o")
        dz, *mixer_out[l] = _mixer_bwd(sv['z'], dycat, sv['states'], sv['aux'], mp, l, "mixer_bwd")
        dx, acc_h1 = _input_grad([dz], [full['w_in']], l, du1, sv['x'], mod, SC1, TM, "dh1")
        grads['w_in'][l] = _weight_grad(sv['x'], dz, INW // 2, TM, "dw_in", mod=(mod, l, SC1, SH1))
        dmod[l] = jnp.concatenate([acc_h1[0], acc_h1[1], acc1[2], acc_h2[0], acc_h2[1], acc2[2]])
        grads['ln2_g'][l], grads['ln2_b'][l] = acc2[0], acc2[1]
        grads['ln1_g'][l], grads['ln1_b'][l] = acc1[0], acc1[1]
    partial = {n: jnp.stack(grads[n]) for n in COL_SHARDED + ROW_SHARDED + ['ln1_g', 'ln1_b', 'ln2_g', 'ln2_b']}
    partial['b_ada'] = jnp.stack(dmod)
    vec, small, dpblk, dwglu, dbblk, dcblk, dlam = (jnp.stack(t) for t in zip(*mixer_out))
    partial.update(
        b_in=vec[:, 0], sc_w=small[:, 0:SC_TAPS], pool_scale=small[:, 3], cf_dw_b=small[:, 4], cf_ln_g=small[:, 5],
        cf_ln_b=small[:, 6], ssm_d=small[:, 7], ssm_b_glu=small[:, 8], cf_dw_w=small[:, 16:16 + CF_TAPS],
        pool_w=_blockdiag_extract(dpblk, 64, 64), ssm_w_glu=dwglu,
        ssm_c_re=_t(_blockdiag_extract(dcblk[:, :XH], NSTATE, NCH)),
        ssm_c_im=-_t(_blockdiag_extract(dcblk[:, XH:], NSTATE, NCH)))
    dlr, dli, dlog, dbr, dbi = _ssm_param_bwd(
        *ssm_in, dlam[:, 0, :XH].reshape(DEPTH, NGRP, NSTATE), dlam[:, 0, XH:].reshape(DEPTH, NGRP, NSTATE),
        _blockdiag_extract(dbblk[:, :, :XH], NCH, NSTATE), _blockdiag_extract(dbblk[:, :, XH:], NCH, NSTATE),
        "ssm_param_bwd")
    partial.update(ssm_lam_re=dlr, ssm_lam_im=dli, ssm_log_dt=dlog.reshape(DEPTH, NGRP), ssm_b_re=_t(dbr),
                   ssm_b_im=_t(dbi))

    out = {'loss': loss, 'grad_x': dx.reshape(1, s, D)}

    def emit(n, g, delta, nm, nv):
        shp = p[n].shape
        out['grad_' + n], out['delta_' + n] = g.reshape(shp), delta.reshape(shp)
        out['new_m_' + n], out['new_v_' + n] = nm.reshape(shp), nv.reshape(shp)

    names_a = REPLICATED + SMALL_SHARDED
    shapes_a = [partial[n].shape for n in names_a]
    zeros_like_full = lambda n: jnp.zeros(partial[n].shape, F32)
    (parts_a,) = _allgather([_pack([partial[n] for n in names_a])], "gather_small_grads")
    packs = [_pack([p[pre + n] for n in REPLICATED] + [zeros_like_full(n) for n in SMALL_SHARDED])
             for pre in ('', 'm_', 'v_')]
    res_a = _adamw(parts_a, *packs, PACK_ROWS, "adamw_small")
    res_a = [_unpack(r, shapes_a) for r in res_a]
    for k, n in enumerate(REPLICATED):
        emit(n, *[r[k] for r in res_a])
    reduced = dict(zip(names_a, res_a[0]))

    blocks = []
    for n in SMALL_SHARDED:
        axis = 1 if n == 'ssm_w_glu' else 2
        width = p[n].shape[axis]
        blocks.append(lax.dynamic_slice_in_dim(reduced[n], me * width, width, axis=axis))
    shapes_b = [b.shape for b in blocks]
    packs = [_pack([p[pre + n] for n in SMALL_SHARDED]) for pre in ('', 'm_', 'v_')]
    res_b = _adamw(_pack(blocks)[None], *packs, PACK_ROWS, "adamw_small_sharded")
    res_b = [_unpack(r, shapes_b) for r in res_b]
    for k, n in enumerate(SMALL_SHARDED):
        emit(n, *[r[k] for r in res_b])

    dmod_all = _unpack(parts_a, shapes_a, lead=(NDEV,))[0]
    dmod_cols = lax.dynamic_slice_in_dim(dmod_all, me * ncol, ncol, axis=2).transpose(1, 0, 2)
    g_ada = _ada_bwd(cond, dmod_cols, "ada_bwd")
    flat2 = lambda a: a.reshape(-1, a.shape[-1])
    emit('w_ada', *_adamw(flat2(g_ada)[None], flat2(p['w_ada']), flat2(p['m_w_ada']), flat2(p['v_w_ada']),
                          512, "adamw_w_ada"))

    big = COL_SHARDED + ROW_SHARDED
    sends = [_full_to_cols(partial[n]).astype(_MXU) for n in COL_SHARDED]
    sends += [_full_to_rows(partial[n]).astype(_MXU) for n in ROW_SHARDED]
    recvs = _alltoall(sends, "exchange_grads")
    for n, r in zip(big, recvs):
        parts = r.reshape(NDEV, -1, r.shape[-1])
        tr = 512 if parts.shape[1] % 512 == 0 else 352
        emit(n, *_adamw(parts, flat2(p[n]), flat2(p['m_' + n]), flat2(p['v_' + n]), tr, "adamw_" + n))
    return out


def kernel(x, c, w_ada, b_ada, w_in, b_in, sc_w, pool_w, pool_scale, cf_dw_w, cf_dw_b, cf_ln_g, cf_ln_b, ssm_lam_re, ssm_lam_im, ssm_log_dt, ssm_b_re, ssm_b_im, ssm_c_re, ssm_c_im, ssm_d, ssm_w_glu, ssm_b_glu, w_o, ln1_g, ln1_b, w_gate, w_up, w_down, ln2_g, ln2_b, loss_target, m_w_ada, m_b_ada, m_w_in, m_b_in, m_sc_w, m_pool_w, m_pool_scale, m_cf_dw_w, m_cf_dw_b, m_cf_ln_g, m_cf_ln_b, m_ssm_lam_re, m_ssm_lam_im, m_ssm_log_dt, m_ssm_b_re, m_ssm_b_im, m_ssm_c_re, m_ssm_c_im, m_ssm_d, m_ssm_w_glu, m_ssm_b_glu, m_w_o, m_ln1_g, m_ln1_b, m_w_gate, m_w_up, m_w_down, m_ln2_g, m_ln2_b, v_w_ada, v_b_ada, v_w_in, v_b_in, v_sc_w, v_pool_w, v_pool_scale, v_cf_dw_w, v_cf_dw_b, v_cf_ln_g, v_cf_ln_b, v_ssm_lam_re, v_ssm_lam_im, v_ssm_log_dt, v_ssm_b_re, v_ssm_b_im, v_ssm_c_re, v_ssm_c_im, v_ssm_d, v_ssm_w_glu, v_ssm_b_glu, v_w_o, v_ln1_g, v_ln1_b, v_w_gate, v_w_up, v_w_down, v_ln2_g, v_ln2_b):
    out = _train_step(dict(locals()))
    return (out['loss'], out['grad_x'], *[out[pre + n] for pre in ('grad_', 'delta_', 'new_m_', 'new_v_')
                                          for n in WEIGHTS])
```

```python
import math

import jax
import jax.numpy as jnp
from jax import lax
from jax.experimental import pallas as pl
from jax.experimental.pallas import tpu as pltpu

F32 = jnp.float32
_MXU = jnp.bfloat16

D = 1024
GW = 256
INW = 7 * GW
DFF = 2816
DEPTH = 4
NDEV = 8
SC_TAPS = 3
CF_TAPS = 31
NGRP = 16
NCH = 16
NSTATE = 64
XH = NGRP * NSTATE
XW = 2 * XH
ALPHA = (2 * DEPTH) ** 0.25
LN_EPS = 1e-5
T = 128
SUB = 8
HALO = 32
TM = 512
TM_WIDE = 256
VMEM_LIMIT = 56 * 1024 * 1024

ADAM_LR = 0.001
ADAM_B1 = 0.9
ADAM_B2 = 0.999
ADAM_EPS = 1e-08
ADAM_WD = 0.01
ADAM_STEP = 10

MESH = pl.DeviceIdType.MESH
ANY = pl.BlockSpec(memory_space=pl.ANY)
AXES = ("x", "y", "c")


def _full(shape):
    nd = len(shape)
    return pl.BlockSpec(shape, lambda *_: (0,) * nd)


def _sel(arr, *idx):
    tail = arr.shape[len(idx):]
    return pl.BlockSpec((None,) * len(idx) + tail, lambda *_: idx + (0,) * len(tail))


def _dot(a, b):
    return jnp.dot(a.astype(_MXU), b.astype(_MXU), preferred_element_type=F32)


def _dot_nt(a, b):
    return lax.dot_general(a.astype(_MXU), b.astype(_MXU), (((1,), (1,)), ((), ())), preferred_element_type=F32)


def _dot_tn(a, b):
    return lax.dot_general(a.astype(_MXU), b.astype(_MXU), (((0,), (0,)), ((), ())), preferred_element_type=F32)


def _sigmoid(x):
    return 1.0 / (1.0 + jnp.exp(-x))


def _colsum(x):
    return jnp.sum(x, axis=0, keepdims=True)


def _me():
    return lax.axis_index("x"), lax.axis_index("y"), lax.axis_index("c")


def _flat(p):
    return 4 * p[0] + 2 * p[1] + p[2]


def _allgather(arrays, name):
    n = len(arrays)

    def body(*refs):
        ins, outs = refs[:n], refs[n:2 * n]
        send_sems, recv_sems, local_sems = refs[2 * n:]
        x, y, c = _me()
        me, sibling = (x, y, c), (x, y, 1 - c)
        chips = [(1 - x, y), (x, 1 - y), (1 - x, 1 - y)]

        def copy(a, k, block, to, src=None):
            dst = outs[a].at[_flat(block)]
            return pltpu.make_async_remote_copy(
                src_ref=dst if src is None else src, dst_ref=dst,
                send_sem=send_sems.at[a, k], recv_sem=recv_sems.at[a, k],
                device_id=to, device_id_type=MESH)

        started = []
        for a in range(n):
            mine = pltpu.make_async_copy(ins[a], outs[a].at[_flat(me)], local_sems.at[a])
            mine.start()
            started.append(mine)
        first = []
        for a in range(n):
            first.append(copy(a, 0, me, sibling, src=ins[a]))
            first += [copy(a, 1 + j, me, (*chip, c), src=ins[a]) for j, chip in enumerate(chips)]
        for cp in first:
            cp.start()
        passed = []
        for a in range(n):
            for j, chip in enumerate(chips):
                copy(a, 1 + j, (*chip, c), me).wait_recv()
                fwd = copy(a, 4 + j, (*chip, c), sibling)
                fwd.start()
                passed.append(fwd)
        for a in range(n):
            copy(a, 0, sibling, me).wait_recv()
            for j, chip in enumerate(chips):
                copy(a, 4 + j, (*chip, 1 - c), me).wait_recv()
        for cp in first + passed:
            cp.wait_send()
        for cp in started:
            cp.wait()

    return pl.pallas_call(
        body, name=name,
        out_shape=[jax.ShapeDtypeStruct((NDEV,) + a.shape, a.dtype) for a in arrays],
        in_specs=[ANY] * n, out_specs=[ANY] * n,
        scratch_shapes=[pltpu.SemaphoreType.DMA((n, 7)), pltpu.SemaphoreType.DMA((n, 7)),
                        pltpu.SemaphoreType.DMA((n,))],
    )(*arrays)


GATHER, EXCHANGE = "gather", "exchange"


def _direct_copies(kind, ins, outs, send_sems, recv_sems, local_sems, start):
    x, y, c = _me()
    mine = _flat((x, y, c))
    for a in range(len(ins)):
        own = ins[a] if kind == GATHER else ins[a].at[mine]
        loc = pltpu.make_async_copy(own, outs[a].at[mine], local_sems.at[a])
        if start:
            loc.start()
        for k in range(1, NDEV):
            peer = (lax.rem(x + ((k >> 2) & 1), 2), lax.rem(y + ((k >> 1) & 1), 2), lax.rem(c + (k & 1), 2))
            cp = pltpu.make_async_remote_copy(
                src_ref=ins[a] if kind == GATHER else ins[a].at[_flat(peer)], dst_ref=outs[a].at[mine],
                send_sem=send_sems.at[a, k - 1], recv_sem=recv_sems.at[a, k - 1],
                device_id=peer, device_id_type=MESH)
            if start:
                cp.start()
            else:
                cp.wait_send()
                cp.wait_recv()
        if not start:
            loc.wait()


def _comm_out_shapes(comm):
    kind, arrays = comm
    return [jax.ShapeDtypeStruct(((NDEV,) + a.shape) if kind == GATHER else a.shape, a.dtype) for a in arrays]


def _call(body, *, name, grid, in_specs, out_specs, out_shape, args, scratch=(), comm=None):
    params = pltpu.CompilerParams(dimension_semantics=("arbitrary",) * len(grid), vmem_limit_bytes=VMEM_LIMIT)
    if comm is None:
        return pl.pallas_call(body, name=name, grid=grid, in_specs=in_specs, out_specs=out_specs,
                              out_shape=out_shape, scratch_shapes=list(scratch), compiler_params=params)(*args)
    kind, arrays = comm
    n, n_in, n_out, n_scr = len(arrays), len(in_specs), len(out_specs), len(scratch)

    def wrapped(*refs):
        ins, cin = refs[:n_in], refs[n_in:n_in + n]
        outs, cout = refs[n_in + n:n_in + n + n_out], refs[n_in + n + n_out:n_in + 2 * n + n_out]
        scr = refs[n_in + 2 * n + n_out:]
        user_scr, sems = scr[:n_scr], scr[n_scr:]
        ids = [pl.program_id(k) for k in range(len(grid))]
        first, last = ids[0] == 0, ids[0] == grid[0] - 1
        for k in range(1, len(grid)):
            first, last = first & (ids[k] == 0), last & (ids[k] == grid[k] - 1)

        @pl.when(first)
        def _():
            _direct_copies(kind, cin, cout, *sems, start=True)

        body(*ins, *outs, *user_scr)

        @pl.when(last)
        def _():
            _direct_copies(kind, cin, cout, *sems, start=False)

    sems = [pltpu.SemaphoreType.DMA((n, 7)), pltpu.SemaphoreType.DMA((n, 7)), pltpu.SemaphoreType.DMA((n,))]
    res = pl.pallas_call(
        wrapped, name=name, grid=grid, in_specs=list(in_specs) + [ANY] * n, out_specs=list(out_specs) + [ANY] * n,
        out_shape=list(out_shape) + _comm_out_shapes(comm), scratch_shapes=list(scratch) + sems,
        compiler_params=params)(*args, *arrays)
    return res


def _alltoall(arrays, name):
    n = len(arrays)

    def body(*refs):
        _direct_copies(EXCHANGE, refs[:n], refs[n:2 * n], *refs[2 * n:], start=True)
        _direct_copies(EXCHANGE, refs[:n], refs[n:2 * n], *refs[2 * n:], start=False)

    return pl.pallas_call(
        body, name=name,
        out_shape=[jax.ShapeDtypeStruct(a.shape, a.dtype) for a in arrays],
        in_specs=[ANY] * n, out_specs=[ANY] * n,
        scratch_shapes=[pltpu.SemaphoreType.DMA((n, 7)), pltpu.SemaphoreType.DMA((n, 7)),
                        pltpu.SemaphoreType.DMA((n,))],
    )(*arrays)


def _modulated_matmul(x, mod, l, ksc, ksh, wts, bias, tm, out_dtypes, epilogue, name, comm=None):
    s, k = x.shape
    nw = len(wts)
    n = wts[0].shape[0]
    nb = 0 if bias is None else 1

    def body(*refs):
        x_ref, sc_ref, sh_ref = refs[:3]
        w_refs = refs[3:3 + nw]
        b_refs = refs[3 + nw:3 + nw + nb]
        o_refs = refs[3 + nw + nb:]
        h = (x_ref[...] * (1.0 + sc_ref[...]) + sh_ref[...]).astype(_MXU)
        prods = [lax.dot_general(h, w[...], (((1,), (1,)), ((), ())), preferred_element_type=F32) for w in w_refs]
        if nb:
            prods[0] = prods[0] + b_refs[0][...]
        for o, v in zip(o_refs, epilogue(*prods)):
            o[...] = v.astype(o.dtype)

    return _call(
        body, name=name, grid=(s // tm,),
        out_shape=[jax.ShapeDtypeStruct((s, n), dt) for dt in out_dtypes],
        in_specs=[pl.BlockSpec((tm, k), lambda i: (i, 0)), _sel(mod, l, ksc), _sel(mod, l, ksh)]
        + [_full(w.shape) for w in wts] + ([_sel(bias, l)] if nb else []),
        out_specs=[pl.BlockSpec((tm, n), lambda i: (i, 0))] * len(out_dtypes),
        args=(x, mod, mod, *wts, *([bias] if nb else [])), comm=comm)


def _swiglu(gate, up):
    return gate, up, gate * _sigmoid(gate) * up


def _proj_residual_ln(a, w, l, xres, mod, kg, lng, lnb, name, comm=None):
    s, k = a.shape
    n = w.shape[1]

    def body(a_ref, w_ref, x_ref, g_ref, lg_ref, lb_ref, r_ref, u_ref, o_ref):
        r = jnp.dot(a_ref[...], w_ref[...], preferred_element_type=F32)
        u = ALPHA * x_ref[...] + (1.0 + g_ref[...]) * r
        mu = jnp.mean(u, axis=-1, keepdims=True)
        xc = u - mu
        var = jnp.mean(xc * xc, axis=-1, keepdims=True)
        r_ref[...] = r
        u_ref[...] = u
        o_ref[...] = xc * lax.rsqrt(var + LN_EPS) * lg_ref[...] + lb_ref[...]

    row = pl.BlockSpec((TM, n), lambda i: (i, 0))
    return _call(
        body, name=name, grid=(s // TM,),
        out_shape=[jax.ShapeDtypeStruct((s, n), F32)] * 3,
        in_specs=[pl.BlockSpec((TM, k), lambda i: (i, 0)), _full(w.shape), row, _sel(mod, l, kg), _sel(lng, l),
                  _sel(lnb, l)],
        out_specs=[row, row, row], args=(a, w, xres, mod, lng, lnb), comm=comm)


def _loss_head(xo, target, name):
    s, n = xo.shape

    def body(x_ref, t_ref, dx_ref, acc_ref):
        @pl.when(pl.program_id(0) == 0)
        def _():
            acc_ref[...] = jnp.zeros_like(acc_ref)

        err = x_ref[...] - t_ref[...]
        dx_ref[...] = err * (1.0 / n)
        acc_ref[...] += jnp.sum(err * err)

    row = pl.BlockSpec((TM, n), lambda i: (i, 0))
    return _call(
        body, name=name, grid=(s // TM,),
        out_shape=[jax.ShapeDtypeStruct((s, n), F32), jax.ShapeDtypeStruct((8, 128), F32)],
        in_specs=[row, row], out_specs=[row, _full((8, 128))], args=(xo, target))


def _ln_bwd(dxo, u, res, lng, l, mod, kg, name, comm=None):
    s, n = u.shape

    def body(dy_ref, u_ref, r_ref, lg_ref, g_ref, du_ref, dr_ref, acc_ref):
        @pl.when(pl.program_id(0) == 0)
        def _():
            acc_ref[...] = jnp.zeros_like(acc_ref)

        uu = u_ref[...]
        mu = jnp.mean(uu, axis=-1, keepdims=True)
        xc = uu - mu
        var = jnp.mean(xc * xc, axis=-1, keepdims=True)
        rstd = lax.rsqrt(var + LN_EPS)
        xh = xc * rstd
        dy = dy_ref[...]
        dxh = dy * lg_ref[...]
        du = rstd * (dxh - jnp.mean(dxh, axis=-1, keepdims=True) - xh * jnp.mean(dxh * xh, axis=-1, keepdims=True))
        du_ref[...] = du
        dr_ref[...] = (du * (1.0 + g_ref[...])).astype(dr_ref.dtype)
        acc_ref[0:1, :] += _colsum(dy * xh)
        acc_ref[1:2, :] += _colsum(dy)
        acc_ref[2:3, :] += _colsum(du * r_ref[...])

    row = pl.BlockSpec((TM, n), lambda i: (i, 0))
    return _call(
        body, name=name, grid=(s // TM,),
        out_shape=[jax.ShapeDtypeStruct((s, n), F32), jax.ShapeDtypeStruct((s, n), _MXU),
                   jax.ShapeDtypeStruct((8, n), F32)],
        in_specs=[row, row, row, _sel(lng, l), _sel(mod, l, kg)], out_specs=[row, row, _full((8, n))],
        args=(dxo, u, res, lng, mod), comm=comm)


def _swiglu_bwd(dr, w_down, gate, up, name, comm=None):
    s, n = dr.shape
    f = w_down.shape[0]

    def body(dr_ref, w_ref, g_ref, u_ref, dg_ref, du_ref):
        dp = lax.dot_general(dr_ref[...], w_ref[...], (((1,), (1,)), ((), ())), preferred_element_type=F32)
        g = g_ref[...].astype(F32)
        sg = _sigmoid(g)
        dg_ref[...] = (dp * u_ref[...].astype(F32) * (sg * (1.0 + g * (1.0 - sg)))).astype(dg_ref.dtype)
        du_ref[...] = (dp * (g * sg)).astype(du_ref.dtype)

    tile = pl.BlockSpec((TM_WIDE, f), lambda i: (i, 0))
    return _call(
        body, name=name, grid=(s // TM_WIDE,),
        out_shape=[jax.ShapeDtypeStruct((s, f), _MXU)] * 2,
        in_specs=[pl.BlockSpec((TM_WIDE, n), lambda i: (i, 0)), _full(w_down.shape), tile, tile],
        out_specs=[tile, tile], args=(dr, w_down, gate, up), comm=comm)


def _input_grad(acts, wts, l, du, xin, mod, ksc, tm, name, comm=None):
    s = acts[0].shape[0]
    n = wts[0].shape[1]
    na = len(acts)

    def body(*refs):
        a_refs, w_refs = refs[:na], refs[na:2 * na]
        du_ref, x_ref, sc_ref, dx_ref, acc_ref = refs[2 * na:]

        @pl.when(pl.program_id(0) == 0)
        def _():
            acc_ref[...] = jnp.zeros_like(acc_ref)

        dh = None
        for a, w in zip(a_refs, w_refs):
            t = jnp.dot(a[...], w[...], preferred_element_type=F32)
            dh = t if dh is None else dh + t
        dx_ref[...] = ALPHA * du_ref[...] + dh * (1.0 + sc_ref[...])
        acc_ref[0:1, :] += _colsum(dh)
        acc_ref[1:2, :] += _colsum(dh * x_ref[...])

    row = pl.BlockSpec((tm, n), lambda i: (i, 0))
    return _call(
        body, name=name, grid=(s // tm,),
        out_shape=[jax.ShapeDtypeStruct((s, n), F32), jax.ShapeDtypeStruct((8, n), F32)],
        in_specs=[pl.BlockSpec((tm, a.shape[1]), lambda i: (i, 0)) for a in acts]
        + [_full(w.shape) for w in wts] + [row, row, _sel(mod, l, ksc)],
        out_specs=[row, _full((8, n))], args=(*acts, *wts, du, xin, mod), comm=comm)


def _matmul_nt(a, w, name):
    s, k = a.shape
    n = w.shape[0]

    def body(a_ref, w_ref, o_ref):
        o_ref[...] = lax.dot_general(a_ref[...], w_ref[...], (((1,), (1,)), ((), ())), preferred_element_type=F32)

    return _call(
        body, name=name, grid=(s // TM,),
        out_shape=[jax.ShapeDtypeStruct((s, n), F32)],
        in_specs=[pl.BlockSpec((TM, k), lambda i: (i, 0)), _full(w.shape)],
        out_specs=[pl.BlockSpec((TM, n), lambda i: (i, 0))], args=(a, w))[0]


def _weight_grad(a, b, name, mod=None, comm=None):
    s, ka = a.shape
    nbc = b.shape[1]
    nm = 0 if mod is None else 2
    steps = s // TM

    def body(*refs):
        a_ref, b_ref = refs[:2]
        m_refs = refs[2:2 + nm]
        o_ref, acc = refs[2 + nm:]

        @pl.when(pl.program_id(0) == 0)
        def _():
            acc[...] = jnp.zeros_like(acc)

        bv = b_ref[...]
        if nm:
            bv = bv * (1.0 + m_refs[0][...]) + m_refs[1][...]
        acc[...] += lax.dot_general(a_ref[...], bv.astype(_MXU), (((0,), (0,)), ((), ())),
                                    preferred_element_type=F32)

        @pl.when(pl.program_id(0) == steps - 1)
        def _():
            o_ref[...] = acc[...].astype(o_ref.dtype)

    mspecs, margs = [], []
    if nm:
        marr, l, ksc, ksh = mod
        mspecs, margs = [_sel(marr, l, ksc), _sel(marr, l, ksh)], [marr, marr]
    res = _call(
        body, name=name, grid=(steps,),
        out_shape=[jax.ShapeDtypeStruct((ka, nbc), _MXU)],
        in_specs=[pl.BlockSpec((TM, ka), lambda t: (t, 0)), pl.BlockSpec((TM, nbc), lambda t: (t, 0))] + mspecs,
        out_specs=[_full((ka, nbc))], args=(a, b, *margs), scratch=[pltpu.VMEM((ka, nbc), F32)], comm=comm)
    return [res[0].reshape(NDEV, ka // NDEV, nbc)] + list(res[1:])


def _ada_fwd(c_all, w_ada, b_cols, name):
    cols = w_ada.shape[2]

    def body(c_ref, w_ref, b_ref, cond_ref, o_ref):
        cv = c_ref[...]
        cond = cv * _sigmoid(cv)

        @pl.when(pl.program_id(0) == 0)
        def _():
            cond_ref[...] = cond

        o_ref[...] = _dot(cond, w_ref[...]) + b_ref[...]

    return _call(
        body, name=name, grid=(DEPTH,),
        out_shape=[jax.ShapeDtypeStruct((NDEV, D), F32), jax.ShapeDtypeStruct((DEPTH, NDEV, cols), F32)],
        in_specs=[_full((NDEV, D)), pl.BlockSpec((None, D, cols), lambda l: (l, 0, 0)),
                  pl.BlockSpec((None, 1, cols), lambda l: (l, 0, 0))],
        out_specs=[_full((NDEV, D)), pl.BlockSpec((None, NDEV, cols), lambda l: (l, 0, 0))],
        args=(c_all, w_ada, b_cols))


def _ada_bwd(cond, dmod_cols, name):
    cols = dmod_cols.shape[2]

    def body(c_ref, d_ref, o_ref):
        o_ref[...] = _dot_tn(c_ref[...], d_ref[...])

    return _call(
        body, name=name, grid=(DEPTH,),
        out_shape=[jax.ShapeDtypeStruct((DEPTH, D, cols), F32)],
        in_specs=[_full((NDEV, D)), pl.BlockSpec((None, NDEV, cols), lambda l: (l, 0, 0))],
        out_specs=[pl.BlockSpec((None, D, cols), lambda l: (l, 0, 0))], args=(cond, dmod_cols))[0]


def _adam_math(w, g, m, v):
    m = ADAM_B1 * m + (1.0 - ADAM_B1) * g
    v = ADAM_B2 * v + (1.0 - ADAM_B2) * (g * g)
    m_hat = m / (1.0 - ADAM_B1 ** ADAM_STEP)
    v_hat = v / (1.0 - ADAM_B2 ** ADAM_STEP)
    delta = -ADAM_LR * (m_hat / (jnp.sqrt(v_hat) + ADAM_EPS) + ADAM_WD * w)
    return delta, m, v


def _sum_parts(p_ref):
    g = p_ref[0].astype(F32)
    for i in range(1, p_ref.shape[0]):
        g = g + p_ref[i].astype(F32)
    return g


def _adamw(parts, w, m, v, tr, name):
    r, c = w.shape
    nparts = parts.shape[0]

    def body(p_ref, w_ref, m_ref, v_ref, g_ref, d_ref, nm_ref, nv_ref):
        g = _sum_parts(p_ref)
        delta, nm, nv = _adam_math(w_ref[...], g, m_ref[...], v_ref[...])
        g_ref[...] = g
        d_ref[...] = delta
        nm_ref[...] = nm
        nv_ref[...] = nv

    blk = pl.BlockSpec((tr, c), lambda i: (i, 0))
    return _call(
        body, name=name, grid=(r // tr,),
        out_shape=[jax.ShapeDtypeStruct((r, c), F32)] * 4,
        in_specs=[pl.BlockSpec((nparts, tr, c), lambda i: (0, i, 0)), blk, blk, blk],
        out_specs=[blk] * 4, args=(parts, w, m, v))


def _reduce_parts(parts, tr, name):
    nparts, r, c = parts.shape

    def body(p_ref, g_ref):
        g_ref[...] = _sum_parts(p_ref)

    return _call(
        body, name=name, grid=(r // tr,), out_shape=[jax.ShapeDtypeStruct((r, c), F32)],
        in_specs=[pl.BlockSpec((nparts, tr, c), lambda i: (0, i, 0))],
        out_specs=[pl.BlockSpec((tr, c), lambda i: (i, 0))], args=(parts,))[0]


def _cmul(ar, ai, br, bi):
    return ar * br - ai * bi, ar * bi + ai * br


def _lam_bar(lr, li, log_dt):
    dt = jnp.exp(log_dt)
    mag = jnp.exp(lr * dt)
    return dt, mag * jnp.cos(li * dt), mag * jnp.sin(li * dt)


def _layer_spec(arr):
    tail = arr.shape[1:]
    return pl.BlockSpec((None,) + tail, lambda l: (l,) + (0,) * len(tail))


def _ssm_prep(lam_re, lam_im, log_dt, b_re, b_im, name):
    def body(lr_ref, li_ref, dt_ref, br_ref, bi_ref, pr_ref, pi_ref, or_ref, oi_ref):
        lr, li = lr_ref[...], li_ref[...]
        _, ar, ai = _lam_bar(lr, li, dt_ref[...])
        den = lr * lr + li * li
        fr, fi = _cmul(ar - 1.0, ai, lr / den, -li / den)
        obr, obi = _cmul(fr[:, None, :], fi[:, None, :], br_ref[...], bi_ref[...])
        or_ref[...] = obr
        oi_ref[...] = obi
        qr, qi = ar, ai
        for k in range(SUB):
            pr_ref[k] = qr
            pi_ref[k] = qi
            qr, qi = _cmul(qr, qi, ar, ai)

    nl, g, p = lam_re.shape
    h = b_re.shape[2]
    outs = [jax.ShapeDtypeStruct((nl, SUB, g, p), F32)] * 2 + [jax.ShapeDtypeStruct((nl, g, h, p), F32)] * 2
    args = (lam_re, lam_im, log_dt, b_re, b_im)
    return _call(body, name=name, grid=(nl,), out_shape=outs, in_specs=[_layer_spec(a) for a in args],
                 out_specs=[_layer_spec(o) for o in outs], args=args)


def _ssm_param_bwd(lam_re, lam_im, log_dt, b_re, b_im, dlr, dli, dbr, dbi, name):
    def body(lr_ref, li_ref, dt_ref, br_ref, bi_ref, gar_ref, gai_ref, gbr_ref, gbi_ref,
             olr_ref, oli_ref, odt_ref, obr_ref, obi_ref):
        lr, li = lr_ref[...], li_ref[...]
        dt, ar, ai = _lam_bar(lr, li, dt_ref[...])
        den = lr * lr + li * li
        ir, ii = lr / den, -li / den
        fr, fi = _cmul(ar - 1.0, ai, ir, ii)
        br, bi = br_ref[...], bi_ref[...]
        gbr, gbi = gbr_ref[...], gbi_ref[...]
        obr, obi = _cmul(fr[:, None, :], -fi[:, None, :], gbr, gbi)
        obr_ref[...] = obr
        obi_ref[...] = obi
        pr, pi = _cmul(br, -bi, gbr, gbi)
        gfr, gfi = jnp.sum(pr, axis=1), jnp.sum(pi, axis=1)
        t_r, t_i = _cmul(gfr, gfi, ir, -ii)
        gar, gai = gar_ref[...] + t_r, gai_ref[...] + t_i
        qr, qi = _cmul(fr, fi, ir, ii)
        l1r, l1i = _cmul(gfr, gfi, qr, -qi)
        l2r, l2i = _cmul(gar, gai, dt * ar, -dt * ai)
        olr_ref[...] = l2r - l1r
        oli_ref[...] = l2i - l1i
        mr, mi = _cmul(lr, li, ar, ai)
        dr, _ = _cmul(gar, gai, mr, -mi)
        odt_ref[...] = jnp.sum(dr, axis=1, keepdims=True) * dt

    nl, g, p = lam_re.shape
    h = b_re.shape[2]
    outs = ([jax.ShapeDtypeStruct((nl, g, p), F32)] * 2 + [jax.ShapeDtypeStruct((nl, g, 1), F32)]
            + [jax.ShapeDtypeStruct((nl, g, h, p), F32)] * 2)
    args = (lam_re, lam_im, log_dt, b_re, b_im, dlr, dli, dbr, dbi)
    return _call(body, name=name, grid=(nl,), out_shape=outs, in_specs=[_layer_spec(a) for a in args],
                 out_specs=[_layer_spec(o) for o in outs], args=args)


def _gelu(x):
    k = math.sqrt(2.0 / math.pi)
    return 0.5 * x * (1.0 + jnp.tanh(k * (x + 0.044715 * x * x * x)))


def _gelu_grad(x):
    k = math.sqrt(2.0 / math.pi)
    th = jnp.tanh(k * (x + 0.044715 * x * x * x))
    return 0.5 * (1.0 + th) + 0.5 * x * (1.0 - th * th) * k * (1.0 + 3 * 0.044715 * x * x)


def _pool_inv_count(chunk):
    t = lax.broadcasted_iota(jnp.int32, (T, GW), 0) + chunk * T + 1
    lane = lax.broadcasted_iota(jnp.int32, (T, GW), 1)
    win = jnp.where(lane < 64, 2, jnp.where(lane < 128, 4, jnp.where(lane < 192, 8, 16)))
    return 1.0 / jnp.minimum(t, win).astype(F32)


def _window_sums(v, bufs, base, step):
    sums = []
    for k, buf in enumerate(bufs):
        buf[base:base + T, :] = v
        v = v + buf[pl.ds(base + step * (1 << k), T), :]
        sums.append(v)
    lane = lax.broadcasted_iota(jnp.int32, (T, GW), 1)
    return jnp.where(lane < 64, sums[0], jnp.where(lane < 128, sums[1], jnp.where(lane < 192, sums[2], sums[3])))


def _phase_copies(buf, rot):
    n = HALO + T - SUB
    for p in range(1, SUB):
        rot[p - 1, 0:n, :] = buf[pl.ds(p, n), :]


def _tap(buf, rot, off):
    q, p = divmod(off, SUB)
    return buf[off:off + T, :] if p == 0 else rot[p - 1, SUB * q:SUB * q + T, :]


def _scan(x_scr, pw_ref, pwrev_ref, carry, reverse):
    row8 = lax.broadcasted_iota(jnp.int32, (T, 128), 0) % SUB
    nb = T // SUB
    for j in range(XH // 128):
        lo, hi = j * 128, XH + j * 128
        re, im = x_scr[:, lo:lo + 128], x_scr[:, hi:hi + 128]
        for d in (1, 2, 4):
            ar, ai = pw_ref[d - 1:d, lo:lo + 128], pw_ref[d - 1:d, hi:hi + 128]
            if reverse:
                ai = -ai
                keep = row8 < SUB - d
                sre, sim = pltpu.roll(re, T - d, 0), pltpu.roll(im, T - d, 0)
            else:
                keep = row8 >= d
                sre, sim = pltpu.roll(re, d, 0), pltpu.roll(im, d, 0)
            sre, sim = jnp.where(keep, sre, 0.0), jnp.where(keep, sim, 0.0)
            re, im = re + ar * sre - ai * sim, im + ar * sim + ai * sre
        cr, ci = carry[:, lo:lo + 128], carry[:, hi:hi + 128]
        if reverse:
            pr, pi = pwrev_ref[:, lo:lo + 128], -pwrev_ref[:, hi:hi + 128]
            order, edge = reversed(range(nb)), 0
        else:
            pr, pi = pw_ref[:, lo:lo + 128], pw_ref[:, hi:hi + 128]
            order, edge = range(nb), SUB - 1
        for b in order:
            rb, ib = re[SUB * b:SUB * b + SUB], im[SUB * b:SUB * b + SUB]
            crb, cib = jnp.broadcast_to(cr, (SUB, 128)), jnp.broadcast_to(ci, (SUB, 128))
            rb, ib = rb + pr * crb - pi * cib, ib + pr * cib + pi * crb
            x_scr[SUB * b:SUB * b + SUB, lo:lo + 128] = rb
            x_scr[SUB * b:SUB * b + SUB, hi:hi + 128] = ib
            cr, ci = rb[edge:edge + 1], ib[edge:edge + 1]


MIXER_PARAMS = ('scw', 'pblk', 'pscale', 'dww', 'dwb', 'clg', 'clb', 'bblk', 'cblk', 'pw', 'pwrev', 'ssd', 'wglu',
                'bglu')


def _mixer_fwd(z, mp, l, name):
    s = z.shape[0]
    nch = s // T
    params = [mp[k] for k in MIXER_PARAMS]

    def body(z_ref, scw_ref, pblk_ref, ps_ref, dww_ref, dwb_ref, clg_ref, clb_ref, bblk_ref, cblk_ref, pw_ref,
             pwrev_ref, ssd_ref, wglu_ref, bglu_ref, y_ref, xs_ref, aux_ref,
             qbuf, hbuf, pb1, pb2, pb4, pb8, rot, carry):
        i = pl.program_id(0)

        @pl.when(i == 0)
        def _():
            for ref in (qbuf, hbuf, pb1, pb2, pb4, pb8):
                ref[0:HALO, :] = jnp.zeros((HALO, GW), F32)
            carry[...] = jnp.zeros_like(carry)

        zh, zb, zc = z_ref[:, 0:GW], z_ref[:, GW:2 * GW], z_ref[:, 2 * GW:3 * GW]
        zp, zcv, zcg, zs = (z_ref[:, 3 * GW:4 * GW], z_ref[:, 4 * GW:5 * GW], z_ref[:, 5 * GW:6 * GW],
                            z_ref[:, 6 * GW:7 * GW])
        qbuf[HALO:HALO + T, :] = zc * zh
        conv = None
        for k in range(SC_TAPS):
            v = scw_ref[k:k + 1, :] * qbuf[pl.ds(HALO - (SC_TAPS - 1) + k, T), :]
            conv = v if conv is None else conv + v
        aux_ref[:, 0:GW] = conv
        y_ref[:, 0:GW] = (zb * conv).astype(y_ref.dtype)
        r = _window_sums(zp, (pb1, pb2, pb4, pb8), HALO, -1) * _pool_inv_count(i) - zp
        aux_ref[:, GW:2 * GW] = r
        y_ref[:, GW:2 * GW] = (_dot(r, pblk_ref[...]) * ps_ref[...]).astype(y_ref.dtype)
        hbuf[HALO:HALO + T, :] = zcv * _sigmoid(zcg)
        _phase_copies(hbuf, rot)
        hc = None
        for k in range(CF_TAPS):
            v = dww_ref[k:k + 1, :] * _tap(hbuf, rot, HALO - (CF_TAPS - 1) + k)
            hc = v if hc is None else hc + v
        hc = hc + dwb_ref[...]
        aux_ref[:, 2 * GW:3 * GW] = hc
        mu = jnp.mean(hc, axis=-1, keepdims=True)
        xc = hc - mu
        var = jnp.mean(xc * xc, axis=-1, keepdims=True)
        ln = xc * lax.rsqrt(var + LN_EPS) * clg_ref[...] + clb_ref[...]
        y_ref[:, 2 * GW:3 * GW] = (ln * _sigmoid(ln)).astype(y_ref.dtype)
        xs_ref[...] = _dot(zs, bblk_ref[...])
        _scan(xs_ref, pw_ref, pwrev_ref, carry[...], reverse=False)
        carry[...] = xs_ref[T - 1:T, :]
        yy = _dot(xs_ref[...], cblk_ref[...]) + ssd_ref[...] * zs
        aux_ref[:, 3 * GW:4 * GW] = yy
        yg = _gelu(yy)
        v = _dot(yg, wglu_ref[...]) + bglu_ref[...]
        y_ref[:, 3 * GW:4 * GW] = (yg * _sigmoid(v)).astype(y_ref.dtype)
        for ref in (qbuf, hbuf, pb1, pb2, pb4, pb8):
            ref[0:HALO, :] = ref[T:T + HALO, :]

    fwd = lambda i: (i, 0)
    return _call(
        body, name=name, grid=(nch,),
        out_shape=[jax.ShapeDtypeStruct((s, D), _MXU), jax.ShapeDtypeStruct((s, XW), F32),
                   jax.ShapeDtypeStruct((s, 4 * GW), F32)],
        in_specs=[pl.BlockSpec((T, INW), fwd)] + [_sel(p, l) for p in params],
        out_specs=[pl.BlockSpec((T, D), fwd), pl.BlockSpec((T, XW), fwd), pl.BlockSpec((T, 4 * GW), fwd)],
        scratch=[pltpu.VMEM((HALO + T, GW), F32)] * 6
        + [pltpu.VMEM((SUB - 1, HALO + T, GW), F32), pltpu.VMEM((1, XW), F32)],
        args=(z, *params))


def _mixer_bwd(z, dy, xs, aux, mp, l, name):
    s = z.shape[0]
    nch = s // T
    params = [mp[k] for k in MIXER_PARAMS]

    def body(z_ref, zprev_ref, dy_ref, xs_ref, xprev_ref, aux_ref, scw_ref, pblk_ref, ps_ref, dww_ref, dwb_ref,
             clg_ref, clb_ref, bblk_ref, cblk_ref, pw_ref, pwrev_ref, ssd_ref, wglu_ref, bglu_ref,
             dz_ref, vec_ref, small_ref, dpblk_ref, dwglu_ref, dbblk_ref, dcblk_ref, dlam_ref,
             qbuf, hbuf, dcbuf, dhbuf, eb1, eb2, eb4, eb8, rot_h, rot_d, acarry, a_scr):
        i = pl.program_id(0)
        chunk = nch - 1 - i

        @pl.when(i == 0)
        def _():
            for ref in (vec_ref, small_ref, dpblk_ref, dwglu_ref, dbblk_ref, dcblk_ref, dlam_ref, acarry):
                ref[...] = jnp.zeros_like(ref)
            for ref in (dcbuf, dhbuf, eb1, eb2, eb4, eb8):
                ref[T:T + HALO, :] = jnp.zeros((HALO, GW), F32)

        zh, zb, zc = z_ref[:, 0:GW], z_ref[:, GW:2 * GW], z_ref[:, 2 * GW:3 * GW]
        zcv, zcg, zs = z_ref[:, 4 * GW:5 * GW], z_ref[:, 5 * GW:6 * GW], z_ref[:, 6 * GW:7 * GW]
        conv, r, hc, yy = (aux_ref[:, 0:GW], aux_ref[:, GW:2 * GW], aux_ref[:, 2 * GW:3 * GW],
                           aux_ref[:, 3 * GW:4 * GW])
        has_prev = (chunk > 0).astype(F32)
        qbuf[0:HALO, :] = zprev_ref[:, 2 * GW:3 * GW] * zprev_ref[:, 0:GW] * has_prev
        hbuf[0:HALO, :] = zprev_ref[:, 4 * GW:5 * GW] * _sigmoid(zprev_ref[:, 5 * GW:6 * GW]) * has_prev

        dya = dy_ref[:, 0:GW]
        qbuf[HALO:HALO + T, :] = zc * zh
        dconv = dya * zb
        dcbuf[0:T, :] = dconv
        dq = None
        for k in range(SC_TAPS):
            small_ref[k:k + 1, :] += _colsum(dconv * qbuf[pl.ds(HALO - (SC_TAPS - 1) + k, T), :])
            v = scw_ref[k:k + 1, :] * dcbuf[pl.ds(SC_TAPS - 1 - k, T), :]
            dq = v if dq is None else dq + v
        dzh, dzb, dzc = dq * zc, dya * conv, dq * zh

        dyb = dy_ref[:, GW:2 * GW]
        o = _dot(r, pblk_ref[...])
        small_ref[3:4, :] += _colsum(dyb * o)
        do = dyb * ps_ref[...]
        dpblk_ref[...] += _dot_tn(r, do)
        dr = _dot_nt(do, pblk_ref[...])
        dzp = _window_sums(dr * _pool_inv_count(chunk), (eb1, eb2, eb4, eb8), 0, 1) - dr

        dyc = dy_ref[:, 2 * GW:3 * GW]
        sgc = _sigmoid(zcg)
        hbuf[HALO:HALO + T, :] = zcv * sgc
        mu = jnp.mean(hc, axis=-1, keepdims=True)
        xc = hc - mu
        var = jnp.mean(xc * xc, axis=-1, keepdims=True)
        rstd = lax.rsqrt(var + LN_EPS)
        nrm = xc * rstd
        ln = nrm * clg_ref[...] + clb_ref[...]
        sl = _sigmoid(ln)
        dln = dyc * (sl * (1.0 + ln * (1.0 - sl)))
        small_ref[5:6, :] += _colsum(dln * nrm)
        small_ref[6:7, :] += _colsum(dln)
        dn = dln * clg_ref[...]
        dhc = rstd * (dn - jnp.mean(dn, axis=-1, keepdims=True) - nrm * jnp.mean(dn * nrm, axis=-1, keepdims=True))
        small_ref[4:5, :] += _colsum(dhc)
        dhbuf[0:T, :] = dhc
        _phase_copies(hbuf, rot_h)
        _phase_copies(dhbuf, rot_d)
        dhg = None
        for k in range(CF_TAPS):
            small_ref[16 + k:17 + k, :] += _colsum(dhc * _tap(hbuf, rot_h, HALO - (CF_TAPS - 1) + k))
            v = dww_ref[k:k + 1, :] * _tap(dhbuf, rot_d, CF_TAPS - 1 - k)
            dhg = v if dhg is None else dhg + v
        dzcv = dhg * sgc
        dzcg = dhg * zcv * sgc * (1.0 - sgc)

        dyd = dy_ref[:, 3 * GW:4 * GW]
        yg = _gelu(yy)
        v = _dot(yg, wglu_ref[...]) + bglu_ref[...]
        sg = _sigmoid(v)
        dv = dyd * yg * sg * (1.0 - sg)
        small_ref[8:9, :] += _colsum(dv)
        dwglu_ref[...] += _dot_tn(yg, dv)
        g = (dyd * sg + _dot_nt(dv, wglu_ref[...])) * _gelu_grad(yy)
        small_ref[7:8, :] += _colsum(g * zs)
        dcblk_ref[...] += _dot_tn(xs_ref[...], g)
        a_scr[...] = _dot_nt(g, cblk_ref[...])
        _scan(a_scr, pw_ref, pwrev_ref, acarry[...], reverse=True)
        acarry[...] = a_scr[0:1, :]
        dzs = _dot_nt(a_scr[...], bblk_ref[...]) + ssd_ref[...] * g
        dbblk_ref[...] += _dot_tn(zs, a_scr[...])
        row = lax.broadcasted_iota(jnp.int32, (T, 128), 0)
        for j in range(XH // 128):
            lo, hi = j * 128, XH + j * 128
            first_r = xprev_ref[SUB - 1:SUB, lo:lo + 128] * has_prev
            first_i = xprev_ref[SUB - 1:SUB, hi:hi + 128] * has_prev
            pr = jnp.where(row == 0, first_r, pltpu.roll(xs_ref[:, lo:lo + 128], 1, 0))
            pi = jnp.where(row == 0, first_i, pltpu.roll(xs_ref[:, hi:hi + 128], 1, 0))
            ar, ai = a_scr[:, lo:lo + 128], a_scr[:, hi:hi + 128]
            dlam_ref[0:1, lo:lo + 128] += _colsum(ar * pr + ai * pi)
            dlam_ref[0:1, hi:hi + 128] += _colsum(ai * pr - ar * pi)

        parts = (dzh, dzb, dzc, dzp, dzcv, dzcg, dzs)
        for k, part in enumerate(parts):
            dz_ref[:, k * GW:(k + 1) * GW] = part.astype(dz_ref.dtype)
            vec_ref[0:1, k * GW:(k + 1) * GW] += _colsum(part)
        for ref in (dcbuf, dhbuf, eb1, eb2, eb4, eb8):
            ref[T:T + HALO, :] = ref[0:HALO, :]

    rev = lambda i: (nch - 1 - i, 0)

    def before(rows):
        return lambda i: (jnp.maximum((nch - 1 - i) * (T // rows) - 1, 0), 0)

    outs = [((s, INW), _MXU), ((8, INW), F32), ((48, GW), F32), ((GW, GW), F32), ((GW, GW), F32),
            ((GW, XW), F32), ((XW, GW), F32), ((8, XW), F32)]
    return _call(
        body, name=name, grid=(nch,),
        out_shape=[jax.ShapeDtypeStruct(sh, dt) for sh, dt in outs],
        in_specs=[pl.BlockSpec((T, INW), rev), pl.BlockSpec((HALO, INW), before(HALO)), pl.BlockSpec((T, D), rev),
                  pl.BlockSpec((T, XW), rev), pl.BlockSpec((SUB, XW), before(SUB)), pl.BlockSpec((T, 4 * GW), rev)]
        + [_sel(p, l) for p in params],
        out_specs=[pl.BlockSpec((T, INW), rev)] + [_full(sh) for sh, _ in outs[1:]],
        scratch=[pltpu.VMEM((HALO + T, GW), F32)] * 8
        + [pltpu.VMEM((SUB - 1, HALO + T, GW), F32)] * 2 + [pltpu.VMEM((1, XW), F32), pltpu.VMEM((T, XW), F32)],
        args=(z, z, dy, xs, xs, aux, *params))


def _blockdiag(m):
    *lead, g, h, p = m.shape
    eye = jnp.eye(g, dtype=m.dtype)
    return (m[..., :, :, None, :] * eye[:, None, :, None]).reshape(*lead, g * h, g * p)


def _blockdiag_extract(f, h, p):
    *lead, r, _ = f.shape
    g = r // h
    eye = jnp.eye(g, dtype=f.dtype)
    return jnp.sum(f.reshape(*lead, g, h, g, p) * eye[:, None, :, None], axis=-2)


def _t(a):
    return jnp.swapaxes(a, -1, -2)


PACK_ROWS = 512


def _pack(arrs):
    flat = jnp.concatenate([a.reshape(-1).astype(F32) for a in arrs])
    quantum = PACK_ROWS * 128
    padded = -(-flat.shape[0] // quantum) * quantum
    return jnp.pad(flat, (0, padded - flat.shape[0])).reshape(-1, 128)


def _unpack(buf, shapes, lead=()):
    flat = buf.reshape(lead + (-1,))
    out, off = [], 0
    for shp in shapes:
        n = math.prod(shp)
        out.append(flat[..., off:off + n].reshape(lead + tuple(shp)))
        off += n
    return out


def _cols_to_full(g):
    n, l, r, c = g.shape
    return g.transpose(1, 2, 0, 3).reshape(l, r, n * c)


def _rows_to_full(g):
    n, l, r, c = g.shape
    return g.transpose(1, 0, 2, 3).reshape(l, n * r, c)


REPLICATED = ['b_ada', 'b_in', 'pool_w', 'pool_scale', 'cf_dw_b', 'cf_ln_g', 'cf_ln_b', 'ssm_lam_re', 'ssm_lam_im',
              'ssm_log_dt', 'ssm_b_re', 'ssm_b_im', 'ssm_c_re', 'ssm_c_im', 'ssm_d', 'ssm_b_glu', 'ln1_g', 'ln1_b',
              'ln2_g', 'ln2_b']
SMALL_SHARDED = ['sc_w', 'cf_dw_w', 'ssm_w_glu']
COL_SHARDED = ['w_in', 'w_gate', 'w_up']
ROW_SHARDED = ['w_o', 'w_down']
BIG = COL_SHARDED + ROW_SHARDED
WEIGHTS = ['w_ada', 'b_ada', 'w_in', 'b_in', 'sc_w', 'pool_w', 'pool_scale', 'cf_dw_w', 'cf_dw_b', 'cf_ln_g',
           'cf_ln_b', 'ssm_lam_re', 'ssm_lam_im', 'ssm_log_dt', 'ssm_b_re', 'ssm_b_im', 'ssm_c_re', 'ssm_c_im',
           'ssm_d', 'ssm_w_glu', 'ssm_b_glu', 'w_o', 'ln1_g', 'ln1_b', 'w_gate', 'w_up', 'w_down', 'ln2_g', 'ln2_b']
SH1, SC1, G1, SH2, SC2, G2 = range(6)


def _train_step(p):
    xi, yi, ci = _me()
    me = _flat((xi, yi, ci))
    s = p['x'].shape[1]
    xs = p['x'].reshape(s, D)
    target = p['loss_target'].reshape(s, D)
    vec3 = lambda a: a.reshape(DEPTH, 1, -1)

    def shard(n, l):
        w = p[n][l]
        return (w.T if n in COL_SHARDED else w).astype(_MXU)

    def whole(g):
        return g.reshape(-1, g.shape[-1])

    small_shapes = [p[n].shape for n in SMALL_SHARDED] + [p['c'].shape]
    gathered = _allgather([_pack([p[n] for n in SMALL_SHARDED] + [p['c']])] + [shard(n, 0) for n in BIG],
                          "gather_first")
    scw_g, dww_g, wglu_g, c_g = _unpack(gathered[0], small_shapes, lead=(NDEV,))
    full = {'sc_w': _cols_to_full(scw_g), 'cf_dw_w': _cols_to_full(dww_g), 'ssm_w_glu': _rows_to_full(wglu_g)}
    wts = [dict(zip(BIG, (whole(g) for g in gathered[1:])))] + [dict() for _ in range(DEPTH - 1)]
    c_all = c_g.reshape(NDEV, D)

    ncol = p['w_ada'].shape[2]
    b_cols = lax.dynamic_slice_in_dim(p['b_ada'], me * ncol, ncol, axis=1).reshape(DEPTH, 1, ncol)
    cond, mod_cols = _ada_fwd(c_all, p['w_ada'], b_cols, "ada_fwd")
    (mod_x,) = _alltoall([mod_cols.transpose(1, 0, 2)], "mod_exchange")
    mod = mod_x.transpose(1, 0, 2).reshape(DEPTH, 6, 1, D)

    ssm_in = (p['ssm_lam_re'], p['ssm_lam_im'], p['ssm_log_dt'].reshape(DEPTH, NGRP, 1),
              _t(p['ssm_b_re']), _t(p['ssm_b_im']))
    pwr, pwi, bbr, bbi = _ssm_prep(*ssm_in, "ssm_prep")
    pw = jnp.concatenate([pwr.reshape(DEPTH, SUB, XH), pwi.reshape(DEPTH, SUB, XH)], axis=2)
    mp = dict(
        scw=full['sc_w'], pblk=_blockdiag(p['pool_w']).astype(_MXU), pscale=vec3(p['pool_scale']),
        dww=full['cf_dw_w'], dwb=vec3(p['cf_dw_b']), clg=vec3(p['cf_ln_g']), clb=vec3(p['cf_ln_b']),
        bblk=jnp.concatenate([_blockdiag(bbr), _blockdiag(bbi)], axis=2).astype(_MXU),
        cblk=jnp.concatenate([_blockdiag(_t(p['ssm_c_re'])), -_blockdiag(_t(p['ssm_c_im']))], axis=1).astype(_MXU),
        pw=pw, pwrev=pw[:, ::-1], ssd=vec3(p['ssm_d']), wglu=full['ssm_w_glu'].astype(_MXU),
        bglu=vec3(p['ssm_b_glu']))
    b_in, ln1_g, ln1_b, ln2_g, ln2_b = (vec3(p[n]) for n in ('b_in', 'ln1_g', 'ln1_b', 'ln2_g', 'ln2_b'))

    saved = []
    for l in range(DEPTH):
        w = wts[l]
        nxt = l + 1 < DEPTH
        ahead = (lambda *names: (GATHER, [shard(n, l + 1) for n in names])) if nxt else (lambda *names: None)

        def take(res, n_own, *names):
            for n, g in zip(names, res[n_own:]):
                wts[l + 1][n] = whole(g)
            return res[:n_own]

        (z,) = take(_modulated_matmul(xs, mod, l, SC1, SH1, [w['w_in']], b_in, TM, [F32], lambda a: (a,),
                                      "in_proj", comm=ahead('w_in')), 1, 'w_in')
        ycat, states, aux = _mixer_fwd(z, mp, l, "mixer_fwd")
        y, u1, x1 = take(_proj_residual_ln(ycat, w['w_o'], l, xs, mod, G1, ln1_g, ln1_b, "o_proj_ln",
                                           comm=ahead('w_o')), 3, 'w_o')
        gate, up, act = take(_modulated_matmul(x1, mod, l, SC2, SH2, [w['w_gate'], w['w_up']], None, TM_WIDE,
                                               [_MXU, _MXU, _MXU], _swiglu, "gate_up",
                                               comm=ahead('w_down', 'w_gate')), 3, 'w_down', 'w_gate')
        f, u2, x2 = take(_proj_residual_ln(act, w['w_down'], l, x1, mod, G2, ln2_g, ln2_b, "down_proj_ln",
                                           comm=ahead('w_up')), 3, 'w_up')
        saved.append(dict(x=xs, z=z, ycat=ycat, states=states, aux=aux, y=y, u1=u1, x1=x1, gate=gate, up=up,
                          act=act, f=f, u2=u2))
        xs = x2

    dx, sq = _loss_head(xs, target, "loss_head")
    loss = lax.psum(sq[0, 0] * (0.5 / D), AXES)

    recv = {n: [None] * DEPTH for n in BIG}
    small_g = {n: [None] * DEPTH for n in ('ln1_g', 'ln1_b', 'ln2_g', 'ln2_b')}
    dmod = [None] * DEPTH
    mixer_out = [None] * DEPTH
    pending = None
    for l in reversed(range(DEPTH)):
        sv, w = saved[l], wts[l]
        du2, df, acc2 = _ln_bwd(dx, sv['u2'], sv['f'], ln2_g, l, mod, G2, "ln2_bwd")
        if pending is None:
            dgate, dup = _swiglu_bwd(df, w['w_down'], sv['gate'], sv['up'], "swiglu_bwd")
        else:
            lp, g_o, g_in = pending
            dgate, dup, recv['w_o'][lp], recv['w_in'][lp] = _swiglu_bwd(
                df, w['w_down'], sv['gate'], sv['up'], "swiglu_bwd", comm=(EXCHANGE, [g_o, g_in]))
        (g_down,) = _weight_grad(sv['act'], df, "dw_down")
        dx1, acc_h2 = _input_grad([dgate, dup], [w['w_gate'], w['w_up']], l, du2, sv['x1'], mod, SC2, TM_WIDE, "dh2")
        (g_gate,) = _weight_grad(dgate, sv['x1'], "dw_gate", mod=(mod, l, SC2, SH2))
        (g_up,) = _weight_grad(dup, sv['x1'], "dw_up", mod=(mod, l, SC2, SH2))
        du1, dyb, acc1, recv['w_down'][l] = _ln_bwd(dx1, sv['u1'], sv['y'], ln1_g, l, mod, G1, "ln1_bwd",
                                                    comm=(EXCHANGE, [g_down]))
        dycat = _matmul_nt(dyb, w['w_o'], "dycat")
        (g_o,) = _weight_grad(sv['ycat'], dyb, "dw_o")
        dz, *mixer_out[l] = _mixer_bwd(sv['z'], dycat, sv['states'], sv['aux'], mp, l, "mixer_bwd")
        dx, acc_h1, recv['w_gate'][l] = _input_grad([dz], [w['w_in']], l, du1, sv['x'], mod, SC1, TM, "dh1",
                                                    comm=(EXCHANGE, [g_gate]))
        g_in, recv['w_up'][l] = _weight_grad(dz, sv['x'], "dw_in", mod=(mod, l, SC1, SH1), comm=(EXCHANGE, [g_up]))
        pending = (l, g_o, g_in)
        dmod[l] = jnp.concatenate([acc_h1[0], acc_h1[1], acc1[2], acc_h2[0], acc_h2[1], acc2[2]])
        small_g['ln2_g'][l], small_g['ln2_b'][l] = acc2[0], acc2[1]
        small_g['ln1_g'][l], small_g['ln1_b'][l] = acc1[0], acc1[1]
    lp, g_o, g_in = pending
    recv['w_o'][lp], recv['w_in'][lp] = _alltoall([g_o, g_in], "exchange_last")

    partial = {n: jnp.stack(v) for n, v in small_g.items()}
    partial['b_ada'] = jnp.stack(dmod)
    vec, small, dpblk, dwglu, dbblk, dcblk, dlam = (jnp.stack(t) for t in zip(*mixer_out))
    partial.update(
        b_in=vec[:, 0], sc_w=small[:, 0:SC_TAPS], pool_scale=small[:, 3], cf_dw_b=small[:, 4], cf_ln_g=small[:, 5],
        cf_ln_b=small[:, 6], ssm_d=small[:, 7], ssm_b_glu=small[:, 8], cf_dw_w=small[:, 16:16 + CF_TAPS],
        pool_w=_blockdiag_extract(dpblk, 64, 64), ssm_w_glu=dwglu,
        ssm_c_re=_t(_blockdiag_extract(dcblk[:, :XH], NSTATE, NCH)),
        ssm_c_im=-_t(_blockdiag_extract(dcblk[:, XH:], NSTATE, NCH)))
    dlr, dli, dlog, dbr, dbi = _ssm_param_bwd(
        *ssm_in, dlam[:, 0, :XH].reshape(DEPTH, NGRP, NSTATE), dlam[:, 0, XH:].reshape(DEPTH, NGRP, NSTATE),
        _blockdiag_extract(dbblk[:, :, :XH], NCH, NSTATE), _blockdiag_extract(dbblk[:, :, XH:], NCH, NSTATE),
        "ssm_param_bwd")
    partial.update(ssm_lam_re=dlr, ssm_lam_im=dli, ssm_log_dt=dlog.reshape(DEPTH, NGRP), ssm_b_re=_t(dbr),
                   ssm_b_im=_t(dbi))

    out = {'loss': loss, 'grad_x': dx.reshape(1, s, D)}

    def emit(n, g, delta, nm, nv):
        shp = p[n].shape
        out['grad_' + n], out['delta_' + n] = g.reshape(shp), delta.reshape(shp)
        out['new_m_' + n], out['new_v_' + n] = nm.reshape(shp), nv.reshape(shp)

    names_a = REPLICATED + SMALL_SHARDED
    shapes_a = [partial[n].shape for n in names_a]
    zeros_like_full = lambda n: jnp.zeros(partial[n].shape, F32)
    (parts_a,) = _allgather([_pack([partial[n] for n in names_a])], "gather_small_grads")
    packs = [_pack([p[pre + n] for n in REPLICATED] + [zeros_like_full(n) for n in SMALL_SHARDED])
             for pre in ('', 'm_', 'v_')]
    res_a = _adamw(parts_a, *packs, PACK_ROWS, "adamw_small")
    res_a = [_unpack(r, shapes_a) for r in res_a]
    for k, n in enumerate(REPLICATED):
        emit(n, *[r[k] for r in res_a])
    reduced = dict(zip(names_a, res_a[0]))

    blocks = []
    for n in SMALL_SHARDED:
        axis = 1 if n == 'ssm_w_glu' else 2
        width = p[n].shape[axis]
        blocks.append(lax.dynamic_slice_in_dim(reduced[n], me * width, width, axis=axis))
    shapes_b = [b.shape for b in blocks]
    packs = [_pack([p[pre + n] for n in SMALL_SHARDED]) for pre in ('', 'm_', 'v_')]
    res_b = _adamw(_pack(blocks)[None], *packs, PACK_ROWS, "adamw_small_sharded")
    res_b = [_unpack(r, shapes_b) for r in res_b]
    for k, n in enumerate(SMALL_SHARDED):
        emit(n, *[r[k] for r in res_b])

    dmod_all = _unpack(parts_a, shapes_a, lead=(NDEV,))[0]
    dmod_cols = lax.dynamic_slice_in_dim(dmod_all, me * ncol, ncol, axis=2).transpose(1, 0, 2)
    g_ada = _ada_bwd(cond, dmod_cols, "ada_bwd")
    flat2 = lambda a: a.reshape(-1, a.shape[-1])
    emit('w_ada', *_adamw(flat2(g_ada)[None], flat2(p['w_ada']), flat2(p['m_w_ada']), flat2(p['v_w_ada']),
                          512, "adamw_w_ada"))

    for n in BIG:
        parts = jnp.stack(recv[n], axis=1)
        parts = parts.reshape(NDEV, -1, D)
        tr = 512 if parts.shape[1] % 512 == 0 else parts.shape[1] // DEPTH
        if n in ROW_SHARDED:
            emit(n, *_adamw(parts, flat2(p[n]), flat2(p['m_' + n]), flat2(p['v_' + n]), tr, "adamw_" + n))
        else:
            g_t = _reduce_parts(parts, tr, "sum_" + n).reshape(DEPTH, -1, D)
            g = flat2(_t(g_t))
            emit(n, *_adamw(g[None], flat2(p[n]), flat2(p['m_' + n]), flat2(p['v_' + n]), 512, "adamw_" + n))
    return out


def kernel(x, c, w_ada, b_ada, w_in, b_in, sc_w, pool_w, pool_scale, cf_dw_w, cf_dw_b, cf_ln_g, cf_ln_b, ssm_lam_re, ssm_lam_im, ssm_log_dt, ssm_b_re, ssm_b_im, ssm_c_re, ssm_c_im, ssm_d, ssm_w_glu, ssm_b_glu, w_o, ln1_g, ln1_b, w_gate, w_up, w_down, ln2_g, ln2_b, loss_target, m_w_ada, m_b_ada, m_w_in, m_b_in, m_sc_w, m_pool_w, m_pool_scale, m_cf_dw_w, m_cf_dw_b, m_cf_ln_g, m_cf_ln_b, m_ssm_lam_re, m_ssm_lam_im, m_ssm_log_dt, m_ssm_b_re, m_ssm_b_im, m_ssm_c_re, m_ssm_c_im, m_ssm_d, m_ssm_w_glu, m_ssm_b_glu, m_w_o, m_ln1_g, m_ln1_b, m_w_gate, m_w_up, m_w_down, m_ln2_g, m_ln2_b, v_w_ada, v_b_ada, v_w_in, v_b_in, v_sc_w, v_pool_w, v_pool_scale, v_cf_dw_w, v_cf_dw_b, v_cf_ln_g, v_cf_ln_b, v_ssm_lam_re, v_ssm_lam_im, v_ssm_log_dt, v_ssm_b_re, v_ssm_b_im, v_ssm_c_re, v_ssm_c_im, v_ssm_d, v_ssm_w_glu, v_ssm_b_glu, v_w_o, v_ln1_g, v_ln1_b, v_w_gate, v_w_up, v_w_down, v_ln2_g, v_ln2_b):
    out = _train_step(dict(locals()))
    return (out['loss'], out['grad_x'], *[out[pre + n] for pre in ('grad_', 'delta_', 'new_m_', 'new_v_')
                                          for n in WEIGHTS])
```

```python
import math

import jax
import jax.numpy as jnp
from jax import lax
from jax.experimental import pallas as pl
from jax.experimental.pallas import tpu as pltpu

F32 = jnp.float32
_MXU = jnp.bfloat16

D = 1024
GW = 256
INW = 7 * GW
DFF = 2816
DEPTH = 4
NDEV = 8
SC_TAPS = 3
CF_TAPS = 31
NGRP = 16
NCH = 16
NSTATE = 64
XH = NGRP * NSTATE
XW = 2 * XH
ALPHA = (2 * DEPTH) ** 0.25
LN_EPS = 1e-5
T = 256
SUB = 8
HALO = 32
TM = 512
TM_WIDE = 256
VMEM_LIMIT = 56 * 1024 * 1024

ADAM_LR = 0.001
ADAM_B1 = 0.9
ADAM_B2 = 0.999
ADAM_EPS = 1e-08
ADAM_WD = 0.01
ADAM_STEP = 10

MESH = pl.DeviceIdType.MESH
ANY = pl.BlockSpec(memory_space=pl.ANY)
AXES = ("x", "y", "c")


def _full(shape):
    nd = len(shape)
    return pl.BlockSpec(shape, lambda *_: (0,) * nd)


def _sel(arr, *idx):
    tail = arr.shape[len(idx):]
    return pl.BlockSpec((None,) * len(idx) + tail, lambda *_: idx + (0,) * len(tail))


def _dot(a, b):
    return jnp.dot(a.astype(_MXU), b.astype(_MXU), preferred_element_type=F32)


def _dot_nt(a, b):
    return lax.dot_general(a.astype(_MXU), b.astype(_MXU), (((1,), (1,)), ((), ())), preferred_element_type=F32)


def _dot_tn(a, b):
    return lax.dot_general(a.astype(_MXU), b.astype(_MXU), (((0,), (0,)), ((), ())), preferred_element_type=F32)


def _sigmoid(x):
    return 1.0 / (1.0 + jnp.exp(-x))


def _colsum(x):
    return jnp.sum(x, axis=0, keepdims=True)


def _me():
    return lax.axis_index("x"), lax.axis_index("y"), lax.axis_index("c")


def _flat(p):
    return 4 * p[0] + 2 * p[1] + p[2]


def _allgather(arrays, name):
    n = len(arrays)

    def body(*refs):
        ins, outs = refs[:n], refs[n:2 * n]
        send_sems, recv_sems, local_sems = refs[2 * n:]
        x, y, c = _me()
        me, sibling = (x, y, c), (x, y, 1 - c)
        chips = [(1 - x, y), (x, 1 - y), (1 - x, 1 - y)]

        def copy(a, k, block, to, src=None):
            dst = outs[a].at[_flat(block)]
            return pltpu.make_async_remote_copy(
                src_ref=dst if src is None else src, dst_ref=dst,
                send_sem=send_sems.at[a, k], recv_sem=recv_sems.at[a, k],
                device_id=to, device_id_type=MESH)

        started = []
        for a in range(n):
            mine = pltpu.make_async_copy(ins[a], outs[a].at[_flat(me)], local_sems.at[a])
            mine.start()
            started.append(mine)
        first = []
        for a in range(n):
            first.append(copy(a, 0, me, sibling, src=ins[a]))
            first += [copy(a, 1 + j, me, (*chip, c), src=ins[a]) for j, chip in enumerate(chips)]
        for cp in first:
            cp.start()
        passed = []
        for a in range(n):
            for j, chip in enumerate(chips):
                copy(a, 1 + j, (*chip, c), me).wait_recv()
                fwd = copy(a, 4 + j, (*chip, c), sibling)
                fwd.start()
                passed.append(fwd)
        for a in range(n):
            copy(a, 0, sibling, me).wait_recv()
            for j, chip in enumerate(chips):
                copy(a, 4 + j, (*chip, 1 - c), me).wait_recv()
        for cp in first + passed:
            cp.wait_send()
        for cp in started:
            cp.wait()

    return pl.pallas_call(
        body, name=name,
        out_shape=[jax.ShapeDtypeStruct((NDEV,) + a.shape, a.dtype) for a in arrays],
        in_specs=[ANY] * n, out_specs=[ANY] * n,
        scratch_shapes=[pltpu.SemaphoreType.DMA((n, 7)), pltpu.SemaphoreType.DMA((n, 7)),
                        pltpu.SemaphoreType.DMA((n,))],
    )(*arrays)


GATHER, EXCHANGE = "gather", "exchange"


def _direct_copies(kind, ins, outs, send_sems, recv_sems, local_sems, start):
    x, y, c = _me()
    mine = _flat((x, y, c))
    for a in range(len(ins)):
        own = ins[a] if kind == GATHER else ins[a].at[mine]
        loc = pltpu.make_async_copy(own, outs[a].at[mine], local_sems.at[a])
        if start:
            loc.start()
        for k in range(1, NDEV):
            peer = (lax.rem(x + ((k >> 2) & 1), 2), lax.rem(y + ((k >> 1) & 1), 2), lax.rem(c + (k & 1), 2))
            cp = pltpu.make_async_remote_copy(
                src_ref=ins[a] if kind == GATHER else ins[a].at[_flat(peer)], dst_ref=outs[a].at[mine],
                send_sem=send_sems.at[a, k - 1], recv_sem=recv_sems.at[a, k - 1],
                device_id=peer, device_id_type=MESH)
            if start:
                cp.start()
            else:
                cp.wait_send()
                cp.wait_recv()
        if not start:
            loc.wait()


def _comm_out_shapes(comm):
    kind, arrays = comm
    return [jax.ShapeDtypeStruct(((NDEV,) + a.shape) if kind == GATHER else a.shape, a.dtype) for a in arrays]


def _call(body, *, name, grid, in_specs, out_specs, out_shape, args, scratch=(), comm=None):
    params = pltpu.CompilerParams(dimension_semantics=("arbitrary",) * len(grid), vmem_limit_bytes=VMEM_LIMIT)
    if comm is None:
        return pl.pallas_call(body, name=name, grid=grid, in_specs=in_specs, out_specs=out_specs,
                              out_shape=out_shape, scratch_shapes=list(scratch), compiler_params=params)(*args)
    kind, arrays = comm
    n, n_in, n_out, n_scr = len(arrays), len(in_specs), len(out_specs), len(scratch)

    def wrapped(*refs):
        ins, cin = refs[:n_in], refs[n_in:n_in + n]
        outs, cout = refs[n_in + n:n_in + n + n_out], refs[n_in + n + n_out:n_in + 2 * n + n_out]
        scr = refs[n_in + 2 * n + n_out:]
        user_scr, sems = scr[:n_scr], scr[n_scr:]
        ids = [pl.program_id(k) for k in range(len(grid))]
        first, last = ids[0] == 0, ids[0] == grid[0] - 1
        for k in range(1, len(grid)):
            first, last = first & (ids[k] == 0), last & (ids[k] == grid[k] - 1)

        @pl.when(first)
        def _():
            _direct_copies(kind, cin, cout, *sems, start=True)

        body(*ins, *outs, *user_scr)

        @pl.when(last)
        def _():
            _direct_copies(kind, cin, cout, *sems, start=False)

    sems = [pltpu.SemaphoreType.DMA((n, 7)), pltpu.SemaphoreType.DMA((n, 7)), pltpu.SemaphoreType.DMA((n,))]
    res = pl.pallas_call(
        wrapped, name=name, grid=grid, in_specs=list(in_specs) + [ANY] * n, out_specs=list(out_specs) + [ANY] * n,
        out_shape=list(out_shape) + _comm_out_shapes(comm), scratch_shapes=list(scratch) + sems,
        compiler_params=params)(*args, *arrays)
    return res


def _alltoall(arrays, name):
    n = len(arrays)

    def body(*refs):
        _direct_copies(EXCHANGE, refs[:n], refs[n:2 * n], *refs[2 * n:], start=True)
        _direct_copies(EXCHANGE, refs[:n], refs[n:2 * n], *refs[2 * n:], start=False)

    return pl.pallas_call(
        body, name=name,
        out_shape=[jax.ShapeDtypeStruct(a.shape, a.dtype) for a in arrays],
        in_specs=[ANY] * n, out_specs=[ANY] * n,
        scratch_shapes=[pltpu.SemaphoreType.DMA((n, 7)), pltpu.SemaphoreType.DMA((n, 7)),
                        pltpu.SemaphoreType.DMA((n,))],
    )(*arrays)


def _modulated_matmul(x, mod, l, ksc, ksh, wts, bias, tm, out_dtypes, epilogue, name, comm=None):
    s, k = x.shape
    nw = len(wts)
    n = wts[0].shape[0]
    nb = 0 if bias is None else 1

    def body(*refs):
        x_ref, sc_ref, sh_ref = refs[:3]
        w_refs = refs[3:3 + nw]
        b_refs = refs[3 + nw:3 + nw + nb]
        o_refs = refs[3 + nw + nb:]
        h = (x_ref[...] * (1.0 + sc_ref[...]) + sh_ref[...]).astype(_MXU)
        prods = [lax.dot_general(h, w[...], (((1,), (1,)), ((), ())), preferred_element_type=F32) for w in w_refs]
        if nb:
            prods[0] = prods[0] + b_refs[0][...]
        for o, v in zip(o_refs, epilogue(*prods)):
            o[...] = v.astype(o.dtype)

    return _call(
        body, name=name, grid=(s // tm,),
        out_shape=[jax.ShapeDtypeStruct((s, n), dt) for dt in out_dtypes],
        in_specs=[pl.BlockSpec((tm, k), lambda i: (i, 0)), _sel(mod, l, ksc), _sel(mod, l, ksh)]
        + [_full(w.shape) for w in wts] + ([_sel(bias, l)] if nb else []),
        out_specs=[pl.BlockSpec((tm, n), lambda i: (i, 0))] * len(out_dtypes),
        args=(x, mod, mod, *wts, *([bias] if nb else [])), comm=comm)


def _swiglu(gate, up):
    return gate, up, gate * _sigmoid(gate) * up


def _proj_residual_ln(a, w, l, xres, mod, kg, lng, lnb, name, comm=None):
    s, k = a.shape
    n = w.shape[1]

    def body(a_ref, w_ref, x_ref, g_ref, lg_ref, lb_ref, r_ref, u_ref, o_ref):
        r = jnp.dot(a_ref[...], w_ref[...], preferred_element_type=F32)
        u = ALPHA * x_ref[...] + (1.0 + g_ref[...]) * r
        mu = jnp.mean(u, axis=-1, keepdims=True)
        xc = u - mu
        var = jnp.mean(xc * xc, axis=-1, keepdims=True)
        r_ref[...] = r
        u_ref[...] = u
        o_ref[...] = xc * lax.rsqrt(var + LN_EPS) * lg_ref[...] + lb_ref[...]

    row = pl.BlockSpec((TM, n), lambda i: (i, 0))
    return _call(
        body, name=name, grid=(s // TM,),
        out_shape=[jax.ShapeDtypeStruct((s, n), F32)] * 3,
        in_specs=[pl.BlockSpec((TM, k), lambda i: (i, 0)), _full(w.shape), row, _sel(mod, l, kg), _sel(lng, l),
                  _sel(lnb, l)],
        out_specs=[row, row, row], args=(a, w, xres, mod, lng, lnb), comm=comm)


def _loss_head(xo, target, name):
    s, n = xo.shape

    def body(x_ref, t_ref, dx_ref, acc_ref):
        @pl.when(pl.program_id(0) == 0)
        def _():
            acc_ref[...] = jnp.zeros_like(acc_ref)

        err = x_ref[...] - t_ref[...]
        dx_ref[...] = err * (1.0 / n)
        acc_ref[...] += jnp.sum(err * err)

    row = pl.BlockSpec((TM, n), lambda i: (i, 0))
    return _call(
        body, name=name, grid=(s // TM,),
        out_shape=[jax.ShapeDtypeStruct((s, n), F32), jax.ShapeDtypeStruct((8, 128), F32)],
        in_specs=[row, row], out_specs=[row, _full((8, 128))], args=(xo, target))


def _ln_bwd(dxo, u, res, lng, l, mod, kg, name, comm=None):
    s, n = u.shape

    def body(dy_ref, u_ref, r_ref, lg_ref, g_ref, du_ref, dr_ref, acc_ref):
        @pl.when(pl.program_id(0) == 0)
        def _():
            acc_ref[...] = jnp.zeros_like(acc_ref)

        uu = u_ref[...]
        mu = jnp.mean(uu, axis=-1, keepdims=True)
        xc = uu - mu
        var = jnp.mean(xc * xc, axis=-1, keepdims=True)
        rstd = lax.rsqrt(var + LN_EPS)
        xh = xc * rstd
        dy = dy_ref[...]
        dxh = dy * lg_ref[...]
        du = rstd * (dxh - jnp.mean(dxh, axis=-1, keepdims=True) - xh * jnp.mean(dxh * xh, axis=-1, keepdims=True))
        du_ref[...] = du
        dr_ref[...] = (du * (1.0 + g_ref[...])).astype(dr_ref.dtype)
        acc_ref[0:1, :] += _colsum(dy * xh)
        acc_ref[1:2, :] += _colsum(dy)
        acc_ref[2:3, :] += _colsum(du * r_ref[...])

    row = pl.BlockSpec((TM, n), lambda i: (i, 0))
    return _call(
        body, name=name, grid=(s // TM,),
        out_shape=[jax.ShapeDtypeStruct((s, n), F32), jax.ShapeDtypeStruct((s, n), _MXU),
                   jax.ShapeDtypeStruct((8, n), F32)],
        in_specs=[row, row, row, _sel(lng, l), _sel(mod, l, kg)], out_specs=[row, row, _full((8, n))],
        args=(dxo, u, res, lng, mod), comm=comm)


def _swiglu_bwd(dr, w_down, gate, up, name, comm=None):
    s, n = dr.shape
    f = w_down.shape[0]

    def body(dr_ref, w_ref, g_ref, u_ref, dg_ref, du_ref):
        dp = lax.dot_general(dr_ref[...], w_ref[...], (((1,), (1,)), ((), ())), preferred_element_type=F32)
        g = g_ref[...].astype(F32)
        sg = _sigmoid(g)
        dg_ref[...] = (dp * u_ref[...].astype(F32) * (sg * (1.0 + g * (1.0 - sg)))).astype(dg_ref.dtype)
        du_ref[...] = (dp * (g * sg)).astype(du_ref.dtype)

    tile = pl.BlockSpec((TM_WIDE, f), lambda i: (i, 0))
    return _call(
        body, name=name, grid=(s // TM_WIDE,),
        out_shape=[jax.ShapeDtypeStruct((s, f), _MXU)] * 2,
        in_specs=[pl.BlockSpec((TM_WIDE, n), lambda i: (i, 0)), _full(w_down.shape), tile, tile],
        out_specs=[tile, tile], args=(dr, w_down, gate, up), comm=comm)


def _input_grad(acts, wts, l, du, xin, mod, ksc, tm, name, comm=None):
    s = acts[0].shape[0]
    n = wts[0].shape[1]
    na = len(acts)

    def body(*refs):
        a_refs, w_refs = refs[:na], refs[na:2 * na]
        du_ref, x_ref, sc_ref, dx_ref, acc_ref = refs[2 * na:]

        @pl.when(pl.program_id(0) == 0)
        def _():
            acc_ref[...] = jnp.zeros_like(acc_ref)

        dh = None
        for a, w in zip(a_refs, w_refs):
            t = jnp.dot(a[...], w[...], preferred_element_type=F32)
            dh = t if dh is None else dh + t
        dx_ref[...] = ALPHA * du_ref[...] + dh * (1.0 + sc_ref[...])
        acc_ref[0:1, :] += _colsum(dh)
        acc_ref[1:2, :] += _colsum(dh * x_ref[...])

    row = pl.BlockSpec((tm, n), lambda i: (i, 0))
    return _call(
        body, name=name, grid=(s // tm,),
        out_shape=[jax.ShapeDtypeStruct((s, n), F32), jax.ShapeDtypeStruct((8, n), F32)],
        in_specs=[pl.BlockSpec((tm, a.shape[1]), lambda i: (i, 0)) for a in acts]
        + [_full(w.shape) for w in wts] + [row, row, _sel(mod, l, ksc)],
        out_specs=[row, _full((8, n))], args=(*acts, *wts, du, xin, mod), comm=comm)


def _input_grad_ln_bwd(acts, wts, du_in, xin, sc, u, res, lng, g, tm, name, comm=None):
    s = acts[0].shape[0]
    n = wts[0].shape[1]
    na = len(acts)

    def body(*refs):
        a_refs, w_refs = refs[:na], refs[na:2 * na]
        dui_ref, x_ref, sc_ref, u_ref, r_ref, lg_ref, g_ref, du_ref, dr_ref, acc_ref = refs[2 * na:]

        @pl.when(pl.program_id(0) == 0)
        def _():
            acc_ref[...] = jnp.zeros_like(acc_ref)

        dh = None
        for a, w in zip(a_refs, w_refs):
            t = jnp.dot(a[...], w[...], preferred_element_type=F32)
            dh = t if dh is None else dh + t
        dy = ALPHA * dui_ref[...] + dh * (1.0 + sc_ref[...])
        acc_ref[0:1, :] += _colsum(dh)
        acc_ref[1:2, :] += _colsum(dh * x_ref[...])
        uu = u_ref[...]
        mu = jnp.mean(uu, axis=-1, keepdims=True)
        xc = uu - mu
        var = jnp.mean(xc * xc, axis=-1, keepdims=True)
        rstd = lax.rsqrt(var + LN_EPS)
        xh = xc * rstd
        dxh = dy * lg_ref[...]
        du = rstd * (dxh - jnp.mean(dxh, axis=-1, keepdims=True) - xh * jnp.mean(dxh * xh, axis=-1, keepdims=True))
        du_ref[...] = du
        dr_ref[...] = (du * (1.0 + g_ref[...])).astype(dr_ref.dtype)
        acc_ref[2:3, :] += _colsum(dy * xh)
        acc_ref[3:4, :] += _colsum(dy)
        acc_ref[4:5, :] += _colsum(du * r_ref[...])

    row = pl.BlockSpec((tm, n), lambda i: (i, 0))
    return _call(
        body, name=name, grid=(s // tm,),
        out_shape=[jax.ShapeDtypeStruct((s, n), F32), jax.ShapeDtypeStruct((s, n), _MXU),
                   jax.ShapeDtypeStruct((8, n), F32)],
        in_specs=[pl.BlockSpec((tm, a.shape[1]), lambda i: (i, 0)) for a in acts]
        + [_full(w.shape) for w in wts] + [row, row, _sel(*sc), row, row, _sel(*lng), _sel(*g)],
        out_specs=[row, row, _full((8, n))],
        args=(*acts, *wts, du_in, xin, sc[0], u, res, lng[0], g[0]), comm=comm)


def _matmul_nt(a, w, name):
    s, k = a.shape
    n = w.shape[0]

    def body(a_ref, w_ref, o_ref):
        o_ref[...] = lax.dot_general(a_ref[...], w_ref[...], (((1,), (1,)), ((), ())), preferred_element_type=F32)

    return _call(
        body, name=name, grid=(s // TM,),
        out_shape=[jax.ShapeDtypeStruct((s, n), F32)],
        in_specs=[pl.BlockSpec((TM, k), lambda i: (i, 0)), _full(w.shape)],
        out_specs=[pl.BlockSpec((TM, n), lambda i: (i, 0))], args=(a, w))[0]


def _weight_grad(a, b, name, mod=None, comm=None):
    s, ka = a.shape
    nbc = b.shape[1]
    nm = 0 if mod is None else 2
    steps = s // TM

    def body(*refs):
        a_ref, b_ref = refs[:2]
        m_refs = refs[2:2 + nm]
        o_ref, acc = refs[2 + nm:]

        @pl.when(pl.program_id(0) == 0)
        def _():
            acc[...] = jnp.zeros_like(acc)

        bv = b_ref[...]
        if nm:
            bv = bv * (1.0 + m_refs[0][...]) + m_refs[1][...]
        acc[...] += lax.dot_general(a_ref[...], bv.astype(_MXU), (((0,), (0,)), ((), ())),
                                    preferred_element_type=F32)

        @pl.when(pl.program_id(0) == steps - 1)
        def _():
            o_ref[...] = acc[...].astype(o_ref.dtype)

    mspecs, margs = [], []
    if nm:
        marr, l, ksc, ksh = mod
        mspecs, margs = [_sel(marr, l, ksc), _sel(marr, l, ksh)], [marr, marr]
    res = _call(
        body, name=name, grid=(steps,),
        out_shape=[jax.ShapeDtypeStruct((ka, nbc), _MXU)],
        in_specs=[pl.BlockSpec((TM, ka), lambda t: (t, 0)), pl.BlockSpec((TM, nbc), lambda t: (t, 0))] + mspecs,
        out_specs=[_full((ka, nbc))], args=(a, b, *margs), scratch=[pltpu.VMEM((ka, nbc), F32)], comm=comm)
    return [res[0].reshape(NDEV, ka // NDEV, nbc)] + list(res[1:])


def _ada_fwd(c_all, w_ada, b_cols, name):
    cols = w_ada.shape[2]

    def body(c_ref, w_ref, b_ref, cond_ref, o_ref):
        cv = c_ref[...]
        cond = cv * _sigmoid(cv)

        @pl.when(pl.program_id(0) == 0)
        def _():
            cond_ref[...] = cond

        o_ref[...] = _dot(cond, w_ref[...]) + b_ref[...]

    return _call(
        body, name=name, grid=(DEPTH,),
        out_shape=[jax.ShapeDtypeStruct((NDEV, D), F32), jax.ShapeDtypeStruct((DEPTH, NDEV, cols), F32)],
        in_specs=[_full((NDEV, D)), pl.BlockSpec((None, D, cols), lambda l: (l, 0, 0)),
                  pl.BlockSpec((None, 1, cols), lambda l: (l, 0, 0))],
        out_specs=[_full((NDEV, D)), pl.BlockSpec((None, NDEV, cols), lambda l: (l, 0, 0))],
        args=(c_all, w_ada, b_cols))


def _ada_bwd(cond, dmod_cols, name):
    cols = dmod_cols.shape[2]

    def body(c_ref, d_ref, o_ref):
        o_ref[...] = _dot_tn(c_ref[...], d_ref[...])

    return _call(
        body, name=name, grid=(DEPTH,),
        out_shape=[jax.ShapeDtypeStruct((DEPTH, D, cols), F32)],
        in_specs=[_full((NDEV, D)), pl.BlockSpec((None, NDEV, cols), lambda l: (l, 0, 0))],
        out_specs=[pl.BlockSpec((None, D, cols), lambda l: (l, 0, 0))], args=(cond, dmod_cols))[0]


def _adam_math(w, g, m, v):
    m = ADAM_B1 * m + (1.0 - ADAM_B1) * g
    v = ADAM_B2 * v + (1.0 - ADAM_B2) * (g * g)
    m_hat = m / (1.0 - ADAM_B1 ** ADAM_STEP)
    v_hat = v / (1.0 - ADAM_B2 ** ADAM_STEP)
    delta = -ADAM_LR * (m_hat / (jnp.sqrt(v_hat) + ADAM_EPS) + ADAM_WD * w)
    return delta, m, v


def _sum_parts(p_ref):
    g = p_ref[0].astype(F32)
    for i in range(1, p_ref.shape[0]):
        g = g + p_ref[i].astype(F32)
    return g


def _adamw(parts, w, m, v, tr, name):
    r, c = w.shape
    nparts = parts.shape[0]

    def body(p_ref, w_ref, m_ref, v_ref, g_ref, d_ref, nm_ref, nv_ref):
        g = _sum_parts(p_ref)
        delta, nm, nv = _adam_math(w_ref[...], g, m_ref[...], v_ref[...])
        g_ref[...] = g
        d_ref[...] = delta
        nm_ref[...] = nm
        nv_ref[...] = nv

    blk = pl.BlockSpec((tr, c), lambda i: (i, 0))
    return _call(
        body, name=name, grid=(r // tr,),
        out_shape=[jax.ShapeDtypeStruct((r, c), F32)] * 4,
        in_specs=[pl.BlockSpec((nparts, tr, c), lambda i: (0, i, 0)), blk, blk, blk],
        out_specs=[blk] * 4, args=(parts, w, m, v))


def _layer_parts_specs(parts, tr):
    nparts, r, c = parts[0].shape
    last = r // tr - 1

    def spec(k):
        return pl.BlockSpec((nparts, tr, c),
                            lambda l, i: (0, jnp.where(l < k, 0, jnp.where(l == k, i, last)), 0))

    return [spec(k) for k in range(len(parts))]


def _for_layer(p_refs, fn):
    for k, ref in enumerate(p_refs):
        pl.when(pl.program_id(0) == k)(lambda ref=ref: fn(ref))


def _adamw_layers(parts, w, m, v, tr, name):
    nl = len(parts)
    _, r, c = parts[0].shape

    def body(*refs):
        p_refs = refs[:nl]
        w_ref, m_ref, v_ref, g_ref, d_ref, nm_ref, nv_ref = refs[nl:]

        def update(p_ref):
            g = _sum_parts(p_ref)
            delta, nm, nv = _adam_math(w_ref[...], g, m_ref[...], v_ref[...])
            g_ref[...] = g
            d_ref[...] = delta
            nm_ref[...] = nm
            nv_ref[...] = nv

        _for_layer(p_refs, update)

    blk = pl.BlockSpec((tr, c), lambda l, i: (l * (r // tr) + i, 0))
    return _call(
        body, name=name, grid=(nl, r // tr),
        out_shape=[jax.ShapeDtypeStruct((nl * r, c), F32)] * 4,
        in_specs=_layer_parts_specs(parts, tr) + [blk, blk, blk],
        out_specs=[blk] * 4, args=(*parts, w, m, v))


def _reduce_layers(parts, tr, name):
    nl = len(parts)
    _, r, c = parts[0].shape

    def body(*refs):
        g_ref = refs[nl]

        def reduce(p_ref):
            g_ref[...] = _sum_parts(p_ref)

        _for_layer(refs[:nl], reduce)

    return _call(
        body, name=name, grid=(nl, r // tr), out_shape=[jax.ShapeDtypeStruct((nl, r, c), F32)],
        in_specs=_layer_parts_specs(parts, tr),
        out_specs=[pl.BlockSpec((None, tr, c), lambda l, i: (l, i, 0))], args=tuple(parts))[0]


def _cmul(ar, ai, br, bi):
    return ar * br - ai * bi, ar * bi + ai * br


def _lam_bar(lr, li, log_dt):
    dt = jnp.exp(log_dt)
    mag = jnp.exp(lr * dt)
    return dt, mag * jnp.cos(li * dt), mag * jnp.sin(li * dt)


def _layer_spec(arr):
    tail = arr.shape[1:]
    return pl.BlockSpec((None,) + tail, lambda l: (l,) + (0,) * len(tail))


def _ssm_prep(lam_re, lam_im, log_dt, b_re, b_im, name):
    def body(lr_ref, li_ref, dt_ref, br_ref, bi_ref, pr_ref, pi_ref, or_ref, oi_ref):
        lr, li = lr_ref[...], li_ref[...]
        _, ar, ai = _lam_bar(lr, li, dt_ref[...])
        den = lr * lr + li * li
        fr, fi = _cmul(ar - 1.0, ai, lr / den, -li / den)
        obr, obi = _cmul(fr[:, None, :], fi[:, None, :], br_ref[...], bi_ref[...])
        or_ref[...] = obr
        oi_ref[...] = obi
        qr, qi = ar, ai
        for k in range(SUB):
            pr_ref[k] = qr
            pi_ref[k] = qi
            qr, qi = _cmul(qr, qi, ar, ai)

    nl, g, p = lam_re.shape
    h = b_re.shape[2]
    outs = [jax.ShapeDtypeStruct((nl, SUB, g, p), F32)] * 2 + [jax.ShapeDtypeStruct((nl, g, h, p), F32)] * 2
    args = (lam_re, lam_im, log_dt, b_re, b_im)
    return _call(body, name=name, grid=(nl,), out_shape=outs, in_specs=[_layer_spec(a) for a in args],
                 out_specs=[_layer_spec(o) for o in outs], args=args)


def _ssm_param_bwd(lam_re, lam_im, log_dt, b_re, b_im, dlr, dli, dbr, dbi, name):
    def body(lr_ref, li_ref, dt_ref, br_ref, bi_ref, gar_ref, gai_ref, gbr_ref, gbi_ref,
             olr_ref, oli_ref, odt_ref, obr_ref, obi_ref):
        lr, li = lr_ref[...], li_ref[...]
        dt, ar, ai = _lam_bar(lr, li, dt_ref[...])
        den = lr * lr + li * li
        ir, ii = lr / den, -li / den
        fr, fi = _cmul(ar - 1.0, ai, ir, ii)
        br, bi = br_ref[...], bi_ref[...]
        gbr, gbi = gbr_ref[...], gbi_ref[...]
        obr, obi = _cmul(fr[:, None, :], -fi[:, None, :], gbr, gbi)
        obr_ref[...] = obr
        obi_ref[...] = obi
        pr, pi = _cmul(br, -bi, gbr, gbi)
        gfr, gfi = jnp.sum(pr, axis=1), jnp.sum(pi, axis=1)
        t_r, t_i = _cmul(gfr, gfi, ir, -ii)
        gar, gai = gar_ref[...] + t_r, gai_ref[...] + t_i
        qr, qi = _cmul(fr, fi, ir, ii)
        l1r, l1i = _cmul(gfr, gfi, qr, -qi)
        l2r, l2i = _cmul(gar, gai, dt * ar, -dt * ai)
        olr_ref[...] = l2r - l1r
        oli_ref[...] = l2i - l1i
        mr, mi = _cmul(lr, li, ar, ai)
        dr, _ = _cmul(gar, gai, mr, -mi)
        odt_ref[...] = jnp.sum(dr, axis=1, keepdims=True) * dt

    nl, g, p = lam_re.shape
    h = b_re.shape[2]
    outs = ([jax.ShapeDtypeStruct((nl, g, p), F32)] * 2 + [jax.ShapeDtypeStruct((nl, g, 1), F32)]
            + [jax.ShapeDtypeStruct((nl, g, h, p), F32)] * 2)
    args = (lam_re, lam_im, log_dt, b_re, b_im, dlr, dli, dbr, dbi)
    return _call(body, name=name, grid=(nl,), out_shape=outs, in_specs=[_layer_spec(a) for a in args],
                 out_specs=[_layer_spec(o) for o in outs], args=args)


def _gelu(x):
    k = math.sqrt(2.0 / math.pi)
    return 0.5 * x * (1.0 + jnp.tanh(k * (x + 0.044715 * x * x * x)))


def _gelu_grad(x):
    k = math.sqrt(2.0 / math.pi)
    th = jnp.tanh(k * (x + 0.044715 * x * x * x))
    return 0.5 * (1.0 + th) + 0.5 * x * (1.0 - th * th) * k * (1.0 + 3 * 0.044715 * x * x)


def _pool_inv_count(chunk):
    t = lax.broadcasted_iota(jnp.int32, (T, GW), 0) + chunk * T + 1
    lane = lax.broadcasted_iota(jnp.int32, (T, GW), 1)
    win = jnp.where(lane < 64, 2, jnp.where(lane < 128, 4, jnp.where(lane < 192, 8, 16)))
    return 1.0 / jnp.minimum(t, win).astype(F32)


def _window_sums(v, bufs, base, step):
    sums = []
    for k, buf in enumerate(bufs):
        buf[base:base + T, :] = v
        v = v + buf[pl.ds(base + step * (1 << k), T), :]
        sums.append(v)
    lane = lax.broadcasted_iota(jnp.int32, (T, GW), 1)
    return jnp.where(lane < 64, sums[0], jnp.where(lane < 128, sums[1], jnp.where(lane < 192, sums[2], sums[3])))


def _phase_copies(buf, rot):
    n = HALO + T - SUB
    for p in range(1, SUB):
        rot[p - 1, 0:n, :] = buf[pl.ds(p, n), :]


def _tap(buf, rot, off):
    q, p = divmod(off, SUB)
    return buf[off:off + T, :] if p == 0 else rot[p - 1, SUB * q:SUB * q + T, :]


def _scan(x_scr, pw_ref, pwrev_ref, carry, reverse):
    row8 = lax.broadcasted_iota(jnp.int32, (T, 128), 0) % SUB
    nb = T // SUB
    for j in range(XH // 128):
        lo, hi = j * 128, XH + j * 128
        re, im = x_scr[:, lo:lo + 128], x_scr[:, hi:hi + 128]
        for d in (1, 2, 4):
            ar, ai = pw_ref[d - 1:d, lo:lo + 128], pw_ref[d - 1:d, hi:hi + 128]
            if reverse:
                ai = -ai
                keep = row8 < SUB - d
                sre, sim = pltpu.roll(re, T - d, 0), pltpu.roll(im, T - d, 0)
            else:
                keep = row8 >= d
                sre, sim = pltpu.roll(re, d, 0), pltpu.roll(im, d, 0)
            sre, sim = jnp.where(keep, sre, 0.0), jnp.where(keep, sim, 0.0)
            re, im = re + ar * sre - ai * sim, im + ar * sim + ai * sre
        cr, ci = carry[:, lo:lo + 128], carry[:, hi:hi + 128]
        if reverse:
            pr, pi = pwrev_ref[:, lo:lo + 128], -pwrev_ref[:, hi:hi + 128]
            order, edge = reversed(range(nb)), 0
        else:
            pr, pi = pw_ref[:, lo:lo + 128], pw_ref[:, hi:hi + 128]
            order, edge = range(nb), SUB - 1
        for b in order:
            rb, ib = re[SUB * b:SUB * b + SUB], im[SUB * b:SUB * b + SUB]
            crb, cib = jnp.broadcast_to(cr, (SUB, 128)), jnp.broadcast_to(ci, (SUB, 128))
            rb, ib = rb + pr * crb - pi * cib, ib + pr * cib + pi * crb
            x_scr[SUB * b:SUB * b + SUB, lo:lo + 128] = rb
            x_scr[SUB * b:SUB * b + SUB, hi:hi + 128] = ib
            cr, ci = rb[edge:edge + 1], ib[edge:edge + 1]


MIXER_PARAMS = ('scw', 'pblk', 'pscale', 'dww', 'dwb', 'clg', 'clb', 'bblk', 'cblk', 'pw', 'pwrev', 'ssd', 'wglu',
                'bglu')


def _mixer_fwd(z, mp, l, name, comm=None):
    s = z.shape[0]
    nch = s // T
    params = [mp[k] for k in MIXER_PARAMS]

    def body(z_ref, scw_ref, pblk_ref, ps_ref, dww_ref, dwb_ref, clg_ref, clb_ref, bblk_ref, cblk_ref, pw_ref,
             pwrev_ref, ssd_ref, wglu_ref, bglu_ref, y_ref, xs_ref, aux_ref,
             qbuf, hbuf, pb1, pb2, pb4, pb8, rot, carry):
        i = pl.program_id(0)

        @pl.when(i == 0)
        def _():
            for ref in (qbuf, hbuf, pb1, pb2, pb4, pb8):
                ref[0:HALO, :] = jnp.zeros((HALO, GW), F32)
            carry[...] = jnp.zeros_like(carry)

        zh, zb, zc = z_ref[:, 0:GW], z_ref[:, GW:2 * GW], z_ref[:, 2 * GW:3 * GW]
        zp, zcv, zcg, zs = (z_ref[:, 3 * GW:4 * GW], z_ref[:, 4 * GW:5 * GW], z_ref[:, 5 * GW:6 * GW],
                            z_ref[:, 6 * GW:7 * GW])
        qbuf[HALO:HALO + T, :] = zc * zh
        conv = None
        for k in range(SC_TAPS):
            v = scw_ref[k:k + 1, :] * qbuf[pl.ds(HALO - (SC_TAPS - 1) + k, T), :]
            conv = v if conv is None else conv + v
        aux_ref[:, 0:GW] = conv
        y_ref[:, 0:GW] = (zb * conv).astype(y_ref.dtype)
        r = _window_sums(zp, (pb1, pb2, pb4, pb8), HALO, -1) * _pool_inv_count(i) - zp
        aux_ref[:, GW:2 * GW] = r
        y_ref[:, GW:2 * GW] = (_dot(r, pblk_ref[...]) * ps_ref[...]).astype(y_ref.dtype)
        hbuf[HALO:HALO + T, :] = zcv * _sigmoid(zcg)
        _phase_copies(hbuf, rot)
        hc = None
        for k in range(CF_TAPS):
            v = dww_ref[k:k + 1, :] * _tap(hbuf, rot, HALO - (CF_TAPS - 1) + k)
            hc = v if hc is None else hc + v
        hc = hc + dwb_ref[...]
        aux_ref[:, 2 * GW:3 * GW] = hc
        mu = jnp.mean(hc, axis=-1, keepdims=True)
        xc = hc - mu
        var = jnp.mean(xc * xc, axis=-1, keepdims=True)
        ln = xc * lax.rsqrt(var + LN_EPS) * clg_ref[...] + clb_ref[...]
        y_ref[:, 2 * GW:3 * GW] = (ln * _sigmoid(ln)).astype(y_ref.dtype)
        xs_ref[...] = _dot(zs, bblk_ref[...])
        _scan(xs_ref, pw_ref, pwrev_ref, carry[...], reverse=False)
        carry[...] = xs_ref[T - 1:T, :]
        yy = _dot(xs_ref[...], cblk_ref[...]) + ssd_ref[...] * zs
        aux_ref[:, 3 * GW:4 * GW] = yy
        yg = _gelu(yy)
        v = _dot(yg, wglu_ref[...]) + bglu_ref[...]
        y_ref[:, 3 * GW:4 * GW] = (yg * _sigmoid(v)).astype(y_ref.dtype)
        for ref in (qbuf, hbuf, pb1, pb2, pb4, pb8):
            ref[0:HALO, :] = ref[T:T + HALO, :]

    fwd = lambda i: (i, 0)
    return _call(
        body, name=name, grid=(nch,),
        out_shape=[jax.ShapeDtypeStruct((s, D), _MXU), jax.ShapeDtypeStruct((s, XW), F32),
                   jax.ShapeDtypeStruct((s, 4 * GW), F32)],
        in_specs=[pl.BlockSpec((T, INW), fwd)] + [_sel(p, l) for p in params],
        out_specs=[pl.BlockSpec((T, D), fwd), pl.BlockSpec((T, XW), fwd), pl.BlockSpec((T, 4 * GW), fwd)],
        scratch=[pltpu.VMEM((HALO + T, GW), F32)] * 6
        + [pltpu.VMEM((SUB - 1, HALO + T, GW), F32), pltpu.VMEM((1, XW), F32)],
        args=(z, *params), comm=comm)


def _mixer_bwd(z, dy, xs, aux, mp, l, name):
    s = z.shape[0]
    nch = s // T
    params = [mp[k] for k in MIXER_PARAMS]

    def body(z_ref, zprev_ref, dy_ref, xs_ref, xprev_ref, aux_ref, scw_ref, pblk_ref, ps_ref, dww_ref, dwb_ref,
             clg_ref, clb_ref, bblk_ref, cblk_ref, pw_ref, pwrev_ref, ssd_ref, wglu_ref, bglu_ref,
             dz_ref, vec_ref, small_ref, dpblk_ref, dwglu_ref, dbblk_ref, dcblk_ref, dlam_ref,
             qbuf, hbuf, dcbuf, dhbuf, eb1, eb2, eb4, eb8, rot_h, rot_d, acarry, a_scr):
        i = pl.program_id(0)
        chunk = nch - 1 - i

        @pl.when(i == 0)
        def _():
            for ref in (vec_ref, small_ref, dpblk_ref, dwglu_ref, dbblk_ref, dcblk_ref, dlam_ref, acarry):
                ref[...] = jnp.zeros_like(ref)
            for ref in (dcbuf, dhbuf, eb1, eb2, eb4, eb8):
                ref[T:T + HALO, :] = jnp.zeros((HALO, GW), F32)

        zh, zb, zc = z_ref[:, 0:GW], z_ref[:, GW:2 * GW], z_ref[:, 2 * GW:3 * GW]
        zcv, zcg, zs = z_ref[:, 4 * GW:5 * GW], z_ref[:, 5 * GW:6 * GW], z_ref[:, 6 * GW:7 * GW]
        conv, r, hc, yy = (aux_ref[:, 0:GW], aux_ref[:, GW:2 * GW], aux_ref[:, 2 * GW:3 * GW],
                           aux_ref[:, 3 * GW:4 * GW])
        has_prev = (chunk > 0).astype(F32)
        qbuf[0:HALO, :] = zprev_ref[:, 2 * GW:3 * GW] * zprev_ref[:, 0:GW] * has_prev
        hbuf[0:HALO, :] = zprev_ref[:, 4 * GW:5 * GW] * _sigmoid(zprev_ref[:, 5 * GW:6 * GW]) * has_prev

        dya = dy_ref[:, 0:GW]
        qbuf[HALO:HALO + T, :] = zc * zh
        dconv = dya * zb
        dcbuf[0:T, :] = dconv
        dq = None
        for k in range(SC_TAPS):
            small_ref[k:k + 1, :] += _colsum(dconv * qbuf[pl.ds(HALO - (SC_TAPS - 1) + k, T), :])
            v = scw_ref[k:k + 1, :] * dcbuf[pl.ds(SC_TAPS - 1 - k, T), :]
            dq = v if dq is None else dq + v
        dzh, dzb, dzc = dq * zc, dya * conv, dq * zh

        dyb = dy_ref[:, GW:2 * GW]
        o = _dot(r, pblk_ref[...])
        small_ref[3:4, :] += _colsum(dyb * o)
        do = dyb * ps_ref[...]
        dpblk_ref[...] += _dot_tn(r, do)
        dr = _dot_nt(do, pblk_ref[...])
        dzp = _window_sums(dr * _pool_inv_count(chunk), (eb1, eb2, eb4, eb8), 0, 1) - dr

        dyc = dy_ref[:, 2 * GW:3 * GW]
        sgc = _sigmoid(zcg)
        hbuf[HALO:HALO + T, :] = zcv * sgc
        mu = jnp.mean(hc, axis=-1, keepdims=True)
        xc = hc - mu
        var = jnp.mean(xc * xc, axis=-1, keepdims=True)
        rstd = lax.rsqrt(var + LN_EPS)
        nrm = xc * rstd
        ln = nrm * clg_ref[...] + clb_ref[...]
        sl = _sigmoid(ln)
        dln = dyc * (sl * (1.0 + ln * (1.0 - sl)))
        small_ref[5:6, :] += _colsum(dln * nrm)
        small_ref[6:7, :] += _colsum(dln)
        dn = dln * clg_ref[...]
        dhc = rstd * (dn - jnp.mean(dn, axis=-1, keepdims=True) - nrm * jnp.mean(dn * nrm, axis=-1, keepdims=True))
        small_ref[4:5, :] += _colsum(dhc)
        dhbuf[0:T, :] = dhc
        _phase_copies(hbuf, rot_h)
        _phase_copies(dhbuf, rot_d)
        dhg = None
        for k in range(CF_TAPS):
            small_ref[16 + k:17 + k, :] += _colsum(dhc * _tap(hbuf, rot_h, HALO - (CF_TAPS - 1) + k))
            v = dww_ref[k:k + 1, :] * _tap(dhbuf, rot_d, CF_TAPS - 1 - k)
            dhg = v if dhg is None else dhg + v
        dzcv = dhg * sgc
        dzcg = dhg * zcv * sgc * (1.0 - sgc)

        dyd = dy_ref[:, 3 * GW:4 * GW]
        yg = _gelu(yy)
        v = _dot(yg, wglu_ref[...]) + bglu_ref[...]
        sg = _sigmoid(v)
        dv = dyd * yg * sg * (1.0 - sg)
        small_ref[8:9, :] += _colsum(dv)
        dwglu_ref[...] += _dot_tn(yg, dv)
        g = (dyd * sg + _dot_nt(dv, wglu_ref[...])) * _gelu_grad(yy)
        small_ref[7:8, :] += _colsum(g * zs)
        dcblk_ref[...] += _dot_tn(xs_ref[...], g)
        a_scr[...] = _dot_nt(g, cblk_ref[...])
        _scan(a_scr, pw_ref, pwrev_ref, acarry[...], reverse=True)
        acarry[...] = a_scr[0:1, :]
        dzs = _dot_nt(a_scr[...], bblk_ref[...]) + ssd_ref[...] * g
        dbblk_ref[...] += _dot_tn(zs, a_scr[...])
        row = lax.broadcasted_iota(jnp.int32, (T, 128), 0)
        for j in range(XH // 128):
            lo, hi = j * 128, XH + j * 128
            first_r = xprev_ref[SUB - 1:SUB, lo:lo + 128] * has_prev
            first_i = xprev_ref[SUB - 1:SUB, hi:hi + 128] * has_prev
            pr = jnp.where(row == 0, first_r, pltpu.roll(xs_ref[:, lo:lo + 128], 1, 0))
            pi = jnp.where(row == 0, first_i, pltpu.roll(xs_ref[:, hi:hi + 128], 1, 0))
            ar, ai = a_scr[:, lo:lo + 128], a_scr[:, hi:hi + 128]
            dlam_ref[0:1, lo:lo + 128] += _colsum(ar * pr + ai * pi)
            dlam_ref[0:1, hi:hi + 128] += _colsum(ai * pr - ar * pi)

        parts = (dzh, dzb, dzc, dzp, dzcv, dzcg, dzs)
        for k, part in enumerate(parts):
            dz_ref[:, k * GW:(k + 1) * GW] = part.astype(dz_ref.dtype)
            vec_ref[0:1, k * GW:(k + 1) * GW] += _colsum(part)
        for ref in (dcbuf, dhbuf, eb1, eb2, eb4, eb8):
            ref[T:T + HALO, :] = ref[0:HALO, :]

    rev = lambda i: (nch - 1 - i, 0)

    def before(rows):
        return lambda i: (jnp.maximum((nch - 1 - i) * (T // rows) - 1, 0), 0)

    outs = [((s, INW), _MXU), ((8, INW), F32), ((48, GW), F32), ((GW, GW), F32), ((GW, GW), F32),
            ((GW, XW), F32), ((XW, GW), F32), ((8, XW), F32)]
    return _call(
        body, name=name, grid=(nch,),
        out_shape=[jax.ShapeDtypeStruct(sh, dt) for sh, dt in outs],
        in_specs=[pl.BlockSpec((T, INW), rev), pl.BlockSpec((HALO, INW), before(HALO)), pl.BlockSpec((T, D), rev),
                  pl.BlockSpec((T, XW), rev), pl.BlockSpec((SUB, XW), before(SUB)), pl.BlockSpec((T, 4 * GW), rev)]
        + [_sel(p, l) for p in params],
        out_specs=[pl.BlockSpec((T, INW), rev)] + [_full(sh) for sh, _ in outs[1:]],
        scratch=[pltpu.VMEM((HALO + T, GW), F32)] * 8
        + [pltpu.VMEM((SUB - 1, HALO + T, GW), F32)] * 2 + [pltpu.VMEM((1, XW), F32), pltpu.VMEM((T, XW), F32)],
        args=(z, z, dy, xs, xs, aux, *params))


def _blockdiag(m):
    *lead, g, h, p = m.shape
    eye = jnp.eye(g, dtype=m.dtype)
    return (m[..., :, :, None, :] * eye[:, None, :, None]).reshape(*lead, g * h, g * p)


def _blockdiag_extract(f, h, p):
    *lead, r, _ = f.shape
    g = r // h
    eye = jnp.eye(g, dtype=f.dtype)
    return jnp.sum(f.reshape(*lead, g, h, g, p) * eye[:, None, :, None], axis=-2)


def _t(a):
    return jnp.swapaxes(a, -1, -2)


PACK_ROWS = 512


def _pack(arrs):
    flat = jnp.concatenate([a.reshape(-1).astype(F32) for a in arrs])
    quantum = PACK_ROWS * 128
    padded = -(-flat.shape[0] // quantum) * quantum
    return jnp.pad(flat, (0, padded - flat.shape[0])).reshape(-1, 128)


def _unpack(buf, shapes, lead=()):
    flat = buf.reshape(lead + (-1,))
    out, off = [], 0
    for shp in shapes:
        n = math.prod(shp)
        out.append(flat[..., off:off + n].reshape(lead + tuple(shp)))
        off += n
    return out


def _cols_to_full(g):
    n, l, r, c = g.shape
    return g.transpose(1, 2, 0, 3).reshape(l, r, n * c)


def _rows_to_full(g):
    n, l, r, c = g.shape
    return g.transpose(1, 0, 2, 3).reshape(l, n * r, c)


REPLICATED = ['b_ada', 'b_in', 'pool_w', 'pool_scale', 'cf_dw_b', 'cf_ln_g', 'cf_ln_b', 'ssm_lam_re', 'ssm_lam_im',
              'ssm_log_dt', 'ssm_b_re', 'ssm_b_im', 'ssm_c_re', 'ssm_c_im', 'ssm_d', 'ssm_b_glu', 'ln1_g', 'ln1_b',
              'ln2_g', 'ln2_b']
SMALL_SHARDED = ['sc_w', 'cf_dw_w', 'ssm_w_glu']
COL_SHARDED = ['w_in', 'w_gate', 'w_up']
ROW_SHARDED = ['w_o', 'w_down']
BIG = COL_SHARDED + ROW_SHARDED
WEIGHTS = ['w_ada', 'b_ada', 'w_in', 'b_in', 'sc_w', 'pool_w', 'pool_scale', 'cf_dw_w', 'cf_dw_b', 'cf_ln_g',
           'cf_ln_b', 'ssm_lam_re', 'ssm_lam_im', 'ssm_log_dt', 'ssm_b_re', 'ssm_b_im', 'ssm_c_re', 'ssm_c_im',
           'ssm_d', 'ssm_w_glu', 'ssm_b_glu', 'w_o', 'ln1_g', 'ln1_b', 'w_gate', 'w_up', 'w_down', 'ln2_g', 'ln2_b']
SH1, SC1, G1, SH2, SC2, G2 = range(6)


def _train_step(p):
    xi, yi, ci = _me()
    me = _flat((xi, yi, ci))
    s = p['x'].shape[1]
    xs = p['x'].reshape(s, D)
    target = p['loss_target'].reshape(s, D)
    vec3 = lambda a: a.reshape(DEPTH, 1, -1)

    def shard(n, l):
        w = p[n][l]
        return (w.T if n in COL_SHARDED else w).astype(_MXU)

    def whole(g):
        return g.reshape(-1, g.shape[-1])

    small_shapes = [p[n].shape for n in SMALL_SHARDED] + [p['c'].shape]
    gathered = _allgather([_pack([p[n] for n in SMALL_SHARDED] + [p['c']])] + [shard(n, 0) for n in BIG],
                          "gather_first")
    scw_g, dww_g, wglu_g, c_g = _unpack(gathered[0], small_shapes, lead=(NDEV,))
    full = {'sc_w': _cols_to_full(scw_g), 'cf_dw_w': _cols_to_full(dww_g), 'ssm_w_glu': _rows_to_full(wglu_g)}
    wts = [dict(zip(BIG, (whole(g) for g in gathered[1:])))] + [dict() for _ in range(DEPTH - 1)]
    c_all = c_g.reshape(NDEV, D)

    ncol = p['w_ada'].shape[2]
    b_cols = lax.dynamic_slice_in_dim(p['b_ada'], me * ncol, ncol, axis=1).reshape(DEPTH, 1, ncol)
    cond, mod_cols = _ada_fwd(c_all, p['w_ada'], b_cols, "ada_fwd")
    (mod_x,) = _alltoall([mod_cols.transpose(1, 0, 2)], "mod_exchange")
    mod = mod_x.transpose(1, 0, 2).reshape(DEPTH, 6, 1, D)

    ssm_in = (p['ssm_lam_re'], p['ssm_lam_im'], p['ssm_log_dt'].reshape(DEPTH, NGRP, 1),
              _t(p['ssm_b_re']), _t(p['ssm_b_im']))
    pwr, pwi, bbr, bbi = _ssm_prep(*ssm_in, "ssm_prep")
    pw = jnp.concatenate([pwr.reshape(DEPTH, SUB, XH), pwi.reshape(DEPTH, SUB, XH)], axis=2)
    mp = dict(
        scw=full['sc_w'], pblk=_blockdiag(p['pool_w']).astype(_MXU), pscale=vec3(p['pool_scale']),
        dww=full['cf_dw_w'], dwb=vec3(p['cf_dw_b']), clg=vec3(p['cf_ln_g']), clb=vec3(p['cf_ln_b']),
        bblk=jnp.concatenate([_blockdiag(bbr), _blockdiag(bbi)], axis=2).astype(_MXU),
        cblk=jnp.concatenate([_blockdiag(_t(p['ssm_c_re'])), -_blockdiag(_t(p['ssm_c_im']))], axis=1).astype(_MXU),
        pw=pw, pwrev=pw[:, ::-1], ssd=vec3(p['ssm_d']), wglu=full['ssm_w_glu'].astype(_MXU),
        bglu=vec3(p['ssm_b_glu']))
    b_in, ln1_g, ln1_b, ln2_g, ln2_b = (vec3(p[n]) for n in ('b_in', 'ln1_g', 'ln1_b', 'ln2_g', 'ln2_b'))

    saved = []
    for l in range(DEPTH):
        w = wts[l]
        nxt = l + 1 < DEPTH
        ahead = (lambda *names: (GATHER, [shard(n, l + 1) for n in names])) if nxt else (lambda *names: None)

        def take(res, n_own, *names):
            for n, g in zip(names, res[n_own:]):
                wts[l + 1][n] = whole(g)
            return res[:n_own]

        (z,) = take(_modulated_matmul(xs, mod, l, SC1, SH1, [w['w_in']], b_in, TM, [F32], lambda a: (a,),
                                      "in_proj", comm=ahead('w_in')), 1, 'w_in')
        ycat, states, aux = take(_mixer_fwd(z, mp, l, "mixer_fwd", comm=ahead('w_gate', 'w_up')), 3,
                                 'w_gate', 'w_up')
        y, u1, x1 = take(_proj_residual_ln(ycat, w['w_o'], l, xs, mod, G1, ln1_g, ln1_b, "o_proj_ln",
                                           comm=ahead('w_o')), 3, 'w_o')
        gate, up, act = take(_modulated_matmul(x1, mod, l, SC2, SH2, [w['w_gate'], w['w_up']], None, TM_WIDE,
                                               [_MXU, _MXU, _MXU], _swiglu, "gate_up",
                                               comm=ahead('w_down')), 3, 'w_down')
        f, u2, x2 = _proj_residual_ln(act, w['w_down'], l, x1, mod, G2, ln2_g, ln2_b, "down_proj_ln")
        saved.append(dict(x=xs, z=z, ycat=ycat, states=states, aux=aux, y=y, u1=u1, x1=x1, gate=gate, up=up,
                          act=act, f=f, u2=u2))
        xs = x2

    dx, sq = _loss_head(xs, target, "loss_head")
    loss = lax.psum(sq[0, 0] * (0.5 / D), AXES)

    recv = {n: [None] * DEPTH for n in BIG}
    small_g = {n: [None] * DEPTH for n in ('ln1_g', 'ln1_b', 'ln2_g', 'ln2_b')}
    dmod = [None] * DEPTH
    mixer_out = [None] * DEPTH
    pending = None
    top = DEPTH - 1
    du2, df, acc_top = _ln_bwd(dx, saved[top]['u2'], saved[top]['f'], ln2_g, top, mod, G2, "ln2_bwd")
    small_g['ln2_g'][top], small_g['ln2_b'][top], dg2 = acc_top[0], acc_top[1], acc_top[2]
    for l in reversed(range(DEPTH)):
        sv, w = saved[l], wts[l]
        if pending is None:
            dgate, dup = _swiglu_bwd(df, w['w_down'], sv['gate'], sv['up'], "swiglu_bwd")
        else:
            lp, g_o, g_in = pending
            dgate, dup, recv['w_o'][lp], recv['w_in'][lp] = _swiglu_bwd(
                df, w['w_down'], sv['gate'], sv['up'], "swiglu_bwd", comm=(EXCHANGE, [g_o, g_in]))
        (g_down,) = _weight_grad(sv['act'], df, "dw_down")
        (g_gate,) = _weight_grad(dgate, sv['x1'], "dw_gate", mod=(mod, l, SC2, SH2))
        (g_up,) = _weight_grad(dup, sv['x1'], "dw_up", mod=(mod, l, SC2, SH2))
        du1, dyb, acc_a, recv['w_down'][l], recv['w_gate'][l] = _input_grad_ln_bwd(
            [dgate, dup], [w['w_gate'], w['w_up']], du2, sv['x1'], (mod, l, SC2), sv['u1'], sv['y'], (ln1_g, l),
            (mod, l, G1), TM_WIDE, "dh2_ln1_bwd", comm=(EXCHANGE, [g_down, g_gate]))
        dycat = _matmul_nt(dyb, w['w_o'], "dycat")
        (g_o,) = _weight_grad(sv['ycat'], dyb, "dw_o")
        dz, *mixer_out[l] = _mixer_bwd(sv['z'], dycat, sv['states'], sv['aux'], mp, l, "mixer_bwd")
        if l > 0:
            du2, df, acc_b, recv['w_up'][l] = _input_grad_ln_bwd(
                [dz], [w['w_in']], du1, sv['x'], (mod, l, SC1), saved[l - 1]['u2'], saved[l - 1]['f'],
                (ln2_g, l - 1), (mod, l - 1, G2), TM_WIDE, "dh1_ln2_bwd", comm=(EXCHANGE, [g_up]))
        else:
            dx, acc_b, recv['w_up'][l] = _input_grad([dz], [w['w_in']], l, du1, sv['x'], mod, SC1, TM, "dh1",
                                                     comm=(EXCHANGE, [g_up]))
        (g_in,) = _weight_grad(dz, sv['x'], "dw_in", mod=(mod, l, SC1, SH1))
        pending = (l, g_o, g_in)
        dmod[l] = jnp.concatenate([acc_b[0], acc_b[1], acc_a[4], acc_a[0], acc_a[1], dg2])
        small_g['ln1_g'][l], small_g['ln1_b'][l] = acc_a[2], acc_a[3]
        if l > 0:
            small_g['ln2_g'][l - 1], small_g['ln2_b'][l - 1], dg2 = acc_b[2], acc_b[3], acc_b[4]
    lp, g_o, g_in = pending
    recv['w_o'][lp], recv['w_in'][lp] = _alltoall([g_o, g_in], "exchange_last")

    partial = {n: jnp.stack(v) for n, v in small_g.items()}
    partial['b_ada'] = jnp.stack(dmod)
    vec, small, dpblk, dwglu, dbblk, dcblk, dlam = (jnp.stack(t) for t in zip(*mixer_out))
    partial.update(
        b_in=vec[:, 0], sc_w=small[:, 0:SC_TAPS], pool_scale=small[:, 3], cf_dw_b=small[:, 4], cf_ln_g=small[:, 5],
        cf_ln_b=small[:, 6], ssm_d=small[:, 7], ssm_b_glu=small[:, 8], cf_dw_w=small[:, 16:16 + CF_TAPS],
        pool_w=_blockdiag_extract(dpblk, 64, 64), ssm_w_glu=dwglu,
        ssm_c_re=_t(_blockdiag_extract(dcblk[:, :XH], NSTATE, NCH)),
        ssm_c_im=-_t(_blockdiag_extract(dcblk[:, XH:], NSTATE, NCH)))
    dlr, dli, dlog, dbr, dbi = _ssm_param_bwd(
        *ssm_in, dlam[:, 0, :XH].reshape(DEPTH, NGRP, NSTATE), dlam[:, 0, XH:].reshape(DEPTH, NGRP, NSTATE),
        _blockdiag_extract(dbblk[:, :, :XH], NCH, NSTATE), _blockdiag_extract(dbblk[:, :, XH:], NCH, NSTATE),
        "ssm_param_bwd")
    partial.update(ssm_lam_re=dlr, ssm_lam_im=dli, ssm_log_dt=dlog.reshape(DEPTH, NGRP), ssm_b_re=_t(dbr),
                   ssm_b_im=_t(dbi))

    out = {'loss': loss, 'grad_x': dx.reshape(1, s, D)}

    def emit(n, g, delta, nm, nv):
        shp = p[n].shape
        out['grad_' + n], out['delta_' + n] = g.reshape(shp), delta.reshape(shp)
        out['new_m_' + n], out['new_v_' + n] = nm.reshape(shp), nv.reshape(shp)

    names_a = REPLICATED + SMALL_SHARDED
    shapes_a = [partial[n].shape for n in names_a]
    zeros_like_full = lambda n: jnp.zeros(partial[n].shape, F32)
    (parts_a,) = _allgather([_pack([partial[n] for n in names_a])], "gather_small_grads")
    packs = [_pack([p[pre + n] for n in REPLICATED] + [zeros_like_full(n) for n in SMALL_SHARDED])
             for pre in ('', 'm_', 'v_')]
    res_a = _adamw(parts_a, *packs, PACK_ROWS, "adamw_small")
    res_a = [_unpack(r, shapes_a) for r in res_a]
    for k, n in enumerate(REPLICATED):
        emit(n, *[r[k] for r in res_a])
    reduced = dict(zip(names_a, res_a[0]))

    blocks = []
    for n in SMALL_SHARDED:
        axis = 1 if n == 'ssm_w_glu' else 2
        width = p[n].shape[axis]
        blocks.append(lax.dynamic_slice_in_dim(reduced[n], me * width, width, axis=axis))
    shapes_b = [b.shape for b in blocks]
    packs = [_pack([p[pre + n] for n in SMALL_SHARDED]) for pre in ('', 'm_', 'v_')]
    res_b = _adamw(_pack(blocks)[None], *packs, PACK_ROWS, "adamw_small_sharded")
    res_b = [_unpack(r, shapes_b) for r in res_b]
    for k, n in enumerate(SMALL_SHARDED):
        emit(n, *[r[k] for r in res_b])

    dmod_all = _unpack(parts_a, shapes_a, lead=(NDEV,))[0]
    dmod_cols = lax.dynamic_slice_in_dim(dmod_all, me * ncol, ncol, axis=2).transpose(1, 0, 2)
    g_ada = _ada_bwd(cond, dmod_cols, "ada_bwd")
    flat2 = lambda a: a.reshape(-1, a.shape[-1])
    emit('w_ada', *_adamw(flat2(g_ada)[None], flat2(p['w_ada']), flat2(p['m_w_ada']), flat2(p['v_w_ada']),
                          512, "adamw_w_ada"))

    for n in BIG:
        tr = recv[n][0].shape[1] // 2
        if n in ROW_SHARDED:
            emit(n, *_adamw_layers(recv[n], flat2(p[n]), flat2(p['m_' + n]), flat2(p['v_' + n]), tr, "adamw_" + n))
        else:
            g = flat2(_t(_reduce_layers(recv[n], tr, "sum_" + n)))
            emit(n, *_adamw(g[None], flat2(p[n]), flat2(p['m_' + n]), flat2(p['v_' + n]), 512, "adamw_" + n))
    return out


def kernel(x, c, w_ada, b_ada, w_in, b_in, sc_w, pool_w, pool_scale, cf_dw_w, cf_dw_b, cf_ln_g, cf_ln_b, ssm_lam_re, ssm_lam_im, ssm_log_dt, ssm_b_re, ssm_b_im, ssm_c_re, ssm_c_im, ssm_d, ssm_w_glu, ssm_b_glu, w_o, ln1_g, ln1_b, w_gate, w_up, w_down, ln2_g, ln2_b, loss_target, m_w_ada, m_b_ada, m_w_in, m_b_in, m_sc_w, m_pool_w, m_pool_scale, m_cf_dw_w, m_cf_dw_b, m_cf_ln_g, m_cf_ln_b, m_ssm_lam_re, m_ssm_lam_im, m_ssm_log_dt, m_ssm_b_re, m_ssm_b_im, m_ssm_c_re, m_ssm_c_im, m_ssm_d, m_ssm_w_glu, m_ssm_b_glu, m_w_o, m_ln1_g, m_ln1_b, m_w_gate, m_w_up, m_w_down, m_ln2_g, m_ln2_b, v_w_ada, v_b_ada, v_w_in, v_b_in, v_sc_w, v_pool_w, v_pool_scale, v_cf_dw_w, v_cf_dw_b, v_cf_ln_g, v_cf_ln_b, v_ssm_lam_re, v_ssm_lam_im, v_ssm_log_dt, v_ssm_b_re, v_ssm_b_im, v_ssm_c_re, v_ssm_c_im, v_ssm_d, v_ssm_w_glu, v_ssm_b_glu, v_w_o, v_ln1_g, v_ln1_b, v_w_gate, v_w_up, v_w_down, v_ln2_g, v_ln2_b):
    out = _train_step(dict(locals()))
    return (out['loss'], out['grad_x'], *[out[pre + n] for pre in ('grad_', 'delta_', 'new_m_', 'new_v_')
                                          for n in WEIGHTS])
```

```python
import math

import jax
import jax.numpy as jnp
from jax import lax
from jax.experimental import pallas as pl
from jax.experimental.pallas import tpu as pltpu

F32 = jnp.float32
_MXU = jnp.bfloat16

D = 1024
GW = 256
INW = 7 * GW
DFF = 2816
DEPTH = 4
NDEV = 8
SC_TAPS = 3
CF_TAPS = 31
NGRP = 16
NCH = 16
NSTATE = 64
XH = NGRP * NSTATE
XW = 2 * XH
ALPHA = (2 * DEPTH) ** 0.25
LN_EPS = 1e-5
T = 256
SUB = 8
HALO = 32
TM = 512
TM_WIDE = 256
VMEM_LIMIT = 56 * 1024 * 1024

ADAM_LR = 0.001
ADAM_B1 = 0.9
ADAM_B2 = 0.999
ADAM_EPS = 1e-08
ADAM_WD = 0.01
ADAM_STEP = 10

MESH = pl.DeviceIdType.MESH
ANY = pl.BlockSpec(memory_space=pl.ANY)
AXES = ("x", "y", "c")


def _full(shape):
    nd = len(shape)
    return pl.BlockSpec(shape, lambda *_: (0,) * nd)


def _sel(arr, *idx):
    tail = arr.shape[len(idx):]
    return pl.BlockSpec((None,) * len(idx) + tail, lambda *_: idx + (0,) * len(tail))


def _dot(a, b):
    return jnp.dot(a.astype(_MXU), b.astype(_MXU), preferred_element_type=F32)


def _dot_nt(a, b):
    return lax.dot_general(a.astype(_MXU), b.astype(_MXU), (((1,), (1,)), ((), ())), preferred_element_type=F32)


def _dot_tn(a, b):
    return lax.dot_general(a.astype(_MXU), b.astype(_MXU), (((0,), (0,)), ((), ())), preferred_element_type=F32)


def _sigmoid(x):
    return 1.0 / (1.0 + jnp.exp(-x))


def _colsum(x):
    return jnp.sum(x, axis=0, keepdims=True)


def _me():
    return lax.axis_index("x"), lax.axis_index("y"), lax.axis_index("c")


def _flat(p):
    return 4 * p[0] + 2 * p[1] + p[2]


def _allgather(arrays, name):
    n = len(arrays)

    def body(*refs):
        ins, outs = refs[:n], refs[n:2 * n]
        send_sems, recv_sems, local_sems = refs[2 * n:]
        x, y, c = _me()
        me, sibling = (x, y, c), (x, y, 1 - c)
        chips = [(1 - x, y), (x, 1 - y), (1 - x, 1 - y)]

        def copy(a, k, block, to, src=None):
            dst = outs[a].at[_flat(block)]
            return pltpu.make_async_remote_copy(
                src_ref=dst if src is None else src, dst_ref=dst,
                send_sem=send_sems.at[a, k], recv_sem=recv_sems.at[a, k],
                device_id=to, device_id_type=MESH)

        started = []
        for a in range(n):
            mine = pltpu.make_async_copy(ins[a], outs[a].at[_flat(me)], local_sems.at[a])
            mine.start()
            started.append(mine)
        first = []
        for a in range(n):
            first.append(copy(a, 0, me, sibling, src=ins[a]))
            first += [copy(a, 1 + j, me, (*chip, c), src=ins[a]) for j, chip in enumerate(chips)]
        for cp in first:
            cp.start()
        passed = []
        for a in range(n):
            for j, chip in enumerate(chips):
                copy(a, 1 + j, (*chip, c), me).wait_recv()
                fwd = copy(a, 4 + j, (*chip, c), sibling)
                fwd.start()
                passed.append(fwd)
        for a in range(n):
            copy(a, 0, sibling, me).wait_recv()
            for j, chip in enumerate(chips):
                copy(a, 4 + j, (*chip, 1 - c), me).wait_recv()
        for cp in first + passed:
            cp.wait_send()
        for cp in started:
            cp.wait()

    return pl.pallas_call(
        body, name=name,
        out_shape=[jax.ShapeDtypeStruct((NDEV,) + a.shape, a.dtype) for a in arrays],
        in_specs=[ANY] * n, out_specs=[ANY] * n,
        scratch_shapes=[pltpu.SemaphoreType.DMA((n, 7)), pltpu.SemaphoreType.DMA((n, 7)),
                        pltpu.SemaphoreType.DMA((n,))],
    )(*arrays)


GATHER, EXCHANGE = "gather", "exchange"


def _direct_copies(kind, ins, outs, send_sems, recv_sems, local_sems, start):
    x, y, c = _me()
    mine = _flat((x, y, c))
    for a in range(len(ins)):
        own = ins[a] if kind == GATHER else ins[a].at[mine]
        loc = pltpu.make_async_copy(own, outs[a].at[mine], local_sems.at[a])
        if start:
            loc.start()
        for k in range(1, NDEV):
            peer = (lax.rem(x + ((k >> 2) & 1), 2), lax.rem(y + ((k >> 1) & 1), 2), lax.rem(c + (k & 1), 2))
            cp = pltpu.make_async_remote_copy(
                src_ref=ins[a] if kind == GATHER else ins[a].at[_flat(peer)], dst_ref=outs[a].at[mine],
                send_sem=send_sems.at[a, k - 1], recv_sem=recv_sems.at[a, k - 1],
                device_id=peer, device_id_type=MESH)
            if start:
                cp.start()
            else:
                cp.wait_send()
                cp.wait_recv()
        if not start:
            loc.wait()


def _comm_out_shapes(comm):
    kind, arrays = comm
    return [jax.ShapeDtypeStruct(((NDEV,) + a.shape) if kind == GATHER else a.shape, a.dtype) for a in arrays]


def _call(body, *, name, grid, in_specs, out_specs, out_shape, args, scratch=(), comm=None):
    params = pltpu.CompilerParams(dimension_semantics=("arbitrary",) * len(grid), vmem_limit_bytes=VMEM_LIMIT)
    if comm is None:
        return pl.pallas_call(body, name=name, grid=grid, in_specs=in_specs, out_specs=out_specs,
                              out_shape=out_shape, scratch_shapes=list(scratch), compiler_params=params)(*args)
    kind, arrays = comm
    n, n_in, n_out, n_scr = len(arrays), len(in_specs), len(out_specs), len(scratch)

    def wrapped(*refs):
        ins, cin = refs[:n_in], refs[n_in:n_in + n]
        outs, cout = refs[n_in + n:n_in + n + n_out], refs[n_in + n + n_out:n_in + 2 * n + n_out]
        scr = refs[n_in + 2 * n + n_out:]
        user_scr, sems = scr[:n_scr], scr[n_scr:]
        ids = [pl.program_id(k) for k in range(len(grid))]
        first, last = ids[0] == 0, ids[0] == grid[0] - 1
        for k in range(1, len(grid)):
            first, last = first & (ids[k] == 0), last & (ids[k] == grid[k] - 1)

        @pl.when(first)
        def _():
            _direct_copies(kind, cin, cout, *sems, start=True)

        body(*ins, *outs, *user_scr)

        @pl.when(last)
        def _():
            _direct_copies(kind, cin, cout, *sems, start=False)

    sems = [pltpu.SemaphoreType.DMA((n, 7)), pltpu.SemaphoreType.DMA((n, 7)), pltpu.SemaphoreType.DMA((n,))]
    res = pl.pallas_call(
        wrapped, name=name, grid=grid, in_specs=list(in_specs) + [ANY] * n, out_specs=list(out_specs) + [ANY] * n,
        out_shape=list(out_shape) + _comm_out_shapes(comm), scratch_shapes=list(scratch) + sems,
        compiler_params=params)(*args, *arrays)
    return res


def _alltoall(arrays, name):
    n = len(arrays)

    def body(*refs):
        _direct_copies(EXCHANGE, refs[:n], refs[n:2 * n], *refs[2 * n:], start=True)
        _direct_copies(EXCHANGE, refs[:n], refs[n:2 * n], *refs[2 * n:], start=False)

    return pl.pallas_call(
        body, name=name,
        out_shape=[jax.ShapeDtypeStruct(a.shape, a.dtype) for a in arrays],
        in_specs=[ANY] * n, out_specs=[ANY] * n,
        scratch_shapes=[pltpu.SemaphoreType.DMA((n, 7)), pltpu.SemaphoreType.DMA((n, 7)),
                        pltpu.SemaphoreType.DMA((n,))],
    )(*arrays)


def _modulated_matmul(x, mod, l, ksc, ksh, wts, bias, tm, out_dtypes, epilogue, name, comm=None):
    s, k = x.shape
    nw = len(wts)
    n = wts[0].shape[0]
    nb = 0 if bias is None else 1

    def body(*refs):
        x_ref, sc_ref, sh_ref = refs[:3]
        w_refs = refs[3:3 + nw]
        b_refs = refs[3 + nw:3 + nw + nb]
        o_refs = refs[3 + nw + nb:]
        h = (x_ref[...] * (1.0 + sc_ref[...]) + sh_ref[...]).astype(_MXU)
        prods = [lax.dot_general(h, w[...], (((1,), (1,)), ((), ())), preferred_element_type=F32) for w in w_refs]
        if nb:
            prods[0] = prods[0] + b_refs[0][...]
        for o, v in zip(o_refs, epilogue(*prods)):
            o[...] = v.astype(o.dtype)

    return _call(
        body, name=name, grid=(s // tm,),
        out_shape=[jax.ShapeDtypeStruct((s, n), dt) for dt in out_dtypes],
        in_specs=[pl.BlockSpec((tm, k), lambda i: (i, 0)), _sel(mod, l, ksc), _sel(mod, l, ksh)]
        + [_full(w.shape) for w in wts] + ([_sel(bias, l)] if nb else []),
        out_specs=[pl.BlockSpec((tm, n), lambda i: (i, 0))] * len(out_dtypes),
        args=(x, mod, mod, *wts, *([bias] if nb else [])), comm=comm)


def _swiglu(gate, up):
    return gate, up, gate * _sigmoid(gate) * up


def _proj_residual_ln(a, w, l, xres, mod, kg, lng, lnb, name, comm=None):
    s, k = a.shape
    n = w.shape[1]

    def body(a_ref, w_ref, x_ref, g_ref, lg_ref, lb_ref, r_ref, u_ref, o_ref):
        r = jnp.dot(a_ref[...], w_ref[...], preferred_element_type=F32)
        u = ALPHA * x_ref[...] + (1.0 + g_ref[...]) * r
        mu = jnp.mean(u, axis=-1, keepdims=True)
        xc = u - mu
        var = jnp.mean(xc * xc, axis=-1, keepdims=True)
        r_ref[...] = r
        u_ref[...] = u
        o_ref[...] = xc * lax.rsqrt(var + LN_EPS) * lg_ref[...] + lb_ref[...]

    row = pl.BlockSpec((TM, n), lambda i: (i, 0))
    return _call(
        body, name=name, grid=(s // TM,),
        out_shape=[jax.ShapeDtypeStruct((s, n), F32)] * 3,
        in_specs=[pl.BlockSpec((TM, k), lambda i: (i, 0)), _full(w.shape), row, _sel(mod, l, kg), _sel(lng, l),
                  _sel(lnb, l)],
        out_specs=[row, row, row], args=(a, w, xres, mod, lng, lnb), comm=comm)


def _loss_ln_bwd(xo, target, u, res, lng, l, mod, kg, name):
    s, n = u.shape

    def body(x_ref, t_ref, u_ref, r_ref, lg_ref, g_ref, du_ref, dr_ref, acc_ref):
        @pl.when(pl.program_id(0) == 0)
        def _():
            acc_ref[...] = jnp.zeros_like(acc_ref)

        uu = u_ref[...]
        mu = jnp.mean(uu, axis=-1, keepdims=True)
        xc = uu - mu
        var = jnp.mean(xc * xc, axis=-1, keepdims=True)
        rstd = lax.rsqrt(var + LN_EPS)
        xh = xc * rstd
        err = x_ref[...] - t_ref[...]
        acc_ref[3:4, :] += jnp.sum(err * err)
        dy = err * (1.0 / n)
        dxh = dy * lg_ref[...]
        du = rstd * (dxh - jnp.mean(dxh, axis=-1, keepdims=True) - xh * jnp.mean(dxh * xh, axis=-1, keepdims=True))
        du_ref[...] = du
        dr_ref[...] = (du * (1.0 + g_ref[...])).astype(dr_ref.dtype)
        acc_ref[0:1, :] += _colsum(dy * xh)
        acc_ref[1:2, :] += _colsum(dy)
        acc_ref[2:3, :] += _colsum(du * r_ref[...])

    row = pl.BlockSpec((TM, n), lambda i: (i, 0))
    return _call(
        body, name=name, grid=(s // TM,),
        out_shape=[jax.ShapeDtypeStruct((s, n), F32), jax.ShapeDtypeStruct((s, n), _MXU),
                   jax.ShapeDtypeStruct((8, n), F32)],
        in_specs=[row, row, row, row, _sel(lng, l), _sel(mod, l, kg)], out_specs=[row, row, _full((8, n))],
        args=(xo, target, u, res, lng, mod))


def _swiglu_bwd(dr, w_down, gate, up, name, comm=None):
    s, n = dr.shape
    f = w_down.shape[0]

    def body(dr_ref, w_ref, g_ref, u_ref, dg_ref, du_ref):
        dp = lax.dot_general(dr_ref[...], w_ref[...], (((1,), (1,)), ((), ())), preferred_element_type=F32)
        g = g_ref[...].astype(F32)
        sg = _sigmoid(g)
        dg_ref[...] = (dp * u_ref[...].astype(F32) * (sg * (1.0 + g * (1.0 - sg)))).astype(dg_ref.dtype)
        du_ref[...] = (dp * (g * sg)).astype(du_ref.dtype)

    tile = pl.BlockSpec((TM_WIDE, f), lambda i: (i, 0))
    return _call(
        body, name=name, grid=(s // TM_WIDE,),
        out_shape=[jax.ShapeDtypeStruct((s, f), _MXU)] * 2,
        in_specs=[pl.BlockSpec((TM_WIDE, n), lambda i: (i, 0)), _full(w_down.shape), tile, tile],
        out_specs=[tile, tile], args=(dr, w_down, gate, up), comm=comm)


def _input_grad(acts, wts, l, du, xin, mod, ksc, tm, name, comm=None):
    s = acts[0].shape[0]
    n = wts[0].shape[1]
    na = len(acts)

    def body(*refs):
        a_refs, w_refs = refs[:na], refs[na:2 * na]
        du_ref, x_ref, sc_ref, dx_ref, acc_ref = refs[2 * na:]

        @pl.when(pl.program_id(0) == 0)
        def _():
            acc_ref[...] = jnp.zeros_like(acc_ref)

        dh = None
        for a, w in zip(a_refs, w_refs):
            t = jnp.dot(a[...], w[...], preferred_element_type=F32)
            dh = t if dh is None else dh + t
        dx_ref[...] = ALPHA * du_ref[...] + dh * (1.0 + sc_ref[...])
        acc_ref[0:1, :] += _colsum(dh)
        acc_ref[1:2, :] += _colsum(dh * x_ref[...])

    row = pl.BlockSpec((tm, n), lambda i: (i, 0))
    return _call(
        body, name=name, grid=(s // tm,),
        out_shape=[jax.ShapeDtypeStruct((s, n), F32), jax.ShapeDtypeStruct((8, n), F32)],
        in_specs=[pl.BlockSpec((tm, a.shape[1]), lambda i: (i, 0)) for a in acts]
        + [_full(w.shape) for w in wts] + [row, row, _sel(mod, l, ksc)],
        out_specs=[row, _full((8, n))], args=(*acts, *wts, du, xin, mod), comm=comm)


def _input_grad_ln_bwd(acts, wts, du_in, xin, sc, u, res, lng, g, tm, name, comm=None):
    s = acts[0].shape[0]
    n = wts[0].shape[1]
    na = len(acts)

    def body(*refs):
        a_refs, w_refs = refs[:na], refs[na:2 * na]
        dui_ref, x_ref, sc_ref, u_ref, r_ref, lg_ref, g_ref, du_ref, dr_ref, acc_ref = refs[2 * na:]

        @pl.when(pl.program_id(0) == 0)
        def _():
            acc_ref[...] = jnp.zeros_like(acc_ref)

        dh = None
        for a, w in zip(a_refs, w_refs):
            t = jnp.dot(a[...], w[...], preferred_element_type=F32)
            dh = t if dh is None else dh + t
        dy = ALPHA * dui_ref[...] + dh * (1.0 + sc_ref[...])
        acc_ref[0:1, :] += _colsum(dh)
        acc_ref[1:2, :] += _colsum(dh * x_ref[...])
        uu = u_ref[...]
        mu = jnp.mean(uu, axis=-1, keepdims=True)
        xc = uu - mu
        var = jnp.mean(xc * xc, axis=-1, keepdims=True)
        rstd = lax.rsqrt(var + LN_EPS)
        xh = xc * rstd
        dxh = dy * lg_ref[...]
        du = rstd * (dxh - jnp.mean(dxh, axis=-1, keepdims=True) - xh * jnp.mean(dxh * xh, axis=-1, keepdims=True))
        du_ref[...] = du
        dr_ref[...] = (du * (1.0 + g_ref[...])).astype(dr_ref.dtype)
        acc_ref[2:3, :] += _colsum(dy * xh)
        acc_ref[3:4, :] += _colsum(dy)
        acc_ref[4:5, :] += _colsum(du * r_ref[...])

    row = pl.BlockSpec((tm, n), lambda i: (i, 0))
    return _call(
        body, name=name, grid=(s // tm,),
        out_shape=[jax.ShapeDtypeStruct((s, n), F32), jax.ShapeDtypeStruct((s, n), _MXU),
                   jax.ShapeDtypeStruct((8, n), F32)],
        in_specs=[pl.BlockSpec((tm, a.shape[1]), lambda i: (i, 0)) for a in acts]
        + [_full(w.shape) for w in wts] + [row, row, _sel(*sc), row, row, _sel(*lng), _sel(*g)],
        out_specs=[row, row, _full((8, n))],
        args=(*acts, *wts, du_in, xin, sc[0], u, res, lng[0], g[0]), comm=comm)


def _weight_grad(a, b, name, mod=None, comm=None):
    s, ka = a.shape
    nbc = b.shape[1]
    nm = 0 if mod is None else 2
    steps = s // TM

    def body(*refs):
        a_ref, b_ref = refs[:2]
        m_refs = refs[2:2 + nm]
        o_ref, acc = refs[2 + nm:]

        @pl.when(pl.program_id(0) == 0)
        def _():
            acc[...] = jnp.zeros_like(acc)

        bv = b_ref[...]
        if nm:
            bv = bv * (1.0 + m_refs[0][...]) + m_refs[1][...]
        acc[...] += lax.dot_general(a_ref[...], bv.astype(_MXU), (((0,), (0,)), ((), ())),
                                    preferred_element_type=F32)

        @pl.when(pl.program_id(0) == steps - 1)
        def _():
            o_ref[...] = acc[...].astype(o_ref.dtype)

    mspecs, margs = [], []
    if nm:
        marr, l, ksc, ksh = mod
        mspecs, margs = [_sel(marr, l, ksc), _sel(marr, l, ksh)], [marr, marr]
    res = _call(
        body, name=name, grid=(steps,),
        out_shape=[jax.ShapeDtypeStruct((ka, nbc), _MXU)],
        in_specs=[pl.BlockSpec((TM, ka), lambda t: (t, 0)), pl.BlockSpec((TM, nbc), lambda t: (t, 0))] + mspecs,
        out_specs=[_full((ka, nbc))], args=(a, b, *margs), scratch=[pltpu.VMEM((ka, nbc), F32)], comm=comm)
    return [res[0].reshape(NDEV, ka // NDEV, nbc)] + list(res[1:])


def _ada_fwd(c_all, w_ada, b_cols, name):
    cols = w_ada.shape[2]

    def body(c_ref, w_ref, b_ref, cond_ref, o_ref):
        cv = c_ref[...]
        cond = cv * _sigmoid(cv)

        @pl.when(pl.program_id(0) == 0)
        def _():
            cond_ref[...] = cond

        o_ref[...] = _dot(cond, w_ref[...]) + b_ref[...]

    return _call(
        body, name=name, grid=(DEPTH,),
        out_shape=[jax.ShapeDtypeStruct((NDEV, D), F32), jax.ShapeDtypeStruct((DEPTH, NDEV, cols), F32)],
        in_specs=[_full((NDEV, D)), pl.BlockSpec((None, D, cols), lambda l: (l, 0, 0)),
                  pl.BlockSpec((None, 1, cols), lambda l: (l, 0, 0))],
        out_specs=[_full((NDEV, D)), pl.BlockSpec((None, NDEV, cols), lambda l: (l, 0, 0))],
        args=(c_all, w_ada, b_cols))


def _ada_bwd(cond, dmod_cols, name):
    cols = dmod_cols.shape[2]

    def body(c_ref, d_ref, o_ref):
        o_ref[...] = _dot_tn(c_ref[...], d_ref[...])

    return _call(
        body, name=name, grid=(DEPTH,),
        out_shape=[jax.ShapeDtypeStruct((DEPTH, D, cols), F32)],
        in_specs=[_full((NDEV, D)), pl.BlockSpec((None, NDEV, cols), lambda l: (l, 0, 0))],
        out_specs=[pl.BlockSpec((None, D, cols), lambda l: (l, 0, 0))], args=(cond, dmod_cols))[0]


def _adam_math(w, g, m, v):
    m = ADAM_B1 * m + (1.0 - ADAM_B1) * g
    v = ADAM_B2 * v + (1.0 - ADAM_B2) * (g * g)
    m_hat = m / (1.0 - ADAM_B1 ** ADAM_STEP)
    v_hat = v / (1.0 - ADAM_B2 ** ADAM_STEP)
    delta = -ADAM_LR * (m_hat / (jnp.sqrt(v_hat) + ADAM_EPS) + ADAM_WD * w)
    return delta, m, v


def _sum_parts(p_ref):
    g = p_ref[0].astype(F32)
    for i in range(1, p_ref.shape[0]):
        g = g + p_ref[i].astype(F32)
    return g


def _adamw(parts, w, m, v, tr, name):
    r, c = w.shape
    nparts = parts.shape[0]

    def body(p_ref, w_ref, m_ref, v_ref, g_ref, d_ref, nm_ref, nv_ref):
        g = _sum_parts(p_ref)
        delta, nm, nv = _adam_math(w_ref[...], g, m_ref[...], v_ref[...])
        g_ref[...] = g
        d_ref[...] = delta
        nm_ref[...] = nm
        nv_ref[...] = nv

    blk = pl.BlockSpec((tr, c), lambda i: (i, 0))
    return _call(
        body, name=name, grid=(r // tr,),
        out_shape=[jax.ShapeDtypeStruct((r, c), F32)] * 4,
        in_specs=[pl.BlockSpec((nparts, tr, c), lambda i: (0, i, 0)), blk, blk, blk],
        out_specs=[blk] * 4, args=(parts, w, m, v))


def _layer_parts_specs(parts, tr):
    nparts, r, c = parts[0].shape
    last = r // tr - 1

    def spec(k):
        return pl.BlockSpec((nparts, tr, c),
                            lambda l, i: (0, jnp.where(l < k, 0, jnp.where(l == k, i, last)), 0))

    return [spec(k) for k in range(len(parts))]


def _for_layer(p_refs, fn):
    for k, ref in enumerate(p_refs):
        pl.when(pl.program_id(0) == k)(lambda ref=ref: fn(ref))


def _adamw_layers(parts, w, m, v, tr, name):
    nl = len(parts)
    _, r, c = parts[0].shape

    def body(*refs):
        p_refs = refs[:nl]
        w_ref, m_ref, v_ref, g_ref, d_ref, nm_ref, nv_ref = refs[nl:]

        def update(p_ref):
            g = _sum_parts(p_ref)
            delta, nm, nv = _adam_math(w_ref[...], g, m_ref[...], v_ref[...])
            g_ref[...] = g
            d_ref[...] = delta
            nm_ref[...] = nm
            nv_ref[...] = nv

        _for_layer(p_refs, update)

    blk = pl.BlockSpec((tr, c), lambda l, i: (l * (r // tr) + i, 0))
    return _call(
        body, name=name, grid=(nl, r // tr),
        out_shape=[jax.ShapeDtypeStruct((nl * r, c), F32)] * 4,
        in_specs=_layer_parts_specs(parts, tr) + [blk, blk, blk],
        out_specs=[blk] * 4, args=(*parts, w, m, v))


def _reduce_layers(parts, tr, name):
    nl = len(parts)
    _, r, c = parts[0].shape

    def body(*refs):
        g_ref = refs[nl]

        def reduce(p_ref):
            g_ref[...] = _sum_parts(p_ref)

        _for_layer(refs[:nl], reduce)

    return _call(
        body, name=name, grid=(nl, r // tr), out_shape=[jax.ShapeDtypeStruct((nl, r, c), F32)],
        in_specs=_layer_parts_specs(parts, tr),
        out_specs=[pl.BlockSpec((None, tr, c), lambda l, i: (l, i, 0))], args=tuple(parts))[0]


def _cmul(ar, ai, br, bi):
    return ar * br - ai * bi, ar * bi + ai * br


def _lam_bar(lr, li, log_dt):
    dt = jnp.exp(log_dt)
    mag = jnp.exp(lr * dt)
    return dt, mag * jnp.cos(li * dt), mag * jnp.sin(li * dt)


def _layer_spec(arr):
    tail = arr.shape[1:]
    return pl.BlockSpec((None,) + tail, lambda l: (l,) + (0,) * len(tail))


def _ssm_prep(lam_re, lam_im, log_dt, b_re, b_im, name):
    def body(lr_ref, li_ref, dt_ref, br_ref, bi_ref, pr_ref, pi_ref, or_ref, oi_ref):
        lr, li = lr_ref[...], li_ref[...]
        _, ar, ai = _lam_bar(lr, li, dt_ref[...])
        den = lr * lr + li * li
        fr, fi = _cmul(ar - 1.0, ai, lr / den, -li / den)
        obr, obi = _cmul(fr[:, None, :], fi[:, None, :], br_ref[...], bi_ref[...])
        or_ref[...] = obr
        oi_ref[...] = obi
        qr, qi = ar, ai
        for k in range(SUB):
            pr_ref[k] = qr
            pi_ref[k] = qi
            qr, qi = _cmul(qr, qi, ar, ai)

    nl, g, p = lam_re.shape
    h = b_re.shape[2]
    outs = [jax.ShapeDtypeStruct((nl, SUB, g, p), F32)] * 2 + [jax.ShapeDtypeStruct((nl, g, h, p), F32)] * 2
    args = (lam_re, lam_im, log_dt, b_re, b_im)
    return _call(body, name=name, grid=(nl,), out_shape=outs, in_specs=[_layer_spec(a) for a in args],
                 out_specs=[_layer_spec(o) for o in outs], args=args)


def _ssm_param_bwd(lam_re, lam_im, log_dt, b_re, b_im, dlr, dli, dbr, dbi, name):
    def body(lr_ref, li_ref, dt_ref, br_ref, bi_ref, gar_ref, gai_ref, gbr_ref, gbi_ref,
             olr_ref, oli_ref, odt_ref, obr_ref, obi_ref):
        lr, li = lr_ref[...], li_ref[...]
        dt, ar, ai = _lam_bar(lr, li, dt_ref[...])
        den = lr * lr + li * li
        ir, ii = lr / den, -li / den
        fr, fi = _cmul(ar - 1.0, ai, ir, ii)
        br, bi = br_ref[...], bi_ref[...]
        gbr, gbi = gbr_ref[...], gbi_ref[...]
        obr, obi = _cmul(fr[:, None, :], -fi[:, None, :], gbr, gbi)
        obr_ref[...] = obr
        obi_ref[...] = obi
        pr, pi = _cmul(br, -bi, gbr, gbi)
        gfr, gfi = jnp.sum(pr, axis=1), jnp.sum(pi, axis=1)
        t_r, t_i = _cmul(gfr, gfi, ir, -ii)
        gar, gai = gar_ref[...] + t_r, gai_ref[...] + t_i
        qr, qi = _cmul(fr, fi, ir, ii)
        l1r, l1i = _cmul(gfr, gfi, qr, -qi)
        l2r, l2i = _cmul(gar, gai, dt * ar, -dt * ai)
        olr_ref[...] = l2r - l1r
        oli_ref[...] = l2i - l1i
        mr, mi = _cmul(lr, li, ar, ai)
        dr, _ = _cmul(gar, gai, mr, -mi)
        odt_ref[...] = jnp.sum(dr, axis=1, keepdims=True) * dt

    nl, g, p = lam_re.shape
    h = b_re.shape[2]
    outs = ([jax.ShapeDtypeStruct((nl, g, p), F32)] * 2 + [jax.ShapeDtypeStruct((nl, g, 1), F32)]
            + [jax.ShapeDtypeStruct((nl, g, h, p), F32)] * 2)
    args = (lam_re, lam_im, log_dt, b_re, b_im, dlr, dli, dbr, dbi)
    return _call(body, name=name, grid=(nl,), out_shape=outs, in_specs=[_layer_spec(a) for a in args],
                 out_specs=[_layer_spec(o) for o in outs], args=args)


def _gelu(x):
    k = math.sqrt(2.0 / math.pi)
    return 0.5 * x * (1.0 + jnp.tanh(k * (x + 0.044715 * x * x * x)))


def _gelu_grad(x):
    k = math.sqrt(2.0 / math.pi)
    th = jnp.tanh(k * (x + 0.044715 * x * x * x))
    return 0.5 * (1.0 + th) + 0.5 * x * (1.0 - th * th) * k * (1.0 + 3 * 0.044715 * x * x)


def _pool_inv_count(chunk):
    t = lax.broadcasted_iota(jnp.int32, (T, GW), 0) + chunk * T + 1
    lane = lax.broadcasted_iota(jnp.int32, (T, GW), 1)
    win = jnp.where(lane < 64, 2, jnp.where(lane < 128, 4, jnp.where(lane < 192, 8, 16)))
    return 1.0 / jnp.minimum(t, win).astype(F32)


def _window_sums(v, bufs, base, step):
    sums = []
    for k, buf in enumerate(bufs):
        buf[base:base + T, :] = v
        v = v + buf[pl.ds(base + step * (1 << k), T), :]
        sums.append(v)
    lane = lax.broadcasted_iota(jnp.int32, (T, GW), 1)
    return jnp.where(lane < 64, sums[0], jnp.where(lane < 128, sums[1], jnp.where(lane < 192, sums[2], sums[3])))


def _phase_copies(buf, rot):
    n = HALO + T - SUB
    for p in range(1, SUB):
        rot[p - 1, 0:n, :] = buf[pl.ds(p, n), :]


def _tap(buf, rot, off):
    q, p = divmod(off, SUB)
    return buf[off:off + T, :] if p == 0 else rot[p - 1, SUB * q:SUB * q + T, :]


def _scan(x_scr, pw_ref, pwrev_ref, carry, reverse):
    row8 = lax.broadcasted_iota(jnp.int32, (T, 128), 0) % SUB
    nb = T // SUB
    for j in range(XH // 128):
        lo, hi = j * 128, XH + j * 128
        re, im = x_scr[:, lo:lo + 128], x_scr[:, hi:hi + 128]
        for d in (1, 2, 4):
            ar, ai = pw_ref[d - 1:d, lo:lo + 128], pw_ref[d - 1:d, hi:hi + 128]
            if reverse:
                ai = -ai
                keep = row8 < SUB - d
                sre, sim = pltpu.roll(re, T - d, 0), pltpu.roll(im, T - d, 0)
            else:
                keep = row8 >= d
                sre, sim = pltpu.roll(re, d, 0), pltpu.roll(im, d, 0)
            sre, sim = jnp.where(keep, sre, 0.0), jnp.where(keep, sim, 0.0)
            re, im = re + ar * sre - ai * sim, im + ar * sim + ai * sre
        cr, ci = carry[:, lo:lo + 128], carry[:, hi:hi + 128]
        if reverse:
            pr, pi = pwrev_ref[:, lo:lo + 128], -pwrev_ref[:, hi:hi + 128]
            order, edge = reversed(range(nb)), 0
        else:
            pr, pi = pw_ref[:, lo:lo + 128], pw_ref[:, hi:hi + 128]
            order, edge = range(nb), SUB - 1
        for b in order:
            rb, ib = re[SUB * b:SUB * b + SUB], im[SUB * b:SUB * b + SUB]
            crb, cib = jnp.broadcast_to(cr, (SUB, 128)), jnp.broadcast_to(ci, (SUB, 128))
            rb, ib = rb + pr * crb - pi * cib, ib + pr * cib + pi * crb
            x_scr[SUB * b:SUB * b + SUB, lo:lo + 128] = rb
            x_scr[SUB * b:SUB * b + SUB, hi:hi + 128] = ib
            cr, ci = rb[edge:edge + 1], ib[edge:edge + 1]


MIXER_PARAMS = ('scw', 'pblk', 'pscale', 'dww', 'dwb', 'clg', 'clb', 'bblk', 'cblk', 'pw', 'pwrev', 'ssd', 'wglu',
                'bglu')


def _mixer_fwd(x, mod, l, w_in_t, b_in, mp, name, comm=None):
    s = x.shape[0]
    nch = s // T
    params = [mp[k] for k in MIXER_PARAMS]

    def body(x_ref, sc_ref, sh_ref, w_ref, b_ref, scw_ref, pblk_ref, ps_ref, dww_ref, dwb_ref, clg_ref, clb_ref,
             bblk_ref, cblk_ref, pw_ref, pwrev_ref, ssd_ref, wglu_ref, bglu_ref, y_ref, xs_ref, aux_ref, z_ref,
             qbuf, hbuf, pb1, pb2, pb4, pb8, rot, carry):
        i = pl.program_id(0)

        @pl.when(i == 0)
        def _():
            for ref in (qbuf, hbuf, pb1, pb2, pb4, pb8):
                ref[0:HALO, :] = jnp.zeros((HALO, GW), F32)
            carry[...] = jnp.zeros_like(carry)

        h = x_ref[...] * (1.0 + sc_ref[...]) + sh_ref[...]
        z_ref[...] = _dot_nt(h, w_ref[...]) + b_ref[...]
        zh, zb, zc = z_ref[:, 0:GW], z_ref[:, GW:2 * GW], z_ref[:, 2 * GW:3 * GW]
        zp, zcv, zcg, zs = (z_ref[:, 3 * GW:4 * GW], z_ref[:, 4 * GW:5 * GW], z_ref[:, 5 * GW:6 * GW],
                            z_ref[:, 6 * GW:7 * GW])
        qbuf[HALO:HALO + T, :] = zc * zh
        conv = None
        for k in range(SC_TAPS):
            v = scw_ref[k:k + 1, :] * qbuf[pl.ds(HALO - (SC_TAPS - 1) + k, T), :]
            conv = v if conv is None else conv + v
        aux_ref[:, 0:GW] = conv
        y_ref[:, 0:GW] = (zb * conv).astype(y_ref.dtype)
        r = _window_sums(zp, (pb1, pb2, pb4, pb8), HALO, -1) * _pool_inv_count(i) - zp
        aux_ref[:, GW:2 * GW] = r
        y_ref[:, GW:2 * GW] = (_dot(r, pblk_ref[...]) * ps_ref[...]).astype(y_ref.dtype)
        hbuf[HALO:HALO + T, :] = zcv * _sigmoid(zcg)
        _phase_copies(hbuf, rot)
        hc = None
        for k in range(CF_TAPS):
            v = dww_ref[k:k + 1, :] * _tap(hbuf, rot, HALO - (CF_TAPS - 1) + k)
            hc = v if hc is None else hc + v
        hc = hc + dwb_ref[...]
        aux_ref[:, 2 * GW:3 * GW] = hc
        mu = jnp.mean(hc, axis=-1, keepdims=True)
        xc = hc - mu
        var = jnp.mean(xc * xc, axis=-1, keepdims=True)
        ln = xc * lax.rsqrt(var + LN_EPS) * clg_ref[...] + clb_ref[...]
        y_ref[:, 2 * GW:3 * GW] = (ln * _sigmoid(ln)).astype(y_ref.dtype)
        xs_ref[...] = _dot(zs, bblk_ref[...])
        _scan(xs_ref, pw_ref, pwrev_ref, carry[...], reverse=False)
        carry[...] = xs_ref[T - 1:T, :]
        yy = _dot(xs_ref[...], cblk_ref[...]) + ssd_ref[...] * zs
        aux_ref[:, 3 * GW:4 * GW] = yy
        yg = _gelu(yy)
        v = _dot(yg, wglu_ref[...]) + bglu_ref[...]
        y_ref[:, 3 * GW:4 * GW] = (yg * _sigmoid(v)).astype(y_ref.dtype)
        for ref in (qbuf, hbuf, pb1, pb2, pb4, pb8):
            ref[0:HALO, :] = ref[T:T + HALO, :]

    fwd = lambda i: (i, 0)
    return _call(
        body, name=name, grid=(nch,),
        out_shape=[jax.ShapeDtypeStruct((s, D), _MXU), jax.ShapeDtypeStruct((s, XW), F32),
                   jax.ShapeDtypeStruct((s, 4 * GW), F32), jax.ShapeDtypeStruct((s, INW), F32)],
        in_specs=[pl.BlockSpec((T, D), fwd), _sel(mod, l, SC1), _sel(mod, l, SH1), _full(w_in_t.shape),
                  _sel(b_in, l)] + [_sel(p, l) for p in params],
        out_specs=[pl.BlockSpec((T, D), fwd), pl.BlockSpec((T, XW), fwd), pl.BlockSpec((T, 4 * GW), fwd),
                   pl.BlockSpec((T, INW), fwd)],
        scratch=[pltpu.VMEM((HALO + T, GW), F32)] * 6
        + [pltpu.VMEM((SUB - 1, HALO + T, GW), F32), pltpu.VMEM((1, XW), F32)],
        args=(x, mod, mod, w_in_t, b_in, *params), comm=comm)


def _mixer_bwd(z, dres, w_o, xs, aux, mp, l, name):
    s = z.shape[0]
    nch = s // T
    params = [mp[k] for k in MIXER_PARAMS]

    def body(z_ref, zprev_ref, dres_ref, wo_ref, xs_ref, xprev_ref, aux_ref, scw_ref, pblk_ref, ps_ref, dww_ref,
             dwb_ref, clg_ref, clb_ref, bblk_ref, cblk_ref, pw_ref, pwrev_ref, ssd_ref, wglu_ref, bglu_ref,
             dz_ref, vec_ref, small_ref, dpblk_ref, dwglu_ref, dbblk_ref, dcblk_ref, dlam_ref,
             qbuf, hbuf, dcbuf, dhbuf, eb1, eb2, eb4, eb8, rot_h, rot_d, acarry, a_scr, dy_ref):
        i = pl.program_id(0)
        chunk = nch - 1 - i
        dy_ref[...] = _dot_nt(dres_ref[...], wo_ref[...])

        @pl.when(i == 0)
        def _():
            for ref in (vec_ref, small_ref, dpblk_ref, dwglu_ref, dbblk_ref, dcblk_ref, dlam_ref, acarry):
                ref[...] = jnp.zeros_like(ref)
            for ref in (dcbuf, dhbuf, eb1, eb2, eb4, eb8):
                ref[T:T + HALO, :] = jnp.zeros((HALO, GW), F32)

        zh, zb, zc = z_ref[:, 0:GW], z_ref[:, GW:2 * GW], z_ref[:, 2 * GW:3 * GW]
        zcv, zcg, zs = z_ref[:, 4 * GW:5 * GW], z_ref[:, 5 * GW:6 * GW], z_ref[:, 6 * GW:7 * GW]
        conv, r, hc, yy = (aux_ref[:, 0:GW], aux_ref[:, GW:2 * GW], aux_ref[:, 2 * GW:3 * GW],
                           aux_ref[:, 3 * GW:4 * GW])
        has_prev = (chunk > 0).astype(F32)
        qbuf[0:HALO, :] = zprev_ref[:, 2 * GW:3 * GW] * zprev_ref[:, 0:GW] * has_prev
        hbuf[0:HALO, :] = zprev_ref[:, 4 * GW:5 * GW] * _sigmoid(zprev_ref[:, 5 * GW:6 * GW]) * has_prev

        dya = dy_ref[:, 0:GW]
        qbuf[HALO:HALO + T, :] = zc * zh
        dconv = dya * zb
        dcbuf[0:T, :] = dconv
        dq = None
        for k in range(SC_TAPS):
            small_ref[k:k + 1, :] += _colsum(dconv * qbuf[pl.ds(HALO - (SC_TAPS - 1) + k, T), :])
            v = scw_ref[k:k + 1, :] * dcbuf[pl.ds(SC_TAPS - 1 - k, T), :]
            dq = v if dq is None else dq + v
        dzh, dzb, dzc = dq * zc, dya * conv, dq * zh

        dyb = dy_ref[:, GW:2 * GW]
        o = _dot(r, pblk_ref[...])
        small_ref[3:4, :] += _colsum(dyb * o)
        do = dyb * ps_ref[...]
        dpblk_ref[...] += _dot_tn(r, do)
        dr = _dot_nt(do, pblk_ref[...])
        dzp = _window_sums(dr * _pool_inv_count(chunk), (eb1, eb2, eb4, eb8), 0, 1) - dr

        dyc = dy_ref[:, 2 * GW:3 * GW]
        sgc = _sigmoid(zcg)
        hbuf[HALO:HALO + T, :] = zcv * sgc
        mu = jnp.mean(hc, axis=-1, keepdims=True)
        xc = hc - mu
        var = jnp.mean(xc * xc, axis=-1, keepdims=True)
        rstd = lax.rsqrt(var + LN_EPS)
        nrm = xc * rstd
        ln = nrm * clg_ref[...] + clb_ref[...]
        sl = _sigmoid(ln)
        dln = dyc * (sl * (1.0 + ln * (1.0 - sl)))
        small_ref[5:6, :] += _colsum(dln * nrm)
        small_ref[6:7, :] += _colsum(dln)
        dn = dln * clg_ref[...]
        dhc = rstd * (dn - jnp.mean(dn, axis=-1, keepdims=True) - nrm * jnp.mean(dn * nrm, axis=-1, keepdims=True))
        small_ref[4:5, :] += _colsum(dhc)
        dhbuf[0:T, :] = dhc
        _phase_copies(hbuf, rot_h)
        _phase_copies(dhbuf, rot_d)
        dhg = None
        for k in range(CF_TAPS):
            small_ref[16 + k:17 + k, :] += _colsum(dhc * _tap(hbuf, rot_h, HALO - (CF_TAPS - 1) + k))
            v = dww_ref[k:k + 1, :] * _tap(dhbuf, rot_d, CF_TAPS - 1 - k)
            dhg = v if dhg is None else dhg + v
        dzcv = dhg * sgc
        dzcg = dhg * zcv * sgc * (1.0 - sgc)

        dyd = dy_ref[:, 3 * GW:4 * GW]
        yg = _gelu(yy)
        v = _dot(yg, wglu_ref[...]) + bglu_ref[...]
        sg = _sigmoid(v)
        dv = dyd * yg * sg * (1.0 - sg)
        small_ref[8:9, :] += _colsum(dv)
        dwglu_ref[...] += _dot_tn(yg, dv)
        g = (dyd * sg + _dot_nt(dv, wglu_ref[...])) * _gelu_grad(yy)
        small_ref[7:8, :] += _colsum(g * zs)
        dcblk_ref[...] += _dot_tn(xs_ref[...], g)
        a_scr[...] = _dot_nt(g, cblk_ref[...])
        _scan(a_scr, pw_ref, pwrev_ref, acarry[...], reverse=True)
        acarry[...] = a_scr[0:1, :]
        dzs = _dot_nt(a_scr[...], bblk_ref[...]) + ssd_ref[...] * g
        dbblk_ref[...] += _dot_tn(zs, a_scr[...])
        row = lax.broadcasted_iota(jnp.int32, (T, 128), 0)
        for j in range(XH // 128):
            lo, hi = j * 128, XH + j * 128
            first_r = xprev_ref[SUB - 1:SUB, lo:lo + 128] * has_prev
            first_i = xprev_ref[SUB - 1:SUB, hi:hi + 128] * has_prev
            pr = jnp.where(row == 0, first_r, pltpu.roll(xs_ref[:, lo:lo + 128], 1, 0))
            pi = jnp.where(row == 0, first_i, pltpu.roll(xs_ref[:, hi:hi + 128], 1, 0))
            ar, ai = a_scr[:, lo:lo + 128], a_scr[:, hi:hi + 128]
            dlam_ref[0:1, lo:lo + 128] += _colsum(ar * pr + ai * pi)
            dlam_ref[0:1, hi:hi + 128] += _colsum(ai * pr - ar * pi)

        parts = (dzh, dzb, dzc, dzp, dzcv, dzcg, dzs)
        for k, part in enumerate(parts):
            dz_ref[:, k * GW:(k + 1) * GW] = part.astype(dz_ref.dtype)
            vec_ref[0:1, k * GW:(k + 1) * GW] += _colsum(part)
        for ref in (dcbuf, dhbuf, eb1, eb2, eb4, eb8):
            ref[T:T + HALO, :] = ref[0:HALO, :]

    rev = lambda i: (nch - 1 - i, 0)

    def before(rows):
        return lambda i: (jnp.maximum((nch - 1 - i) * (T // rows) - 1, 0), 0)

    outs = [((s, INW), _MXU), ((8, INW), F32), ((48, GW), F32), ((GW, GW), F32), ((GW, GW), F32),
            ((GW, XW), F32), ((XW, GW), F32), ((8, XW), F32)]
    return _call(
        body, name=name, grid=(nch,),
        out_shape=[jax.ShapeDtypeStruct(sh, dt) for sh, dt in outs],
        in_specs=[pl.BlockSpec((T, INW), rev), pl.BlockSpec((HALO, INW), before(HALO)), pl.BlockSpec((T, D), rev),
                  _full(w_o.shape), pl.BlockSpec((T, XW), rev), pl.BlockSpec((SUB, XW), before(SUB)),
                  pl.BlockSpec((T, 4 * GW), rev)]
        + [_sel(p, l) for p in params],
        out_specs=[pl.BlockSpec((T, INW), rev)] + [_full(sh) for sh, _ in outs[1:]],
        scratch=[pltpu.VMEM((HALO + T, GW), F32)] * 8
        + [pltpu.VMEM((SUB - 1, HALO + T, GW), F32)] * 2
        + [pltpu.VMEM((1, XW), F32), pltpu.VMEM((T, XW), F32), pltpu.VMEM((T, D), F32)],
        args=(z, z, dres, w_o, xs, xs, aux, *params))


def _blockdiag(m):
    *lead, g, h, p = m.shape
    eye = jnp.eye(g, dtype=m.dtype)
    return (m[..., :, :, None, :] * eye[:, None, :, None]).reshape(*lead, g * h, g * p)


def _blockdiag_extract(f, h, p):
    *lead, r, _ = f.shape
    g = r // h
    eye = jnp.eye(g, dtype=f.dtype)
    return jnp.sum(f.reshape(*lead, g, h, g, p) * eye[:, None, :, None], axis=-2)


def _t(a):
    return jnp.swapaxes(a, -1, -2)


PACK_ROWS = 512


def _pack(arrs):
    flat = jnp.concatenate([a.reshape(-1).astype(F32) for a in arrs])
    quantum = PACK_ROWS * 128
    padded = -(-flat.shape[0] // quantum) * quantum
    return jnp.pad(flat, (0, padded - flat.shape[0])).reshape(-1, 128)


def _unpack(buf, shapes, lead=()):
    flat = buf.reshape(lead + (-1,))
    out, off = [], 0
    for shp in shapes:
        n = math.prod(shp)
        out.append(flat[..., off:off + n].reshape(lead + tuple(shp)))
        off += n
    return out


def _cols_to_full(g):
    n, l, r, c = g.shape
    return g.transpose(1, 2, 0, 3).reshape(l, r, n * c)


def _rows_to_full(g):
    n, l, r, c = g.shape
    return g.transpose(1, 0, 2, 3).reshape(l, n * r, c)


REPLICATED = ['b_ada', 'b_in', 'pool_w', 'pool_scale', 'cf_dw_b', 'cf_ln_g', 'cf_ln_b', 'ssm_lam_re', 'ssm_lam_im',
              'ssm_log_dt', 'ssm_b_re', 'ssm_b_im', 'ssm_c_re', 'ssm_c_im', 'ssm_d', 'ssm_b_glu', 'ln1_g', 'ln1_b',
              'ln2_g', 'ln2_b']
SMALL_SHARDED = ['sc_w', 'cf_dw_w', 'ssm_w_glu']
COL_SHARDED = ['w_in', 'w_gate', 'w_up']
ROW_SHARDED = ['w_o', 'w_down']
BIG = COL_SHARDED + ROW_SHARDED
WEIGHTS = ['w_ada', 'b_ada', 'w_in', 'b_in', 'sc_w', 'pool_w', 'pool_scale', 'cf_dw_w', 'cf_dw_b', 'cf_ln_g',
           'cf_ln_b', 'ssm_lam_re', 'ssm_lam_im', 'ssm_log_dt', 'ssm_b_re', 'ssm_b_im', 'ssm_c_re', 'ssm_c_im',
           'ssm_d', 'ssm_w_glu', 'ssm_b_glu', 'w_o', 'ln1_g', 'ln1_b', 'w_gate', 'w_up', 'w_down', 'ln2_g', 'ln2_b']
SH1, SC1, G1, SH2, SC2, G2 = range(6)


def _train_step(p):
    xi, yi, ci = _me()
    me = _flat((xi, yi, ci))
    s = p['x'].shape[1]
    xs = p['x'].reshape(s, D)
    target = p['loss_target'].reshape(s, D)
    vec3 = lambda a: a.reshape(DEPTH, 1, -1)

    def shard(n, l):
        w = p[n][l]
        return (w.T if n in COL_SHARDED else w).astype(_MXU)

    def whole(g):
        return g.reshape(-1, g.shape[-1])

    small_shapes = [p[n].shape for n in SMALL_SHARDED] + [p['c'].shape]
    first = ('w_in', 'w_o')
    gathered = _allgather([_pack([p[n] for n in SMALL_SHARDED] + [p['c']])] + [shard(n, 0) for n in first],
                          "gather_first")
    scw_g, dww_g, wglu_g, c_g = _unpack(gathered[0], small_shapes, lead=(NDEV,))
    full = {'sc_w': _cols_to_full(scw_g), 'cf_dw_w': _cols_to_full(dww_g), 'ssm_w_glu': _rows_to_full(wglu_g)}
    wts = [dict(zip(first, (whole(g) for g in gathered[1:])))] + [dict() for _ in range(DEPTH - 1)]
    c_all = c_g.reshape(NDEV, D)

    ncol = p['w_ada'].shape[2]
    b_cols = lax.dynamic_slice_in_dim(p['b_ada'], me * ncol, ncol, axis=1).reshape(DEPTH, 1, ncol)
    cond, mod_cols = _ada_fwd(c_all, p['w_ada'], b_cols, "ada_fwd")
    (mod_x,) = _alltoall([mod_cols.transpose(1, 0, 2)], "mod_exchange")
    mod = mod_x.transpose(1, 0, 2).reshape(DEPTH, 6, 1, D)

    ssm_in = (p['ssm_lam_re'], p['ssm_lam_im'], p['ssm_log_dt'].reshape(DEPTH, NGRP, 1),
              _t(p['ssm_b_re']), _t(p['ssm_b_im']))
    pwr, pwi, bbr, bbi = _ssm_prep(*ssm_in, "ssm_prep")
    pw = jnp.concatenate([pwr.reshape(DEPTH, SUB, XH), pwi.reshape(DEPTH, SUB, XH)], axis=2)
    mp = dict(
        scw=full['sc_w'], pblk=_blockdiag(p['pool_w']).astype(_MXU), pscale=vec3(p['pool_scale']),
        dww=full['cf_dw_w'], dwb=vec3(p['cf_dw_b']), clg=vec3(p['cf_ln_g']), clb=vec3(p['cf_ln_b']),
        bblk=jnp.concatenate([_blockdiag(bbr), _blockdiag(bbi)], axis=2).astype(_MXU),
        cblk=jnp.concatenate([_blockdiag(_t(p['ssm_c_re'])), -_blockdiag(_t(p['ssm_c_im']))], axis=1).astype(_MXU),
        pw=pw, pwrev=pw[:, ::-1], ssd=vec3(p['ssm_d']), wglu=full['ssm_w_glu'].astype(_MXU),
        bglu=vec3(p['ssm_b_glu']))
    b_in, ln1_g, ln1_b, ln2_g, ln2_b = (vec3(p[n]) for n in ('b_in', 'ln1_g', 'ln1_b', 'ln2_g', 'ln2_b'))

    saved = []
    for l in range(DEPTH):
        w = wts[l]

        def gather(at, *names):
            return None if at >= DEPTH or not names else (GATHER, [shard(n, at) for n in names])

        def take(res, n_own, at, *names):
            if at < DEPTH:
                for n, g in zip(names, res[n_own:]):
                    wts[at][n] = whole(g)
            return res[:n_own]

        ycat, states, aux, z = take(_mixer_fwd(xs, mod, l, w['w_in'], b_in, mp, "mixer_fwd",
                                               comm=gather(l, 'w_gate', 'w_up')), 4, l, 'w_gate', 'w_up')
        early = ('w_down',) if l == 0 else ()
        y, u1, x1 = take(_proj_residual_ln(ycat, w['w_o'], l, xs, mod, G1, ln1_g, ln1_b, "o_proj_ln",
                                           comm=gather(l, *early)), 3, l, *early)
        gate, up, act = take(_modulated_matmul(x1, mod, l, SC2, SH2, [w['w_gate'], w['w_up']], None, TM_WIDE,
                                               [_MXU, _MXU, _MXU], _swiglu, "gate_up",
                                               comm=gather(l + 1, 'w_in', 'w_o')), 3, l + 1, 'w_in', 'w_o')
        f, u2, x2 = take(_proj_residual_ln(act, w['w_down'], l, x1, mod, G2, ln2_g, ln2_b, "down_proj_ln",
                                           comm=gather(l + 1, 'w_down')), 3, l + 1, 'w_down')
        saved.append(dict(x=xs, z=z, ycat=ycat, states=states, aux=aux, y=y, u1=u1, x1=x1, gate=gate, up=up,
                          act=act, f=f, u2=u2))
        xs = x2

    recv = {n: [None] * DEPTH for n in BIG}
    small_g = {n: [None] * DEPTH for n in ('ln1_g', 'ln1_b', 'ln2_g', 'ln2_b')}
    dmod = [None] * DEPTH
    mixer_out = [None] * DEPTH
    pending = None
    top = DEPTH - 1
    du2, df, acc_top = _loss_ln_bwd(xs, target, saved[top]['u2'], saved[top]['f'], ln2_g, top, mod, G2,
                                    "loss_ln2_bwd")
    small_g['ln2_g'][top], small_g['ln2_b'][top], dg2 = acc_top[0], acc_top[1], acc_top[2]
    loss = lax.psum(acc_top[3, 0] * (0.5 / D), AXES)
    for l in reversed(range(DEPTH)):
        sv, w = saved[l], wts[l]
        if pending is None:
            dgate, dup = _swiglu_bwd(df, w['w_down'], sv['gate'], sv['up'], "swiglu_bwd")
        else:
            lp, g_o, g_in = pending
            dgate, dup, recv['w_o'][lp], recv['w_in'][lp] = _swiglu_bwd(
                df, w['w_down'], sv['gate'], sv['up'], "swiglu_bwd", comm=(EXCHANGE, [g_o, g_in]))
        (g_down,) = _weight_grad(sv['act'], df, "dw_down")
        (g_gate,) = _weight_grad(dgate, sv['x1'], "dw_gate", mod=(mod, l, SC2, SH2))
        (g_up,) = _weight_grad(dup, sv['x1'], "dw_up", mod=(mod, l, SC2, SH2))
        du1, dyb, acc_a, recv['w_down'][l], recv['w_gate'][l] = _input_grad_ln_bwd(
            [dgate, dup], [w['w_gate'], w['w_up']], du2, sv['x1'], (mod, l, SC2), sv['u1'], sv['y'], (ln1_g, l),
            (mod, l, G1), TM_WIDE, "dh2_ln1_bwd", comm=(EXCHANGE, [g_down, g_gate]))
        (g_o,) = _weight_grad(sv['ycat'], dyb, "dw_o")
        dz, *mixer_out[l] = _mixer_bwd(sv['z'], dyb, w['w_o'], sv['states'], sv['aux'], mp, l, "mixer_bwd")
        if l > 0:
            du2, df, acc_b, recv['w_up'][l] = _input_grad_ln_bwd(
                [dz], [w['w_in']], du1, sv['x'], (mod, l, SC1), saved[l - 1]['u2'], saved[l - 1]['f'],
                (ln2_g, l - 1), (mod, l - 1, G2), TM_WIDE, "dh1_ln2_bwd", comm=(EXCHANGE, [g_up]))
        else:
            dx, acc_b, recv['w_up'][l] = _input_grad([dz], [w['w_in']], l, du1, sv['x'], mod, SC1, TM, "dh1",
                                                     comm=(EXCHANGE, [g_up]))
        (g_in,) = _weight_grad(dz, sv['x'], "dw_in", mod=(mod, l, SC1, SH1))
        pending = (l, g_o, g_in)
        dmod[l] = jnp.concatenate([acc_b[0], acc_b[1], acc_a[4], acc_a[0], acc_a[1], dg2])
        small_g['ln1_g'][l], small_g['ln1_b'][l] = acc_a[2], acc_a[3]
        if l > 0:
            small_g['ln2_g'][l - 1], small_g['ln2_b'][l - 1], dg2 = acc_b[2], acc_b[3], acc_b[4]
    lp, g_o, g_in = pending
    recv['w_o'][lp], recv['w_in'][lp] = _alltoall([g_o, g_in], "exchange_last")

    partial = {n: jnp.stack(v) for n, v in small_g.items()}
    partial['b_ada'] = jnp.stack(dmod)
    vec, small, dpblk, dwglu, dbblk, dcblk, dlam = (jnp.stack(t) for t in zip(*mixer_out))
    partial.update(
        b_in=vec[:, 0], sc_w=small[:, 0:SC_TAPS], pool_scale=small[:, 3], cf_dw_b=small[:, 4], cf_ln_g=small[:, 5],
        cf_ln_b=small[:, 6], ssm_d=small[:, 7], ssm_b_glu=small[:, 8], cf_dw_w=small[:, 16:16 + CF_TAPS],
        pool_w=_blockdiag_extract(dpblk, 64, 64), ssm_w_glu=dwglu,
        ssm_c_re=_t(_blockdiag_extract(dcblk[:, :XH], NSTATE, NCH)),
        ssm_c_im=-_t(_blockdiag_extract(dcblk[:, XH:], NSTATE, NCH)))
    dlr, dli, dlog, dbr, dbi = _ssm_param_bwd(
        *ssm_in, dlam[:, 0, :XH].reshape(DEPTH, NGRP, NSTATE), dlam[:, 0, XH:].reshape(DEPTH, NGRP, NSTATE),
        _blockdiag_extract(dbblk[:, :, :XH], NCH, NSTATE), _blockdiag_extract(dbblk[:, :, XH:], NCH, NSTATE),
        "ssm_param_bwd")
    partial.update(ssm_lam_re=dlr, ssm_lam_im=dli, ssm_log_dt=dlog.reshape(DEPTH, NGRP), ssm_b_re=_t(dbr),
                   ssm_b_im=_t(dbi))

    out = {'loss': loss, 'grad_x': dx.reshape(1, s, D)}

    def emit(n, g, delta, nm, nv):
        shp = p[n].shape
        out['grad_' + n], out['delta_' + n] = g.reshape(shp), delta.reshape(shp)
        out['new_m_' + n], out['new_v_' + n] = nm.reshape(shp), nv.reshape(shp)

    names_a = REPLICATED + SMALL_SHARDED
    shapes_a = [partial[n].shape for n in names_a]
    zeros_like_full = lambda n: jnp.zeros(partial[n].shape, F32)
    (parts_a,) = _allgather([_pack([partial[n] for n in names_a])], "gather_small_grads")
    packs = [_pack([p[pre + n] for n in REPLICATED] + [zeros_like_full(n) for n in SMALL_SHARDED])
             for pre in ('', 'm_', 'v_')]
    res_a = _adamw(parts_a, *packs, PACK_ROWS, "adamw_small")
    res_a = [_unpack(r, shapes_a) for r in res_a]
    for k, n in enumerate(REPLICATED):
        emit(n, *[r[k] for r in res_a])
    reduced = dict(zip(names_a, res_a[0]))

    blocks = []
    for n in SMALL_SHARDED:
        axis = 1 if n == 'ssm_w_glu' else 2
        width = p[n].shape[axis]
        blocks.append(lax.dynamic_slice_in_dim(reduced[n], me * width, width, axis=axis))
    shapes_b = [b.shape for b in blocks]
    packs = [_pack([p[pre + n] for n in SMALL_SHARDED]) for pre in ('', 'm_', 'v_')]
    res_b = _adamw(_pack(blocks)[None], *packs, PACK_ROWS, "adamw_small_sharded")
    res_b = [_unpack(r, shapes_b) for r in res_b]
    for k, n in enumerate(SMALL_SHARDED):
        emit(n, *[r[k] for r in res_b])

    dmod_all = _unpack(parts_a, shapes_a, lead=(NDEV,))[0]
    dmod_cols = lax.dynamic_slice_in_dim(dmod_all, me * ncol, ncol, axis=2).transpose(1, 0, 2)
    g_ada = _ada_bwd(cond, dmod_cols, "ada_bwd")
    flat2 = lambda a: a.reshape(-1, a.shape[-1])
    emit('w_ada', *_adamw(flat2(g_ada)[None], flat2(p['w_ada']), flat2(p['m_w_ada']), flat2(p['v_w_ada']),
                          512, "adamw_w_ada"))

    for n in BIG:
        tr = recv[n][0].shape[1] // 2
        if n in ROW_SHARDED:
            emit(n, *_adamw_layers(recv[n], flat2(p[n]), flat2(p['m_' + n]), flat2(p['v_' + n]), tr, "adamw_" + n))
        else:
            g = flat2(_t(_reduce_layers(recv[n], tr, "sum_" + n)))
            emit(n, *_adamw(g[None], flat2(p[n]), flat2(p['m_' + n]), flat2(p['v_' + n]), 512, "adamw_" + n))
    return out


def kernel(x, c, w_ada, b_ada, w_in, b_in, sc_w, pool_w, pool_scale, cf_dw_w, cf_dw_b, cf_ln_g, cf_ln_b, ssm_lam_re, ssm_lam_im, ssm_log_dt, ssm_b_re, ssm_b_im, ssm_c_re, ssm_c_im, ssm_d, ssm_w_glu, ssm_b_glu, w_o, ln1_g, ln1_b, w_gate, w_up, w_down, ln2_g, ln2_b, loss_target, m_w_ada, m_b_ada, m_w_in, m_b_in, m_sc_w, m_pool_w, m_pool_scale, m_cf_dw_w, m_cf_dw_b, m_cf_ln_g, m_cf_ln_b, m_ssm_lam_re, m_ssm_lam_im, m_ssm_log_dt, m_ssm_b_re, m_ssm_b_im, m_ssm_c_re, m_ssm_c_im, m_ssm_d, m_ssm_w_glu, m_ssm_b_glu, m_w_o, m_ln1_g, m_ln1_b, m_w_gate, m_w_up, m_w_down, m_ln2_g, m_ln2_b, v_w_ada, v_b_ada, v_w_in, v_b_in, v_sc_w, v_pool_w, v_pool_scale, v_cf_dw_w, v_cf_dw_b, v_cf_ln_g, v_cf_ln_b, v_ssm_lam_re, v_ssm_lam_im, v_ssm_log_dt, v_ssm_b_re, v_ssm_b_im, v_ssm_c_re, v_ssm_c_im, v_ssm_d, v_ssm_w_glu, v_ssm_b_glu, v_w_o, v_ln1_g, v_ln1_b, v_w_gate, v_w_up, v_w_down, v_ln2_g, v_ln2_b):
    out = _train_step(dict(locals()))
    return (out['loss'], out['grad_x'], *[out[pre + n] for pre in ('grad_', 'delta_', 'new_m_', 'new_v_')
                                          for n in WEIGHTS])
```

```python
import math

import jax
import jax.numpy as jnp
from jax import lax
from jax.experimental import pallas as pl
from jax.experimental.pallas import tpu as pltpu

F32 = jnp.float32
_MXU = jnp.bfloat16

D = 1024
GW = 256
INW = 7 * GW
DFF = 2816
DEPTH = 4
NDEV = 8
SC_TAPS = 3
CF_TAPS = 31
NGRP = 16
NCH = 16
NSTATE = 64
XH = NGRP * NSTATE
XW = 2 * XH
ALPHA = (2 * DEPTH) ** 0.25
LN_EPS = 1e-5
T = 256
SUB = 8
HALO = 32
TM = 512
TM_WIDE = 256
VMEM_LIMIT = 56 * 1024 * 1024

ADAM_LR = 0.001
ADAM_B1 = 0.9
ADAM_B2 = 0.999
ADAM_EPS = 1e-08
ADAM_WD = 0.01
ADAM_STEP = 10

MESH = pl.DeviceIdType.MESH
ANY = pl.BlockSpec(memory_space=pl.ANY)
AXES = ("x", "y", "c")


def _full(shape):
    nd = len(shape)
    return pl.BlockSpec(shape, lambda *_: (0,) * nd)


def _resident(shape):
    nd = len(shape)
    return pl.BlockSpec(shape, lambda *_: (0,) * nd, pipeline_mode=pl.Buffered(1))


def _sel(arr, *idx):
    tail = arr.shape[len(idx):]
    return pl.BlockSpec((None,) * len(idx) + tail, lambda *_: idx + (0,) * len(tail))


def _dot(a, b):
    return jnp.dot(a.astype(_MXU), b.astype(_MXU), preferred_element_type=F32)


def _dot_nt(a, b):
    return lax.dot_general(a.astype(_MXU), b.astype(_MXU), (((1,), (1,)), ((), ())), preferred_element_type=F32)


def _dot_tn(a, b):
    return lax.dot_general(a.astype(_MXU), b.astype(_MXU), (((0,), (0,)), ((), ())), preferred_element_type=F32)


def _sigmoid(x):
    return 1.0 / (1.0 + jnp.exp(-x))


def _colsum(x):
    return jnp.sum(x, axis=0, keepdims=True)


def _me():
    return lax.axis_index("x"), lax.axis_index("y"), lax.axis_index("c")


def _flat(p):
    return 4 * p[0] + 2 * p[1] + p[2]


def _allgather(arrays, name):
    n = len(arrays)

    def body(*refs):
        ins, outs = refs[:n], refs[n:2 * n]
        send_sems, recv_sems, local_sems = refs[2 * n:]
        x, y, c = _me()
        me, sibling = (x, y, c), (x, y, 1 - c)
        chips = [(1 - x, y), (x, 1 - y), (1 - x, 1 - y)]

        def copy(a, k, block, to, src=None):
            dst = outs[a].at[_flat(block)]
            return pltpu.make_async_remote_copy(
                src_ref=dst if src is None else src, dst_ref=dst,
                send_sem=send_sems.at[a, k], recv_sem=recv_sems.at[a, k],
                device_id=to, device_id_type=MESH)

        started = []
        for a in range(n):
            mine = pltpu.make_async_copy(ins[a], outs[a].at[_flat(me)], local_sems.at[a])
            mine.start()
            started.append(mine)
        first = []
        for a in range(n):
            first.append(copy(a, 0, me, sibling, src=ins[a]))
            first += [copy(a, 1 + j, me, (*chip, c), src=ins[a]) for j, chip in enumerate(chips)]
        for cp in first:
            cp.start()
        passed = []
        for a in range(n):
            for j, chip in enumerate(chips):
                copy(a, 1 + j, (*chip, c), me).wait_recv()
                fwd = copy(a, 4 + j, (*chip, c), sibling)
                fwd.start()
                passed.append(fwd)
        for a in range(n):
            copy(a, 0, sibling, me).wait_recv()
            for j, chip in enumerate(chips):
                copy(a, 4 + j, (*chip, 1 - c), me).wait_recv()
        for cp in first + passed:
            cp.wait_send()
        for cp in started:
            cp.wait()

    return pl.pallas_call(
        body, name=name,
        out_shape=[jax.ShapeDtypeStruct((NDEV,) + a.shape, a.dtype) for a in arrays],
        in_specs=[ANY] * n, out_specs=[ANY] * n,
        scratch_shapes=[pltpu.SemaphoreType.DMA((n, 7)), pltpu.SemaphoreType.DMA((n, 7)),
                        pltpu.SemaphoreType.DMA((n,))],
    )(*arrays)


GATHER, EXCHANGE = "gather", "exchange"


def _direct_copies(kind, ins, outs, send_sems, recv_sems, local_sems, start):
    x, y, c = _me()
    mine = _flat((x, y, c))
    for a in range(len(ins)):
        own = ins[a] if kind == GATHER else ins[a].at[mine]
        loc = pltpu.make_async_copy(own, outs[a].at[mine], local_sems.at[a])
        if start:
            loc.start()
        for k in range(1, NDEV):
            peer = (lax.rem(x + ((k >> 2) & 1), 2), lax.rem(y + ((k >> 1) & 1), 2), lax.rem(c + (k & 1), 2))
            cp = pltpu.make_async_remote_copy(
                src_ref=ins[a] if kind == GATHER else ins[a].at[_flat(peer)], dst_ref=outs[a].at[mine],
                send_sem=send_sems.at[a, k - 1], recv_sem=recv_sems.at[a, k - 1],
                device_id=peer, device_id_type=MESH)
            if start:
                cp.start()
            else:
                cp.wait_send()
                cp.wait_recv()
        if not start:
            loc.wait()


def _comm_out_shapes(comm):
    kind, arrays = comm
    return [jax.ShapeDtypeStruct(((NDEV,) + a.shape) if kind == GATHER else a.shape, a.dtype) for a in arrays]


def _call(body, *, name, grid, in_specs, out_specs, out_shape, args, scratch=(), comm=None):
    params = pltpu.CompilerParams(dimension_semantics=("arbitrary",) * len(grid), vmem_limit_bytes=VMEM_LIMIT)
    if comm is None:
        return pl.pallas_call(body, name=name, grid=grid, in_specs=in_specs, out_specs=out_specs,
                              out_shape=out_shape, scratch_shapes=list(scratch), compiler_params=params)(*args)
    kind, arrays = comm
    n, n_in, n_out, n_scr = len(arrays), len(in_specs), len(out_specs), len(scratch)

    def wrapped(*refs):
        ins, cin = refs[:n_in], refs[n_in:n_in + n]
        outs, cout = refs[n_in + n:n_in + n + n_out], refs[n_in + n + n_out:n_in + 2 * n + n_out]
        scr = refs[n_in + 2 * n + n_out:]
        user_scr, sems = scr[:n_scr], scr[n_scr:]
        ids = [pl.program_id(k) for k in range(len(grid))]
        first, last = ids[0] == 0, ids[0] == grid[0] - 1
        for k in range(1, len(grid)):
            first, last = first & (ids[k] == 0), last & (ids[k] == grid[k] - 1)

        @pl.when(first)
        def _():
            _direct_copies(kind, cin, cout, *sems, start=True)

        body(*ins, *outs, *user_scr)

        @pl.when(last)
        def _():
            _direct_copies(kind, cin, cout, *sems, start=False)

    sems = [pltpu.SemaphoreType.DMA((n, 7)), pltpu.SemaphoreType.DMA((n, 7)), pltpu.SemaphoreType.DMA((n,))]
    res = pl.pallas_call(
        wrapped, name=name, grid=grid, in_specs=list(in_specs) + [ANY] * n, out_specs=list(out_specs) + [ANY] * n,
        out_shape=list(out_shape) + _comm_out_shapes(comm), scratch_shapes=list(scratch) + sems,
        compiler_params=params)(*args, *arrays)
    return res


def _alltoall(arrays, name):
    n = len(arrays)

    def body(*refs):
        _direct_copies(EXCHANGE, refs[:n], refs[n:2 * n], *refs[2 * n:], start=True)
        _direct_copies(EXCHANGE, refs[:n], refs[n:2 * n], *refs[2 * n:], start=False)

    return pl.pallas_call(
        body, name=name,
        out_shape=[jax.ShapeDtypeStruct(a.shape, a.dtype) for a in arrays],
        in_specs=[ANY] * n, out_specs=[ANY] * n,
        scratch_shapes=[pltpu.SemaphoreType.DMA((n, 7)), pltpu.SemaphoreType.DMA((n, 7)),
                        pltpu.SemaphoreType.DMA((n,))],
    )(*arrays)


def _modulated_matmul(x, mod, l, ksc, ksh, wts, bias, tm, out_dtypes, epilogue, name, comm=None):
    s, k = x.shape
    nw = len(wts)
    n = wts[0].shape[0]
    nb = 0 if bias is None else 1

    def body(*refs):
        x_ref, sc_ref, sh_ref = refs[:3]
        w_refs = refs[3:3 + nw]
        b_refs = refs[3 + nw:3 + nw + nb]
        o_refs = refs[3 + nw + nb:]
        h = (x_ref[...] * (1.0 + sc_ref[...]) + sh_ref[...]).astype(_MXU)
        prods = [lax.dot_general(h, w[...], (((1,), (1,)), ((), ())), preferred_element_type=F32) for w in w_refs]
        if nb:
            prods[0] = prods[0] + b_refs[0][...]
        for o, v in zip(o_refs, epilogue(*prods)):
            o[...] = v.astype(o.dtype)

    return _call(
        body, name=name, grid=(s // tm,),
        out_shape=[jax.ShapeDtypeStruct((s, n), dt) for dt in out_dtypes],
        in_specs=[pl.BlockSpec((tm, k), lambda i: (i, 0)), _sel(mod, l, ksc), _sel(mod, l, ksh)]
        + [_resident(w.shape) for w in wts] + ([_sel(bias, l)] if nb else []),
        out_specs=[pl.BlockSpec((tm, n), lambda i: (i, 0))] * len(out_dtypes),
        args=(x, mod, mod, *wts, *([bias] if nb else [])), comm=comm)


def _swiglu(gate, up):
    return gate, up, gate * _sigmoid(gate) * up


def _proj_residual_ln(a, w, l, xres, mod, kg, lng, lnb, name, comm=None):
    s, k = a.shape
    n = w.shape[1]

    def body(a_ref, w_ref, x_ref, g_ref, lg_ref, lb_ref, r_ref, u_ref, o_ref):
        r = jnp.dot(a_ref[...], w_ref[...], preferred_element_type=F32)
        u = ALPHA * x_ref[...] + (1.0 + g_ref[...]) * r
        mu = jnp.mean(u, axis=-1, keepdims=True)
        xc = u - mu
        var = jnp.mean(xc * xc, axis=-1, keepdims=True)
        r_ref[...] = r
        u_ref[...] = u
        o_ref[...] = xc * lax.rsqrt(var + LN_EPS) * lg_ref[...] + lb_ref[...]

    row = pl.BlockSpec((TM, n), lambda i: (i, 0))
    return _call(
        body, name=name, grid=(s // TM,),
        out_shape=[jax.ShapeDtypeStruct((s, n), F32)] * 3,
        in_specs=[pl.BlockSpec((TM, k), lambda i: (i, 0)), _resident(w.shape), row, _sel(mod, l, kg), _sel(lng, l),
                  _sel(lnb, l)],
        out_specs=[row, row, row], args=(a, w, xres, mod, lng, lnb), comm=comm)


def _loss_ln_bwd(xo, target, u, res, lng, l, mod, kg, name):
    s, n = u.shape

    def body(x_ref, t_ref, u_ref, r_ref, lg_ref, g_ref, du_ref, dr_ref, acc_ref):
        @pl.when(pl.program_id(0) == 0)
        def _():
            acc_ref[...] = jnp.zeros_like(acc_ref)

        uu = u_ref[...]
        mu = jnp.mean(uu, axis=-1, keepdims=True)
        xc = uu - mu
        var = jnp.mean(xc * xc, axis=-1, keepdims=True)
        rstd = lax.rsqrt(var + LN_EPS)
        xh = xc * rstd
        err = x_ref[...] - t_ref[...]
        acc_ref[3:4, :] += jnp.sum(err * err)
        dy = err * (1.0 / n)
        dxh = dy * lg_ref[...]
        du = rstd * (dxh - jnp.mean(dxh, axis=-1, keepdims=True) - xh * jnp.mean(dxh * xh, axis=-1, keepdims=True))
        du_ref[...] = du
        dr_ref[...] = (du * (1.0 + g_ref[...])).astype(dr_ref.dtype)
        acc_ref[0:1, :] += _colsum(dy * xh)
        acc_ref[1:2, :] += _colsum(dy)
        acc_ref[2:3, :] += _colsum(du * r_ref[...])

    row = pl.BlockSpec((TM, n), lambda i: (i, 0))
    return _call(
        body, name=name, grid=(s // TM,),
        out_shape=[jax.ShapeDtypeStruct((s, n), F32), jax.ShapeDtypeStruct((s, n), _MXU),
                   jax.ShapeDtypeStruct((8, n), F32)],
        in_specs=[row, row, row, row, _sel(lng, l), _sel(mod, l, kg)], out_specs=[row, row, _full((8, n))],
        args=(xo, target, u, res, lng, mod))


def _input_grad(acts, wts, l, du, xin, mod, ksc, tm, name, comm=None):
    s = acts[0].shape[0]
    n = wts[0].shape[1]
    na = len(acts)

    def body(*refs):
        a_refs, w_refs = refs[:na], refs[na:2 * na]
        du_ref, x_ref, sc_ref, dx_ref, acc_ref = refs[2 * na:]

        @pl.when(pl.program_id(0) == 0)
        def _():
            acc_ref[...] = jnp.zeros_like(acc_ref)

        dh = None
        for a, w in zip(a_refs, w_refs):
            t = jnp.dot(a[...], w[...], preferred_element_type=F32)
            dh = t if dh is None else dh + t
        dx_ref[...] = ALPHA * du_ref[...] + dh * (1.0 + sc_ref[...])
        acc_ref[0:1, :] += _colsum(dh)
        acc_ref[1:2, :] += _colsum(dh * x_ref[...])

    row = pl.BlockSpec((tm, n), lambda i: (i, 0))
    return _call(
        body, name=name, grid=(s // tm,),
        out_shape=[jax.ShapeDtypeStruct((s, n), F32), jax.ShapeDtypeStruct((8, n), F32)],
        in_specs=[pl.BlockSpec((tm, a.shape[1]), lambda i: (i, 0)) for a in acts]
        + [_resident(w.shape) for w in wts] + [row, row, _sel(mod, l, ksc)],
        out_specs=[row, _full((8, n))], args=(*acts, *wts, du, xin, mod), comm=comm)


SWIGLU_SPLIT = 2


def _input_grad_ln_bwd(acts, wts, du_in, sc, u, res, lng, lnb, g, tm, name, swiglu=None, comm=None):
    n = wts[0].shape[1]
    na = len(wts)
    ns = 0 if swiglu is None else 4
    s = (swiglu[0] if ns else acts[0]).shape[0]
    f = wts[0].shape[0]

    nlead = ns or len(acts)

    def body(*refs):
        a_refs, w_refs = refs[:nlead], refs[nlead:nlead + na]
        dui_ref, sc_ref, u_ref, r_ref, lg_ref, lb_ref, g_ref = refs[nlead + na:nlead + na + 7]
        o_refs = refs[nlead + na + 7:]
        if ns:
            dp_ref, wd_ref, gate_ref, up_ref = a_refs
            dg_ref, dup_ref = o_refs[:2]
            o_refs = o_refs[2:]
        du_ref, dr_ref, acc_ref = o_refs

        @pl.when(pl.program_id(0) == 0)
        def _():
            acc_ref[...] = jnp.zeros_like(acc_ref)

        dh = None
        if ns:
            piece = f // SWIGLU_SPLIT
            for c in range(SWIGLU_SPLIT):
                lo = c * piece
                dp = lax.dot_general(dp_ref[...], wd_ref[lo:lo + piece, :], (((1,), (1,)), ((), ())),
                                     preferred_element_type=F32)
                gt = gate_ref[:, lo:lo + piece].astype(F32)
                sg = _sigmoid(gt)
                dgate = (dp * up_ref[:, lo:lo + piece].astype(F32) * (sg * (1.0 + gt * (1.0 - sg)))).astype(_MXU)
                dup = (dp * (gt * sg)).astype(_MXU)
                dg_ref[:, lo:lo + piece] = dgate
                dup_ref[:, lo:lo + piece] = dup
                t = (jnp.dot(dgate, w_refs[0][lo:lo + piece, :], preferred_element_type=F32)
                     + jnp.dot(dup, w_refs[1][lo:lo + piece, :], preferred_element_type=F32))
                dh = t if dh is None else dh + t
        else:
            for a, w in zip(a_refs, w_refs):
                t = jnp.dot(a[...], w[...], preferred_element_type=F32)
                dh = t if dh is None else dh + t
        dy = ALPHA * dui_ref[...] + dh * (1.0 + sc_ref[...])
        uu = u_ref[...]
        mu = jnp.mean(uu, axis=-1, keepdims=True)
        xc = uu - mu
        var = jnp.mean(xc * xc, axis=-1, keepdims=True)
        rstd = lax.rsqrt(var + LN_EPS)
        xh = xc * rstd
        dh_sum = _colsum(dh)
        acc_ref[0:1, :] += dh_sum
        acc_ref[1:2, :] += lg_ref[...] * _colsum(dh * xh) + lb_ref[...] * dh_sum
        dxh = dy * lg_ref[...]
        du = rstd * (dxh - jnp.mean(dxh, axis=-1, keepdims=True) - xh * jnp.mean(dxh * xh, axis=-1, keepdims=True))
        du_ref[...] = du
        dr_ref[...] = (du * (1.0 + g_ref[...])).astype(dr_ref.dtype)
        acc_ref[2:3, :] += _colsum(dy * xh)
        acc_ref[3:4, :] += _colsum(dy)
        acc_ref[4:5, :] += _colsum(du * r_ref[...])

    row = pl.BlockSpec((tm, n), lambda i: (i, 0))
    wide = pl.BlockSpec((tm, f), lambda i: (i, 0))
    if ns:
        dp_in, w_down, gate, up = swiglu
        lead_specs = [pl.BlockSpec((tm, dp_in.shape[1]), lambda i: (i, 0)), _resident(w_down.shape), wide, wide]
        lead_args = [dp_in, w_down, gate, up]
        lead_outs, lead_out_specs = [jax.ShapeDtypeStruct((s, f), _MXU)] * 2, [wide, wide]
    else:
        lead_specs = [pl.BlockSpec((tm, a.shape[1]), lambda i: (i, 0)) for a in acts]
        lead_args, lead_outs, lead_out_specs = list(acts), [], []
    return _call(
        body, name=name, grid=(s // tm,),
        out_shape=lead_outs + [jax.ShapeDtypeStruct((s, n), F32), jax.ShapeDtypeStruct((s, n), _MXU),
                               jax.ShapeDtypeStruct((8, n), F32)],
        in_specs=lead_specs + [_resident(w.shape) for w in wts]
        + [row, _sel(*sc), row, row, _sel(*lng), _sel(*lnb), _sel(*g)],
        out_specs=lead_out_specs + [row, row, _full((8, n))],
        args=(*lead_args, *wts, du_in, sc[0], u, res, lng[0], lnb[0], g[0]), comm=comm)


def _weight_grad(a, b, name, mod=None, comm=None):
    s, ka = a.shape
    nbc = b.shape[1]
    nm = 0 if mod is None else 2
    steps = s // TM

    def body(*refs):
        a_ref, b_ref = refs[:2]
        m_refs = refs[2:2 + nm]
        o_ref, acc = refs[2 + nm:]

        @pl.when(pl.program_id(0) == 0)
        def _():
            acc[...] = jnp.zeros_like(acc)

        bv = b_ref[...]
        if nm:
            bv = bv * (1.0 + m_refs[0][...]) + m_refs[1][...]
        acc[...] += lax.dot_general(a_ref[...], bv.astype(_MXU), (((0,), (0,)), ((), ())),
                                    preferred_element_type=F32)

        @pl.when(pl.program_id(0) == steps - 1)
        def _():
            o_ref[...] = acc[...].astype(o_ref.dtype)

    mspecs, margs = [], []
    if nm:
        marr, l, ksc, ksh = mod
        mspecs, margs = [_sel(marr, l, ksc), _sel(marr, l, ksh)], [marr, marr]
    res = _call(
        body, name=name, grid=(steps,),
        out_shape=[jax.ShapeDtypeStruct((ka, nbc), _MXU)],
        in_specs=[pl.BlockSpec((TM, ka), lambda t: (t, 0)), pl.BlockSpec((TM, nbc), lambda t: (t, 0))] + mspecs,
        out_specs=[_full((ka, nbc))], args=(a, b, *margs), scratch=[pltpu.VMEM((ka, nbc), F32)], comm=comm)
    return [res[0].reshape(NDEV, ka // NDEV, nbc)] + list(res[1:])


def _ada_fwd(c_all, w_ada, b_cols, name):
    cols = w_ada.shape[2]

    def body(c_ref, w_ref, b_ref, cond_ref, o_ref):
        cv = c_ref[...]
        cond = cv * _sigmoid(cv)

        @pl.when(pl.program_id(0) == 0)
        def _():
            cond_ref[...] = cond

        o_ref[...] = _dot(cond, w_ref[...]) + b_ref[...]

    return _call(
        body, name=name, grid=(DEPTH,),
        out_shape=[jax.ShapeDtypeStruct((NDEV, D), F32), jax.ShapeDtypeStruct((DEPTH, NDEV, cols), F32)],
        in_specs=[_full((NDEV, D)), pl.BlockSpec((None, D, cols), lambda l: (l, 0, 0)),
                  pl.BlockSpec((None, 1, cols), lambda l: (l, 0, 0))],
        out_specs=[_full((NDEV, D)), pl.BlockSpec((None, NDEV, cols), lambda l: (l, 0, 0))],
        args=(c_all, w_ada, b_cols))


def _ada_bwd(cond, dmod_cols, name):
    cols = dmod_cols.shape[2]

    def body(c_ref, d_ref, o_ref):
        o_ref[...] = _dot_tn(c_ref[...], d_ref[...])

    return _call(
        body, name=name, grid=(DEPTH,),
        out_shape=[jax.ShapeDtypeStruct((DEPTH, D, cols), F32)],
        in_specs=[_full((NDEV, D)), pl.BlockSpec((None, NDEV, cols), lambda l: (l, 0, 0))],
        out_specs=[pl.BlockSpec((None, D, cols), lambda l: (l, 0, 0))], args=(cond, dmod_cols))[0]


def _adam_math(w, g, m, v):
    m = ADAM_B1 * m + (1.0 - ADAM_B1) * g
    v = ADAM_B2 * v + (1.0 - ADAM_B2) * (g * g)
    m_hat = m / (1.0 - ADAM_B1 ** ADAM_STEP)
    v_hat = v / (1.0 - ADAM_B2 ** ADAM_STEP)
    delta = -ADAM_LR * (m_hat / (jnp.sqrt(v_hat) + ADAM_EPS) + ADAM_WD * w)
    return delta, m, v


def _sum_parts(p_ref):
    g = p_ref[0].astype(F32)
    for i in range(1, p_ref.shape[0]):
        g = g + p_ref[i].astype(F32)
    return g


def _adamw(parts, w, m, v, tr, name):
    r, c = w.shape
    nparts = parts.shape[0]

    def body(p_ref, w_ref, m_ref, v_ref, g_ref, d_ref, nm_ref, nv_ref):
        g = _sum_parts(p_ref)
        delta, nm, nv = _adam_math(w_ref[...], g, m_ref[...], v_ref[...])
        g_ref[...] = g
        d_ref[...] = delta
        nm_ref[...] = nm
        nv_ref[...] = nv

    blk = pl.BlockSpec((tr, c), lambda i: (i, 0))
    return _call(
        body, name=name, grid=(r // tr,),
        out_shape=[jax.ShapeDtypeStruct((r, c), F32)] * 4,
        in_specs=[pl.BlockSpec((nparts, tr, c), lambda i: (0, i, 0)), blk, blk, blk],
        out_specs=[blk] * 4, args=(parts, w, m, v))


def _layer_parts_specs(parts, tr):
    nparts, r, c = parts[0].shape
    last = r // tr - 1

    def spec(k):
        return pl.BlockSpec((nparts, tr, c),
                            lambda l, i: (0, jnp.where(l < k, 0, jnp.where(l == k, i, last)), 0))

    return [spec(k) for k in range(len(parts))]


def _for_layer(p_refs, fn):
    for k, ref in enumerate(p_refs):
        pl.when(pl.program_id(0) == k)(lambda ref=ref: fn(ref))


def _adamw_layers(parts, w, m, v, tr, name):
    nl = len(parts)
    _, r, c = parts[0].shape

    def body(*refs):
        p_refs = refs[:nl]
        w_ref, m_ref, v_ref, g_ref, d_ref, nm_ref, nv_ref = refs[nl:]

        def update(p_ref):
            g = _sum_parts(p_ref)
            delta, nm, nv = _adam_math(w_ref[...], g, m_ref[...], v_ref[...])
            g_ref[...] = g
            d_ref[...] = delta
            nm_ref[...] = nm
            nv_ref[...] = nv

        _for_layer(p_refs, update)

    blk = pl.BlockSpec((tr, c), lambda l, i: (l * (r // tr) + i, 0))
    return _call(
        body, name=name, grid=(nl, r // tr),
        out_shape=[jax.ShapeDtypeStruct((nl * r, c), F32)] * 4,
        in_specs=_layer_parts_specs(parts, tr) + [blk, blk, blk],
        out_specs=[blk] * 4, args=(*parts, w, m, v))


def _reduce_layers(parts, tr, name):
    nl = len(parts)
    _, r, c = parts[0].shape

    def body(*refs):
        g_ref = refs[nl]

        def reduce(p_ref):
            g_ref[...] = _sum_parts(p_ref)

        _for_layer(refs[:nl], reduce)

    return _call(
        body, name=name, grid=(nl, r // tr), out_shape=[jax.ShapeDtypeStruct((nl, r, c), F32)],
        in_specs=_layer_parts_specs(parts, tr),
        out_specs=[pl.BlockSpec((None, tr, c), lambda l, i: (l, i, 0))], args=tuple(parts))[0]


def _cmul(ar, ai, br, bi):
    return ar * br - ai * bi, ar * bi + ai * br


def _lam_bar(lr, li, log_dt):
    dt = jnp.exp(log_dt)
    mag = jnp.exp(lr * dt)
    return dt, mag * jnp.cos(li * dt), mag * jnp.sin(li * dt)


def _layer_spec(arr):
    tail = arr.shape[1:]
    return pl.BlockSpec((None,) + tail, lambda l: (l,) + (0,) * len(tail))


def _ssm_prep(lam_re, lam_im, log_dt, b_re, b_im, name):
    def body(lr_ref, li_ref, dt_ref, br_ref, bi_ref, pr_ref, pi_ref, or_ref, oi_ref):
        lr, li = lr_ref[...], li_ref[...]
        _, ar, ai = _lam_bar(lr, li, dt_ref[...])
        den = lr * lr + li * li
        fr, fi = _cmul(ar - 1.0, ai, lr / den, -li / den)
        obr, obi = _cmul(fr[:, None, :], fi[:, None, :], br_ref[...], bi_ref[...])
        or_ref[...] = obr
        oi_ref[...] = obi
        qr, qi = ar, ai
        for k in range(SUB):
            pr_ref[k] = qr
            pi_ref[k] = qi
            qr, qi = _cmul(qr, qi, ar, ai)

    nl, g, p = lam_re.shape
    h = b_re.shape[2]
    outs = [jax.ShapeDtypeStruct((nl, SUB, g, p), F32)] * 2 + [jax.ShapeDtypeStruct((nl, g, h, p), F32)] * 2
    args = (lam_re, lam_im, log_dt, b_re, b_im)
    return _call(body, name=name, grid=(nl,), out_shape=outs, in_specs=[_layer_spec(a) for a in args],
                 out_specs=[_layer_spec(o) for o in outs], args=args)


def _ssm_param_bwd(lam_re, lam_im, log_dt, b_re, b_im, dlr, dli, dbr, dbi, name):
    def body(lr_ref, li_ref, dt_ref, br_ref, bi_ref, gar_ref, gai_ref, gbr_ref, gbi_ref,
             olr_ref, oli_ref, odt_ref, obr_ref, obi_ref):
        lr, li = lr_ref[...], li_ref[...]
        dt, ar, ai = _lam_bar(lr, li, dt_ref[...])
        den = lr * lr + li * li
        ir, ii = lr / den, -li / den
        fr, fi = _cmul(ar - 1.0, ai, ir, ii)
        br, bi = br_ref[...], bi_ref[...]
        gbr, gbi = gbr_ref[...], gbi_ref[...]
        obr, obi = _cmul(fr[:, None, :], -fi[:, None, :], gbr, gbi)
        obr_ref[...] = obr
        obi_ref[...] = obi
        pr, pi = _cmul(br, -bi, gbr, gbi)
        gfr, gfi = jnp.sum(pr, axis=1), jnp.sum(pi, axis=1)
        t_r, t_i = _cmul(gfr, gfi, ir, -ii)
        gar, gai = gar_ref[...] + t_r, gai_ref[...] + t_i
        qr, qi = _cmul(fr, fi, ir, ii)
        l1r, l1i = _cmul(gfr, gfi, qr, -qi)
        l2r, l2i = _cmul(gar, gai, dt * ar, -dt * ai)
        olr_ref[...] = l2r - l1r
        oli_ref[...] = l2i - l1i
        mr, mi = _cmul(lr, li, ar, ai)
        dr, _ = _cmul(gar, gai, mr, -mi)
        odt_ref[...] = jnp.sum(dr, axis=1, keepdims=True) * dt

    nl, g, p = lam_re.shape
    h = b_re.shape[2]
    outs = ([jax.ShapeDtypeStruct((nl, g, p), F32)] * 2 + [jax.ShapeDtypeStruct((nl, g, 1), F32)]
            + [jax.ShapeDtypeStruct((nl, g, h, p), F32)] * 2)
    args = (lam_re, lam_im, log_dt, b_re, b_im, dlr, dli, dbr, dbi)
    return _call(body, name=name, grid=(nl,), out_shape=outs, in_specs=[_layer_spec(a) for a in args],
                 out_specs=[_layer_spec(o) for o in outs], args=args)


def _gelu(x):
    k = math.sqrt(2.0 / math.pi)
    return 0.5 * x * (1.0 + jnp.tanh(k * (x + 0.044715 * x * x * x)))


def _gelu_grad(x):
    k = math.sqrt(2.0 / math.pi)
    th = jnp.tanh(k * (x + 0.044715 * x * x * x))
    return 0.5 * (1.0 + th) + 0.5 * x * (1.0 - th * th) * k * (1.0 + 3 * 0.044715 * x * x)


def _pool_inv_count(chunk):
    t = lax.broadcasted_iota(jnp.int32, (T, GW), 0) + chunk * T + 1
    lane = lax.broadcasted_iota(jnp.int32, (T, GW), 1)
    win = jnp.where(lane < 64, 2, jnp.where(lane < 128, 4, jnp.where(lane < 192, 8, 16)))
    return 1.0 / jnp.minimum(t, win).astype(F32)


def _window_sums(v, bufs, base, step):
    sums = []
    for k, buf in enumerate(bufs):
        buf[base:base + T, :] = v
        v = v + buf[pl.ds(base + step * (1 << k), T), :]
        sums.append(v)
    lane = lax.broadcasted_iota(jnp.int32, (T, GW), 1)
    return jnp.where(lane < 64, sums[0], jnp.where(lane < 128, sums[1], jnp.where(lane < 192, sums[2], sums[3])))


def _phase_copies(buf, rot):
    n = HALO + T - SUB
    for p in range(1, SUB):
        rot[p - 1, 0:n, :] = buf[pl.ds(p, n), :]


def _tap(buf, rot, off):
    q, p = divmod(off, SUB)
    return buf[off:off + T, :] if p == 0 else rot[p - 1, SUB * q:SUB * q + T, :]


def _scan(x_scr, pw_ref, pwrev_ref, carry, reverse):
    row8 = lax.broadcasted_iota(jnp.int32, (T, 128), 0) % SUB
    nb = T // SUB
    for j in range(XH // 128):
        lo, hi = j * 128, XH + j * 128
        re, im = x_scr[:, lo:lo + 128], x_scr[:, hi:hi + 128]
        for d in (1, 2, 4):
            ar, ai = pw_ref[d - 1:d, lo:lo + 128], pw_ref[d - 1:d, hi:hi + 128]
            if reverse:
                ai = -ai
                keep = row8 < SUB - d
                sre, sim = pltpu.roll(re, T - d, 0), pltpu.roll(im, T - d, 0)
            else:
                keep = row8 >= d
                sre, sim = pltpu.roll(re, d, 0), pltpu.roll(im, d, 0)
            sre, sim = jnp.where(keep, sre, 0.0), jnp.where(keep, sim, 0.0)
            re, im = re + ar * sre - ai * sim, im + ar * sim + ai * sre
        cr, ci = carry[:, lo:lo + 128], carry[:, hi:hi + 128]
        if reverse:
            pr, pi = pwrev_ref[:, lo:lo + 128], -pwrev_ref[:, hi:hi + 128]
            order, edge = reversed(range(nb)), 0
        else:
            pr, pi = pw_ref[:, lo:lo + 128], pw_ref[:, hi:hi + 128]
            order, edge = range(nb), SUB - 1
        for b in order:
            rb, ib = re[SUB * b:SUB * b + SUB], im[SUB * b:SUB * b + SUB]
            crb, cib = jnp.broadcast_to(cr, (SUB, 128)), jnp.broadcast_to(ci, (SUB, 128))
            rb, ib = rb + pr * crb - pi * cib, ib + pr * cib + pi * crb
            x_scr[SUB * b:SUB * b + SUB, lo:lo + 128] = rb
            x_scr[SUB * b:SUB * b + SUB, hi:hi + 128] = ib
            cr, ci = rb[edge:edge + 1], ib[edge:edge + 1]


MIXER_PARAMS = ('scw', 'pblk', 'pscale', 'dww', 'dwb', 'clg', 'clb', 'bblk', 'cblk', 'pw', 'pwrev', 'ssd', 'wglu',
                'bglu')


def _mixer_fwd(x, mod, l, w_in_t, b_in, mp, name, comm=None):
    s = x.shape[0]
    nch = s // T
    params = [mp[k] for k in MIXER_PARAMS]

    def body(x_ref, sc_ref, sh_ref, w_ref, b_ref, scw_ref, pblk_ref, ps_ref, dww_ref, dwb_ref, clg_ref, clb_ref,
             bblk_ref, cblk_ref, pw_ref, pwrev_ref, ssd_ref, wglu_ref, bglu_ref, y_ref, xs_ref, aux_ref, z_ref,
             qbuf, hbuf, pb1, pb2, pb4, pb8, rot, carry):
        i = pl.program_id(0)

        @pl.when(i == 0)
        def _():
            for ref in (qbuf, hbuf, pb1, pb2, pb4, pb8):
                ref[0:HALO, :] = jnp.zeros((HALO, GW), F32)
            carry[...] = jnp.zeros_like(carry)

        h = x_ref[...] * (1.0 + sc_ref[...]) + sh_ref[...]
        z_ref[...] = _dot_nt(h, w_ref[...]) + b_ref[...]
        zh, zb, zc = z_ref[:, 0:GW], z_ref[:, GW:2 * GW], z_ref[:, 2 * GW:3 * GW]
        zp, zcv, zcg, zs = (z_ref[:, 3 * GW:4 * GW], z_ref[:, 4 * GW:5 * GW], z_ref[:, 5 * GW:6 * GW],
                            z_ref[:, 6 * GW:7 * GW])
        qbuf[HALO:HALO + T, :] = zc * zh
        conv = None
        for k in range(SC_TAPS):
            v = scw_ref[k:k + 1, :] * qbuf[pl.ds(HALO - (SC_TAPS - 1) + k, T), :]
            conv = v if conv is None else conv + v
        aux_ref[:, 0:GW] = conv
        y_ref[:, 0:GW] = (zb * conv).astype(y_ref.dtype)
        r = _window_sums(zp, (pb1, pb2, pb4, pb8), HALO, -1) * _pool_inv_count(i) - zp
        aux_ref[:, GW:2 * GW] = r
        y_ref[:, GW:2 * GW] = (_dot(r, pblk_ref[...]) * ps_ref[...]).astype(y_ref.dtype)
        hbuf[HALO:HALO + T, :] = zcv * _sigmoid(zcg)
        _phase_copies(hbuf, rot)
        hc = None
        for k in range(CF_TAPS):
            v = dww_ref[k:k + 1, :] * _tap(hbuf, rot, HALO - (CF_TAPS - 1) + k)
            hc = v if hc is None else hc + v
        hc = hc + dwb_ref[...]
        aux_ref[:, 2 * GW:3 * GW] = hc
        mu = jnp.mean(hc, axis=-1, keepdims=True)
        xc = hc - mu
        var = jnp.mean(xc * xc, axis=-1, keepdims=True)
        ln = xc * lax.rsqrt(var + LN_EPS) * clg_ref[...] + clb_ref[...]
        y_ref[:, 2 * GW:3 * GW] = (ln * _sigmoid(ln)).astype(y_ref.dtype)
        xs_ref[...] = _dot(zs, bblk_ref[...])
        _scan(xs_ref, pw_ref, pwrev_ref, carry[...], reverse=False)
        carry[...] = xs_ref[T - 1:T, :]
        yy = _dot(xs_ref[...], cblk_ref[...]) + ssd_ref[...] * zs
        aux_ref[:, 3 * GW:4 * GW] = yy
        yg = _gelu(yy)
        v = _dot(yg, wglu_ref[...]) + bglu_ref[...]
        y_ref[:, 3 * GW:4 * GW] = (yg * _sigmoid(v)).astype(y_ref.dtype)
        for ref in (qbuf, hbuf, pb1, pb2, pb4, pb8):
            ref[0:HALO, :] = ref[T:T + HALO, :]

    fwd = lambda i: (i, 0)
    return _call(
        body, name=name, grid=(nch,),
        out_shape=[jax.ShapeDtypeStruct((s, D), _MXU), jax.ShapeDtypeStruct((s, XW), F32),
                   jax.ShapeDtypeStruct((s, 4 * GW), F32), jax.ShapeDtypeStruct((s, INW), F32)],
        in_specs=[pl.BlockSpec((T, D), fwd), _sel(mod, l, SC1), _sel(mod, l, SH1), _resident(w_in_t.shape),
                  _sel(b_in, l)] + [_sel(p, l) for p in params],
        out_specs=[pl.BlockSpec((T, D), fwd), pl.BlockSpec((T, XW), fwd), pl.BlockSpec((T, 4 * GW), fwd),
                   pl.BlockSpec((T, INW), fwd)],
        scratch=[pltpu.VMEM((HALO + T, GW), F32)] * 6
        + [pltpu.VMEM((SUB - 1, HALO + T, GW), F32), pltpu.VMEM((1, XW), F32)],
        args=(x, mod, mod, w_in_t, b_in, *params), comm=comm)


def _mixer_bwd(z, dres, w_o, xs, aux, mp, l, name):
    s = z.shape[0]
    nch = s // T
    params = [mp[k] for k in MIXER_PARAMS]

    def body(z_ref, zprev_ref, dres_ref, wo_ref, xs_ref, xprev_ref, aux_ref, scw_ref, pblk_ref, ps_ref, dww_ref,
             dwb_ref, clg_ref, clb_ref, bblk_ref, cblk_ref, pw_ref, pwrev_ref, ssd_ref, wglu_ref, bglu_ref,
             dz_ref, vec_ref, small_ref, dpblk_ref, dwglu_ref, dbblk_ref, dcblk_ref, dlam_ref,
             qbuf, hbuf, dcbuf, dhbuf, eb1, eb2, eb4, eb8, rot_h, rot_d, acarry, a_scr, dy_ref):
        i = pl.program_id(0)
        chunk = nch - 1 - i
        dy_ref[...] = _dot_nt(dres_ref[...], wo_ref[...])

        @pl.when(i == 0)
        def _():
            for ref in (vec_ref, small_ref, dpblk_ref, dwglu_ref, dbblk_ref, dcblk_ref, dlam_ref, acarry):
                ref[...] = jnp.zeros_like(ref)
            for ref in (dcbuf, dhbuf, eb1, eb2, eb4, eb8):
                ref[T:T + HALO, :] = jnp.zeros((HALO, GW), F32)

        zh, zb, zc = z_ref[:, 0:GW], z_ref[:, GW:2 * GW], z_ref[:, 2 * GW:3 * GW]
        zcv, zcg, zs = z_ref[:, 4 * GW:5 * GW], z_ref[:, 5 * GW:6 * GW], z_ref[:, 6 * GW:7 * GW]
        conv, r, hc, yy = (aux_ref[:, 0:GW], aux_ref[:, GW:2 * GW], aux_ref[:, 2 * GW:3 * GW],
                           aux_ref[:, 3 * GW:4 * GW])
        has_prev = (chunk > 0).astype(F32)
        qbuf[0:HALO, :] = zprev_ref[:, 2 * GW:3 * GW] * zprev_ref[:, 0:GW] * has_prev
        hbuf[0:HALO, :] = zprev_ref[:, 4 * GW:5 * GW] * _sigmoid(zprev_ref[:, 5 * GW:6 * GW]) * has_prev

        dya = dy_ref[:, 0:GW]
        qbuf[HALO:HALO + T, :] = zc * zh
        dconv = dya * zb
        dcbuf[0:T, :] = dconv
        dq = None
        for k in range(SC_TAPS):
            small_ref[k:k + 1, :] += _colsum(dconv * qbuf[pl.ds(HALO - (SC_TAPS - 1) + k, T), :])
            v = scw_ref[k:k + 1, :] * dcbuf[pl.ds(SC_TAPS - 1 - k, T), :]
            dq = v if dq is None else dq + v
        dzh, dzb, dzc = dq * zc, dya * conv, dq * zh

        dyb = dy_ref[:, GW:2 * GW]
        o = _dot(r, pblk_ref[...])
        small_ref[3:4, :] += _colsum(dyb * o)
        do = dyb * ps_ref[...]
        dpblk_ref[...] += _dot_tn(r, do)
        dr = _dot_nt(do, pblk_ref[...])
        dzp = _window_sums(dr * _pool_inv_count(chunk), (eb1, eb2, eb4, eb8), 0, 1) - dr

        dyc = dy_ref[:, 2 * GW:3 * GW]
        sgc = _sigmoid(zcg)
        hbuf[HALO:HALO + T, :] = zcv * sgc
        mu = jnp.mean(hc, axis=-1, keepdims=True)
        xc = hc - mu
        var = jnp.mean(xc * xc, axis=-1, keepdims=True)
        rstd = lax.rsqrt(var + LN_EPS)
        nrm = xc * rstd
        ln = nrm * clg_ref[...] + clb_ref[...]
        sl = _sigmoid(ln)
        dln = dyc * (sl * (1.0 + ln * (1.0 - sl)))
        small_ref[5:6, :] += _colsum(dln * nrm)
        small_ref[6:7, :] += _colsum(dln)
        dn = dln * clg_ref[...]
        dhc = rstd * (dn - jnp.mean(dn, axis=-1, keepdims=True) - nrm * jnp.mean(dn * nrm, axis=-1, keepdims=True))
        small_ref[4:5, :] += _colsum(dhc)
        dhbuf[0:T, :] = dhc
        _phase_copies(hbuf, rot_h)
        _phase_copies(dhbuf, rot_d)
        dhg = None
        for k in range(CF_TAPS):
            small_ref[16 + k:17 + k, :] += _colsum(dhc * _tap(hbuf, rot_h, HALO - (CF_TAPS - 1) + k))
            v = dww_ref[k:k + 1, :] * _tap(dhbuf, rot_d, CF_TAPS - 1 - k)
            dhg = v if dhg is None else dhg + v
        dzcv = dhg * sgc
        dzcg = dhg * zcv * sgc * (1.0 - sgc)

        dyd = dy_ref[:, 3 * GW:4 * GW]
        yg = _gelu(yy)
        v = _dot(yg, wglu_ref[...]) + bglu_ref[...]
        sg = _sigmoid(v)
        dv = dyd * yg * sg * (1.0 - sg)
        small_ref[8:9, :] += _colsum(dv)
        dwglu_ref[...] += _dot_tn(yg, dv)
        g = (dyd * sg + _dot_nt(dv, wglu_ref[...])) * _gelu_grad(yy)
        small_ref[7:8, :] += _colsum(g * zs)
        dcblk_ref[...] += _dot_tn(xs_ref[...], g)
        a_scr[...] = _dot_nt(g, cblk_ref[...])
        _scan(a_scr, pw_ref, pwrev_ref, acarry[...], reverse=True)
        acarry[...] = a_scr[0:1, :]
        dzs = _dot_nt(a_scr[...], bblk_ref[...]) + ssd_ref[...] * g
        dbblk_ref[...] += _dot_tn(zs, a_scr[...])
        row = lax.broadcasted_iota(jnp.int32, (T, 128), 0)
        for j in range(XH // 128):
            lo, hi = j * 128, XH + j * 128
            first_r = xprev_ref[SUB - 1:SUB, lo:lo + 128] * has_prev
            first_i = xprev_ref[SUB - 1:SUB, hi:hi + 128] * has_prev
            pr = jnp.where(row == 0, first_r, pltpu.roll(xs_ref[:, lo:lo + 128], 1, 0))
            pi = jnp.where(row == 0, first_i, pltpu.roll(xs_ref[:, hi:hi + 128], 1, 0))
            ar, ai = a_scr[:, lo:lo + 128], a_scr[:, hi:hi + 128]
            dlam_ref[0:1, lo:lo + 128] += _colsum(ar * pr + ai * pi)
            dlam_ref[0:1, hi:hi + 128] += _colsum(ai * pr - ar * pi)

        parts = (dzh, dzb, dzc, dzp, dzcv, dzcg, dzs)
        for k, part in enumerate(parts):
            dz_ref[:, k * GW:(k + 1) * GW] = part.astype(dz_ref.dtype)
            vec_ref[0:1, k * GW:(k + 1) * GW] += _colsum(part)
        for ref in (dcbuf, dhbuf, eb1, eb2, eb4, eb8):
            ref[T:T + HALO, :] = ref[0:HALO, :]

    rev = lambda i: (nch - 1 - i, 0)

    def before(rows):
        return lambda i: (jnp.maximum((nch - 1 - i) * (T // rows) - 1, 0), 0)

    outs = [((s, INW), _MXU), ((8, INW), F32), ((48, GW), F32), ((GW, GW), F32), ((GW, GW), F32),
            ((GW, XW), F32), ((XW, GW), F32), ((8, XW), F32)]
    return _call(
        body, name=name, grid=(nch,),
        out_shape=[jax.ShapeDtypeStruct(sh, dt) for sh, dt in outs],
        in_specs=[pl.BlockSpec((T, INW), rev), pl.BlockSpec((HALO, INW), before(HALO)), pl.BlockSpec((T, D), rev),
                  _resident(w_o.shape), pl.BlockSpec((T, XW), rev), pl.BlockSpec((SUB, XW), before(SUB)),
                  pl.BlockSpec((T, 4 * GW), rev)]
        + [_sel(p, l) for p in params],
        out_specs=[pl.BlockSpec((T, INW), rev)] + [_full(sh) for sh, _ in outs[1:]],
        scratch=[pltpu.VMEM((HALO + T, GW), F32)] * 8
        + [pltpu.VMEM((SUB - 1, HALO + T, GW), F32)] * 2
        + [pltpu.VMEM((1, XW), F32), pltpu.VMEM((T, XW), F32), pltpu.VMEM((T, D), F32)],
        args=(z, z, dres, w_o, xs, xs, aux, *params))


def _blockdiag(m):
    *lead, g, h, p = m.shape
    eye = jnp.eye(g, dtype=m.dtype)
    return (m[..., :, :, None, :] * eye[:, None, :, None]).reshape(*lead, g * h, g * p)


def _blockdiag_extract(f, h, p):
    *lead, r, _ = f.shape
    g = r // h
    eye = jnp.eye(g, dtype=f.dtype)
    return jnp.sum(f.reshape(*lead, g, h, g, p) * eye[:, None, :, None], axis=-2)


def _t(a):
    return jnp.swapaxes(a, -1, -2)


PACK_ROWS = 512


def _pack(arrs):
    flat = jnp.concatenate([a.reshape(-1).astype(F32) for a in arrs])
    quantum = PACK_ROWS * 128
    padded = -(-flat.shape[0] // quantum) * quantum
    return jnp.pad(flat, (0, padded - flat.shape[0])).reshape(-1, 128)


def _unpack(buf, shapes, lead=()):
    flat = buf.reshape(lead + (-1,))
    out, off = [], 0
    for shp in shapes:
        n = math.prod(shp)
        out.append(flat[..., off:off + n].reshape(lead + tuple(shp)))
        off += n
    return out


def _cols_to_full(g):
    n, l, r, c = g.shape
    return g.transpose(1, 2, 0, 3).reshape(l, r, n * c)


def _rows_to_full(g):
    n, l, r, c = g.shape
    return g.transpose(1, 0, 2, 3).reshape(l, n * r, c)


REPLICATED = ['b_ada', 'b_in', 'pool_w', 'pool_scale', 'cf_dw_b', 'cf_ln_g', 'cf_ln_b', 'ssm_lam_re', 'ssm_lam_im',
              'ssm_log_dt', 'ssm_b_re', 'ssm_b_im', 'ssm_c_re', 'ssm_c_im', 'ssm_d', 'ssm_b_glu', 'ln1_g', 'ln1_b',
              'ln2_g', 'ln2_b']
SMALL_SHARDED = ['sc_w', 'cf_dw_w', 'ssm_w_glu']
COL_SHARDED = ['w_in', 'w_gate', 'w_up']
ROW_SHARDED = ['w_o', 'w_down']
BIG = COL_SHARDED + ROW_SHARDED
WEIGHTS = ['w_ada', 'b_ada', 'w_in', 'b_in', 'sc_w', 'pool_w', 'pool_scale', 'cf_dw_w', 'cf_dw_b', 'cf_ln_g',
           'cf_ln_b', 'ssm_lam_re', 'ssm_lam_im', 'ssm_log_dt', 'ssm_b_re', 'ssm_b_im', 'ssm_c_re', 'ssm_c_im',
           'ssm_d', 'ssm_w_glu', 'ssm_b_glu', 'w_o', 'ln1_g', 'ln1_b', 'w_gate', 'w_up', 'w_down', 'ln2_g', 'ln2_b']
SH1, SC1, G1, SH2, SC2, G2 = range(6)


def _train_step(p):
    xi, yi, ci = _me()
    me = _flat((xi, yi, ci))
    s = p['x'].shape[1]
    xs = p['x'].reshape(s, D)
    target = p['loss_target'].reshape(s, D)
    vec3 = lambda a: a.reshape(DEPTH, 1, -1)

    def shard(n, l):
        w = p[n][l]
        return (w.T if n in COL_SHARDED else w).astype(_MXU)

    def whole(g):
        return g.reshape(-1, g.shape[-1])

    small_shapes = [p[n].shape for n in SMALL_SHARDED] + [p['c'].shape]
    first = ('w_in', 'w_o')
    gathered = _allgather([_pack([p[n] for n in SMALL_SHARDED] + [p['c']])] + [shard(n, 0) for n in first],
                          "gather_first")
    scw_g, dww_g, wglu_g, c_g = _unpack(gathered[0], small_shapes, lead=(NDEV,))
    full = {'sc_w': _cols_to_full(scw_g), 'cf_dw_w': _cols_to_full(dww_g), 'ssm_w_glu': _rows_to_full(wglu_g)}
    wts = [dict(zip(first, (whole(g) for g in gathered[1:])))] + [dict() for _ in range(DEPTH - 1)]
    c_all = c_g.reshape(NDEV, D)

    ncol = p['w_ada'].shape[2]
    b_cols = lax.dynamic_slice_in_dim(p['b_ada'], me * ncol, ncol, axis=1).reshape(DEPTH, 1, ncol)
    cond, mod_cols = _ada_fwd(c_all, p['w_ada'], b_cols, "ada_fwd")
    (mod_x,) = _alltoall([mod_cols.transpose(1, 0, 2)], "mod_exchange")
    mod = mod_x.transpose(1, 0, 2).reshape(DEPTH, 6, 1, D)

    ssm_in = (p['ssm_lam_re'], p['ssm_lam_im'], p['ssm_log_dt'].reshape(DEPTH, NGRP, 1),
              _t(p['ssm_b_re']), _t(p['ssm_b_im']))
    pwr, pwi, bbr, bbi = _ssm_prep(*ssm_in, "ssm_prep")
    pw = jnp.concatenate([pwr.reshape(DEPTH, SUB, XH), pwi.reshape(DEPTH, SUB, XH)], axis=2)
    mp = dict(
        scw=full['sc_w'], pblk=_blockdiag(p['pool_w']).astype(_MXU), pscale=vec3(p['pool_scale']),
        dww=full['cf_dw_w'], dwb=vec3(p['cf_dw_b']), clg=vec3(p['cf_ln_g']), clb=vec3(p['cf_ln_b']),
        bblk=jnp.concatenate([_blockdiag(bbr), _blockdiag(bbi)], axis=2).astype(_MXU),
        cblk=jnp.concatenate([_blockdiag(_t(p['ssm_c_re'])), -_blockdiag(_t(p['ssm_c_im']))], axis=1).astype(_MXU),
        pw=pw, pwrev=pw[:, ::-1], ssd=vec3(p['ssm_d']), wglu=full['ssm_w_glu'].astype(_MXU),
        bglu=vec3(p['ssm_b_glu']))
    b_in, ln1_g, ln1_b, ln2_g, ln2_b = (vec3(p[n]) for n in ('b_in', 'ln1_g', 'ln1_b', 'ln2_g', 'ln2_b'))

    saved = []
    for l in range(DEPTH):
        w = wts[l]

        def gather(at, *names):
            return None if at >= DEPTH or not names else (GATHER, [shard(n, at) for n in names])

        def take(res, n_own, at, *names):
            if at < DEPTH:
                for n, g in zip(names, res[n_own:]):
                    wts[at][n] = whole(g)
            return res[:n_own]

        ycat, states, aux, z = take(_mixer_fwd(xs, mod, l, w['w_in'], b_in, mp, "mixer_fwd",
                                               comm=gather(l, 'w_gate', 'w_up')), 4, l, 'w_gate', 'w_up')
        early = ('w_down',) if l == 0 else ()
        y, u1, x1 = take(_proj_residual_ln(ycat, w['w_o'], l, xs, mod, G1, ln1_g, ln1_b, "o_proj_ln",
                                           comm=gather(l, *early)), 3, l, *early)
        gate, up, act = take(_modulated_matmul(x1, mod, l, SC2, SH2, [w['w_gate'], w['w_up']], None, TM_WIDE,
                                               [_MXU, _MXU, _MXU], _swiglu, "gate_up",
                                               comm=gather(l + 1, 'w_in', 'w_o')), 3, l + 1, 'w_in', 'w_o')
        f, u2, x2 = take(_proj_residual_ln(act, w['w_down'], l, x1, mod, G2, ln2_g, ln2_b, "down_proj_ln",
                                           comm=gather(l + 1, 'w_down')), 3, l + 1, 'w_down')
        saved.append(dict(x=xs, z=z, ycat=ycat, states=states, aux=aux, y=y, u1=u1, x1=x1, gate=gate, up=up,
                          act=act, f=f, u2=u2))
        xs = x2

    recv = {n: [None] * DEPTH for n in BIG}
    small_g = {n: [None] * DEPTH for n in ('ln1_g', 'ln1_b', 'ln2_g', 'ln2_b')}
    dmod = [None] * DEPTH
    mixer_out = [None] * DEPTH
    pending = None
    top = DEPTH - 1
    du2, df, acc_top = _loss_ln_bwd(xs, target, saved[top]['u2'], saved[top]['f'], ln2_g, top, mod, G2,
                                    "loss_ln2_bwd")
    small_g['ln2_g'][top], small_g['ln2_b'][top], dg2 = acc_top[0], acc_top[1], acc_top[2]
    loss = lax.psum(acc_top[3, 0] * (0.5 / D), AXES)
    for l in reversed(range(DEPTH)):
        sv, w = saved[l], wts[l]
        (g_down,) = _weight_grad(sv['act'], df, "dw_down")
        lp, late_names, late = pending or (None, [], [])
        dgate, dup, du1, dyb, acc_a, recv['w_down'][l], *landed = _input_grad_ln_bwd(
            None, [w['w_gate'], w['w_up']], du2, (mod, l, SC2), sv['u1'], sv['y'], (ln1_g, l), (ln1_b, l),
            (mod, l, G1), TM_WIDE, "mlp_bwd", swiglu=(df, w['w_down'], sv['gate'], sv['up']),
            comm=(EXCHANGE, [g_down] + late))
        for n, r in zip(late_names, landed):
            recv[n][lp] = r
        (g_gate,) = _weight_grad(dgate, sv['x1'], "dw_gate", mod=(mod, l, SC2, SH2))
        (g_up,) = _weight_grad(dup, sv['x1'], "dw_up", mod=(mod, l, SC2, SH2))
        (g_o,) = _weight_grad(sv['ycat'], dyb, "dw_o")
        dz, *mixer_out[l] = _mixer_bwd(sv['z'], dyb, w['w_o'], sv['states'], sv['aux'], mp, l, "mixer_bwd")
        if l > 0:
            du2, df, acc_b, recv['w_gate'][l] = _input_grad_ln_bwd(
                [dz], [w['w_in']], du1, (mod, l, SC1), saved[l - 1]['u2'], saved[l - 1]['f'], (ln2_g, l - 1),
                (ln2_b, l - 1), (mod, l - 1, G2), TM_WIDE, "dh1_ln2_bwd", comm=(EXCHANGE, [g_gate]))
            (g_in,) = _weight_grad(dz, sv['x'], "dw_in", mod=(mod, l, SC1, SH1))
            pending = (l, ['w_up', 'w_o', 'w_in'], [g_up, g_o, g_in])
        else:
            dx, acc_b, recv['w_gate'][l] = _input_grad([dz], [w['w_in']], l, du1, sv['x'], mod, SC1, TM, "dh1",
                                                       comm=(EXCHANGE, [g_gate]))
            g_in, recv['w_up'][l] = _weight_grad(dz, sv['x'], "dw_in", mod=(mod, l, SC1, SH1),
                                                 comm=(EXCHANGE, [g_up]))
            pending = (l, ['w_o', 'w_in'], [g_o, g_in])
        dmod[l] = jnp.concatenate([acc_b[0], acc_b[1], acc_a[4], acc_a[0], acc_a[1], dg2])
        small_g['ln1_g'][l], small_g['ln1_b'][l] = acc_a[2], acc_a[3]
        if l > 0:
            small_g['ln2_g'][l - 1], small_g['ln2_b'][l - 1], dg2 = acc_b[2], acc_b[3], acc_b[4]
    lp, late_names, late = pending
    for n, r in zip(late_names, _alltoall(late, "exchange_last")):
        recv[n][lp] = r

    partial = {n: jnp.stack(v) for n, v in small_g.items()}
    partial['b_ada'] = jnp.stack(dmod)
    vec, small, dpblk, dwglu, dbblk, dcblk, dlam = (jnp.stack(t) for t in zip(*mixer_out))
    partial.update(
        b_in=vec[:, 0], sc_w=small[:, 0:SC_TAPS], pool_scale=small[:, 3], cf_dw_b=small[:, 4], cf_ln_g=small[:, 5],
        cf_ln_b=small[:, 6], ssm_d=small[:, 7], ssm_b_glu=small[:, 8], cf_dw_w=small[:, 16:16 + CF_TAPS],
        pool_w=_blockdiag_extract(dpblk, 64, 64), ssm_w_glu=dwglu,
        ssm_c_re=_t(_blockdiag_extract(dcblk[:, :XH], NSTATE, NCH)),
        ssm_c_im=-_t(_blockdiag_extract(dcblk[:, XH:], NSTATE, NCH)))
    dlr, dli, dlog, dbr, dbi = _ssm_param_bwd(
        *ssm_in, dlam[:, 0, :XH].reshape(DEPTH, NGRP, NSTATE), dlam[:, 0, XH:].reshape(DEPTH, NGRP, NSTATE),
        _blockdiag_extract(dbblk[:, :, :XH], NCH, NSTATE), _blockdiag_extract(dbblk[:, :, XH:], NCH, NSTATE),
        "ssm_param_bwd")
    partial.update(ssm_lam_re=dlr, ssm_lam_im=dli, ssm_log_dt=dlog.reshape(DEPTH, NGRP), ssm_b_re=_t(dbr),
                   ssm_b_im=_t(dbi))

    out = {'loss': loss, 'grad_x': dx.reshape(1, s, D)}

    def emit(n, g, delta, nm, nv):
        shp = p[n].shape
        out['grad_' + n], out['delta_' + n] = g.reshape(shp), delta.reshape(shp)
        out['new_m_' + n], out['new_v_' + n] = nm.reshape(shp), nv.reshape(shp)

    names_a = REPLICATED + SMALL_SHARDED
    shapes_a = [partial[n].shape for n in names_a]
    zeros_like_full = lambda n: jnp.zeros(partial[n].shape, F32)
    (parts_a,) = _allgather([_pack([partial[n] for n in names_a])], "gather_small_grads")
    packs = [_pack([p[pre + n] for n in REPLICATED] + [zeros_like_full(n) for n in SMALL_SHARDED])
             for pre in ('', 'm_', 'v_')]
    res_a = _adamw(parts_a, *packs, PACK_ROWS, "adamw_small")
    res_a = [_unpack(r, shapes_a) for r in res_a]
    for k, n in enumerate(REPLICATED):
        emit(n, *[r[k] for r in res_a])
    reduced = dict(zip(names_a, res_a[0]))

    blocks = []
    for n in SMALL_SHARDED:
        axis = 1 if n == 'ssm_w_glu' else 2
        width = p[n].shape[axis]
        blocks.append(lax.dynamic_slice_in_dim(reduced[n], me * width, width, axis=axis))
    shapes_b = [b.shape for b in blocks]
    packs = [_pack([p[pre + n] for n in SMALL_SHARDED]) for pre in ('', 'm_', 'v_')]
    res_b = _adamw(_pack(blocks)[None], *packs, PACK_ROWS, "adamw_small_sharded")
    res_b = [_unpack(r, shapes_b) for r in res_b]
    for k, n in enumerate(SMALL_SHARDED):
        emit(n, *[r[k] for r in res_b])

    dmod_all = _unpack(parts_a, shapes_a, lead=(NDEV,))[0]
    dmod_cols = lax.dynamic_slice_in_dim(dmod_all, me * ncol, ncol, axis=2).transpose(1, 0, 2)
    g_ada = _ada_bwd(cond, dmod_cols, "ada_bwd")
    flat2 = lambda a: a.reshape(-1, a.shape[-1])
    emit('w_ada', *_adamw(flat2(g_ada)[None], flat2(p['w_ada']), flat2(p['m_w_ada']), flat2(p['v_w_ada']),
                          512, "adamw_w_ada"))

    for n in BIG:
        tr = recv[n][0].shape[1] // 2
        if n in ROW_SHARDED:
            emit(n, *_adamw_layers(recv[n], flat2(p[n]), flat2(p['m_' + n]), flat2(p['v_' + n]), tr, "adamw_" + n))
        else:
            g = flat2(_t(_reduce_layers(recv[n], tr, "sum_" + n)))
            emit(n, *_adamw(g[None], flat2(p[n]), flat2(p['m_' + n]), flat2(p['v_' + n]), 512, "adamw_" + n))
    return out


def kernel(x, c, w_ada, b_ada, w_in, b_in, sc_w, pool_w, pool_scale, cf_dw_w, cf_dw_b, cf_ln_g, cf_ln_b, ssm_lam_re, ssm_lam_im, ssm_log_dt, ssm_b_re, ssm_b_im, ssm_c_re, ssm_c_im, ssm_d, ssm_w_glu, ssm_b_glu, w_o, ln1_g, ln1_b, w_gate, w_up, w_down, ln2_g, ln2_b, loss_target, m_w_ada, m_b_ada, m_w_in, m_b_in, m_sc_w, m_pool_w, m_pool_scale, m_cf_dw_w, m_cf_dw_b, m_cf_ln_g, m_cf_ln_b, m_ssm_lam_re, m_ssm_lam_im, m_ssm_log_dt, m_ssm_b_re, m_ssm_b_im, m_ssm_c_re, m_ssm_c_im, m_ssm_d, m_ssm_w_glu, m_ssm_b_glu, m_w_o, m_ln1_g, m_ln1_b, m_w_gate, m_w_up, m_w_down, m_ln2_g, m_ln2_b, v_w_ada, v_b_ada, v_w_in, v_b_in, v_sc_w, v_pool_w, v_pool_scale, v_cf_dw_w, v_cf_dw_b, v_cf_ln_g, v_cf_ln_b, v_ssm_lam_re, v_ssm_lam_im, v_ssm_log_dt, v_ssm_b_re, v_ssm_b_im, v_ssm_c_re, v_ssm_c_im, v_ssm_d, v_ssm_w_glu, v_ssm_b_glu, v_w_o, v_ln1_g, v_ln1_b, v_w_gate, v_w_up, v_w_down, v_ln2_g, v_ln2_b):
    out = _train_step(dict(locals()))
    return (out['loss'], out['grad_x'], *[out[pre + n] for pre in ('grad_', 'delta_', 'new_m_', 'new_v_')
                                          for n in WEIGHTS])
```

```python
import math

import jax
import jax.numpy as jnp
from jax import lax
from jax.experimental import pallas as pl
from jax.experimental.pallas import tpu as pltpu

F32 = jnp.float32
_MXU = jnp.bfloat16

D = 1024
GW = 256
INW = 7 * GW
DFF = 2816
DEPTH = 4
NDEV = 8
SC_TAPS = 3
CF_TAPS = 31
NGRP = 16
NCH = 16
NSTATE = 64
XH = NGRP * NSTATE
XW = 2 * XH
ALPHA = (2 * DEPTH) ** 0.25
LN_EPS = 1e-5
T = 512
SUB = 8
HALO = 32
TM = 512
TM_WIDE = 256
VMEM_LIMIT = 56 * 1024 * 1024

ADAM_LR = 0.001
ADAM_B1 = 0.9
ADAM_B2 = 0.999
ADAM_EPS = 1e-08
ADAM_WD = 0.01
ADAM_STEP = 10

MESH = pl.DeviceIdType.MESH
ANY = pl.BlockSpec(memory_space=pl.ANY)
AXES = ("x", "y", "c")


def _full(shape):
    nd = len(shape)
    return pl.BlockSpec(shape, lambda *_: (0,) * nd)


def _resident(shape):
    nd = len(shape)
    return pl.BlockSpec(shape, lambda *_: (0,) * nd, pipeline_mode=pl.Buffered(1))


def _sel(arr, *idx):
    tail = arr.shape[len(idx):]
    return pl.BlockSpec((None,) * len(idx) + tail, lambda *_: idx + (0,) * len(tail))


def _dot(a, b):
    return jnp.dot(a.astype(_MXU), b.astype(_MXU), preferred_element_type=F32)


def _dot_nt(a, b):
    return lax.dot_general(a.astype(_MXU), b.astype(_MXU), (((1,), (1,)), ((), ())), preferred_element_type=F32)


def _dot_tn(a, b):
    return lax.dot_general(a.astype(_MXU), b.astype(_MXU), (((0,), (0,)), ((), ())), preferred_element_type=F32)


def _sigmoid(x):
    return 1.0 / (1.0 + jnp.exp(-x))


def _colsum(x):
    return jnp.sum(x, axis=0, keepdims=True)


def _me():
    return lax.axis_index("x"), lax.axis_index("y"), lax.axis_index("c")


def _flat(p):
    return 4 * p[0] + 2 * p[1] + p[2]


def _allgather(arrays, name):
    n = len(arrays)

    def body(*refs):
        ins, outs = refs[:n], refs[n:2 * n]
        send_sems, recv_sems, local_sems = refs[2 * n:]
        x, y, c = _me()
        me, sibling = (x, y, c), (x, y, 1 - c)
        chips = [(1 - x, y), (x, 1 - y), (1 - x, 1 - y)]

        def copy(a, k, block, to, src=None):
            dst = outs[a].at[_flat(block)]
            return pltpu.make_async_remote_copy(
                src_ref=dst if src is None else src, dst_ref=dst,
                send_sem=send_sems.at[a, k], recv_sem=recv_sems.at[a, k],
                device_id=to, device_id_type=MESH)

        started = []
        for a in range(n):
            mine = pltpu.make_async_copy(ins[a], outs[a].at[_flat(me)], local_sems.at[a])
            mine.start()
            started.append(mine)
        first = []
        for a in range(n):
            first.append(copy(a, 0, me, sibling, src=ins[a]))
            first += [copy(a, 1 + j, me, (*chip, c), src=ins[a]) for j, chip in enumerate(chips)]
        for cp in first:
            cp.start()
        passed = []
        for a in range(n):
            for j, chip in enumerate(chips):
                copy(a, 1 + j, (*chip, c), me).wait_recv()
                fwd = copy(a, 4 + j, (*chip, c), sibling)
                fwd.start()
                passed.append(fwd)
        for a in range(n):
            copy(a, 0, sibling, me).wait_recv()
            for j, chip in enumerate(chips):
                copy(a, 4 + j, (*chip, 1 - c), me).wait_recv()
        for cp in first + passed:
            cp.wait_send()
        for cp in started:
            cp.wait()

    return pl.pallas_call(
        body, name=name,
        out_shape=[jax.ShapeDtypeStruct((NDEV,) + a.shape, a.dtype) for a in arrays],
        in_specs=[ANY] * n, out_specs=[ANY] * n,
        scratch_shapes=[pltpu.SemaphoreType.DMA((n, 7)), pltpu.SemaphoreType.DMA((n, 7)),
                        pltpu.SemaphoreType.DMA((n,))],
    )(*arrays)


GATHER, EXCHANGE = "gather", "exchange"


def _direct_copies(kind, ins, outs, send_sems, recv_sems, local_sems, start):
    x, y, c = _me()
    mine = _flat((x, y, c))
    for a in range(len(ins)):
        own = ins[a] if kind == GATHER else ins[a].at[mine]
        loc = pltpu.make_async_copy(own, outs[a].at[mine], local_sems.at[a])
        if start:
            loc.start()
        for k in range(1, NDEV):
            peer = (lax.rem(x + ((k >> 2) & 1), 2), lax.rem(y + ((k >> 1) & 1), 2), lax.rem(c + (k & 1), 2))
            cp = pltpu.make_async_remote_copy(
                src_ref=ins[a] if kind == GATHER else ins[a].at[_flat(peer)], dst_ref=outs[a].at[mine],
                send_sem=send_sems.at[a, k - 1], recv_sem=recv_sems.at[a, k - 1],
                device_id=peer, device_id_type=MESH)
            if start:
                cp.start()
            else:
                cp.wait_send()
                cp.wait_recv()
        if not start:
            loc.wait()


def _comm_out_shapes(comm):
    kind, arrays = comm
    return [jax.ShapeDtypeStruct(((NDEV,) + a.shape) if kind == GATHER else a.shape, a.dtype) for a in arrays]


def _call(body, *, name, grid, in_specs, out_specs, out_shape, args, scratch=(), comm=None):
    params = pltpu.CompilerParams(dimension_semantics=("arbitrary",) * len(grid), vmem_limit_bytes=VMEM_LIMIT)
    if comm is None:
        return pl.pallas_call(body, name=name, grid=grid, in_specs=in_specs, out_specs=out_specs,
                              out_shape=out_shape, scratch_shapes=list(scratch), compiler_params=params)(*args)
    kind, arrays = comm
    n, n_in, n_out, n_scr = len(arrays), len(in_specs), len(out_specs), len(scratch)

    def wrapped(*refs):
        ins, cin = refs[:n_in], refs[n_in:n_in + n]
        outs, cout = refs[n_in + n:n_in + n + n_out], refs[n_in + n + n_out:n_in + 2 * n + n_out]
        scr = refs[n_in + 2 * n + n_out:]
        user_scr, sems = scr[:n_scr], scr[n_scr:]
        ids = [pl.program_id(k) for k in range(len(grid))]
        first, last = ids[0] == 0, ids[0] == grid[0] - 1
        for k in range(1, len(grid)):
            first, last = first & (ids[k] == 0), last & (ids[k] == grid[k] - 1)

        @pl.when(first)
        def _():
            _direct_copies(kind, cin, cout, *sems, start=True)

        body(*ins, *outs, *user_scr)

        @pl.when(last)
        def _():
            _direct_copies(kind, cin, cout, *sems, start=False)

    sems = [pltpu.SemaphoreType.DMA((n, 7)), pltpu.SemaphoreType.DMA((n, 7)), pltpu.SemaphoreType.DMA((n,))]
    res = pl.pallas_call(
        wrapped, name=name, grid=grid, in_specs=list(in_specs) + [ANY] * n, out_specs=list(out_specs) + [ANY] * n,
        out_shape=list(out_shape) + _comm_out_shapes(comm), scratch_shapes=list(scratch) + sems,
        compiler_params=params)(*args, *arrays)
    return res


def _alltoall(arrays, name):
    n = len(arrays)

    def body(*refs):
        _direct_copies(EXCHANGE, refs[:n], refs[n:2 * n], *refs[2 * n:], start=True)
        _direct_copies(EXCHANGE, refs[:n], refs[n:2 * n], *refs[2 * n:], start=False)

    return pl.pallas_call(
        body, name=name,
        out_shape=[jax.ShapeDtypeStruct(a.shape, a.dtype) for a in arrays],
        in_specs=[ANY] * n, out_specs=[ANY] * n,
        scratch_shapes=[pltpu.SemaphoreType.DMA((n, 7)), pltpu.SemaphoreType.DMA((n, 7)),
                        pltpu.SemaphoreType.DMA((n,))],
    )(*arrays)


def _modulated_matmul(x, mod, l, ksc, ksh, wts, bias, tm, out_dtypes, epilogue, name, comm=None):
    s, k = x.shape
    nw = len(wts)
    n = wts[0].shape[0]
    nb = 0 if bias is None else 1

    def body(*refs):
        x_ref, sc_ref, sh_ref = refs[:3]
        w_refs = refs[3:3 + nw]
        b_refs = refs[3 + nw:3 + nw + nb]
        o_refs = refs[3 + nw + nb:]
        h = (x_ref[...] * (1.0 + sc_ref[...]) + sh_ref[...]).astype(_MXU)
        prods = [lax.dot_general(h, w[...], (((1,), (1,)), ((), ())), preferred_element_type=F32) for w in w_refs]
        if nb:
            prods[0] = prods[0] + b_refs[0][...]
        for o, v in zip(o_refs, epilogue(*prods)):
            o[...] = v.astype(o.dtype)

    return _call(
        body, name=name, grid=(s // tm,),
        out_shape=[jax.ShapeDtypeStruct((s, n), dt) for dt in out_dtypes],
        in_specs=[pl.BlockSpec((tm, k), lambda i: (i, 0)), _sel(mod, l, ksc), _sel(mod, l, ksh)]
        + [_resident(w.shape) for w in wts] + ([_sel(bias, l)] if nb else []),
        out_specs=[pl.BlockSpec((tm, n), lambda i: (i, 0))] * len(out_dtypes),
        args=(x, mod, mod, *wts, *([bias] if nb else [])), comm=comm)


def _swiglu(gate, up):
    return gate, up, gate * _sigmoid(gate) * up


def _proj_residual_ln(a, w, l, xres, mod, kg, lng, lnb, name, comm=None):
    s, k = a.shape
    n = w.shape[1]

    def body(a_ref, w_ref, x_ref, g_ref, lg_ref, lb_ref, r_ref, u_ref, o_ref):
        r = jnp.dot(a_ref[...], w_ref[...], preferred_element_type=F32)
        u = ALPHA * x_ref[...] + (1.0 + g_ref[...]) * r
        mu = jnp.mean(u, axis=-1, keepdims=True)
        xc = u - mu
        var = jnp.mean(xc * xc, axis=-1, keepdims=True)
        r_ref[...] = r.astype(r_ref.dtype)
        u_ref[...] = u
        o_ref[...] = xc * lax.rsqrt(var + LN_EPS) * lg_ref[...] + lb_ref[...]

    row = pl.BlockSpec((TM, n), lambda i: (i, 0))
    return _call(
        body, name=name, grid=(s // TM,),
        out_shape=[jax.ShapeDtypeStruct((s, n), _MXU)] + [jax.ShapeDtypeStruct((s, n), F32)] * 2,
        in_specs=[pl.BlockSpec((TM, k), lambda i: (i, 0)), _resident(w.shape), row, _sel(mod, l, kg), _sel(lng, l),
                  _sel(lnb, l)],
        out_specs=[row, row, row], args=(a, w, xres, mod, lng, lnb), comm=comm)


def _loss_ln_bwd(xo, target, u, res, lng, l, mod, kg, name):
    s, n = u.shape

    def body(x_ref, t_ref, u_ref, r_ref, lg_ref, g_ref, du_ref, dr_ref, acc_ref):
        @pl.when(pl.program_id(0) == 0)
        def _():
            acc_ref[...] = jnp.zeros_like(acc_ref)

        uu = u_ref[...]
        mu = jnp.mean(uu, axis=-1, keepdims=True)
        xc = uu - mu
        var = jnp.mean(xc * xc, axis=-1, keepdims=True)
        rstd = lax.rsqrt(var + LN_EPS)
        xh = xc * rstd
        err = x_ref[...] - t_ref[...]
        acc_ref[3:4, :] += jnp.sum(err * err)
        dy = err * (1.0 / n)
        dxh = dy * lg_ref[...]
        du = rstd * (dxh - jnp.mean(dxh, axis=-1, keepdims=True) - xh * jnp.mean(dxh * xh, axis=-1, keepdims=True))
        du_ref[...] = du
        dr_ref[...] = (du * (1.0 + g_ref[...])).astype(dr_ref.dtype)
        acc_ref[0:1, :] += _colsum(dy * xh)
        acc_ref[1:2, :] += _colsum(dy)
        acc_ref[2:3, :] += _colsum(du * r_ref[...].astype(F32))

    row = pl.BlockSpec((TM, n), lambda i: (i, 0))
    return _call(
        body, name=name, grid=(s // TM,),
        out_shape=[jax.ShapeDtypeStruct((s, n), F32), jax.ShapeDtypeStruct((s, n), _MXU),
                   jax.ShapeDtypeStruct((8, n), F32)],
        in_specs=[row, row, row, row, _sel(lng, l), _sel(mod, l, kg)], out_specs=[row, row, _full((8, n))],
        args=(xo, target, u, res, lng, mod))


def _input_grad(acts, wts, l, du, xin, mod, ksc, tm, name, comm=None):
    s = acts[0].shape[0]
    n = wts[0].shape[1]
    na = len(acts)

    def body(*refs):
        a_refs, w_refs = refs[:na], refs[na:2 * na]
        du_ref, x_ref, sc_ref, dx_ref, acc_ref = refs[2 * na:]

        @pl.when(pl.program_id(0) == 0)
        def _():
            acc_ref[...] = jnp.zeros_like(acc_ref)

        dh = None
        for a, w in zip(a_refs, w_refs):
            t = jnp.dot(a[...], w[...], preferred_element_type=F32)
            dh = t if dh is None else dh + t
        dx_ref[...] = ALPHA * du_ref[...] + dh * (1.0 + sc_ref[...])
        acc_ref[0:1, :] += _colsum(dh)
        acc_ref[1:2, :] += _colsum(dh * x_ref[...])

    row = pl.BlockSpec((tm, n), lambda i: (i, 0))
    return _call(
        body, name=name, grid=(s // tm,),
        out_shape=[jax.ShapeDtypeStruct((s, n), F32), jax.ShapeDtypeStruct((8, n), F32)],
        in_specs=[pl.BlockSpec((tm, a.shape[1]), lambda i: (i, 0)) for a in acts]
        + [_resident(w.shape) for w in wts] + [row, row, _sel(mod, l, ksc)],
        out_specs=[row, _full((8, n))], args=(*acts, *wts, du, xin, mod), comm=comm)


SWIGLU_SPLIT = 2


def _input_grad_ln_bwd(acts, wts, du_in, sc, u, res, lng, lnb, g, tm, name, swiglu=None, comm=None):
    n = wts[0].shape[1]
    na = len(wts)
    ns = 0 if swiglu is None else 4
    s = (swiglu[0] if ns else acts[0]).shape[0]
    f = wts[0].shape[0]

    nlead = ns or len(acts)

    def body(*refs):
        a_refs, w_refs = refs[:nlead], refs[nlead:nlead + na]
        dui_ref, sc_ref, u_ref, r_ref, lg_ref, lb_ref, g_ref = refs[nlead + na:nlead + na + 7]
        o_refs = refs[nlead + na + 7:]
        if ns:
            dp_ref, wd_ref, gate_ref, up_ref = a_refs
            dg_ref, dup_ref = o_refs[:2]
            o_refs = o_refs[2:]
        du_ref, dr_ref, acc_ref = o_refs

        @pl.when(pl.program_id(0) == 0)
        def _():
            acc_ref[...] = jnp.zeros_like(acc_ref)

        dh = None
        if ns:
            piece = f // SWIGLU_SPLIT
            for c in range(SWIGLU_SPLIT):
                lo = c * piece
                dp = lax.dot_general(dp_ref[...], wd_ref[lo:lo + piece, :], (((1,), (1,)), ((), ())),
                                     preferred_element_type=F32)
                gt = gate_ref[:, lo:lo + piece].astype(F32)
                sg = _sigmoid(gt)
                dgate = (dp * up_ref[:, lo:lo + piece].astype(F32) * (sg * (1.0 + gt * (1.0 - sg)))).astype(_MXU)
                dup = (dp * (gt * sg)).astype(_MXU)
                dg_ref[:, lo:lo + piece] = dgate
                dup_ref[:, lo:lo + piece] = dup
                t = (jnp.dot(dgate, w_refs[0][lo:lo + piece, :], preferred_element_type=F32)
                     + jnp.dot(dup, w_refs[1][lo:lo + piece, :], preferred_element_type=F32))
                dh = t if dh is None else dh + t
        else:
            for a, w in zip(a_refs, w_refs):
                t = jnp.dot(a[...], w[...], preferred_element_type=F32)
                dh = t if dh is None else dh + t
        dy = ALPHA * dui_ref[...] + dh * (1.0 + sc_ref[...])
        uu = u_ref[...]
        mu = jnp.mean(uu, axis=-1, keepdims=True)
        xc = uu - mu
        var = jnp.mean(xc * xc, axis=-1, keepdims=True)
        rstd = lax.rsqrt(var + LN_EPS)
        xh = xc * rstd
        dh_sum = _colsum(dh)
        acc_ref[0:1, :] += dh_sum
        acc_ref[1:2, :] += lg_ref[...] * _colsum(dh * xh) + lb_ref[...] * dh_sum
        dxh = dy * lg_ref[...]
        du = rstd * (dxh - jnp.mean(dxh, axis=-1, keepdims=True) - xh * jnp.mean(dxh * xh, axis=-1, keepdims=True))
        du_ref[...] = du
        dr_ref[...] = (du * (1.0 + g_ref[...])).astype(dr_ref.dtype)
        acc_ref[2:3, :] += _colsum(dy * xh)
        acc_ref[3:4, :] += _colsum(dy)
        acc_ref[4:5, :] += _colsum(du * r_ref[...].astype(F32))

    row = pl.BlockSpec((tm, n), lambda i: (i, 0))
    wide = pl.BlockSpec((tm, f), lambda i: (i, 0))
    if ns:
        dp_in, w_down, gate, up = swiglu
        lead_specs = [pl.BlockSpec((tm, dp_in.shape[1]), lambda i: (i, 0)), _resident(w_down.shape), wide, wide]
        lead_args = [dp_in, w_down, gate, up]
        lead_outs, lead_out_specs = [jax.ShapeDtypeStruct((s, f), _MXU)] * 2, [wide, wide]
    else:
        lead_specs = [pl.BlockSpec((tm, a.shape[1]), lambda i: (i, 0)) for a in acts]
        lead_args, lead_outs, lead_out_specs = list(acts), [], []
    return _call(
        body, name=name, grid=(s // tm,),
        out_shape=lead_outs + [jax.ShapeDtypeStruct((s, n), F32), jax.ShapeDtypeStruct((s, n), _MXU),
                               jax.ShapeDtypeStruct((8, n), F32)],
        in_specs=lead_specs + [_resident(w.shape) for w in wts]
        + [row, _sel(*sc), row, row, _sel(*lng), _sel(*lnb), _sel(*g)],
        out_specs=lead_out_specs + [row, row, _full((8, n))],
        args=(*lead_args, *wts, du_in, sc[0], u, res, lng[0], lnb[0], g[0]), comm=comm)


def _weight_grad(a, b, name, mod=None, comm=None):
    s, ka = a.shape
    nbc = b.shape[1]
    nm = 0 if mod is None else 2
    steps = s // TM

    def body(*refs):
        a_ref, b_ref = refs[:2]
        m_refs = refs[2:2 + nm]
        o_ref, acc = refs[2 + nm:]

        @pl.when(pl.program_id(0) == 0)
        def _():
            acc[...] = jnp.zeros_like(acc)

        bv = b_ref[...]
        if nm:
            bv = bv * (1.0 + m_refs[0][...]) + m_refs[1][...]
        acc[...] += lax.dot_general(a_ref[...], bv.astype(_MXU), (((0,), (0,)), ((), ())),
                                    preferred_element_type=F32)

        @pl.when(pl.program_id(0) == steps - 1)
        def _():
            o_ref[...] = acc[...].astype(o_ref.dtype)

    mspecs, margs = [], []
    if nm:
        marr, l, ksc, ksh = mod
        mspecs, margs = [_sel(marr, l, ksc), _sel(marr, l, ksh)], [marr, marr]
    res = _call(
        body, name=name, grid=(steps,),
        out_shape=[jax.ShapeDtypeStruct((ka, nbc), _MXU)],
        in_specs=[pl.BlockSpec((TM, ka), lambda t: (t, 0)), pl.BlockSpec((TM, nbc), lambda t: (t, 0))] + mspecs,
        out_specs=[_full((ka, nbc))], args=(a, b, *margs), scratch=[pltpu.VMEM((ka, nbc), F32)], comm=comm)
    return [res[0].reshape(NDEV, ka // NDEV, nbc)] + list(res[1:])


def _ada_fwd(c_all, w_ada, b_cols, name):
    cols = w_ada.shape[2]

    def body(c_ref, w_ref, b_ref, cond_ref, o_ref):
        cv = c_ref[...]
        cond = cv * _sigmoid(cv)

        @pl.when(pl.program_id(0) == 0)
        def _():
            cond_ref[...] = cond

        o_ref[...] = _dot(cond, w_ref[...]) + b_ref[...]

    return _call(
        body, name=name, grid=(DEPTH,),
        out_shape=[jax.ShapeDtypeStruct((NDEV, D), F32), jax.ShapeDtypeStruct((DEPTH, NDEV, cols), F32)],
        in_specs=[_full((NDEV, D)), pl.BlockSpec((None, D, cols), lambda l: (l, 0, 0)),
                  pl.BlockSpec((None, 1, cols), lambda l: (l, 0, 0))],
        out_specs=[_full((NDEV, D)), pl.BlockSpec((None, NDEV, cols), lambda l: (l, 0, 0))],
        args=(c_all, w_ada, b_cols))


def _ada_bwd(cond, dmod_cols, name):
    cols = dmod_cols.shape[2]

    def body(c_ref, d_ref, o_ref):
        o_ref[...] = _dot_tn(c_ref[...], d_ref[...])

    return _call(
        body, name=name, grid=(DEPTH,),
        out_shape=[jax.ShapeDtypeStruct((DEPTH, D, cols), F32)],
        in_specs=[_full((NDEV, D)), pl.BlockSpec((None, NDEV, cols), lambda l: (l, 0, 0))],
        out_specs=[pl.BlockSpec((None, D, cols), lambda l: (l, 0, 0))], args=(cond, dmod_cols))[0]


def _adam_math(w, g, m, v):
    m = ADAM_B1 * m + (1.0 - ADAM_B1) * g
    v = ADAM_B2 * v + (1.0 - ADAM_B2) * (g * g)
    m_hat = m / (1.0 - ADAM_B1 ** ADAM_STEP)
    v_hat = v / (1.0 - ADAM_B2 ** ADAM_STEP)
    delta = -ADAM_LR * (m_hat / (jnp.sqrt(v_hat) + ADAM_EPS) + ADAM_WD * w)
    return delta, m, v


def _sum_parts(p_ref):
    g = p_ref[0].astype(F32)
    for i in range(1, p_ref.shape[0]):
        g = g + p_ref[i].astype(F32)
    return g


def _adamw(parts, w, m, v, tr, name):
    r, c = w.shape
    nparts = parts.shape[0]

    def body(p_ref, w_ref, m_ref, v_ref, g_ref, d_ref, nm_ref, nv_ref):
        g = _sum_parts(p_ref)
        delta, nm, nv = _adam_math(w_ref[...], g, m_ref[...], v_ref[...])
        g_ref[...] = g
        d_ref[...] = delta
        nm_ref[...] = nm
        nv_ref[...] = nv

    blk = pl.BlockSpec((tr, c), lambda i: (i, 0))
    return _call(
        body, name=name, grid=(r // tr,),
        out_shape=[jax.ShapeDtypeStruct((r, c), F32)] * 4,
        in_specs=[pl.BlockSpec((nparts, tr, c), lambda i: (0, i, 0)), blk, blk, blk],
        out_specs=[blk] * 4, args=(parts, w, m, v))


def _layer_parts_specs(parts, tr):
    nparts, r, c = parts[0].shape
    last = r // tr - 1

    def spec(k):
        return pl.BlockSpec((nparts, tr, c),
                            lambda l, i: (0, jnp.where(l < k, 0, jnp.where(l == k, i, last)), 0))

    return [spec(k) for k in range(len(parts))]


def _for_layer(p_refs, fn):
    for k, ref in enumerate(p_refs):
        pl.when(pl.program_id(0) == k)(lambda ref=ref: fn(ref))


def _adamw_layers(parts, w, m, v, tr, name):
    nl = len(parts)
    _, r, c = parts[0].shape

    def body(*refs):
        p_refs = refs[:nl]
        w_ref, m_ref, v_ref, g_ref, d_ref, nm_ref, nv_ref = refs[nl:]

        def update(p_ref):
            g = _sum_parts(p_ref)
            delta, nm, nv = _adam_math(w_ref[...], g, m_ref[...], v_ref[...])
            g_ref[...] = g
            d_ref[...] = delta
            nm_ref[...] = nm
            nv_ref[...] = nv

        _for_layer(p_refs, update)

    blk = pl.BlockSpec((tr, c), lambda l, i: (l * (r // tr) + i, 0))
    return _call(
        body, name=name, grid=(nl, r // tr),
        out_shape=[jax.ShapeDtypeStruct((nl * r, c), F32)] * 4,
        in_specs=_layer_parts_specs(parts, tr) + [blk, blk, blk],
        out_specs=[blk] * 4, args=(*parts, w, m, v))


def _reduce_layers(parts, tr, name):
    nl = len(parts)
    _, r, c = parts[0].shape

    def body(*refs):
        g_ref = refs[nl]

        def reduce(p_ref):
            g_ref[...] = _sum_parts(p_ref)

        _for_layer(refs[:nl], reduce)

    return _call(
        body, name=name, grid=(nl, r // tr), out_shape=[jax.ShapeDtypeStruct((nl, r, c), F32)],
        in_specs=_layer_parts_specs(parts, tr),
        out_specs=[pl.BlockSpec((None, tr, c), lambda l, i: (l, i, 0))], args=tuple(parts))[0]


def _cmul(ar, ai, br, bi):
    return ar * br - ai * bi, ar * bi + ai * br


def _lam_bar(lr, li, log_dt):
    dt = jnp.exp(log_dt)
    mag = jnp.exp(lr * dt)
    return dt, mag * jnp.cos(li * dt), mag * jnp.sin(li * dt)


def _layer_spec(arr):
    tail = arr.shape[1:]
    return pl.BlockSpec((None,) + tail, lambda l: (l,) + (0,) * len(tail))


def _ssm_prep(lam_re, lam_im, log_dt, b_re, b_im, name):
    def body(lr_ref, li_ref, dt_ref, br_ref, bi_ref, pr_ref, pi_ref, or_ref, oi_ref):
        lr, li = lr_ref[...], li_ref[...]
        _, ar, ai = _lam_bar(lr, li, dt_ref[...])
        den = lr * lr + li * li
        fr, fi = _cmul(ar - 1.0, ai, lr / den, -li / den)
        obr, obi = _cmul(fr[:, None, :], fi[:, None, :], br_ref[...], bi_ref[...])
        or_ref[...] = obr
        oi_ref[...] = obi
        qr, qi = ar, ai
        for k in range(SUB):
            pr_ref[k] = qr
            pi_ref[k] = qi
            qr, qi = _cmul(qr, qi, ar, ai)

    nl, g, p = lam_re.shape
    h = b_re.shape[2]
    outs = [jax.ShapeDtypeStruct((nl, SUB, g, p), F32)] * 2 + [jax.ShapeDtypeStruct((nl, g, h, p), F32)] * 2
    args = (lam_re, lam_im, log_dt, b_re, b_im)
    return _call(body, name=name, grid=(nl,), out_shape=outs, in_specs=[_layer_spec(a) for a in args],
                 out_specs=[_layer_spec(o) for o in outs], args=args)


def _ssm_param_bwd(lam_re, lam_im, log_dt, b_re, b_im, dlr, dli, dbr, dbi, name):
    def body(lr_ref, li_ref, dt_ref, br_ref, bi_ref, gar_ref, gai_ref, gbr_ref, gbi_ref,
             olr_ref, oli_ref, odt_ref, obr_ref, obi_ref):
        lr, li = lr_ref[...], li_ref[...]
        dt, ar, ai = _lam_bar(lr, li, dt_ref[...])
        den = lr * lr + li * li
        ir, ii = lr / den, -li / den
        fr, fi = _cmul(ar - 1.0, ai, ir, ii)
        br, bi = br_ref[...], bi_ref[...]
        gbr, gbi = gbr_ref[...], gbi_ref[...]
        obr, obi = _cmul(fr[:, None, :], -fi[:, None, :], gbr, gbi)
        obr_ref[...] = obr
        obi_ref[...] = obi
        pr, pi = _cmul(br, -bi, gbr, gbi)
        gfr, gfi = jnp.sum(pr, axis=1), jnp.sum(pi, axis=1)
        t_r, t_i = _cmul(gfr, gfi, ir, -ii)
        gar, gai = gar_ref[...] + t_r, gai_ref[...] + t_i
        qr, qi = _cmul(fr, fi, ir, ii)
        l1r, l1i = _cmul(gfr, gfi, qr, -qi)
        l2r, l2i = _cmul(gar, gai, dt * ar, -dt * ai)
        olr_ref[...] = l2r - l1r
        oli_ref[...] = l2i - l1i
        mr, mi = _cmul(lr, li, ar, ai)
        dr, _ = _cmul(gar, gai, mr, -mi)
        odt_ref[...] = jnp.sum(dr, axis=1, keepdims=True) * dt

    nl, g, p = lam_re.shape
    h = b_re.shape[2]
    outs = ([jax.ShapeDtypeStruct((nl, g, p), F32)] * 2 + [jax.ShapeDtypeStruct((nl, g, 1), F32)]
            + [jax.ShapeDtypeStruct((nl, g, h, p), F32)] * 2)
    args = (lam_re, lam_im, log_dt, b_re, b_im, dlr, dli, dbr, dbi)
    return _call(body, name=name, grid=(nl,), out_shape=outs, in_specs=[_layer_spec(a) for a in args],
                 out_specs=[_layer_spec(o) for o in outs], args=args)


def _gelu(x):
    k = math.sqrt(2.0 / math.pi)
    return 0.5 * x * (1.0 + jnp.tanh(k * (x + 0.044715 * x * x * x)))


def _gelu_grad(x):
    k = math.sqrt(2.0 / math.pi)
    th = jnp.tanh(k * (x + 0.044715 * x * x * x))
    return 0.5 * (1.0 + th) + 0.5 * x * (1.0 - th * th) * k * (1.0 + 3 * 0.044715 * x * x)


def _pool_inv_count(chunk):
    t = lax.broadcasted_iota(jnp.int32, (T, GW), 0) + chunk * T + 1
    lane = lax.broadcasted_iota(jnp.int32, (T, GW), 1)
    win = jnp.where(lane < 64, 2, jnp.where(lane < 128, 4, jnp.where(lane < 192, 8, 16)))
    return 1.0 / jnp.minimum(t, win).astype(F32)


def _window_sums(v, bufs, base, step):
    sums = []
    for k, buf in enumerate(bufs):
        buf[base:base + T, :] = v
        v = v + buf[pl.ds(base + step * (1 << k), T), :]
        sums.append(v)
    lane = lax.broadcasted_iota(jnp.int32, (T, GW), 1)
    return jnp.where(lane < 64, sums[0], jnp.where(lane < 128, sums[1], jnp.where(lane < 192, sums[2], sums[3])))


def _phase_copies(buf, rot):
    n = HALO + T - SUB
    for p in range(1, SUB):
        rot[p - 1, 0:n, :] = buf[pl.ds(p, n), :]


def _tap(buf, rot, off):
    q, p = divmod(off, SUB)
    return buf[off:off + T, :] if p == 0 else rot[p - 1, SUB * q:SUB * q + T, :]


SCAN_STEPS = (1, 2, 4)


def _scan(x_scr, steps_ref, chain_ref, carry, reverse):
    strips = range(XH // 128)
    cr = [carry[:, j * 128:(j + 1) * 128] for j in strips]
    ci = [carry[:, XH + j * 128:XH + (j + 1) * 128] for j in strips]
    nb = T // SUB
    edge = 0 if reverse else SUB - 1
    for b in (reversed(range(nb)) if reverse else range(nb)):
        rows = slice(SUB * b, SUB * b + SUB)
        for j in strips:
            lo, hi = j * 128, XH + j * 128
            rb, ib = x_scr[rows, lo:lo + 128], x_scr[rows, hi:hi + 128]
            for k, d in enumerate(SCAN_STEPS):
                ar, ai = steps_ref[k, :, lo:lo + 128], steps_ref[k, :, hi:hi + 128]
                shift = SUB - d if reverse else d
                sre, sim = pltpu.roll(rb, shift, 0), pltpu.roll(ib, shift, 0)
                rb, ib = rb + ar * sre - ai * sim, ib + ar * sim + ai * sre
            pr, pi = chain_ref[:, lo:lo + 128], chain_ref[:, hi:hi + 128]
            crb, cib = jnp.broadcast_to(cr[j], (SUB, 128)), jnp.broadcast_to(ci[j], (SUB, 128))
            rb, ib = rb + pr * crb - pi * cib, ib + pr * cib + pi * crb
            x_scr[rows, lo:lo + 128] = rb
            x_scr[rows, hi:hi + 128] = ib
            cr[j], ci[j] = rb[edge:edge + 1], ib[edge:edge + 1]


def _scan_tables(pw):
    conj = jnp.concatenate([jnp.ones((XH,), F32), -jnp.ones((XH,), F32)])
    steps = jnp.stack([pw[:, d - 1] for d in SCAN_STEPS], axis=1)[:, :, None, :]
    r = jnp.arange(SUB).reshape(1, 1, SUB, 1)
    d = jnp.array(SCAN_STEPS).reshape(1, len(SCAN_STEPS), 1, 1)
    fwd_steps = jnp.where(r >= d, steps, 0.0)
    rev_steps = jnp.where(r < SUB - d, steps, 0.0) * conj
    return fwd_steps, pw, rev_steps, pw[:, ::-1] * conj


MIXER_SHARED = ('scw', 'pblk', 'pscale', 'dww', 'dwb', 'clg', 'clb', 'bblk', 'cblk', 'ssd', 'wglu', 'bglu')
MIXER_FWD_PARAMS = MIXER_SHARED + ('fwd_steps', 'fwd_chain')
MIXER_BWD_PARAMS = MIXER_SHARED + ('rev_steps', 'rev_chain')


def _mixer_fwd(x, mod, l, w_in_t, b_in, mp, name, comm=None):
    s = x.shape[0]
    nch = s // T
    params = [mp[k] for k in MIXER_FWD_PARAMS]

    def body(x_ref, sc_ref, sh_ref, w_ref, b_ref, scw_ref, pblk_ref, ps_ref, dww_ref, dwb_ref, clg_ref, clb_ref,
             bblk_ref, cblk_ref, ssd_ref, wglu_ref, bglu_ref, steps_ref, chain_ref, y_ref, xs_ref, aux_ref, z_ref,
             qbuf, hbuf, pb1, pb2, pb4, pb8, rot, carry):
        i = pl.program_id(0)

        @pl.when(i == 0)
        def _():
            for ref in (qbuf, hbuf, pb1, pb2, pb4, pb8):
                ref[0:HALO, :] = jnp.zeros((HALO, GW), F32)
            carry[...] = jnp.zeros_like(carry)

        h = x_ref[...] * (1.0 + sc_ref[...]) + sh_ref[...]
        z_ref[...] = _dot_nt(h, w_ref[...]) + b_ref[...]
        zh, zb, zc = z_ref[:, 0:GW], z_ref[:, GW:2 * GW], z_ref[:, 2 * GW:3 * GW]
        zp, zcv, zcg, zs = (z_ref[:, 3 * GW:4 * GW], z_ref[:, 4 * GW:5 * GW], z_ref[:, 5 * GW:6 * GW],
                            z_ref[:, 6 * GW:7 * GW])
        qbuf[HALO:HALO + T, :] = zc * zh
        conv = None
        for k in range(SC_TAPS):
            v = scw_ref[k:k + 1, :] * qbuf[pl.ds(HALO - (SC_TAPS - 1) + k, T), :]
            conv = v if conv is None else conv + v
        aux_ref[:, 0:GW] = conv
        y_ref[:, 0:GW] = (zb * conv).astype(y_ref.dtype)
        r = _window_sums(zp, (pb1, pb2, pb4, pb8), HALO, -1) * _pool_inv_count(i) - zp
        aux_ref[:, GW:2 * GW] = r
        y_ref[:, GW:2 * GW] = (_dot(r, pblk_ref[...]) * ps_ref[...]).astype(y_ref.dtype)
        hbuf[HALO:HALO + T, :] = zcv * _sigmoid(zcg)
        _phase_copies(hbuf, rot)
        hc = None
        for k in range(CF_TAPS):
            v = dww_ref[k:k + 1, :] * _tap(hbuf, rot, HALO - (CF_TAPS - 1) + k)
            hc = v if hc is None else hc + v
        hc = hc + dwb_ref[...]
        aux_ref[:, 2 * GW:3 * GW] = hc
        mu = jnp.mean(hc, axis=-1, keepdims=True)
        xc = hc - mu
        var = jnp.mean(xc * xc, axis=-1, keepdims=True)
        ln = xc * lax.rsqrt(var + LN_EPS) * clg_ref[...] + clb_ref[...]
        y_ref[:, 2 * GW:3 * GW] = (ln * _sigmoid(ln)).astype(y_ref.dtype)
        xs_ref[...] = _dot(zs, bblk_ref[...])
        _scan(xs_ref, steps_ref, chain_ref, carry[...], reverse=False)
        carry[...] = xs_ref[T - 1:T, :]
        yy = _dot(xs_ref[...], cblk_ref[...]) + ssd_ref[...] * zs
        aux_ref[:, 3 * GW:4 * GW] = yy
        yg = _gelu(yy)
        v = _dot(yg, wglu_ref[...]) + bglu_ref[...]
        y_ref[:, 3 * GW:4 * GW] = (yg * _sigmoid(v)).astype(y_ref.dtype)
        for ref in (qbuf, hbuf, pb1, pb2, pb4, pb8):
            ref[0:HALO, :] = ref[T:T + HALO, :]

    fwd = lambda i: (i, 0)
    return _call(
        body, name=name, grid=(nch,),
        out_shape=[jax.ShapeDtypeStruct((s, D), _MXU), jax.ShapeDtypeStruct((s, XW), F32),
                   jax.ShapeDtypeStruct((s, 4 * GW), F32), jax.ShapeDtypeStruct((s, INW), F32)],
        in_specs=[pl.BlockSpec((T, D), fwd), _sel(mod, l, SC1), _sel(mod, l, SH1), _resident(w_in_t.shape),
                  _sel(b_in, l)] + [_sel(p, l) for p in params],
        out_specs=[pl.BlockSpec((T, D), fwd), pl.BlockSpec((T, XW), fwd), pl.BlockSpec((T, 4 * GW), fwd),
                   pl.BlockSpec((T, INW), fwd)],
        scratch=[pltpu.VMEM((HALO + T, GW), F32)] * 6
        + [pltpu.VMEM((SUB - 1, HALO + T, GW), F32), pltpu.VMEM((1, XW), F32)],
        args=(x, mod, mod, w_in_t, b_in, *params), comm=comm)


def _mixer_bwd(z, dres, w_o, xs, aux, mp, l, name):
    s = z.shape[0]
    nch = s // T
    params = [mp[k] for k in MIXER_BWD_PARAMS]

    def body(z_ref, zprev_ref, dres_ref, wo_ref, xs_ref, xprev_ref, aux_ref, scw_ref, pblk_ref, ps_ref, dww_ref,
             dwb_ref, clg_ref, clb_ref, bblk_ref, cblk_ref, ssd_ref, wglu_ref, bglu_ref, steps_ref, chain_ref,
             dz_ref, vec_ref, small_ref, dpblk_ref, dwglu_ref, dbblk_ref, dcblk_ref, dlam_ref,
             qbuf, hbuf, dcbuf, dhbuf, eb1, eb2, eb4, eb8, rot_h, rot_d, acarry, a_scr, dy_ref):
        i = pl.program_id(0)
        chunk = nch - 1 - i
        dy_ref[...] = _dot_nt(dres_ref[...], wo_ref[...])

        @pl.when(i == 0)
        def _():
            for ref in (vec_ref, small_ref, dpblk_ref, dwglu_ref, dbblk_ref, dcblk_ref, dlam_ref, acarry):
                ref[...] = jnp.zeros_like(ref)
            for ref in (dcbuf, dhbuf, eb1, eb2, eb4, eb8):
                ref[T:T + HALO, :] = jnp.zeros((HALO, GW), F32)

        zh, zb, zc = z_ref[:, 0:GW], z_ref[:, GW:2 * GW], z_ref[:, 2 * GW:3 * GW]
        zcv, zcg, zs = z_ref[:, 4 * GW:5 * GW], z_ref[:, 5 * GW:6 * GW], z_ref[:, 6 * GW:7 * GW]
        conv, r, hc, yy = (aux_ref[:, 0:GW], aux_ref[:, GW:2 * GW], aux_ref[:, 2 * GW:3 * GW],
                           aux_ref[:, 3 * GW:4 * GW])
        has_prev = (chunk > 0).astype(F32)
        qbuf[0:HALO, :] = zprev_ref[:, 2 * GW:3 * GW] * zprev_ref[:, 0:GW] * has_prev
        hbuf[0:HALO, :] = zprev_ref[:, 4 * GW:5 * GW] * _sigmoid(zprev_ref[:, 5 * GW:6 * GW]) * has_prev

        dya = dy_ref[:, 0:GW]
        qbuf[HALO:HALO + T, :] = zc * zh
        dconv = dya * zb
        dcbuf[0:T, :] = dconv
        dq = None
        for k in range(SC_TAPS):
            small_ref[k:k + 1, :] += _colsum(dconv * qbuf[pl.ds(HALO - (SC_TAPS - 1) + k, T), :])
            v = scw_ref[k:k + 1, :] * dcbuf[pl.ds(SC_TAPS - 1 - k, T), :]
            dq = v if dq is None else dq + v
        dzh, dzb, dzc = dq * zc, dya * conv, dq * zh

        dyb = dy_ref[:, GW:2 * GW]
        o = _dot(r, pblk_ref[...])
        small_ref[3:4, :] += _colsum(dyb * o)
        do = dyb * ps_ref[...]
        dpblk_ref[...] += _dot_tn(r, do)
        dr = _dot_nt(do, pblk_ref[...])
        dzp = _window_sums(dr * _pool_inv_count(chunk), (eb1, eb2, eb4, eb8), 0, 1) - dr

        dyc = dy_ref[:, 2 * GW:3 * GW]
        sgc = _sigmoid(zcg)
        hbuf[HALO:HALO + T, :] = zcv * sgc
        mu = jnp.mean(hc, axis=-1, keepdims=True)
        xc = hc - mu
        var = jnp.mean(xc * xc, axis=-1, keepdims=True)
        rstd = lax.rsqrt(var + LN_EPS)
        nrm = xc * rstd
        ln = nrm * clg_ref[...] + clb_ref[...]
        sl = _sigmoid(ln)
        dln = dyc * (sl * (1.0 + ln * (1.0 - sl)))
        small_ref[5:6, :] += _colsum(dln * nrm)
        small_ref[6:7, :] += _colsum(dln)
        dn = dln * clg_ref[...]
        dhc = rstd * (dn - jnp.mean(dn, axis=-1, keepdims=True) - nrm * jnp.mean(dn * nrm, axis=-1, keepdims=True))
        small_ref[4:5, :] += _colsum(dhc)
        dhbuf[0:T, :] = dhc
        _phase_copies(hbuf, rot_h)
        _phase_copies(dhbuf, rot_d)
        dhg = None
        for k in range(CF_TAPS):
            small_ref[16 + k:17 + k, :] += _colsum(dhc * _tap(hbuf, rot_h, HALO - (CF_TAPS - 1) + k))
            v = dww_ref[k:k + 1, :] * _tap(dhbuf, rot_d, CF_TAPS - 1 - k)
            dhg = v if dhg is None else dhg + v
        dzcv = dhg * sgc
        dzcg = dhg * zcv * sgc * (1.0 - sgc)

        dyd = dy_ref[:, 3 * GW:4 * GW]
        yg = _gelu(yy)
        v = _dot(yg, wglu_ref[...]) + bglu_ref[...]
        sg = _sigmoid(v)
        dv = dyd * yg * sg * (1.0 - sg)
        small_ref[8:9, :] += _colsum(dv)
        dwglu_ref[...] += _dot_tn(yg, dv)
        g = (dyd * sg + _dot_nt(dv, wglu_ref[...])) * _gelu_grad(yy)
        small_ref[7:8, :] += _colsum(g * zs)
        dcblk_ref[...] += _dot_tn(xs_ref[...], g)
        a_scr[...] = _dot_nt(g, cblk_ref[...])
        _scan(a_scr, steps_ref, chain_ref, acarry[...], reverse=True)
        acarry[...] = a_scr[0:1, :]
        dzs = _dot_nt(a_scr[...], bblk_ref[...]) + ssd_ref[...] * g
        dbblk_ref[...] += _dot_tn(zs, a_scr[...])
        row = lax.broadcasted_iota(jnp.int32, (T, 128), 0)
        for j in range(XH // 128):
            lo, hi = j * 128, XH + j * 128
            first_r = xprev_ref[SUB - 1:SUB, lo:lo + 128] * has_prev
            first_i = xprev_ref[SUB - 1:SUB, hi:hi + 128] * has_prev
            pr = jnp.where(row == 0, first_r, pltpu.roll(xs_ref[:, lo:lo + 128], 1, 0))
            pi = jnp.where(row == 0, first_i, pltpu.roll(xs_ref[:, hi:hi + 128], 1, 0))
            ar, ai = a_scr[:, lo:lo + 128], a_scr[:, hi:hi + 128]
            dlam_ref[0:1, lo:lo + 128] += _colsum(ar * pr + ai * pi)
            dlam_ref[0:1, hi:hi + 128] += _colsum(ai * pr - ar * pi)

        parts = (dzh, dzb, dzc, dzp, dzcv, dzcg, dzs)
        for k, part in enumerate(parts):
            dz_ref[:, k * GW:(k + 1) * GW] = part.astype(dz_ref.dtype)
            vec_ref[0:1, k * GW:(k + 1) * GW] += _colsum(part)
        for ref in (dcbuf, dhbuf, eb1, eb2, eb4, eb8):
            ref[T:T + HALO, :] = ref[0:HALO, :]

    rev = lambda i: (nch - 1 - i, 0)

    def before(rows):
        return lambda i: (jnp.maximum((nch - 1 - i) * (T // rows) - 1, 0), 0)

    outs = [((s, INW), _MXU), ((8, INW), F32), ((48, GW), F32), ((GW, GW), F32), ((GW, GW), F32),
            ((GW, XW), F32), ((XW, GW), F32), ((8, XW), F32)]
    return _call(
        body, name=name, grid=(nch,),
        out_shape=[jax.ShapeDtypeStruct(sh, dt) for sh, dt in outs],
        in_specs=[pl.BlockSpec((T, INW), rev), pl.BlockSpec((HALO, INW), before(HALO)), pl.BlockSpec((T, D), rev),
                  _resident(w_o.shape), pl.BlockSpec((T, XW), rev), pl.BlockSpec((SUB, XW), before(SUB)),
                  pl.BlockSpec((T, 4 * GW), rev)]
        + [_sel(p, l) for p in params],
        out_specs=[pl.BlockSpec((T, INW), rev)] + [_full(sh) for sh, _ in outs[1:]],
        scratch=[pltpu.VMEM((HALO + T, GW), F32)] * 8
        + [pltpu.VMEM((SUB - 1, HALO + T, GW), F32)] * 2
        + [pltpu.VMEM((1, XW), F32), pltpu.VMEM((T, XW), F32), pltpu.VMEM((T, D), F32)],
        args=(z, z, dres, w_o, xs, xs, aux, *params))


def _blockdiag(m):
    *lead, g, h, p = m.shape
    eye = jnp.eye(g, dtype=m.dtype)
    return (m[..., :, :, None, :] * eye[:, None, :, None]).reshape(*lead, g * h, g * p)


def _blockdiag_extract(f, h, p):
    *lead, r, _ = f.shape
    g = r // h
    eye = jnp.eye(g, dtype=f.dtype)
    return jnp.sum(f.reshape(*lead, g, h, g, p) * eye[:, None, :, None], axis=-2)


def _t(a):
    return jnp.swapaxes(a, -1, -2)


PACK_ROWS = 512


def _pack(arrs):
    flat = jnp.concatenate([a.reshape(-1).astype(F32) for a in arrs])
    quantum = PACK_ROWS * 128
    padded = -(-flat.shape[0] // quantum) * quantum
    return jnp.pad(flat, (0, padded - flat.shape[0])).reshape(-1, 128)


def _unpack(buf, shapes, lead=()):
    flat = buf.reshape(lead + (-1,))
    out, off = [], 0
    for shp in shapes:
        n = math.prod(shp)
        out.append(flat[..., off:off + n].reshape(lead + tuple(shp)))
        off += n
    return out


def _cols_to_full(g):
    n, l, r, c = g.shape
    return g.transpose(1, 2, 0, 3).reshape(l, r, n * c)


def _rows_to_full(g):
    n, l, r, c = g.shape
    return g.transpose(1, 0, 2, 3).reshape(l, n * r, c)


REPLICATED = ['b_ada', 'b_in', 'pool_w', 'pool_scale', 'cf_dw_b', 'cf_ln_g', 'cf_ln_b', 'ssm_lam_re', 'ssm_lam_im',
              'ssm_log_dt', 'ssm_b_re', 'ssm_b_im', 'ssm_c_re', 'ssm_c_im', 'ssm_d', 'ssm_b_glu', 'ln1_g', 'ln1_b',
              'ln2_g', 'ln2_b']
SMALL_SHARDED = ['sc_w', 'cf_dw_w', 'ssm_w_glu']
COL_SHARDED = ['w_in', 'w_gate', 'w_up']
ROW_SHARDED = ['w_o', 'w_down']
BIG = COL_SHARDED + ROW_SHARDED
WEIGHTS = ['w_ada', 'b_ada', 'w_in', 'b_in', 'sc_w', 'pool_w', 'pool_scale', 'cf_dw_w', 'cf_dw_b', 'cf_ln_g',
           'cf_ln_b', 'ssm_lam_re', 'ssm_lam_im', 'ssm_log_dt', 'ssm_b_re', 'ssm_b_im', 'ssm_c_re', 'ssm_c_im',
           'ssm_d', 'ssm_w_glu', 'ssm_b_glu', 'w_o', 'ln1_g', 'ln1_b', 'w_gate', 'w_up', 'w_down', 'ln2_g', 'ln2_b']
SH1, SC1, G1, SH2, SC2, G2 = range(6)


def _train_step(p):
    xi, yi, ci = _me()
    me = _flat((xi, yi, ci))
    s = p['x'].shape[1]
    xs = p['x'].reshape(s, D)
    target = p['loss_target'].reshape(s, D)
    vec3 = lambda a: a.reshape(DEPTH, 1, -1)

    def shard(n, l):
        w = p[n][l]
        return (w.T if n in COL_SHARDED else w).astype(_MXU)

    def whole(g):
        return g.reshape(-1, g.shape[-1])

    small_shapes = [p[n].shape for n in SMALL_SHARDED] + [p['c'].shape]
    first = ('w_in', 'w_o')
    gathered = _allgather([_pack([p[n] for n in SMALL_SHARDED] + [p['c']])] + [shard(n, 0) for n in first],
                          "gather_first")
    scw_g, dww_g, wglu_g, c_g = _unpack(gathered[0], small_shapes, lead=(NDEV,))
    full = {'sc_w': _cols_to_full(scw_g), 'cf_dw_w': _cols_to_full(dww_g), 'ssm_w_glu': _rows_to_full(wglu_g)}
    wts = [dict(zip(first, (whole(g) for g in gathered[1:])))] + [dict() for _ in range(DEPTH - 1)]
    c_all = c_g.reshape(NDEV, D)

    ncol = p['w_ada'].shape[2]
    b_cols = lax.dynamic_slice_in_dim(p['b_ada'], me * ncol, ncol, axis=1).reshape(DEPTH, 1, ncol)
    cond, mod_cols = _ada_fwd(c_all, p['w_ada'], b_cols, "ada_fwd")
    (mod_x,) = _alltoall([mod_cols.transpose(1, 0, 2)], "mod_exchange")
    mod = mod_x.transpose(1, 0, 2).reshape(DEPTH, 6, 1, D)

    ssm_in = (p['ssm_lam_re'], p['ssm_lam_im'], p['ssm_log_dt'].reshape(DEPTH, NGRP, 1),
              _t(p['ssm_b_re']), _t(p['ssm_b_im']))
    pwr, pwi, bbr, bbi = _ssm_prep(*ssm_in, "ssm_prep")
    pw = jnp.concatenate([pwr.reshape(DEPTH, SUB, XH), pwi.reshape(DEPTH, SUB, XH)], axis=2)
    fwd_steps, fwd_chain, rev_steps, rev_chain = _scan_tables(pw)
    mp = dict(
        fwd_steps=fwd_steps, fwd_chain=fwd_chain, rev_steps=rev_steps, rev_chain=rev_chain,
        scw=full['sc_w'], pblk=_blockdiag(p['pool_w']).astype(_MXU), pscale=vec3(p['pool_scale']),
        dww=full['cf_dw_w'], dwb=vec3(p['cf_dw_b']), clg=vec3(p['cf_ln_g']), clb=vec3(p['cf_ln_b']),
        bblk=jnp.concatenate([_blockdiag(bbr), _blockdiag(bbi)], axis=2).astype(_MXU),
        cblk=jnp.concatenate([_blockdiag(_t(p['ssm_c_re'])), -_blockdiag(_t(p['ssm_c_im']))], axis=1).astype(_MXU),
        ssd=vec3(p['ssm_d']), wglu=full['ssm_w_glu'].astype(_MXU),
        bglu=vec3(p['ssm_b_glu']))
    b_in, ln1_g, ln1_b, ln2_g, ln2_b = (vec3(p[n]) for n in ('b_in', 'ln1_g', 'ln1_b', 'ln2_g', 'ln2_b'))

    saved = []
    for l in range(DEPTH):
        w = wts[l]

        def gather(at, *names):
            return None if at >= DEPTH or not names else (GATHER, [shard(n, at) for n in names])

        def take(res, n_own, at, *names):
            if at < DEPTH:
                for n, g in zip(names, res[n_own:]):
                    wts[at][n] = whole(g)
            return res[:n_own]

        ycat, states, aux, z = take(_mixer_fwd(xs, mod, l, w['w_in'], b_in, mp, "mixer_fwd",
                                               comm=gather(l, 'w_gate', 'w_up')), 4, l, 'w_gate', 'w_up')
        early = ('w_down',) if l == 0 else ()
        y, u1, x1 = take(_proj_residual_ln(ycat, w['w_o'], l, xs, mod, G1, ln1_g, ln1_b, "o_proj_ln",
                                           comm=gather(l, *early)), 3, l, *early)
        gate, up, act = take(_modulated_matmul(x1, mod, l, SC2, SH2, [w['w_gate'], w['w_up']], None, TM_WIDE,
                                               [_MXU, _MXU, _MXU], _swiglu, "gate_up",
                                               comm=gather(l + 1, 'w_in', 'w_o')), 3, l + 1, 'w_in', 'w_o')
        f, u2, x2 = take(_proj_residual_ln(act, w['w_down'], l, x1, mod, G2, ln2_g, ln2_b, "down_proj_ln",
                                           comm=gather(l + 1, 'w_down')), 3, l + 1, 'w_down')
        saved.append(dict(x=xs, z=z, ycat=ycat, states=states, aux=aux, y=y, u1=u1, x1=x1, gate=gate, up=up,
                          act=act, f=f, u2=u2))
        xs = x2

    recv = {n: [None] * DEPTH for n in BIG}
    small_g = {n: [None] * DEPTH for n in ('ln1_g', 'ln1_b', 'ln2_g', 'ln2_b')}
    dmod = [None] * DEPTH
    mixer_out = [None] * DEPTH
    pending = None
    top = DEPTH - 1
    du2, df, acc_top = _loss_ln_bwd(xs, target, saved[top]['u2'], saved[top]['f'], ln2_g, top, mod, G2,
                                    "loss_ln2_bwd")
    small_g['ln2_g'][top], small_g['ln2_b'][top], dg2 = acc_top[0], acc_top[1], acc_top[2]
    loss = lax.psum(acc_top[3, 0] * (0.5 / D), AXES)
    for l in reversed(range(DEPTH)):
        sv, w = saved[l], wts[l]
        (g_down,) = _weight_grad(sv['act'], df, "dw_down")
        lp, late_names, late = pending or (None, [], [])
        dgate, dup, du1, dyb, acc_a, recv['w_down'][l], *landed = _input_grad_ln_bwd(
            None, [w['w_gate'], w['w_up']], du2, (mod, l, SC2), sv['u1'], sv['y'], (ln1_g, l), (ln1_b, l),
            (mod, l, G1), TM_WIDE, "mlp_bwd", swiglu=(df, w['w_down'], sv['gate'], sv['up']),
            comm=(EXCHANGE, [g_down] + late))
        for n, r in zip(late_names, landed):
            recv[n][lp] = r
        (g_gate,) = _weight_grad(dgate, sv['x1'], "dw_gate", mod=(mod, l, SC2, SH2))
        (g_up,) = _weight_grad(dup, sv['x1'], "dw_up", mod=(mod, l, SC2, SH2))
        (g_o,) = _weight_grad(sv['ycat'], dyb, "dw_o")
        dz, *mixer_out[l] = _mixer_bwd(sv['z'], dyb, w['w_o'], sv['states'], sv['aux'], mp, l, "mixer_bwd")
        if l > 0:
            du2, df, acc_b, recv['w_gate'][l] = _input_grad_ln_bwd(
                [dz], [w['w_in']], du1, (mod, l, SC1), saved[l - 1]['u2'], saved[l - 1]['f'], (ln2_g, l - 1),
                (ln2_b, l - 1), (mod, l - 1, G2), TM_WIDE, "dh1_ln2_bwd", comm=(EXCHANGE, [g_gate]))
            (g_in,) = _weight_grad(dz, sv['x'], "dw_in", mod=(mod, l, SC1, SH1))
            pending = (l, ['w_up', 'w_o', 'w_in'], [g_up, g_o, g_in])
        else:
            dx, acc_b, recv['w_gate'][l] = _input_grad([dz], [w['w_in']], l, du1, sv['x'], mod, SC1, TM, "dh1",
                                                       comm=(EXCHANGE, [g_gate]))
            g_in, recv['w_up'][l] = _weight_grad(dz, sv['x'], "dw_in", mod=(mod, l, SC1, SH1),
                                                 comm=(EXCHANGE, [g_up]))
            pending = (l, ['w_o', 'w_in'], [g_o, g_in])
        dmod[l] = jnp.concatenate([acc_b[0], acc_b[1], acc_a[4], acc_a[0], acc_a[1], dg2])
        small_g['ln1_g'][l], small_g['ln1_b'][l] = acc_a[2], acc_a[3]
        if l > 0:
            small_g['ln2_g'][l - 1], small_g['ln2_b'][l - 1], dg2 = acc_b[2], acc_b[3], acc_b[4]
    lp, late_names, late = pending
    for n, r in zip(late_names, _alltoall(late, "exchange_last")):
        recv[n][lp] = r

    partial = {n: jnp.stack(v) for n, v in small_g.items()}
    partial['b_ada'] = jnp.stack(dmod)
    vec, small, dpblk, dwglu, dbblk, dcblk, dlam = (jnp.stack(t) for t in zip(*mixer_out))
    partial.update(
        b_in=vec[:, 0], sc_w=small[:, 0:SC_TAPS], pool_scale=small[:, 3], cf_dw_b=small[:, 4], cf_ln_g=small[:, 5],
        cf_ln_b=small[:, 6], ssm_d=small[:, 7], ssm_b_glu=small[:, 8], cf_dw_w=small[:, 16:16 + CF_TAPS],
        pool_w=_blockdiag_extract(dpblk, 64, 64), ssm_w_glu=dwglu,
        ssm_c_re=_t(_blockdiag_extract(dcblk[:, :XH], NSTATE, NCH)),
        ssm_c_im=-_t(_blockdiag_extract(dcblk[:, XH:], NSTATE, NCH)))
    dlr, dli, dlog, dbr, dbi = _ssm_param_bwd(
        *ssm_in, dlam[:, 0, :XH].reshape(DEPTH, NGRP, NSTATE), dlam[:, 0, XH:].reshape(DEPTH, NGRP, NSTATE),
        _blockdiag_extract(dbblk[:, :, :XH], NCH, NSTATE), _blockdiag_extract(dbblk[:, :, XH:], NCH, NSTATE),
        "ssm_param_bwd")
    partial.update(ssm_lam_re=dlr, ssm_lam_im=dli, ssm_log_dt=dlog.reshape(DEPTH, NGRP), ssm_b_re=_t(dbr),
                   ssm_b_im=_t(dbi))

    out = {'loss': loss, 'grad_x': dx.reshape(1, s, D)}

    def emit(n, g, delta, nm, nv):
        shp = p[n].shape
        out['grad_' + n], out['delta_' + n] = g.reshape(shp), delta.reshape(shp)
        out['new_m_' + n], out['new_v_' + n] = nm.reshape(shp), nv.reshape(shp)

    names_a = REPLICATED + SMALL_SHARDED
    shapes_a = [partial[n].shape for n in names_a]
    zeros_like_full = lambda n: jnp.zeros(partial[n].shape, F32)
    (parts_a,) = _allgather([_pack([partial[n] for n in names_a])], "gather_small_grads")
    packs = [_pack([p[pre + n] for n in REPLICATED] + [zeros_like_full(n) for n in SMALL_SHARDED])
             for pre in ('', 'm_', 'v_')]
    res_a = _adamw(parts_a, *packs, PACK_ROWS, "adamw_small")
    res_a = [_unpack(r, shapes_a) for r in res_a]
    for k, n in enumerate(REPLICATED):
        emit(n, *[r[k] for r in res_a])
    reduced = dict(zip(names_a, res_a[0]))

    blocks = []
    for n in SMALL_SHARDED:
        axis = 1 if n == 'ssm_w_glu' else 2
        width = p[n].shape[axis]
        blocks.append(lax.dynamic_slice_in_dim(reduced[n], me * width, width, axis=axis))
    shapes_b = [b.shape for b in blocks]
    packs = [_pack([p[pre + n] for n in SMALL_SHARDED]) for pre in ('', 'm_', 'v_')]
    res_b = _adamw(_pack(blocks)[None], *packs, PACK_ROWS, "adamw_small_sharded")
    res_b = [_unpack(r, shapes_b) for r in res_b]
    for k, n in enumerate(SMALL_SHARDED):
        emit(n, *[r[k] for r in res_b])

    dmod_all = _unpack(parts_a, shapes_a, lead=(NDEV,))[0]
    dmod_cols = lax.dynamic_slice_in_dim(dmod_all, me * ncol, ncol, axis=2).transpose(1, 0, 2)
    g_ada = _ada_bwd(cond, dmod_cols, "ada_bwd")
    flat2 = lambda a: a.reshape(-1, a.shape[-1])
    emit('w_ada', *_adamw(flat2(g_ada)[None], flat2(p['w_ada']), flat2(p['m_w_ada']), flat2(p['v_w_ada']),
                          512, "adamw_w_ada"))

    for n in BIG:
        tr = recv[n][0].shape[1] // 2
        if n in ROW_SHARDED:
            emit(n, *_adamw_layers(recv[n], flat2(p[n]), flat2(p['m_' + n]), flat2(p['v_' + n]), tr, "adamw_" + n))
        else:
            g = flat2(_t(_reduce_layers(recv[n], tr, "sum_" + n)))
            emit(n, *_adamw(g[None], flat2(p[n]), flat2(p['m_' + n]), flat2(p['v_' + n]), 512, "adamw_" + n))
    return out


def kernel(x, c, w_ada, b_ada, w_in, b_in, sc_w, pool_w, pool_scale, cf_dw_w, cf_dw_b, cf_ln_g, cf_ln_b, ssm_lam_re, ssm_lam_im, ssm_log_dt, ssm_b_re, ssm_b_im, ssm_c_re, ssm_c_im, ssm_d, ssm_w_glu, ssm_b_glu, w_o, ln1_g, ln1_b, w_gate, w_up, w_down, ln2_g, ln2_b, loss_target, m_w_ada, m_b_ada, m_w_in, m_b_in, m_sc_w, m_pool_w, m_pool_scale, m_cf_dw_w, m_cf_dw_b, m_cf_ln_g, m_cf_ln_b, m_ssm_lam_re, m_ssm_lam_im, m_ssm_log_dt, m_ssm_b_re, m_ssm_b_im, m_ssm_c_re, m_ssm_c_im, m_ssm_d, m_ssm_w_glu, m_ssm_b_glu, m_w_o, m_ln1_g, m_ln1_b, m_w_gate, m_w_up, m_w_down, m_ln2_g, m_ln2_b, v_w_ada, v_b_ada, v_w_in, v_b_in, v_sc_w, v_pool_w, v_pool_scale, v_cf_dw_w, v_cf_dw_b, v_cf_ln_g, v_cf_ln_b, v_ssm_lam_re, v_ssm_lam_im, v_ssm_log_dt, v_ssm_b_re, v_ssm_b_im, v_ssm_c_re, v_ssm_c_im, v_ssm_d, v_ssm_w_glu, v_ssm_b_glu, v_w_o, v_ln1_g, v_ln1_b, v_w_gate, v_w_up, v_w_down, v_ln2_g, v_ln2_b):
    out = _train_step(dict(locals()))
    return (out['loss'], out['grad_x'], *[out[pre + n] for pre in ('grad_', 'delta_', 'new_m_', 'new_v_')
                                          for n in WEIGHTS])
```

```python
import math

import jax
import jax.numpy as jnp
from jax import lax
from jax.experimental import pallas as pl
from jax.experimental.pallas import tpu as pltpu

F32 = jnp.float32
_MXU = jnp.bfloat16

D = 1024
GW = 256
INW = 7 * GW
DFF = 2816
DEPTH = 4
NDEV = 8
SC_TAPS = 3
CF_TAPS = 31
NGRP = 16
NCH = 16
NSTATE = 64
XH = NGRP * NSTATE
XW = 2 * XH
ALPHA = (2 * DEPTH) ** 0.25
LN_EPS = 1e-5
T = 512
SUB = 8
HALO = 32
TM = 512
TM_WIDE = 256
VMEM_LIMIT = 56 * 1024 * 1024

ADAM_LR = 0.001
ADAM_B1 = 0.9
ADAM_B2 = 0.999
ADAM_EPS = 1e-08
ADAM_WD = 0.01
ADAM_STEP = 10

MESH = pl.DeviceIdType.MESH
ANY = pl.BlockSpec(memory_space=pl.ANY)
AXES = ("x", "y", "c")


def _full(shape):
    nd = len(shape)
    return pl.BlockSpec(shape, lambda *_: (0,) * nd)


def _resident(shape):
    nd = len(shape)
    return pl.BlockSpec(shape, lambda *_: (0,) * nd, pipeline_mode=pl.Buffered(1))


def _sel(arr, *idx):
    tail = arr.shape[len(idx):]
    return pl.BlockSpec((None,) * len(idx) + tail, lambda *_: idx + (0,) * len(tail))


def _dot(a, b):
    return jnp.dot(a.astype(_MXU), b.astype(_MXU), preferred_element_type=F32)


def _dot_nt(a, b):
    return lax.dot_general(a.astype(_MXU), b.astype(_MXU), (((1,), (1,)), ((), ())), preferred_element_type=F32)


def _dot_tn(a, b):
    return lax.dot_general(a.astype(_MXU), b.astype(_MXU), (((0,), (0,)), ((), ())), preferred_element_type=F32)


def _sigmoid(x):
    return 1.0 / (1.0 + jnp.exp(-x))


def _colsum(x):
    return jnp.sum(x, axis=0, keepdims=True)


def _me():
    return lax.axis_index("x"), lax.axis_index("y"), lax.axis_index("c")


def _flat(p):
    return 4 * p[0] + 2 * p[1] + p[2]


def _allgather(arrays, name):
    n = len(arrays)

    def body(*refs):
        ins, outs = refs[:n], refs[n:2 * n]
        send_sems, recv_sems, local_sems = refs[2 * n:]
        x, y, c = _me()
        me, sibling = (x, y, c), (x, y, 1 - c)
        chips = [(1 - x, y), (x, 1 - y), (1 - x, 1 - y)]

        def copy(a, k, block, to, src=None):
            dst = outs[a].at[_flat(block)]
            return pltpu.make_async_remote_copy(
                src_ref=dst if src is None else src, dst_ref=dst,
                send_sem=send_sems.at[a, k], recv_sem=recv_sems.at[a, k],
                device_id=to, device_id_type=MESH)

        started = []
        for a in range(n):
            mine = pltpu.make_async_copy(ins[a], outs[a].at[_flat(me)], local_sems.at[a])
            mine.start()
            started.append(mine)
        first = []
        for a in range(n):
            first.append(copy(a, 0, me, sibling, src=ins[a]))
            first += [copy(a, 1 + j, me, (*chip, c), src=ins[a]) for j, chip in enumerate(chips)]
        for cp in first:
            cp.start()
        passed = []
        for a in range(n):
            for j, chip in enumerate(chips):
                copy(a, 1 + j, (*chip, c), me).wait_recv()
                fwd = copy(a, 4 + j, (*chip, c), sibling)
                fwd.start()
                passed.append(fwd)
        for a in range(n):
            copy(a, 0, sibling, me).wait_recv()
            for j, chip in enumerate(chips):
                copy(a, 4 + j, (*chip, 1 - c), me).wait_recv()
        for cp in first + passed:
            cp.wait_send()
        for cp in started:
            cp.wait()

    return pl.pallas_call(
        body, name=name,
        out_shape=[jax.ShapeDtypeStruct((NDEV,) + a.shape, a.dtype) for a in arrays],
        in_specs=[ANY] * n, out_specs=[ANY] * n,
        scratch_shapes=[pltpu.SemaphoreType.DMA((n, 7)), pltpu.SemaphoreType.DMA((n, 7)),
                        pltpu.SemaphoreType.DMA((n,))],
    )(*arrays)


GATHER, EXCHANGE = "gather", "exchange"


def _direct_copies(kind, ins, outs, send_sems, recv_sems, local_sems, start):
    x, y, c = _me()
    mine = _flat((x, y, c))
    for a in range(len(ins)):
        own = ins[a] if kind == GATHER else ins[a].at[mine]
        loc = pltpu.make_async_copy(own, outs[a].at[mine], local_sems.at[a])
        if start:
            loc.start()
        for k in range(1, NDEV):
            peer = (lax.rem(x + ((k >> 2) & 1), 2), lax.rem(y + ((k >> 1) & 1), 2), lax.rem(c + (k & 1), 2))
            cp = pltpu.make_async_remote_copy(
                src_ref=ins[a] if kind == GATHER else ins[a].at[_flat(peer)], dst_ref=outs[a].at[mine],
                send_sem=send_sems.at[a, k - 1], recv_sem=recv_sems.at[a, k - 1],
                device_id=peer, device_id_type=MESH)
            if start:
                cp.start()
            else:
                cp.wait_send()
                cp.wait_recv()
        if not start:
            loc.wait()


def _comm_out_shapes(comm):
    kind, arrays = comm
    return [jax.ShapeDtypeStruct(((NDEV,) + a.shape) if kind == GATHER else a.shape, a.dtype) for a in arrays]


def _call(body, *, name, grid, in_specs, out_specs, out_shape, args, scratch=(), comm=None):
    params = pltpu.CompilerParams(dimension_semantics=("arbitrary",) * len(grid), vmem_limit_bytes=VMEM_LIMIT)
    if comm is None:
        return pl.pallas_call(body, name=name, grid=grid, in_specs=in_specs, out_specs=out_specs,
                              out_shape=out_shape, scratch_shapes=list(scratch), compiler_params=params)(*args)
    kind, arrays = comm
    n, n_in, n_out, n_scr = len(arrays), len(in_specs), len(out_specs), len(scratch)

    def wrapped(*refs):
        ins, cin = refs[:n_in], refs[n_in:n_in + n]
        outs, cout = refs[n_in + n:n_in + n + n_out], refs[n_in + n + n_out:n_in + 2 * n + n_out]
        scr = refs[n_in + 2 * n + n_out:]
        user_scr, sems = scr[:n_scr], scr[n_scr:]
        ids = [pl.program_id(k) for k in range(len(grid))]
        first, last = ids[0] == 0, ids[0] == grid[0] - 1
        for k in range(1, len(grid)):
            first, last = first & (ids[k] == 0), last & (ids[k] == grid[k] - 1)

        @pl.when(first)
        def _():
            _direct_copies(kind, cin, cout, *sems, start=True)

        body(*ins, *outs, *user_scr)

        @pl.when(last)
        def _():
            _direct_copies(kind, cin, cout, *sems, start=False)

    sems = [pltpu.SemaphoreType.DMA((n, 7)), pltpu.SemaphoreType.DMA((n, 7)), pltpu.SemaphoreType.DMA((n,))]
    res = pl.pallas_call(
        wrapped, name=name, grid=grid, in_specs=list(in_specs) + [ANY] * n, out_specs=list(out_specs) + [ANY] * n,
        out_shape=list(out_shape) + _comm_out_shapes(comm), scratch_shapes=list(scratch) + sems,
        compiler_params=params)(*args, *arrays)
    return res


def _alltoall(arrays, name):
    n = len(arrays)

    def body(*refs):
        _direct_copies(EXCHANGE, refs[:n], refs[n:2 * n], *refs[2 * n:], start=True)
        _direct_copies(EXCHANGE, refs[:n], refs[n:2 * n], *refs[2 * n:], start=False)

    return pl.pallas_call(
        body, name=name,
        out_shape=[jax.ShapeDtypeStruct(a.shape, a.dtype) for a in arrays],
        in_specs=[ANY] * n, out_specs=[ANY] * n,
        scratch_shapes=[pltpu.SemaphoreType.DMA((n, 7)), pltpu.SemaphoreType.DMA((n, 7)),
                        pltpu.SemaphoreType.DMA((n,))],
    )(*arrays)


def _gate_up(x, mod, l, w_gate_t, w_up_t, name, comm=None):
    s, k = x.shape
    n = w_gate_t.shape[0]

    def body(x_ref, sc_ref, sh_ref, wg_ref, wu_ref, g_ref, u_ref, a_ref):
        h = (x_ref[...] * (1.0 + sc_ref[...]) + sh_ref[...]).astype(_MXU)
        gate = lax.dot_general(h, wg_ref[...], (((1,), (1,)), ((), ())), preferred_element_type=F32)
        up = lax.dot_general(h, wu_ref[...], (((1,), (1,)), ((), ())), preferred_element_type=F32)
        g_ref[...] = gate.astype(g_ref.dtype)
        u_ref[...] = up.astype(u_ref.dtype)
        a_ref[...] = (gate * _sigmoid(gate) * up).astype(a_ref.dtype)

    wide = pl.BlockSpec((TM_WIDE, n), lambda i: (i, 0))
    return _call(
        body, name=name, grid=(s // TM_WIDE,),
        out_shape=[jax.ShapeDtypeStruct((s, n), _MXU)] * 3,
        in_specs=[pl.BlockSpec((TM_WIDE, k), lambda i: (i, 0)), _sel(mod, l, SC2), _sel(mod, l, SH2),
                  _resident(w_gate_t.shape), _resident(w_up_t.shape)],
        out_specs=[wide, wide, wide], args=(x, mod, mod, w_gate_t, w_up_t), comm=comm)


def _proj_residual_ln(a, w, l, xres, mod, kg, lng, lnb, name, comm=None):
    s, k = a.shape
    n = w.shape[1]

    def body(a_ref, w_ref, x_ref, g_ref, lg_ref, lb_ref, r_ref, u_ref, o_ref):
        r = jnp.dot(a_ref[...], w_ref[...], preferred_element_type=F32)
        u = ALPHA * x_ref[...] + (1.0 + g_ref[...]) * r
        mu = jnp.mean(u, axis=-1, keepdims=True)
        xc = u - mu
        var = jnp.mean(xc * xc, axis=-1, keepdims=True)
        r_ref[...] = r.astype(r_ref.dtype)
        u_ref[...] = u
        o_ref[...] = xc * lax.rsqrt(var + LN_EPS) * lg_ref[...] + lb_ref[...]

    row = pl.BlockSpec((TM, n), lambda i: (i, 0))
    return _call(
        body, name=name, grid=(s // TM,),
        out_shape=[jax.ShapeDtypeStruct((s, n), _MXU)] + [jax.ShapeDtypeStruct((s, n), F32)] * 2,
        in_specs=[pl.BlockSpec((TM, k), lambda i: (i, 0)), _resident(w.shape), row, _sel(mod, l, kg), _sel(lng, l),
                  _sel(lnb, l)],
        out_specs=[row, row, row], args=(a, w, xres, mod, lng, lnb), comm=comm)


def _loss_ln_bwd(xo, target, u, res, lng, l, mod, kg, name):
    s, n = u.shape

    def body(x_ref, t_ref, u_ref, r_ref, lg_ref, g_ref, du_ref, dr_ref, acc_ref):
        @pl.when(pl.program_id(0) == 0)
        def _():
            acc_ref[...] = jnp.zeros_like(acc_ref)

        uu = u_ref[...]
        mu = jnp.mean(uu, axis=-1, keepdims=True)
        xc = uu - mu
        var = jnp.mean(xc * xc, axis=-1, keepdims=True)
        rstd = lax.rsqrt(var + LN_EPS)
        xh = xc * rstd
        err = x_ref[...] - t_ref[...]
        acc_ref[3:4, :] += jnp.sum(err * err)
        dy = err * (1.0 / n)
        dxh = dy * lg_ref[...]
        du = rstd * (dxh - jnp.mean(dxh, axis=-1, keepdims=True) - xh * jnp.mean(dxh * xh, axis=-1, keepdims=True))
        du_ref[...] = du
        dr_ref[...] = (du * (1.0 + g_ref[...])).astype(dr_ref.dtype)
        acc_ref[0:1, :] += _colsum(dy * xh)
        acc_ref[1:2, :] += _colsum(dy)
        acc_ref[2:3, :] += _colsum(du * r_ref[...].astype(F32))

    row = pl.BlockSpec((TM, n), lambda i: (i, 0))
    return _call(
        body, name=name, grid=(s // TM,),
        out_shape=[jax.ShapeDtypeStruct((s, n), F32), jax.ShapeDtypeStruct((s, n), _MXU),
                   jax.ShapeDtypeStruct((8, n), F32)],
        in_specs=[row, row, row, row, _sel(lng, l), _sel(mod, l, kg)], out_specs=[row, row, _full((8, n))],
        args=(xo, target, u, res, lng, mod))


def _input_grad(acts, wts, l, du, xin, mod, ksc, tm, name, comm=None):
    s = acts[0].shape[0]
    n = wts[0].shape[1]
    na = len(acts)

    def body(*refs):
        a_refs, w_refs = refs[:na], refs[na:2 * na]
        du_ref, x_ref, sc_ref, dx_ref, acc_ref = refs[2 * na:]

        @pl.when(pl.program_id(0) == 0)
        def _():
            acc_ref[...] = jnp.zeros_like(acc_ref)

        dh = None
        for a, w in zip(a_refs, w_refs):
            t = jnp.dot(a[...], w[...], preferred_element_type=F32)
            dh = t if dh is None else dh + t
        dx_ref[...] = ALPHA * du_ref[...] + dh * (1.0 + sc_ref[...])
        acc_ref[0:1, :] += _colsum(dh)
        acc_ref[1:2, :] += _colsum(dh * x_ref[...])

    row = pl.BlockSpec((tm, n), lambda i: (i, 0))
    return _call(
        body, name=name, grid=(s // tm,),
        out_shape=[jax.ShapeDtypeStruct((s, n), F32), jax.ShapeDtypeStruct((8, n), F32)],
        in_specs=[pl.BlockSpec((tm, a.shape[1]), lambda i: (i, 0)) for a in acts]
        + [_resident(w.shape) for w in wts] + [row, row, _sel(mod, l, ksc)],
        out_specs=[row, _full((8, n))], args=(*acts, *wts, du, xin, mod), comm=comm)


SWIGLU_SPLIT = 2


def _input_grad_ln_bwd(acts, wts, du_in, sc, u, res, lng, lnb, g, tm, name, swiglu=None, comm=None):
    n = wts[0].shape[1]
    na = len(wts)
    ns = 0 if swiglu is None else 4
    s = (swiglu[0] if ns else acts[0]).shape[0]
    f = wts[0].shape[0]

    nlead = ns or len(acts)

    def body(*refs):
        a_refs, w_refs = refs[:nlead], refs[nlead:nlead + na]
        dui_ref, sc_ref, u_ref, r_ref, lg_ref, lb_ref, g_ref = refs[nlead + na:nlead + na + 7]
        o_refs = refs[nlead + na + 7:]
        if ns:
            dp_ref, wd_ref, gate_ref, up_ref = a_refs
            dg_ref, dup_ref = o_refs[:2]
            o_refs = o_refs[2:]
        du_ref, dr_ref, acc_ref = o_refs

        @pl.when(pl.program_id(0) == 0)
        def _():
            acc_ref[...] = jnp.zeros_like(acc_ref)

        dh = None
        if ns:
            piece = f // SWIGLU_SPLIT
            for c in range(SWIGLU_SPLIT):
                lo = c * piece
                dp = lax.dot_general(dp_ref[...], wd_ref[lo:lo + piece, :], (((1,), (1,)), ((), ())),
                                     preferred_element_type=F32)
                gt = gate_ref[:, lo:lo + piece].astype(F32)
                sg = _sigmoid(gt)
                dgate = (dp * up_ref[:, lo:lo + piece].astype(F32) * (sg * (1.0 + gt * (1.0 - sg)))).astype(_MXU)
                dup = (dp * (gt * sg)).astype(_MXU)
                dg_ref[:, lo:lo + piece] = dgate
                dup_ref[:, lo:lo + piece] = dup
                t = (jnp.dot(dgate, w_refs[0][lo:lo + piece, :], preferred_element_type=F32)
                     + jnp.dot(dup, w_refs[1][lo:lo + piece, :], preferred_element_type=F32))
                dh = t if dh is None else dh + t
        else:
            for a, w in zip(a_refs, w_refs):
                t = jnp.dot(a[...], w[...], preferred_element_type=F32)
                dh = t if dh is None else dh + t
        dy = ALPHA * dui_ref[...] + dh * (1.0 + sc_ref[...])
        uu = u_ref[...]
        mu = jnp.mean(uu, axis=-1, keepdims=True)
        xc = uu - mu
        var = jnp.mean(xc * xc, axis=-1, keepdims=True)
        rstd = lax.rsqrt(var + LN_EPS)
        xh = xc * rstd
        dh_sum = _colsum(dh)
        acc_ref[0:1, :] += dh_sum
        acc_ref[1:2, :] += lg_ref[...] * _colsum(dh * xh) + lb_ref[...] * dh_sum
        dxh = dy * lg_ref[...]
        du = rstd * (dxh - jnp.mean(dxh, axis=-1, keepdims=True) - xh * jnp.mean(dxh * xh, axis=-1, keepdims=True))
        du_ref[...] = du
        dr_ref[...] = (du * (1.0 + g_ref[...])).astype(dr_ref.dtype)
        acc_ref[2:3, :] += _colsum(dy * xh)
        acc_ref[3:4, :] += _colsum(dy)
        acc_ref[4:5, :] += _colsum(du * r_ref[...].astype(F32))

    row = pl.BlockSpec((tm, n), lambda i: (i, 0))
    wide = pl.BlockSpec((tm, f), lambda i: (i, 0))
    if ns:
        dp_in, w_down, gate, up = swiglu
        lead_specs = [pl.BlockSpec((tm, dp_in.shape[1]), lambda i: (i, 0)), _resident(w_down.shape), wide, wide]
        lead_args = [dp_in, w_down, gate, up]
        lead_outs, lead_out_specs = [jax.ShapeDtypeStruct((s, f), _MXU)] * 2, [wide, wide]
    else:
        lead_specs = [pl.BlockSpec((tm, a.shape[1]), lambda i: (i, 0)) for a in acts]
        lead_args, lead_outs, lead_out_specs = list(acts), [], []
    return _call(
        body, name=name, grid=(s // tm,),
        out_shape=lead_outs + [jax.ShapeDtypeStruct((s, n), F32), jax.ShapeDtypeStruct((s, n), _MXU),
                               jax.ShapeDtypeStruct((8, n), F32)],
        in_specs=lead_specs + [_resident(w.shape) for w in wts]
        + [row, _sel(*sc), row, row, _sel(*lng), _sel(*lnb), _sel(*g)],
        out_specs=lead_out_specs + [row, row, _full((8, n))],
        args=(*lead_args, *wts, du_in, sc[0], u, res, lng[0], lnb[0], g[0]), comm=comm)


def _weight_grad(a, b, name, mod=None, comm=None):
    s, ka = a.shape
    nbc = b.shape[1]
    nm = 0 if mod is None else 2
    steps = s // TM

    def body(*refs):
        a_ref, b_ref = refs[:2]
        m_refs = refs[2:2 + nm]
        o_ref, acc = refs[2 + nm:]

        @pl.when(pl.program_id(0) == 0)
        def _():
            acc[...] = jnp.zeros_like(acc)

        bv = b_ref[...]
        if nm:
            bv = bv * (1.0 + m_refs[0][...]) + m_refs[1][...]
        acc[...] += lax.dot_general(a_ref[...], bv.astype(_MXU), (((0,), (0,)), ((), ())),
                                    preferred_element_type=F32)

        @pl.when(pl.program_id(0) == steps - 1)
        def _():
            o_ref[...] = acc[...].astype(o_ref.dtype)

    mspecs, margs = [], []
    if nm:
        marr, l, ksc, ksh = mod
        mspecs, margs = [_sel(marr, l, ksc), _sel(marr, l, ksh)], [marr, marr]
    res = _call(
        body, name=name, grid=(steps,),
        out_shape=[jax.ShapeDtypeStruct((ka, nbc), _MXU)],
        in_specs=[pl.BlockSpec((TM, ka), lambda t: (t, 0)), pl.BlockSpec((TM, nbc), lambda t: (t, 0))] + mspecs,
        out_specs=[_full((ka, nbc))], args=(a, b, *margs), scratch=[pltpu.VMEM((ka, nbc), F32)], comm=comm)
    return [res[0].reshape(NDEV, ka // NDEV, nbc)] + list(res[1:])


def _ada_fwd(c_all, w_ada, b_cols, name):
    cols = w_ada.shape[2]

    def body(c_ref, w_ref, b_ref, cond_ref, o_ref):
        cv = c_ref[...]
        cond = cv * _sigmoid(cv)

        @pl.when(pl.program_id(0) == 0)
        def _():
            cond_ref[...] = cond

        o_ref[...] = _dot(cond, w_ref[...]) + b_ref[...]

    return _call(
        body, name=name, grid=(DEPTH,),
        out_shape=[jax.ShapeDtypeStruct((NDEV, D), F32), jax.ShapeDtypeStruct((DEPTH, NDEV, cols), F32)],
        in_specs=[_full((NDEV, D)), pl.BlockSpec((None, D, cols), lambda l: (l, 0, 0)),
                  pl.BlockSpec((None, 1, cols), lambda l: (l, 0, 0))],
        out_specs=[_full((NDEV, D)), pl.BlockSpec((None, NDEV, cols), lambda l: (l, 0, 0))],
        args=(c_all, w_ada, b_cols))


def _ada_bwd(cond, dmod_cols, name):
    cols = dmod_cols.shape[2]

    def body(c_ref, d_ref, o_ref):
        o_ref[...] = _dot_tn(c_ref[...], d_ref[...])

    return _call(
        body, name=name, grid=(DEPTH,),
        out_shape=[jax.ShapeDtypeStruct((DEPTH, D, cols), F32)],
        in_specs=[_full((NDEV, D)), pl.BlockSpec((None, NDEV, cols), lambda l: (l, 0, 0))],
        out_specs=[pl.BlockSpec((None, D, cols), lambda l: (l, 0, 0))], args=(cond, dmod_cols))[0]


def _adam_math(w, g, m, v):
    m = ADAM_B1 * m + (1.0 - ADAM_B1) * g
    v = ADAM_B2 * v + (1.0 - ADAM_B2) * (g * g)
    m_hat = m / (1.0 - ADAM_B1 ** ADAM_STEP)
    v_hat = v / (1.0 - ADAM_B2 ** ADAM_STEP)
    delta = -ADAM_LR * (m_hat / (jnp.sqrt(v_hat) + ADAM_EPS) + ADAM_WD * w)
    return delta, m, v


def _sum_parts(p_ref):
    g = p_ref[0].astype(F32)
    for i in range(1, p_ref.shape[0]):
        g = g + p_ref[i].astype(F32)
    return g


def _adamw(parts, w, m, v, tr, name):
    r, c = w.shape
    nparts = parts.shape[0]

    def body(p_ref, w_ref, m_ref, v_ref, g_ref, d_ref, nm_ref, nv_ref):
        g = _sum_parts(p_ref)
        delta, nm, nv = _adam_math(w_ref[...], g, m_ref[...], v_ref[...])
        g_ref[...] = g
        d_ref[...] = delta
        nm_ref[...] = nm
        nv_ref[...] = nv

    blk = pl.BlockSpec((tr, c), lambda i: (i, 0))
    return _call(
        body, name=name, grid=(r // tr,),
        out_shape=[jax.ShapeDtypeStruct((r, c), F32)] * 4,
        in_specs=[pl.BlockSpec((nparts, tr, c), lambda i: (0, i, 0)), blk, blk, blk],
        out_specs=[blk] * 4, args=(parts, w, m, v))


def _layer_parts_specs(parts, tr):
    nparts, r, c = parts[0].shape
    last = r // tr - 1

    def spec(k):
        return pl.BlockSpec((nparts, tr, c),
                            lambda l, i: (0, jnp.where(l < k, 0, jnp.where(l == k, i, last)), 0))

    return [spec(k) for k in range(len(parts))]


def _for_layer(p_refs, fn):
    for k, ref in enumerate(p_refs):
        pl.when(pl.program_id(0) == k)(lambda ref=ref: fn(ref))


def _adamw_layers(parts, w, m, v, tr, name):
    nl = len(parts)
    _, r, c = parts[0].shape

    def body(*refs):
        p_refs = refs[:nl]
        w_ref, m_ref, v_ref, g_ref, d_ref, nm_ref, nv_ref = refs[nl:]

        def update(p_ref):
            g = _sum_parts(p_ref)
            delta, nm, nv = _adam_math(w_ref[...], g, m_ref[...], v_ref[...])
            g_ref[...] = g
            d_ref[...] = delta
            nm_ref[...] = nm
            nv_ref[...] = nv

        _for_layer(p_refs, update)

    blk = pl.BlockSpec((tr, c), lambda l, i: (l * (r // tr) + i, 0))
    return _call(
        body, name=name, grid=(nl, r // tr),
        out_shape=[jax.ShapeDtypeStruct((nl * r, c), F32)] * 4,
        in_specs=_layer_parts_specs(parts, tr) + [blk, blk, blk],
        out_specs=[blk] * 4, args=(*parts, w, m, v))


def _reduce_layers(parts, tr, name):
    nl = len(parts)
    _, r, c = parts[0].shape

    def body(*refs):
        g_ref = refs[nl]

        def reduce(p_ref):
            g_ref[...] = _sum_parts(p_ref)

        _for_layer(refs[:nl], reduce)

    return _call(
        body, name=name, grid=(nl, r // tr), out_shape=[jax.ShapeDtypeStruct((nl, r, c), F32)],
        in_specs=_layer_parts_specs(parts, tr),
        out_specs=[pl.BlockSpec((None, tr, c), lambda l, i: (l, i, 0))], args=tuple(parts))[0]


def _cmul(ar, ai, br, bi):
    return ar * br - ai * bi, ar * bi + ai * br


def _lam_bar(lr, li, log_dt):
    dt = jnp.exp(log_dt)
    mag = jnp.exp(lr * dt)
    return dt, mag * jnp.cos(li * dt), mag * jnp.sin(li * dt)


def _layer_spec(arr):
    tail = arr.shape[1:]
    return pl.BlockSpec((None,) + tail, lambda l: (l,) + (0,) * len(tail))


def _ssm_prep(lam_re, lam_im, log_dt, b_re, b_im, name):
    def body(lr_ref, li_ref, dt_ref, br_ref, bi_ref, pr_ref, pi_ref, or_ref, oi_ref):
        lr, li = lr_ref[...], li_ref[...]
        _, ar, ai = _lam_bar(lr, li, dt_ref[...])
        den = lr * lr + li * li
        fr, fi = _cmul(ar - 1.0, ai, lr / den, -li / den)
        obr, obi = _cmul(fr[:, None, :], fi[:, None, :], br_ref[...], bi_ref[...])
        or_ref[...] = obr
        oi_ref[...] = obi
        qr, qi = ar, ai
        for k in range(SUB):
            pr_ref[k] = qr
            pi_ref[k] = qi
            qr, qi = _cmul(qr, qi, ar, ai)

    nl, g, p = lam_re.shape
    h = b_re.shape[2]
    outs = [jax.ShapeDtypeStruct((nl, SUB, g, p), F32)] * 2 + [jax.ShapeDtypeStruct((nl, g, h, p), F32)] * 2
    args = (lam_re, lam_im, log_dt, b_re, b_im)
    return _call(body, name=name, grid=(nl,), out_shape=outs, in_specs=[_layer_spec(a) for a in args],
                 out_specs=[_layer_spec(o) for o in outs], args=args)


def _ssm_param_bwd(lam_re, lam_im, log_dt, b_re, b_im, dlr, dli, dbr, dbi, name):
    def body(lr_ref, li_ref, dt_ref, br_ref, bi_ref, gar_ref, gai_ref, gbr_ref, gbi_ref,
             olr_ref, oli_ref, odt_ref, obr_ref, obi_ref):
        lr, li = lr_ref[...], li_ref[...]
        dt, ar, ai = _lam_bar(lr, li, dt_ref[...])
        den = lr * lr + li * li
        ir, ii = lr / den, -li / den
        fr, fi = _cmul(ar - 1.0, ai, ir, ii)
        br, bi = br_ref[...], bi_ref[...]
        gbr, gbi = gbr_ref[...], gbi_ref[...]
        obr, obi = _cmul(fr[:, None, :], -fi[:, None, :], gbr, gbi)
        obr_ref[...] = obr
        obi_ref[...] = obi
        pr, pi = _cmul(br, -bi, gbr, gbi)
        gfr, gfi = jnp.sum(pr, axis=1), jnp.sum(pi, axis=1)
        t_r, t_i = _cmul(gfr, gfi, ir, -ii)
        gar, gai = gar_ref[...] + t_r, gai_ref[...] + t_i
        qr, qi = _cmul(fr, fi, ir, ii)
        l1r, l1i = _cmul(gfr, gfi, qr, -qi)
        l2r, l2i = _cmul(gar, gai, dt * ar, -dt * ai)
        olr_ref[...] = l2r - l1r
        oli_ref[...] = l2i - l1i
        mr, mi = _cmul(lr, li, ar, ai)
        dr, _ = _cmul(gar, gai, mr, -mi)
        odt_ref[...] = jnp.sum(dr, axis=1, keepdims=True) * dt

    nl, g, p = lam_re.shape
    h = b_re.shape[2]
    outs = ([jax.ShapeDtypeStruct((nl, g, p), F32)] * 2 + [jax.ShapeDtypeStruct((nl, g, 1), F32)]
            + [jax.ShapeDtypeStruct((nl, g, h, p), F32)] * 2)
    args = (lam_re, lam_im, log_dt, b_re, b_im, dlr, dli, dbr, dbi)
    return _call(body, name=name, grid=(nl,), out_shape=outs, in_specs=[_layer_spec(a) for a in args],
                 out_specs=[_layer_spec(o) for o in outs], args=args)


def _gelu(x):
    k = math.sqrt(2.0 / math.pi)
    return 0.5 * x * (1.0 + jnp.tanh(k * (x + 0.044715 * x * x * x)))


def _gelu_grad(x):
    k = math.sqrt(2.0 / math.pi)
    th = jnp.tanh(k * (x + 0.044715 * x * x * x))
    return 0.5 * (1.0 + th) + 0.5 * x * (1.0 - th * th) * k * (1.0 + 3 * 0.044715 * x * x)


def _pool_inv_count(chunk):
    t = lax.broadcasted_iota(jnp.int32, (T, GW), 0) + chunk * T + 1
    lane = lax.broadcasted_iota(jnp.int32, (T, GW), 1)
    win = jnp.where(lane < 64, 2, jnp.where(lane < 128, 4, jnp.where(lane < 192, 8, 16)))
    return 1.0 / jnp.minimum(t, win).astype(F32)


def _window_sums(v, bufs, base, step):
    sums = []
    for k, buf in enumerate(bufs):
        buf[base:base + T, :] = v
        v = v + buf[pl.ds(base + step * (1 << k), T), :]
        sums.append(v)
    lane = lax.broadcasted_iota(jnp.int32, (T, GW), 1)
    return jnp.where(lane < 64, sums[0], jnp.where(lane < 128, sums[1], jnp.where(lane < 192, sums[2], sums[3])))


def _phase_copies(buf, rot):
    n = HALO + T - SUB
    for p in range(1, SUB):
        rot[p - 1, 0:n, :] = buf[pl.ds(p, n), :]


def _tap(buf, rot, off):
    q, p = divmod(off, SUB)
    return buf[off:off + T, :] if p == 0 else rot[p - 1, SUB * q:SUB * q + T, :]


SCAN_STEPS = (1, 2, 4)


def _scan(x_scr, steps_ref, chain_ref, carry, reverse):
    strips = range(XH // 128)
    cr = [carry[:, j * 128:(j + 1) * 128] for j in strips]
    ci = [carry[:, XH + j * 128:XH + (j + 1) * 128] for j in strips]
    nb = T // SUB
    edge = 0 if reverse else SUB - 1
    for b in (reversed(range(nb)) if reverse else range(nb)):
        rows = slice(SUB * b, SUB * b + SUB)
        for j in strips:
            lo, hi = j * 128, XH + j * 128
            rb, ib = x_scr[rows, lo:lo + 128], x_scr[rows, hi:hi + 128]
            for k, d in enumerate(SCAN_STEPS):
                ar, ai = steps_ref[k, :, lo:lo + 128], steps_ref[k, :, hi:hi + 128]
                shift = SUB - d if reverse else d
                sre, sim = pltpu.roll(rb, shift, 0), pltpu.roll(ib, shift, 0)
                rb, ib = rb + ar * sre - ai * sim, ib + ar * sim + ai * sre
            pr, pi = chain_ref[:, lo:lo + 128], chain_ref[:, hi:hi + 128]
            crb, cib = jnp.broadcast_to(cr[j], (SUB, 128)), jnp.broadcast_to(ci[j], (SUB, 128))
            rb, ib = rb + pr * crb - pi * cib, ib + pr * cib + pi * crb
            x_scr[rows, lo:lo + 128] = rb
            x_scr[rows, hi:hi + 128] = ib
            cr[j], ci[j] = rb[edge:edge + 1], ib[edge:edge + 1]


def _scan_tables(pw):
    conj = jnp.concatenate([jnp.ones((XH,), F32), -jnp.ones((XH,), F32)])
    steps = jnp.stack([pw[:, d - 1] for d in SCAN_STEPS], axis=1)[:, :, None, :]
    r = jnp.arange(SUB).reshape(1, 1, SUB, 1)
    d = jnp.array(SCAN_STEPS).reshape(1, len(SCAN_STEPS), 1, 1)
    fwd_steps = jnp.where(r >= d, steps, 0.0)
    rev_steps = jnp.where(r < SUB - d, steps, 0.0) * conj
    return fwd_steps, pw, rev_steps, pw[:, ::-1] * conj


MIXER_SHARED = ('scw', 'pblk', 'pscale', 'dww', 'dwb', 'clg', 'clb', 'bblk', 'cblk', 'ssd', 'wglu', 'bglu')
MIXER_FWD_PARAMS = MIXER_SHARED + ('fwd_steps', 'fwd_chain')
MIXER_BWD_PARAMS = MIXER_SHARED + ('rev_steps', 'rev_chain')


def _mixer_fwd(x, mod, l, w_in_t, b_in, mp, name, comm=None):
    s = x.shape[0]
    nch = s // T
    params = [mp[k] for k in MIXER_FWD_PARAMS]

    def body(x_ref, sc_ref, sh_ref, w_ref, b_ref, scw_ref, pblk_ref, ps_ref, dww_ref, dwb_ref, clg_ref, clb_ref,
             bblk_ref, cblk_ref, ssd_ref, wglu_ref, bglu_ref, steps_ref, chain_ref, y_ref, xs_ref, aux_ref, z_ref,
             qbuf, hbuf, pb1, pb2, pb4, pb8, rot, carry):
        i = pl.program_id(0)

        @pl.when(i == 0)
        def _():
            for ref in (qbuf, hbuf, pb1, pb2, pb4, pb8):
                ref[0:HALO, :] = jnp.zeros((HALO, GW), F32)
            carry[...] = jnp.zeros_like(carry)

        h = x_ref[...] * (1.0 + sc_ref[...]) + sh_ref[...]
        z_ref[...] = _dot_nt(h, w_ref[...]) + b_ref[...]
        zh, zb, zc = z_ref[:, 0:GW], z_ref[:, GW:2 * GW], z_ref[:, 2 * GW:3 * GW]
        zp, zcv, zcg, zs = (z_ref[:, 3 * GW:4 * GW], z_ref[:, 4 * GW:5 * GW], z_ref[:, 5 * GW:6 * GW],
                            z_ref[:, 6 * GW:7 * GW])
        qbuf[HALO:HALO + T, :] = zc * zh
        conv = None
        for k in range(SC_TAPS):
            v = scw_ref[k:k + 1, :] * qbuf[pl.ds(HALO - (SC_TAPS - 1) + k, T), :]
            conv = v if conv is None else conv + v
        aux_ref[:, 0:GW] = conv
        y_ref[:, 0:GW] = (zb * conv).astype(y_ref.dtype)
        r = _window_sums(zp, (pb1, pb2, pb4, pb8), HALO, -1) * _pool_inv_count(i) - zp
        aux_ref[:, GW:2 * GW] = r
        y_ref[:, GW:2 * GW] = (_dot(r, pblk_ref[...]) * ps_ref[...]).astype(y_ref.dtype)
        hbuf[HALO:HALO + T, :] = zcv * _sigmoid(zcg)
        _phase_copies(hbuf, rot)
        hc = None
        for k in range(CF_TAPS):
            v = dww_ref[k:k + 1, :] * _tap(hbuf, rot, HALO - (CF_TAPS - 1) + k)
            hc = v if hc is None else hc + v
        hc = hc + dwb_ref[...]
        aux_ref[:, 2 * GW:3 * GW] = hc
        mu = jnp.mean(hc, axis=-1, keepdims=True)
        xc = hc - mu
        var = jnp.mean(xc * xc, axis=-1, keepdims=True)
        ln = xc * lax.rsqrt(var + LN_EPS) * clg_ref[...] + clb_ref[...]
        y_ref[:, 2 * GW:3 * GW] = (ln * _sigmoid(ln)).astype(y_ref.dtype)
        xs_ref[...] = _dot(zs, bblk_ref[...])
        _scan(xs_ref, steps_ref, chain_ref, carry[...], reverse=False)
        carry[...] = xs_ref[T - 1:T, :]
        yy = _dot(xs_ref[...], cblk_ref[...]) + ssd_ref[...] * zs
        aux_ref[:, 3 * GW:4 * GW] = yy
        yg = _gelu(yy)
        v = _dot(yg, wglu_ref[...]) + bglu_ref[...]
        y_ref[:, 3 * GW:4 * GW] = (yg * _sigmoid(v)).astype(y_ref.dtype)
        for ref in (qbuf, hbuf, pb1, pb2, pb4, pb8):
            ref[0:HALO, :] = ref[T:T + HALO, :]

    fwd = lambda i: (i, 0)
    return _call(
        body, name=name, grid=(nch,),
        out_shape=[jax.ShapeDtypeStruct((s, D), _MXU), jax.ShapeDtypeStruct((s, XW), F32),
                   jax.ShapeDtypeStruct((s, 4 * GW), F32), jax.ShapeDtypeStruct((s, INW), F32)],
        in_specs=[pl.BlockSpec((T, D), fwd), _sel(mod, l, SC1), _sel(mod, l, SH1), _resident(w_in_t.shape),
                  _sel(b_in, l)] + [_sel(p, l) for p in params],
        out_specs=[pl.BlockSpec((T, D), fwd), pl.BlockSpec((T, XW), fwd), pl.BlockSpec((T, 4 * GW), fwd),
                   pl.BlockSpec((T, INW), fwd)],
        scratch=[pltpu.VMEM((HALO + T, GW), F32)] * 6
        + [pltpu.VMEM((SUB - 1, HALO + T, GW), F32), pltpu.VMEM((1, XW), F32)],
        args=(x, mod, mod, w_in_t, b_in, *params), comm=comm)


def _mixer_bwd(z, dres, w_o, xs, aux, mp, l, name):
    s = z.shape[0]
    nch = s // T
    params = [mp[k] for k in MIXER_BWD_PARAMS]

    def body(z_ref, zprev_ref, dres_ref, wo_ref, xs_ref, xprev_ref, aux_ref, scw_ref, pblk_ref, ps_ref, dww_ref,
             dwb_ref, clg_ref, clb_ref, bblk_ref, cblk_ref, ssd_ref, wglu_ref, bglu_ref, steps_ref, chain_ref,
             dz_ref, vec_ref, small_ref, dpblk_ref, dwglu_ref, dbblk_ref, dcblk_ref, dlam_ref,
             qbuf, hbuf, dcbuf, dhbuf, eb1, eb2, eb4, eb8, rot_h, rot_d, acarry, a_scr, dy_ref):
        i = pl.program_id(0)
        chunk = nch - 1 - i
        dy_ref[...] = _dot_nt(dres_ref[...], wo_ref[...])

        @pl.when(i == 0)
        def _():
            for ref in (vec_ref, small_ref, dpblk_ref, dwglu_ref, dbblk_ref, dcblk_ref, dlam_ref, acarry):
                ref[...] = jnp.zeros_like(ref)
            for ref in (dcbuf, dhbuf, eb1, eb2, eb4, eb8):
                ref[T:T + HALO, :] = jnp.zeros((HALO, GW), F32)

        zh, zb, zc = z_ref[:, 0:GW], z_ref[:, GW:2 * GW], z_ref[:, 2 * GW:3 * GW]
        zcv, zcg, zs = z_ref[:, 4 * GW:5 * GW], z_ref[:, 5 * GW:6 * GW], z_ref[:, 6 * GW:7 * GW]
        conv, r, hc, yy = (aux_ref[:, 0:GW], aux_ref[:, GW:2 * GW], aux_ref[:, 2 * GW:3 * GW],
                           aux_ref[:, 3 * GW:4 * GW])
        has_prev = (chunk > 0).astype(F32)
        qbuf[0:HALO, :] = zprev_ref[:, 2 * GW:3 * GW] * zprev_ref[:, 0:GW] * has_prev
        hbuf[0:HALO, :] = zprev_ref[:, 4 * GW:5 * GW] * _sigmoid(zprev_ref[:, 5 * GW:6 * GW]) * has_prev

        dya = dy_ref[:, 0:GW]
        qbuf[HALO:HALO + T, :] = zc * zh
        dconv = dya * zb
        dcbuf[0:T, :] = dconv
        dq = None
        for k in range(SC_TAPS):
            small_ref[k:k + 1, :] += _colsum(dconv * qbuf[pl.ds(HALO - (SC_TAPS - 1) + k, T), :])
            v = scw_ref[k:k + 1, :] * dcbuf[pl.ds(SC_TAPS - 1 - k, T), :]
            dq = v if dq is None else dq + v
        dzh, dzb, dzc = dq * zc, dya * conv, dq * zh

        dyb = dy_ref[:, GW:2 * GW]
        o = _dot(r, pblk_ref[...])
        small_ref[3:4, :] += _colsum(dyb * o)
        do = dyb * ps_ref[...]
        dpblk_ref[...] += _dot_tn(r, do)
        dr = _dot_nt(do, pblk_ref[...])
        dzp = _window_sums(dr * _pool_inv_count(chunk), (eb1, eb2, eb4, eb8), 0, 1) - dr

        dyc = dy_ref[:, 2 * GW:3 * GW]
        sgc = _sigmoid(zcg)
        hbuf[HALO:HALO + T, :] = zcv * sgc
        mu = jnp.mean(hc, axis=-1, keepdims=True)
        xc = hc - mu
        var = jnp.mean(xc * xc, axis=-1, keepdims=True)
        rstd = lax.rsqrt(var + LN_EPS)
        nrm = xc * rstd
        ln = nrm * clg_ref[...] + clb_ref[...]
        sl = _sigmoid(ln)
        dln = dyc * (sl * (1.0 + ln * (1.0 - sl)))
        small_ref[5:6, :] += _colsum(dln * nrm)
        small_ref[6:7, :] += _colsum(dln)
        dn = dln * clg_ref[...]
        dhc = rstd * (dn - jnp.mean(dn, axis=-1, keepdims=True) - nrm * jnp.mean(dn * nrm, axis=-1, keepdims=True))
        small_ref[4:5, :] += _colsum(dhc)
        dhbuf[0:T, :] = dhc
        _phase_copies(hbuf, rot_h)
        _phase_copies(dhbuf, rot_d)
        dhg = None
        for k in range(CF_TAPS):
            small_ref[16 + k:17 + k, :] += _colsum(dhc * _tap(hbuf, rot_h, HALO - (CF_TAPS - 1) + k))
            v = dww_ref[k:k + 1, :] * _tap(dhbuf, rot_d, CF_TAPS - 1 - k)
            dhg = v if dhg is None else dhg + v
        dzcv = dhg * sgc
        dzcg = dhg * zcv * sgc * (1.0 - sgc)

        dyd = dy_ref[:, 3 * GW:4 * GW]
        yg = _gelu(yy)
        v = _dot(yg, wglu_ref[...]) + bglu_ref[...]
        sg = _sigmoid(v)
        dv = dyd * yg * sg * (1.0 - sg)
        small_ref[8:9, :] += _colsum(dv)
        dwglu_ref[...] += _dot_tn(yg, dv)
        g = (dyd * sg + _dot_nt(dv, wglu_ref[...])) * _gelu_grad(yy)
        small_ref[7:8, :] += _colsum(g * zs)
        dcblk_ref[...] += _dot_tn(xs_ref[...], g)
        a_scr[...] = _dot_nt(g, cblk_ref[...])
        _scan(a_scr, steps_ref, chain_ref, acarry[...], reverse=True)
        acarry[...] = a_scr[0:1, :]
        dzs = _dot_nt(a_scr[...], bblk_ref[...]) + ssd_ref[...] * g
        dbblk_ref[...] += _dot_tn(zs, a_scr[...])
        row = lax.broadcasted_iota(jnp.int32, (T, 128), 0)
        for j in range(XH // 128):
            lo, hi = j * 128, XH + j * 128
            first_r = xprev_ref[SUB - 1:SUB, lo:lo + 128] * has_prev
            first_i = xprev_ref[SUB - 1:SUB, hi:hi + 128] * has_prev
            pr = jnp.where(row == 0, first_r, pltpu.roll(xs_ref[:, lo:lo + 128], 1, 0))
            pi = jnp.where(row == 0, first_i, pltpu.roll(xs_ref[:, hi:hi + 128], 1, 0))
            ar, ai = a_scr[:, lo:lo + 128], a_scr[:, hi:hi + 128]
            dlam_ref[0:1, lo:lo + 128] += _colsum(ar * pr + ai * pi)
            dlam_ref[0:1, hi:hi + 128] += _colsum(ai * pr - ar * pi)

        parts = (dzh, dzb, dzc, dzp, dzcv, dzcg, dzs)
        for k, part in enumerate(parts):
            dz_ref[:, k * GW:(k + 1) * GW] = part.astype(dz_ref.dtype)
            vec_ref[0:1, k * GW:(k + 1) * GW] += _colsum(part)
        for ref in (dcbuf, dhbuf, eb1, eb2, eb4, eb8):
            ref[T:T + HALO, :] = ref[0:HALO, :]

    rev = lambda i: (nch - 1 - i, 0)

    def before(rows):
        return lambda i: (jnp.maximum((nch - 1 - i) * (T // rows) - 1, 0), 0)

    outs = [((s, INW), _MXU), ((8, INW), F32), ((48, GW), F32), ((GW, GW), F32), ((GW, GW), F32),
            ((GW, XW), F32), ((XW, GW), F32), ((8, XW), F32)]
    return _call(
        body, name=name, grid=(nch,),
        out_shape=[jax.ShapeDtypeStruct(sh, dt) for sh, dt in outs],
        in_specs=[pl.BlockSpec((T, INW), rev), pl.BlockSpec((HALO, INW), before(HALO)), pl.BlockSpec((T, D), rev),
                  _resident(w_o.shape), pl.BlockSpec((T, XW), rev), pl.BlockSpec((SUB, XW), before(SUB)),
                  pl.BlockSpec((T, 4 * GW), rev)]
        + [_sel(p, l) for p in params],
        out_specs=[pl.BlockSpec((T, INW), rev)] + [_full(sh) for sh, _ in outs[1:]],
        scratch=[pltpu.VMEM((HALO + T, GW), F32)] * 8
        + [pltpu.VMEM((SUB - 1, HALO + T, GW), F32)] * 2
        + [pltpu.VMEM((1, XW), F32), pltpu.VMEM((T, XW), F32), pltpu.VMEM((T, D), F32)],
        args=(z, z, dres, w_o, xs, xs, aux, *params))


def _blockdiag(m):
    *lead, g, h, p = m.shape
    eye = jnp.eye(g, dtype=m.dtype)
    return (m[..., :, :, None, :] * eye[:, None, :, None]).reshape(*lead, g * h, g * p)


def _blockdiag_extract(f, h, p):
    *lead, r, _ = f.shape
    g = r // h
    eye = jnp.eye(g, dtype=f.dtype)
    return jnp.sum(f.reshape(*lead, g, h, g, p) * eye[:, None, :, None], axis=-2)


def _t(a):
    return jnp.swapaxes(a, -1, -2)


PACK_ROWS = 512


def _pack(arrs):
    flat = jnp.concatenate([a.reshape(-1).astype(F32) for a in arrs])
    quantum = PACK_ROWS * 128
    padded = -(-flat.shape[0] // quantum) * quantum
    return jnp.pad(flat, (0, padded - flat.shape[0])).reshape(-1, 128)


def _unpack(buf, shapes, lead=()):
    flat = buf.reshape(lead + (-1,))
    out, off = [], 0
    for shp in shapes:
        n = math.prod(shp)
        out.append(flat[..., off:off + n].reshape(lead + tuple(shp)))
        off += n
    return out


def _cols_to_full(g):
    n, l, r, c = g.shape
    return g.transpose(1, 2, 0, 3).reshape(l, r, n * c)


def _rows_to_full(g):
    n, l, r, c = g.shape
    return g.transpose(1, 0, 2, 3).reshape(l, n * r, c)


REPLICATED = ['b_ada', 'b_in', 'pool_w', 'pool_scale', 'cf_dw_b', 'cf_ln_g', 'cf_ln_b', 'ssm_lam_re', 'ssm_lam_im',
              'ssm_log_dt', 'ssm_b_re', 'ssm_b_im', 'ssm_c_re', 'ssm_c_im', 'ssm_d', 'ssm_b_glu', 'ln1_g', 'ln1_b',
              'ln2_g', 'ln2_b']
SMALL_SHARDED = ['sc_w', 'cf_dw_w', 'ssm_w_glu']
COL_SHARDED = ['w_in', 'w_gate', 'w_up']
ROW_SHARDED = ['w_o', 'w_down']
BIG = COL_SHARDED + ROW_SHARDED
WEIGHTS = ['w_ada', 'b_ada', 'w_in', 'b_in', 'sc_w', 'pool_w', 'pool_scale', 'cf_dw_w', 'cf_dw_b', 'cf_ln_g',
           'cf_ln_b', 'ssm_lam_re', 'ssm_lam_im', 'ssm_log_dt', 'ssm_b_re', 'ssm_b_im', 'ssm_c_re', 'ssm_c_im',
           'ssm_d', 'ssm_w_glu', 'ssm_b_glu', 'w_o', 'ln1_g', 'ln1_b', 'w_gate', 'w_up', 'w_down', 'ln2_g', 'ln2_b']
SH1, SC1, G1, SH2, SC2, G2 = range(6)


def _train_step(p):
    xi, yi, ci = _me()
    me = _flat((xi, yi, ci))
    s = p['x'].shape[1]
    xs = p['x'].reshape(s, D)
    target = p['loss_target'].reshape(s, D)
    vec3 = lambda a: a.reshape(DEPTH, 1, -1)

    def shard(n, l):
        w = p[n][l]
        return (w.T if n in COL_SHARDED else w).astype(_MXU)

    def whole(g):
        return g.reshape(-1, g.shape[-1])

    small_shapes = [p[n].shape for n in SMALL_SHARDED] + [p['c'].shape]
    first = ('w_in', 'w_o')
    gathered = _allgather([_pack([p[n] for n in SMALL_SHARDED] + [p['c']])] + [shard(n, 0) for n in first],
                          "gather_first")
    scw_g, dww_g, wglu_g, c_g = _unpack(gathered[0], small_shapes, lead=(NDEV,))
    full = {'sc_w': _cols_to_full(scw_g), 'cf_dw_w': _cols_to_full(dww_g), 'ssm_w_glu': _rows_to_full(wglu_g)}
    wts = [dict(zip(first, (whole(g) for g in gathered[1:])))] + [dict() for _ in range(DEPTH - 1)]
    c_all = c_g.reshape(NDEV, D)

    ncol = p['w_ada'].shape[2]
    b_cols = lax.dynamic_slice_in_dim(p['b_ada'], me * ncol, ncol, axis=1).reshape(DEPTH, 1, ncol)
    cond, mod_cols = _ada_fwd(c_all, p['w_ada'], b_cols, "ada_fwd")
    (mod_x,) = _alltoall([mod_cols.transpose(1, 0, 2)], "mod_exchange")
    mod = mod_x.transpose(1, 0, 2).reshape(DEPTH, 6, 1, D)

    ssm_in = (p['ssm_lam_re'], p['ssm_lam_im'], p['ssm_log_dt'].reshape(DEPTH, NGRP, 1),
              _t(p['ssm_b_re']), _t(p['ssm_b_im']))
    pwr, pwi, bbr, bbi = _ssm_prep(*ssm_in, "ssm_prep")
    pw = jnp.concatenate([pwr.reshape(DEPTH, SUB, XH), pwi.reshape(DEPTH, SUB, XH)], axis=2)
    fwd_steps, fwd_chain, rev_steps, rev_chain = _scan_tables(pw)
    mp = dict(
        fwd_steps=fwd_steps, fwd_chain=fwd_chain, rev_steps=rev_steps, rev_chain=rev_chain,
        scw=full['sc_w'], pblk=_blockdiag(p['pool_w']).astype(_MXU), pscale=vec3(p['pool_scale']),
        dww=full['cf_dw_w'], dwb=vec3(p['cf_dw_b']), clg=vec3(p['cf_ln_g']), clb=vec3(p['cf_ln_b']),
        bblk=jnp.concatenate([_blockdiag(bbr), _blockdiag(bbi)], axis=2).astype(_MXU),
        cblk=jnp.concatenate([_blockdiag(_t(p['ssm_c_re'])), -_blockdiag(_t(p['ssm_c_im']))], axis=1).astype(_MXU),
        ssd=vec3(p['ssm_d']), wglu=full['ssm_w_glu'].astype(_MXU),
        bglu=vec3(p['ssm_b_glu']))
    b_in, ln1_g, ln1_b, ln2_g, ln2_b = (vec3(p[n]) for n in ('b_in', 'ln1_g', 'ln1_b', 'ln2_g', 'ln2_b'))

    saved = []
    for l in range(DEPTH):
        w = wts[l]

        def gather(at, *names):
            return None if at >= DEPTH or not names else (GATHER, [shard(n, at) for n in names])

        def take(res, n_own, at, *names):
            if at < DEPTH:
                for n, g in zip(names, res[n_own:]):
                    wts[at][n] = whole(g)
            return res[:n_own]

        ycat, states, aux, z = take(_mixer_fwd(xs, mod, l, w['w_in'], b_in, mp, "mixer_fwd",
                                               comm=gather(l, 'w_gate', 'w_up')), 4, l, 'w_gate', 'w_up')
        early = ('w_down',) if l == 0 else ()
        y, u1, x1 = take(_proj_residual_ln(ycat, w['w_o'], l, xs, mod, G1, ln1_g, ln1_b, "o_proj_ln",
                                           comm=gather(l, *early)), 3, l, *early)
        gate, up, act = take(_gate_up(x1, mod, l, w['w_gate'], w['w_up'], "gate_up",
                                      comm=gather(l + 1, 'w_in', 'w_o')), 3, l + 1, 'w_in', 'w_o')
        f, u2, x2 = take(_proj_residual_ln(act, w['w_down'], l, x1, mod, G2, ln2_g, ln2_b, "down_proj_ln",
                                           comm=gather(l + 1, 'w_down')), 3, l + 1, 'w_down')
        saved.append(dict(x=xs, z=z, ycat=ycat, states=states, aux=aux, y=y, u1=u1, x1=x1, gate=gate, up=up,
                          act=act, f=f, u2=u2))
        xs = x2

    recv = {n: [None] * DEPTH for n in BIG}
    small_g = {n: [None] * DEPTH for n in ('ln1_g', 'ln1_b', 'ln2_g', 'ln2_b')}
    dmod = [None] * DEPTH
    mixer_out = [None] * DEPTH
    pending = None
    top = DEPTH - 1
    du2, df, acc_top = _loss_ln_bwd(xs, target, saved[top]['u2'], saved[top]['f'], ln2_g, top, mod, G2,
                                    "loss_ln2_bwd")
    small_g['ln2_g'][top], small_g['ln2_b'][top], dg2 = acc_top[0], acc_top[1], acc_top[2]
    loss = lax.psum(acc_top[3, 0] * (0.5 / D), AXES)
    for l in reversed(range(DEPTH)):
        sv, w = saved[l], wts[l]
        (g_down,) = _weight_grad(sv['act'], df, "dw_down")
        lp, late_names, late = pending or (None, [], [])
        dgate, dup, du1, dyb, acc_a, recv['w_down'][l], *landed = _input_grad_ln_bwd(
            None, [w['w_gate'], w['w_up']], du2, (mod, l, SC2), sv['u1'], sv['y'], (ln1_g, l), (ln1_b, l),
            (mod, l, G1), TM_WIDE, "mlp_bwd", swiglu=(df, w['w_down'], sv['gate'], sv['up']),
            comm=(EXCHANGE, [g_down] + late))
        for n, r in zip(late_names, landed):
            recv[n][lp] = r
        (g_gate,) = _weight_grad(dgate, sv['x1'], "dw_gate", mod=(mod, l, SC2, SH2))
        (g_up,) = _weight_grad(dup, sv['x1'], "dw_up", mod=(mod, l, SC2, SH2))
        (g_o,) = _weight_grad(sv['ycat'], dyb, "dw_o")
        dz, *mixer_out[l] = _mixer_bwd(sv['z'], dyb, w['w_o'], sv['states'], sv['aux'], mp, l, "mixer_bwd")
        if l > 0:
            du2, df, acc_b, recv['w_gate'][l] = _input_grad_ln_bwd(
                [dz], [w['w_in']], du1, (mod, l, SC1), saved[l - 1]['u2'], saved[l - 1]['f'], (ln2_g, l - 1),
                (ln2_b, l - 1), (mod, l - 1, G2), TM_WIDE, "dh1_ln2_bwd", comm=(EXCHANGE, [g_gate]))
            (g_in,) = _weight_grad(dz, sv['x'], "dw_in", mod=(mod, l, SC1, SH1))
            pending = (l, ['w_up', 'w_o', 'w_in'], [g_up, g_o, g_in])
        else:
            dx, acc_b, recv['w_gate'][l] = _input_grad([dz], [w['w_in']], l, du1, sv['x'], mod, SC1, TM, "dh1",
                                                       comm=(EXCHANGE, [g_gate]))
            g_in, recv['w_up'][l] = _weight_grad(dz, sv['x'], "dw_in", mod=(mod, l, SC1, SH1),
                                                 comm=(EXCHANGE, [g_up]))
            pending = (l, ['w_o', 'w_in'], [g_o, g_in])
        dmod[l] = jnp.concatenate([acc_b[0], acc_b[1], acc_a[4], acc_a[0], acc_a[1], dg2])
        small_g['ln1_g'][l], small_g['ln1_b'][l] = acc_a[2], acc_a[3]
        if l > 0:
            small_g['ln2_g'][l - 1], small_g['ln2_b'][l - 1], dg2 = acc_b[2], acc_b[3], acc_b[4]
    lp, late_names, late = pending
    for n, r in zip(late_names, _alltoall(late, "exchange_last")):
        recv[n][lp] = r

    partial = {n: jnp.stack(v) for n, v in small_g.items()}
    partial['b_ada'] = jnp.stack(dmod)
    vec, small, dpblk, dwglu, dbblk, dcblk, dlam = (jnp.stack(t) for t in zip(*mixer_out))
    partial.update(
        b_in=vec[:, 0], sc_w=small[:, 0:SC_TAPS], pool_scale=small[:, 3], cf_dw_b=small[:, 4], cf_ln_g=small[:, 5],
        cf_ln_b=small[:, 6], ssm_d=small[:, 7], ssm_b_glu=small[:, 8], cf_dw_w=small[:, 16:16 + CF_TAPS],
        pool_w=_blockdiag_extract(dpblk, 64, 64), ssm_w_glu=dwglu,
        ssm_c_re=_t(_blockdiag_extract(dcblk[:, :XH], NSTATE, NCH)),
        ssm_c_im=-_t(_blockdiag_extract(dcblk[:, XH:], NSTATE, NCH)))
    dlr, dli, dlog, dbr, dbi = _ssm_param_bwd(
        *ssm_in, dlam[:, 0, :XH].reshape(DEPTH, NGRP, NSTATE), dlam[:, 0, XH:].reshape(DEPTH, NGRP, NSTATE),
        _blockdiag_extract(dbblk[:, :, :XH], NCH, NSTATE), _blockdiag_extract(dbblk[:, :, XH:], NCH, NSTATE),
        "ssm_param_bwd")
    partial.update(ssm_lam_re=dlr, ssm_lam_im=dli, ssm_log_dt=dlog.reshape(DEPTH, NGRP), ssm_b_re=_t(dbr),
                   ssm_b_im=_t(dbi))

    out = {'loss': loss, 'grad_x': dx.reshape(1, s, D)}

    def emit(n, g, delta, nm, nv):
        shp = p[n].shape
        out['grad_' + n], out['delta_' + n] = g.reshape(shp), delta.reshape(shp)
        out['new_m_' + n], out['new_v_' + n] = nm.reshape(shp), nv.reshape(shp)

    names_a = REPLICATED + SMALL_SHARDED
    shapes_a = [partial[n].shape for n in names_a]
    (parts_a,) = _allgather([_pack([partial[n] for n in names_a])], "gather_small_grads")
    parts = dict(zip(names_a, _unpack(parts_a, shapes_a, lead=(NDEV,))))
    flat2 = lambda a: a.reshape(-1, a.shape[-1])

    def update_small(n, prt):
        prt = prt.reshape(NDEV, -1, prt.shape[-1])
        rows = prt.shape[1]
        emit(n, *_adamw(prt, flat2(p[n]), flat2(p['m_' + n]), flat2(p['v_' + n]), min(rows, PACK_ROWS),
                        "adamw_" + n))

    for n in REPLICATED:
        update_small(n, parts[n])
    for n in SMALL_SHARDED:
        axis = 1 if n == 'ssm_w_glu' else 2
        width = p[n].shape[axis]
        update_small(n, lax.dynamic_slice_in_dim(parts[n], me * width, width, axis=axis + 1))

    dmod_cols = lax.dynamic_slice_in_dim(parts['b_ada'], me * ncol, ncol, axis=2).transpose(1, 0, 2)
    g_ada = _ada_bwd(cond, dmod_cols, "ada_bwd")
    emit('w_ada', *_adamw(flat2(g_ada)[None], flat2(p['w_ada']), flat2(p['m_w_ada']), flat2(p['v_w_ada']),
                          512, "adamw_w_ada"))

    for n in BIG:
        tr = recv[n][0].shape[1] // 2
        if n in ROW_SHARDED:
            emit(n, *_adamw_layers(recv[n], flat2(p[n]), flat2(p['m_' + n]), flat2(p['v_' + n]), tr, "adamw_" + n))
        else:
            g = flat2(_t(_reduce_layers(recv[n], tr, "sum_" + n)))
            emit(n, *_adamw(g[None], flat2(p[n]), flat2(p['m_' + n]), flat2(p['v_' + n]), 512, "adamw_" + n))
    return out


def kernel(x, c, w_ada, b_ada, w_in, b_in, sc_w, pool_w, pool_scale, cf_dw_w, cf_dw_b, cf_ln_g, cf_ln_b, ssm_lam_re, ssm_lam_im, ssm_log_dt, ssm_b_re, ssm_b_im, ssm_c_re, ssm_c_im, ssm_d, ssm_w_glu, ssm_b_glu, w_o, ln1_g, ln1_b, w_gate, w_up, w_down, ln2_g, ln2_b, loss_target, m_w_ada, m_b_ada, m_w_in, m_b_in, m_sc_w, m_pool_w, m_pool_scale, m_cf_dw_w, m_cf_dw_b, m_cf_ln_g, m_cf_ln_b, m_ssm_lam_re, m_ssm_lam_im, m_ssm_log_dt, m_ssm_b_re, m_ssm_b_im, m_ssm_c_re, m_ssm_c_im, m_ssm_d, m_ssm_w_glu, m_ssm_b_glu, m_w_o, m_ln1_g, m_ln1_b, m_w_gate, m_w_up, m_w_down, m_ln2_g, m_ln2_b, v_w_ada, v_b_ada, v_w_in, v_b_in, v_sc_w, v_pool_w, v_pool_scale, v_cf_dw_w, v_cf_dw_b, v_cf_ln_g, v_cf_ln_b, v_ssm_lam_re, v_ssm_lam_im, v_ssm_log_dt, v_ssm_b_re, v_ssm_b_im, v_ssm_c_re, v_ssm_c_im, v_ssm_d, v_ssm_w_glu, v_ssm_b_glu, v_w_o, v_ln1_g, v_ln1_b, v_w_gate, v_w_up, v_w_down, v_ln2_g, v_ln2_b):
    out = _train_step(dict(locals()))
    return (out['loss'], out['grad_x'], *[out[pre + n] for pre in ('grad_', 'delta_', 'new_m_', 'new_v_')
                                          for n in WEIGHTS])
```

```python
import math

import jax
import jax.numpy as jnp
from jax import lax
from jax.experimental import pallas as pl
from jax.experimental.pallas import tpu as pltpu

F32 = jnp.float32
_MXU = jnp.bfloat16

D = 1024
GW = 256
INW = 7 * GW
DFF = 2816
DEPTH = 4
NDEV = 8
SC_TAPS = 3
CF_TAPS = 31
NGRP = 16
NCH = 16
NSTATE = 64
XH = NGRP * NSTATE
XW = 2 * XH
ALPHA = (2 * DEPTH) ** 0.25
LN_EPS = 1e-5
T = 512
SUB = 8
HALO = 32
TM = 512
TM_WIDE = 256
VMEM_LIMIT = 56 * 1024 * 1024

ADAM_LR = 0.001
ADAM_B1 = 0.9
ADAM_B2 = 0.999
ADAM_EPS = 1e-08
ADAM_WD = 0.01
ADAM_STEP = 10

MESH = pl.DeviceIdType.MESH
ANY = pl.BlockSpec(memory_space=pl.ANY)
AXES = ("x", "y", "c")


def _full(shape):
    nd = len(shape)
    return pl.BlockSpec(shape, lambda *_: (0,) * nd)


def _resident(shape):
    nd = len(shape)
    return pl.BlockSpec(shape, lambda *_: (0,) * nd, pipeline_mode=pl.Buffered(1))


def _sel(arr, *idx):
    tail = arr.shape[len(idx):]
    return pl.BlockSpec((None,) * len(idx) + tail, lambda *_: idx + (0,) * len(tail))


def _dot(a, b):
    return jnp.dot(a.astype(_MXU), b.astype(_MXU), preferred_element_type=F32)


def _dot_nt(a, b):
    return lax.dot_general(a.astype(_MXU), b.astype(_MXU), (((1,), (1,)), ((), ())), preferred_element_type=F32)


def _dot_tn(a, b):
    return lax.dot_general(a.astype(_MXU), b.astype(_MXU), (((0,), (0,)), ((), ())), preferred_element_type=F32)


def _sigmoid(x):
    return 1.0 / (1.0 + jnp.exp(-x))


def _colsum(x):
    return jnp.sum(x, axis=0, keepdims=True)


def _me():
    return lax.axis_index("x"), lax.axis_index("y"), lax.axis_index("c")


def _flat(p):
    return 4 * p[0] + 2 * p[1] + p[2]


def _allgather(arrays, name):
    n = len(arrays)

    def body(*refs):
        ins, outs = refs[:n], refs[n:2 * n]
        send_sems, recv_sems, local_sems = refs[2 * n:]
        x, y, c = _me()
        me, sibling = (x, y, c), (x, y, 1 - c)
        chips = [(1 - x, y), (x, 1 - y), (1 - x, 1 - y)]

        def copy(a, k, block, to, src=None):
            dst = outs[a].at[_flat(block)]
            return pltpu.make_async_remote_copy(
                src_ref=dst if src is None else src, dst_ref=dst,
                send_sem=send_sems.at[a, k], recv_sem=recv_sems.at[a, k],
                device_id=to, device_id_type=MESH)

        started = []
        for a in range(n):
            mine = pltpu.make_async_copy(ins[a], outs[a].at[_flat(me)], local_sems.at[a])
            mine.start()
            started.append(mine)
        first = []
        for a in range(n):
            first.append(copy(a, 0, me, sibling, src=ins[a]))
            first += [copy(a, 1 + j, me, (*chip, c), src=ins[a]) for j, chip in enumerate(chips)]
        for cp in first:
            cp.start()
        passed = []
        for a in range(n):
            for j, chip in enumerate(chips):
                copy(a, 1 + j, (*chip, c), me).wait_recv()
                fwd = copy(a, 4 + j, (*chip, c), sibling)
                fwd.start()
                passed.append(fwd)
        for a in range(n):
            copy(a, 0, sibling, me).wait_recv()
            for j, chip in enumerate(chips):
                copy(a, 4 + j, (*chip, 1 - c), me).wait_recv()
        for cp in first + passed:
            cp.wait_send()
        for cp in started:
            cp.wait()

    return pl.pallas_call(
        body, name=name,
        out_shape=[jax.ShapeDtypeStruct((NDEV,) + a.shape, a.dtype) for a in arrays],
        in_specs=[ANY] * n, out_specs=[ANY] * n,
        scratch_shapes=[pltpu.SemaphoreType.DMA((n, 7)), pltpu.SemaphoreType.DMA((n, 7)),
                        pltpu.SemaphoreType.DMA((n,))],
    )(*arrays)


GATHER, EXCHANGE = "gather", "exchange"


def _direct_copies(kind, ins, outs, send_sems, recv_sems, local_sems, start):
    x, y, c = _me()
    mine = _flat((x, y, c))
    for a in range(len(ins)):
        own = ins[a] if kind == GATHER else ins[a].at[mine]
        loc = pltpu.make_async_copy(own, outs[a].at[mine], local_sems.at[a])
        if start:
            loc.start()
        for k in range(1, NDEV):
            peer = (lax.rem(x + ((k >> 2) & 1), 2), lax.rem(y + ((k >> 1) & 1), 2), lax.rem(c + (k & 1), 2))
            cp = pltpu.make_async_remote_copy(
                src_ref=ins[a] if kind == GATHER else ins[a].at[_flat(peer)], dst_ref=outs[a].at[mine],
                send_sem=send_sems.at[a, k - 1], recv_sem=recv_sems.at[a, k - 1],
                device_id=peer, device_id_type=MESH)
            if start:
                cp.start()
            else:
                cp.wait_send()
                cp.wait_recv()
        if not start:
            loc.wait()


def _comm_out_shapes(comm):
    kind, arrays = comm
    return [jax.ShapeDtypeStruct(((NDEV,) + a.shape) if kind == GATHER else a.shape, a.dtype) for a in arrays]


def _call(body, *, name, grid, in_specs, out_specs, out_shape, args, scratch=(), comm=None):
    params = pltpu.CompilerParams(dimension_semantics=("arbitrary",) * len(grid), vmem_limit_bytes=VMEM_LIMIT)
    if comm is None:
        return pl.pallas_call(body, name=name, grid=grid, in_specs=in_specs, out_specs=out_specs,
                              out_shape=out_shape, scratch_shapes=list(scratch), compiler_params=params)(*args)
    kind, arrays = comm
    n, n_in, n_out, n_scr = len(arrays), len(in_specs), len(out_specs), len(scratch)

    def wrapped(*refs):
        ins, cin = refs[:n_in], refs[n_in:n_in + n]
        outs, cout = refs[n_in + n:n_in + n + n_out], refs[n_in + n + n_out:n_in + 2 * n + n_out]
        scr = refs[n_in + 2 * n + n_out:]
        user_scr, sems = scr[:n_scr], scr[n_scr:]
        ids = [pl.program_id(k) for k in range(len(grid))]
        first, last = ids[0] == 0, ids[0] == grid[0] - 1
        for k in range(1, len(grid)):
            first, last = first & (ids[k] == 0), last & (ids[k] == grid[k] - 1)

        @pl.when(first)
        def _():
            _direct_copies(kind, cin, cout, *sems, start=True)

        body(*ins, *outs, *user_scr)

        @pl.when(last)
        def _():
            _direct_copies(kind, cin, cout, *sems, start=False)

    sems = [pltpu.SemaphoreType.DMA((n, 7)), pltpu.SemaphoreType.DMA((n, 7)), pltpu.SemaphoreType.DMA((n,))]
    res = pl.pallas_call(
        wrapped, name=name, grid=grid, in_specs=list(in_specs) + [ANY] * n, out_specs=list(out_specs) + [ANY] * n,
        out_shape=list(out_shape) + _comm_out_shapes(comm), scratch_shapes=list(scratch) + sems,
        compiler_params=params)(*args, *arrays)
    return res


def _alltoall(arrays, name):
    n = len(arrays)

    def body(*refs):
        _direct_copies(EXCHANGE, refs[:n], refs[n:2 * n], *refs[2 * n:], start=True)
        _direct_copies(EXCHANGE, refs[:n], refs[n:2 * n], *refs[2 * n:], start=False)

    return pl.pallas_call(
        body, name=name,
        out_shape=[jax.ShapeDtypeStruct(a.shape, a.dtype) for a in arrays],
        in_specs=[ANY] * n, out_specs=[ANY] * n,
        scratch_shapes=[pltpu.SemaphoreType.DMA((n, 7)), pltpu.SemaphoreType.DMA((n, 7)),
                        pltpu.SemaphoreType.DMA((n,))],
    )(*arrays)


def _gate_up(x, mod, l, w_gate_t, w_up_t, name, comm=None):
    s, k = x.shape
    n = w_gate_t.shape[0]

    def body(x_ref, sc_ref, sh_ref, wg_ref, wu_ref, g_ref, u_ref, a_ref):
        h = (x_ref[...] * (1.0 + sc_ref[...]) + sh_ref[...]).astype(_MXU)
        gate = lax.dot_general(h, wg_ref[...], (((1,), (1,)), ((), ())), preferred_element_type=F32)
        up = lax.dot_general(h, wu_ref[...], (((1,), (1,)), ((), ())), preferred_element_type=F32)
        g_ref[...] = gate.astype(g_ref.dtype)
        u_ref[...] = up.astype(u_ref.dtype)
        a_ref[...] = (gate * _sigmoid(gate) * up).astype(a_ref.dtype)

    wide = pl.BlockSpec((TM_WIDE, n), lambda i: (i, 0))
    return _call(
        body, name=name, grid=(s // TM_WIDE,),
        out_shape=[jax.ShapeDtypeStruct((s, n), _MXU)] * 3,
        in_specs=[pl.BlockSpec((TM_WIDE, k), lambda i: (i, 0)), _sel(mod, l, SC2), _sel(mod, l, SH2),
                  _resident(w_gate_t.shape), _resident(w_up_t.shape)],
        out_specs=[wide, wide, wide], args=(x, mod, mod, w_gate_t, w_up_t), comm=comm)


def _proj_residual_ln(a, w, l, xres, mod, kg, lng, lnb, name, comm=None):
    s, k = a.shape
    n = w.shape[1]

    def body(a_ref, w_ref, x_ref, g_ref, lg_ref, lb_ref, r_ref, u_ref, o_ref):
        r = jnp.dot(a_ref[...], w_ref[...], preferred_element_type=F32)
        u = ALPHA * x_ref[...] + (1.0 + g_ref[...]) * r
        mu = jnp.mean(u, axis=-1, keepdims=True)
        xc = u - mu
        var = jnp.mean(xc * xc, axis=-1, keepdims=True)
        r_ref[...] = r.astype(r_ref.dtype)
        u_ref[...] = u
        o_ref[...] = xc * lax.rsqrt(var + LN_EPS) * lg_ref[...] + lb_ref[...]

    row = pl.BlockSpec((TM, n), lambda i: (i, 0))
    return _call(
        body, name=name, grid=(s // TM,),
        out_shape=[jax.ShapeDtypeStruct((s, n), _MXU)] + [jax.ShapeDtypeStruct((s, n), F32)] * 2,
        in_specs=[pl.BlockSpec((TM, k), lambda i: (i, 0)), _resident(w.shape), row, _sel(mod, l, kg), _sel(lng, l),
                  _sel(lnb, l)],
        out_specs=[row, row, row], args=(a, w, xres, mod, lng, lnb), comm=comm)


def _loss_ln_bwd(xo, target, u, res, lng, l, mod, kg, name):
    s, n = u.shape

    def body(x_ref, t_ref, u_ref, r_ref, lg_ref, g_ref, du_ref, dr_ref, acc_ref):
        @pl.when(pl.program_id(0) == 0)
        def _():
            acc_ref[...] = jnp.zeros_like(acc_ref)

        uu = u_ref[...]
        mu = jnp.mean(uu, axis=-1, keepdims=True)
        xc = uu - mu
        var = jnp.mean(xc * xc, axis=-1, keepdims=True)
        rstd = lax.rsqrt(var + LN_EPS)
        xh = xc * rstd
        err = x_ref[...] - t_ref[...]
        acc_ref[3:4, :] += jnp.sum(err * err)
        dy = err * (1.0 / n)
        dxh = dy * lg_ref[...]
        du = rstd * (dxh - jnp.mean(dxh, axis=-1, keepdims=True) - xh * jnp.mean(dxh * xh, axis=-1, keepdims=True))
        du_ref[...] = du
        dr_ref[...] = (du * (1.0 + g_ref[...])).astype(dr_ref.dtype)
        acc_ref[0:1, :] += _colsum(dy * xh)
        acc_ref[1:2, :] += _colsum(dy)
        acc_ref[2:3, :] += _colsum(du * r_ref[...].astype(F32))

    row = pl.BlockSpec((TM, n), lambda i: (i, 0))
    return _call(
        body, name=name, grid=(s // TM,),
        out_shape=[jax.ShapeDtypeStruct((s, n), F32), jax.ShapeDtypeStruct((s, n), _MXU),
                   jax.ShapeDtypeStruct((8, n), F32)],
        in_specs=[row, row, row, row, _sel(lng, l), _sel(mod, l, kg)], out_specs=[row, row, _full((8, n))],
        args=(xo, target, u, res, lng, mod))


def _input_grad(acts, wts, l, du, xin, mod, ksc, tm, name, comm=None):
    s = acts[0].shape[0]
    n = wts[0].shape[1]
    na = len(acts)

    def body(*refs):
        a_refs, w_refs = refs[:na], refs[na:2 * na]
        du_ref, x_ref, sc_ref, dx_ref, acc_ref = refs[2 * na:]

        @pl.when(pl.program_id(0) == 0)
        def _():
            acc_ref[...] = jnp.zeros_like(acc_ref)

        dh = None
        for a, w in zip(a_refs, w_refs):
            t = jnp.dot(a[...], w[...], preferred_element_type=F32)
            dh = t if dh is None else dh + t
        dx_ref[...] = ALPHA * du_ref[...] + dh * (1.0 + sc_ref[...])
        acc_ref[0:1, :] += _colsum(dh)
        acc_ref[1:2, :] += _colsum(dh * x_ref[...])

    row = pl.BlockSpec((tm, n), lambda i: (i, 0))
    return _call(
        body, name=name, grid=(s // tm,),
        out_shape=[jax.ShapeDtypeStruct((s, n), F32), jax.ShapeDtypeStruct((8, n), F32)],
        in_specs=[pl.BlockSpec((tm, a.shape[1]), lambda i: (i, 0)) for a in acts]
        + [_resident(w.shape) for w in wts] + [row, row, _sel(mod, l, ksc)],
        out_specs=[row, _full((8, n))], args=(*acts, *wts, du, xin, mod), comm=comm)


SWIGLU_SPLIT = 2


def _input_grad_ln_bwd(acts, wts, du_in, sc, u, res, lng, lnb, g, tm, name, swiglu=None, comm=None):
    n = wts[0].shape[1]
    na = len(wts)
    ns = 0 if swiglu is None else 4
    s = (swiglu[0] if ns else acts[0]).shape[0]
    f = wts[0].shape[0]

    nlead = ns or len(acts)

    def body(*refs):
        a_refs, w_refs = refs[:nlead], refs[nlead:nlead + na]
        dui_ref, sc_ref, u_ref, r_ref, lg_ref, lb_ref, g_ref = refs[nlead + na:nlead + na + 7]
        o_refs = refs[nlead + na + 7:]
        if ns:
            dp_ref, wd_ref, gate_ref, up_ref = a_refs
            dg_ref, dup_ref = o_refs[:2]
            o_refs = o_refs[2:]
        du_ref, dr_ref, acc_ref = o_refs

        @pl.when(pl.program_id(0) == 0)
        def _():
            acc_ref[...] = jnp.zeros_like(acc_ref)

        dh = None
        if ns:
            piece = f // SWIGLU_SPLIT
            for c in range(SWIGLU_SPLIT):
                lo = c * piece
                dp = lax.dot_general(dp_ref[...], wd_ref[lo:lo + piece, :], (((1,), (1,)), ((), ())),
                                     preferred_element_type=F32)
                gt = gate_ref[:, lo:lo + piece].astype(F32)
                sg = _sigmoid(gt)
                dgate = (dp * up_ref[:, lo:lo + piece].astype(F32) * (sg * (1.0 + gt * (1.0 - sg)))).astype(_MXU)
                dup = (dp * (gt * sg)).astype(_MXU)
                dg_ref[:, lo:lo + piece] = dgate
                dup_ref[:, lo:lo + piece] = dup
                t = (jnp.dot(dgate, w_refs[0][lo:lo + piece, :], preferred_element_type=F32)
                     + jnp.dot(dup, w_refs[1][lo:lo + piece, :], preferred_element_type=F32))
                dh = t if dh is None else dh + t
        else:
            for a, w in zip(a_refs, w_refs):
                t = jnp.dot(a[...], w[...], preferred_element_type=F32)
                dh = t if dh is None else dh + t
        dy = ALPHA * dui_ref[...] + dh * (1.0 + sc_ref[...])
        uu = u_ref[...]
        mu = jnp.mean(uu, axis=-1, keepdims=True)
        xc = uu - mu
        var = jnp.mean(xc * xc, axis=-1, keepdims=True)
        rstd = lax.rsqrt(var + LN_EPS)
        xh = xc * rstd
        dh_sum = _colsum(dh)
        acc_ref[0:1, :] += dh_sum
        acc_ref[1:2, :] += lg_ref[...] * _colsum(dh * xh) + lb_ref[...] * dh_sum
        dxh = dy * lg_ref[...]
        du = rstd * (dxh - jnp.mean(dxh, axis=-1, keepdims=True) - xh * jnp.mean(dxh * xh, axis=-1, keepdims=True))
        du_ref[...] = du
        dr_ref[...] = (du * (1.0 + g_ref[...])).astype(dr_ref.dtype)
        acc_ref[2:3, :] += _colsum(dy * xh)
        acc_ref[3:4, :] += _colsum(dy)
        acc_ref[4:5, :] += _colsum(du * r_ref[...].astype(F32))

    row = pl.BlockSpec((tm, n), lambda i: (i, 0))
    wide = pl.BlockSpec((tm, f), lambda i: (i, 0))
    if ns:
        dp_in, w_down, gate, up = swiglu
        lead_specs = [pl.BlockSpec((tm, dp_in.shape[1]), lambda i: (i, 0)), _resident(w_down.shape), wide, wide]
        lead_args = [dp_in, w_down, gate, up]
        lead_outs, lead_out_specs = [jax.ShapeDtypeStruct((s, f), _MXU)] * 2, [wide, wide]
    else:
        lead_specs = [pl.BlockSpec((tm, a.shape[1]), lambda i: (i, 0)) for a in acts]
        lead_args, lead_outs, lead_out_specs = list(acts), [], []
    return _call(
        body, name=name, grid=(s // tm,),
        out_shape=lead_outs + [jax.ShapeDtypeStruct((s, n), F32), jax.ShapeDtypeStruct((s, n), _MXU),
                               jax.ShapeDtypeStruct((8, n), F32)],
        in_specs=lead_specs + [_resident(w.shape) for w in wts]
        + [row, _sel(*sc), row, row, _sel(*lng), _sel(*lnb), _sel(*g)],
        out_specs=lead_out_specs + [row, row, _full((8, n))],
        args=(*lead_args, *wts, du_in, sc[0], u, res, lng[0], lnb[0], g[0]), comm=comm)


def _weight_grad(a, b, name, mod=None, comm=None):
    s, ka = a.shape
    nbc = b.shape[1]
    nm = 0 if mod is None else 2
    steps = s // TM

    def body(*refs):
        a_ref, b_ref = refs[:2]
        m_refs = refs[2:2 + nm]
        o_ref, acc = refs[2 + nm:]

        @pl.when(pl.program_id(0) == 0)
        def _():
            acc[...] = jnp.zeros_like(acc)

        bv = b_ref[...]
        if nm:
            bv = bv * (1.0 + m_refs[0][...]) + m_refs[1][...]
        acc[...] += lax.dot_general(a_ref[...], bv.astype(_MXU), (((0,), (0,)), ((), ())),
                                    preferred_element_type=F32)

        @pl.when(pl.program_id(0) == steps - 1)
        def _():
            o_ref[...] = acc[...].astype(o_ref.dtype)

    mspecs, margs = [], []
    if nm:
        marr, l, ksc, ksh = mod
        mspecs, margs = [_sel(marr, l, ksc), _sel(marr, l, ksh)], [marr, marr]
    res = _call(
        body, name=name, grid=(steps,),
        out_shape=[jax.ShapeDtypeStruct((ka, nbc), _MXU)],
        in_specs=[pl.BlockSpec((TM, ka), lambda t: (t, 0)), pl.BlockSpec((TM, nbc), lambda t: (t, 0))] + mspecs,
        out_specs=[_full((ka, nbc))], args=(a, b, *margs), scratch=[pltpu.VMEM((ka, nbc), F32)], comm=comm)
    return [res[0].reshape(NDEV, ka // NDEV, nbc)] + list(res[1:])


def _ada_fwd(c_all, w_ada, b_cols, name):
    cols = w_ada.shape[2]

    def body(c_ref, w_ref, b_ref, cond_ref, o_ref):
        cv = c_ref[...]
        cond = cv * _sigmoid(cv)

        @pl.when(pl.program_id(0) == 0)
        def _():
            cond_ref[...] = cond

        o_ref[...] = _dot(cond, w_ref[...]) + b_ref[...]

    return _call(
        body, name=name, grid=(DEPTH,),
        out_shape=[jax.ShapeDtypeStruct((NDEV, D), F32), jax.ShapeDtypeStruct((DEPTH, NDEV, cols), F32)],
        in_specs=[_full((NDEV, D)), pl.BlockSpec((None, D, cols), lambda l: (l, 0, 0)),
                  pl.BlockSpec((None, 1, cols), lambda l: (l, 0, 0))],
        out_specs=[_full((NDEV, D)), pl.BlockSpec((None, NDEV, cols), lambda l: (l, 0, 0))],
        args=(c_all, w_ada, b_cols))


def _ada_bwd(cond, dmod_cols, name):
    cols = dmod_cols.shape[2]

    def body(c_ref, d_ref, o_ref):
        o_ref[...] = _dot_tn(c_ref[...], d_ref[...])

    return _call(
        body, name=name, grid=(DEPTH,),
        out_shape=[jax.ShapeDtypeStruct((DEPTH, D, cols), F32)],
        in_specs=[_full((NDEV, D)), pl.BlockSpec((None, NDEV, cols), lambda l: (l, 0, 0))],
        out_specs=[pl.BlockSpec((None, D, cols), lambda l: (l, 0, 0))], args=(cond, dmod_cols))[0]


def _adam_math(w, g, m, v):
    m = ADAM_B1 * m + (1.0 - ADAM_B1) * g
    v = ADAM_B2 * v + (1.0 - ADAM_B2) * (g * g)
    m_hat = m / (1.0 - ADAM_B1 ** ADAM_STEP)
    v_hat = v / (1.0 - ADAM_B2 ** ADAM_STEP)
    delta = -ADAM_LR * (m_hat / (jnp.sqrt(v_hat) + ADAM_EPS) + ADAM_WD * w)
    return delta, m, v


def _sum_parts(p_ref):
    g = p_ref[0].astype(F32)
    for i in range(1, p_ref.shape[0]):
        g = g + p_ref[i].astype(F32)
    return g


def _adamw(parts, w, m, v, tr, name):
    r, c = w.shape
    nparts = parts.shape[0]

    def body(p_ref, w_ref, m_ref, v_ref, g_ref, d_ref, nm_ref, nv_ref):
        g = _sum_parts(p_ref)
        delta, nm, nv = _adam_math(w_ref[...], g, m_ref[...], v_ref[...])
        g_ref[...] = g
        d_ref[...] = delta
        nm_ref[...] = nm
        nv_ref[...] = nv

    blk = pl.BlockSpec((tr, c), lambda i: (i, 0))
    return _call(
        body, name=name, grid=(r // tr,),
        out_shape=[jax.ShapeDtypeStruct((r, c), F32)] * 4,
        in_specs=[pl.BlockSpec((nparts, tr, c), lambda i: (0, i, 0)), blk, blk, blk],
        out_specs=[blk] * 4, args=(parts, w, m, v))


def _layer_parts_specs(parts, tr):
    nparts, r, c = parts[0].shape
    last = r // tr - 1

    def spec(k):
        return pl.BlockSpec((nparts, tr, c),
                            lambda l, i: (0, jnp.where(l < k, 0, jnp.where(l == k, i, last)), 0))

    return [spec(k) for k in range(len(parts))]


def _for_layer(p_refs, fn):
    for k, ref in enumerate(p_refs):
        pl.when(pl.program_id(0) == k)(lambda ref=ref: fn(ref))


def _adamw_layers(parts, w, m, v, tr, name):
    nl = len(parts)
    _, r, c = parts[0].shape

    def body(*refs):
        p_refs = refs[:nl]
        w_ref, m_ref, v_ref, g_ref, d_ref, nm_ref, nv_ref = refs[nl:]

        def update(p_ref):
            g = _sum_parts(p_ref)
            delta, nm, nv = _adam_math(w_ref[...], g, m_ref[...], v_ref[...])
            g_ref[...] = g
            d_ref[...] = delta
            nm_ref[...] = nm
            nv_ref[...] = nv

        _for_layer(p_refs, update)

    blk = pl.BlockSpec((tr, c), lambda l, i: (l * (r // tr) + i, 0))
    return _call(
        body, name=name, grid=(nl, r // tr),
        out_shape=[jax.ShapeDtypeStruct((nl * r, c), F32)] * 4,
        in_specs=_layer_parts_specs(parts, tr) + [blk, blk, blk],
        out_specs=[blk] * 4, args=(*parts, w, m, v))


def _reduce_layers(parts, tr, name):
    nl = len(parts)
    _, r, c = parts[0].shape

    def body(*refs):
        g_ref = refs[nl]

        def reduce(p_ref):
            g_ref[...] = _sum_parts(p_ref)

        _for_layer(refs[:nl], reduce)

    return _call(
        body, name=name, grid=(nl, r // tr), out_shape=[jax.ShapeDtypeStruct((nl, r, c), F32)],
        in_specs=_layer_parts_specs(parts, tr),
        out_specs=[pl.BlockSpec((None, tr, c), lambda l, i: (l, i, 0))], args=tuple(parts))[0]


def _cmul(ar, ai, br, bi):
    return ar * br - ai * bi, ar * bi + ai * br


def _lam_bar(lr, li, log_dt):
    dt = jnp.exp(log_dt)
    mag = jnp.exp(lr * dt)
    return dt, mag * jnp.cos(li * dt), mag * jnp.sin(li * dt)


def _layer_spec(arr):
    tail = arr.shape[1:]
    return pl.BlockSpec((None,) + tail, lambda l: (l,) + (0,) * len(tail))


def _ssm_prep(lam_re, lam_im, log_dt, b_re, b_im, name):
    def body(lr_ref, li_ref, dt_ref, br_ref, bi_ref, pr_ref, pi_ref, or_ref, oi_ref):
        lr, li = lr_ref[...], li_ref[...]
        _, ar, ai = _lam_bar(lr, li, dt_ref[...])
        den = lr * lr + li * li
        fr, fi = _cmul(ar - 1.0, ai, lr / den, -li / den)
        obr, obi = _cmul(fr[:, None, :], fi[:, None, :], br_ref[...], bi_ref[...])
        or_ref[...] = obr
        oi_ref[...] = obi
        qr, qi = ar, ai
        for k in range(SUB):
            pr_ref[k] = qr
            pi_ref[k] = qi
            qr, qi = _cmul(qr, qi, ar, ai)

    nl, g, p = lam_re.shape
    h = b_re.shape[2]
    outs = [jax.ShapeDtypeStruct((nl, SUB, g, p), F32)] * 2 + [jax.ShapeDtypeStruct((nl, g, h, p), F32)] * 2
    args = (lam_re, lam_im, log_dt, b_re, b_im)
    return _call(body, name=name, grid=(nl,), out_shape=outs, in_specs=[_layer_spec(a) for a in args],
                 out_specs=[_layer_spec(o) for o in outs], args=args)


def _ssm_param_bwd(lam_re, lam_im, log_dt, b_re, b_im, dlr, dli, dbr, dbi, name):
    def body(lr_ref, li_ref, dt_ref, br_ref, bi_ref, gar_ref, gai_ref, gbr_ref, gbi_ref,
             olr_ref, oli_ref, odt_ref, obr_ref, obi_ref):
        lr, li = lr_ref[...], li_ref[...]
        dt, ar, ai = _lam_bar(lr, li, dt_ref[...])
        den = lr * lr + li * li
        ir, ii = lr / den, -li / den
        fr, fi = _cmul(ar - 1.0, ai, ir, ii)
        br, bi = br_ref[...], bi_ref[...]
        gbr, gbi = gbr_ref[...], gbi_ref[...]
        obr, obi = _cmul(fr[:, None, :], -fi[:, None, :], gbr, gbi)
        obr_ref[...] = obr
        obi_ref[...] = obi
        pr, pi = _cmul(br, -bi, gbr, gbi)
        gfr, gfi = jnp.sum(pr, axis=1), jnp.sum(pi, axis=1)
        t_r, t_i = _cmul(gfr, gfi, ir, -ii)
        gar, gai = gar_ref[...] + t_r, gai_ref[...] + t_i
        qr, qi = _cmul(fr, fi, ir, ii)
        l1r, l1i = _cmul(gfr, gfi, qr, -qi)
        l2r, l2i = _cmul(gar, gai, dt * ar, -dt * ai)
        olr_ref[...] = l2r - l1r
        oli_ref[...] = l2i - l1i
        mr, mi = _cmul(lr, li, ar, ai)
        dr, _ = _cmul(gar, gai, mr, -mi)
        odt_ref[...] = jnp.sum(dr, axis=1, keepdims=True) * dt

    nl, g, p = lam_re.shape
    h = b_re.shape[2]
    outs = ([jax.ShapeDtypeStruct((nl, g, p), F32)] * 2 + [jax.ShapeDtypeStruct((nl, g, 1), F32)]
            + [jax.ShapeDtypeStruct((nl, g, h, p), F32)] * 2)
    args = (lam_re, lam_im, log_dt, b_re, b_im, dlr, dli, dbr, dbi)
    return _call(body, name=name, grid=(nl,), out_shape=outs, in_specs=[_layer_spec(a) for a in args],
                 out_specs=[_layer_spec(o) for o in outs], args=args)


def _gelu(x):
    k = math.sqrt(2.0 / math.pi)
    return 0.5 * x * (1.0 + jnp.tanh(k * (x + 0.044715 * x * x * x)))


def _gelu_grad(x):
    k = math.sqrt(2.0 / math.pi)
    th = jnp.tanh(k * (x + 0.044715 * x * x * x))
    return 0.5 * (1.0 + th) + 0.5 * x * (1.0 - th * th) * k * (1.0 + 3 * 0.044715 * x * x)


def _pool_inv_count(chunk):
    t = lax.broadcasted_iota(jnp.int32, (T, GW), 0) + chunk * T + 1
    lane = lax.broadcasted_iota(jnp.int32, (T, GW), 1)
    win = jnp.where(lane < 64, 2, jnp.where(lane < 128, 4, jnp.where(lane < 192, 8, 16)))
    return 1.0 / jnp.minimum(t, win).astype(F32)


def _window_sums(v, bufs, base, step):
    sums = []
    for k, buf in enumerate(bufs):
        buf[base:base + T, :] = v
        v = v + buf[pl.ds(base + step * (1 << k), T), :]
        sums.append(v)
    lane = lax.broadcasted_iota(jnp.int32, (T, GW), 1)
    return jnp.where(lane < 64, sums[0], jnp.where(lane < 128, sums[1], jnp.where(lane < 192, sums[2], sums[3])))


def _phase_copies(buf, rot):
    n = HALO + T - SUB
    for p in range(1, SUB):
        rot[p - 1, 0:n, :] = buf[pl.ds(p, n), :]


def _tap(buf, rot, off):
    q, p = divmod(off, SUB)
    return buf[off:off + T, :] if p == 0 else rot[p - 1, SUB * q:SUB * q + T, :]


SCAN_STEPS = (1, 2, 4)


def _scan(x_scr, steps_ref, chain_ref, carry, reverse):
    strips = range(XH // 128)
    cr = [carry[:, j * 128:(j + 1) * 128] for j in strips]
    ci = [carry[:, XH + j * 128:XH + (j + 1) * 128] for j in strips]
    nb = T // SUB
    edge = 0 if reverse else SUB - 1
    for b in (reversed(range(nb)) if reverse else range(nb)):
        rows = slice(SUB * b, SUB * b + SUB)
        for j in strips:
            lo, hi = j * 128, XH + j * 128
            rb, ib = x_scr[rows, lo:lo + 128], x_scr[rows, hi:hi + 128]
            for k, d in enumerate(SCAN_STEPS):
                ar, ai = steps_ref[k, :, lo:lo + 128], steps_ref[k, :, hi:hi + 128]
                shift = SUB - d if reverse else d
                sre, sim = pltpu.roll(rb, shift, 0), pltpu.roll(ib, shift, 0)
                rb, ib = rb + ar * sre - ai * sim, ib + ar * sim + ai * sre
            pr, pi = chain_ref[:, lo:lo + 128], chain_ref[:, hi:hi + 128]
            crb, cib = jnp.broadcast_to(cr[j], (SUB, 128)), jnp.broadcast_to(ci[j], (SUB, 128))
            rb, ib = rb + pr * crb - pi * cib, ib + pr * cib + pi * crb
            x_scr[rows, lo:lo + 128] = rb
            x_scr[rows, hi:hi + 128] = ib
            cr[j], ci[j] = rb[edge:edge + 1], ib[edge:edge + 1]


def _scan_tables(pw):
    conj = jnp.concatenate([jnp.ones((XH,), F32), -jnp.ones((XH,), F32)])
    steps = jnp.stack([pw[:, d - 1] for d in SCAN_STEPS], axis=1)[:, :, None, :]
    r = jnp.arange(SUB).reshape(1, 1, SUB, 1)
    d = jnp.array(SCAN_STEPS).reshape(1, len(SCAN_STEPS), 1, 1)
    fwd_steps = jnp.where(r >= d, steps, 0.0)
    rev_steps = jnp.where(r < SUB - d, steps, 0.0) * conj
    return fwd_steps, pw, rev_steps, pw[:, ::-1] * conj


MIXER_SHARED = ('scw', 'pblk', 'pscale', 'dww', 'dwb', 'clg', 'clb', 'bblk', 'cblk', 'ssd', 'wglu', 'bglu')
MIXER_FWD_PARAMS = MIXER_SHARED + ('fwd_steps', 'fwd_chain')
MIXER_BWD_PARAMS = MIXER_SHARED + ('rev_steps', 'rev_chain')


def _mixer_fwd(x, mod, l, w_in_t, b_in, mp, name, comm=None):
    s = x.shape[0]
    nch = s // T
    params = [mp[k] for k in MIXER_FWD_PARAMS]

    def body(x_ref, sc_ref, sh_ref, w_ref, b_ref, scw_ref, pblk_ref, ps_ref, dww_ref, dwb_ref, clg_ref, clb_ref,
             bblk_ref, cblk_ref, ssd_ref, wglu_ref, bglu_ref, steps_ref, chain_ref, y_ref, xs_ref, aux_ref, z_ref,
             qbuf, hbuf, pb1, pb2, pb4, pb8, rot, carry):
        i = pl.program_id(0)

        @pl.when(i == 0)
        def _():
            for ref in (qbuf, hbuf, pb1, pb2, pb4, pb8):
                ref[0:HALO, :] = jnp.zeros((HALO, GW), F32)
            carry[...] = jnp.zeros_like(carry)

        h = x_ref[...] * (1.0 + sc_ref[...]) + sh_ref[...]
        z_ref[...] = _dot_nt(h, w_ref[...]) + b_ref[...]
        zh, zb, zc = z_ref[:, 0:GW], z_ref[:, GW:2 * GW], z_ref[:, 2 * GW:3 * GW]
        zp, zcv, zcg, zs = (z_ref[:, 3 * GW:4 * GW], z_ref[:, 4 * GW:5 * GW], z_ref[:, 5 * GW:6 * GW],
                            z_ref[:, 6 * GW:7 * GW])
        qbuf[HALO:HALO + T, :] = zc * zh
        conv = None
        for k in range(SC_TAPS):
            v = scw_ref[k:k + 1, :] * qbuf[pl.ds(HALO - (SC_TAPS - 1) + k, T), :]
            conv = v if conv is None else conv + v
        aux_ref[:, 0:GW] = conv
        y_ref[:, 0:GW] = (zb * conv).astype(y_ref.dtype)
        r = _window_sums(zp, (pb1, pb2, pb4, pb8), HALO, -1) * _pool_inv_count(i) - zp
        aux_ref[:, GW:2 * GW] = r
        y_ref[:, GW:2 * GW] = (_dot(r, pblk_ref[...]) * ps_ref[...]).astype(y_ref.dtype)
        hbuf[HALO:HALO + T, :] = zcv * _sigmoid(zcg)
        _phase_copies(hbuf, rot)
        hc = None
        for k in range(CF_TAPS):
            v = dww_ref[k:k + 1, :] * _tap(hbuf, rot, HALO - (CF_TAPS - 1) + k)
            hc = v if hc is None else hc + v
        hc = hc + dwb_ref[...]
        aux_ref[:, 2 * GW:3 * GW] = hc
        mu = jnp.mean(hc, axis=-1, keepdims=True)
        xc = hc - mu
        var = jnp.mean(xc * xc, axis=-1, keepdims=True)
        ln = xc * lax.rsqrt(var + LN_EPS) * clg_ref[...] + clb_ref[...]
        y_ref[:, 2 * GW:3 * GW] = (ln * _sigmoid(ln)).astype(y_ref.dtype)
        xs_ref[...] = _dot(zs, bblk_ref[...])
        _scan(xs_ref, steps_ref, chain_ref, carry[...], reverse=False)
        carry[...] = xs_ref[T - 1:T, :]
        yy = _dot(xs_ref[...], cblk_ref[...]) + ssd_ref[...] * zs
        aux_ref[:, 3 * GW:4 * GW] = yy
        yg = _gelu(yy)
        v = _dot(yg, wglu_ref[...]) + bglu_ref[...]
        y_ref[:, 3 * GW:4 * GW] = (yg * _sigmoid(v)).astype(y_ref.dtype)
        for ref in (qbuf, hbuf, pb1, pb2, pb4, pb8):
            ref[0:HALO, :] = ref[T:T + HALO, :]

    fwd = lambda i: (i, 0)
    return _call(
        body, name=name, grid=(nch,),
        out_shape=[jax.ShapeDtypeStruct((s, D), _MXU), jax.ShapeDtypeStruct((s, XW), F32),
                   jax.ShapeDtypeStruct((s, 4 * GW), F32), jax.ShapeDtypeStruct((s, INW), F32)],
        in_specs=[pl.BlockSpec((T, D), fwd), _sel(mod, l, SC1), _sel(mod, l, SH1), _resident(w_in_t.shape),
                  _sel(b_in, l)] + [_sel(p, l) for p in params],
        out_specs=[pl.BlockSpec((T, D), fwd), pl.BlockSpec((T, XW), fwd), pl.BlockSpec((T, 4 * GW), fwd),
                   pl.BlockSpec((T, INW), fwd)],
        scratch=[pltpu.VMEM((HALO + T, GW), F32)] * 6
        + [pltpu.VMEM((SUB - 1, HALO + T, GW), F32), pltpu.VMEM((1, XW), F32)],
        args=(x, mod, mod, w_in_t, b_in, *params), comm=comm)


def _mixer_bwd(z, dres, w_o, xs, aux, mp, l, name):
    s = z.shape[0]
    nch = s // T
    params = [mp[k] for k in MIXER_BWD_PARAMS]

    def body(z_ref, zprev_ref, dres_ref, wo_ref, xs_ref, xprev_ref, aux_ref, scw_ref, pblk_ref, ps_ref, dww_ref,
             dwb_ref, clg_ref, clb_ref, bblk_ref, cblk_ref, ssd_ref, wglu_ref, bglu_ref, steps_ref, chain_ref,
             dz_ref, vec_ref, small_ref, dpblk_ref, dwglu_ref, dbblk_ref, dcblk_ref, dlam_ref,
             qbuf, hbuf, dcbuf, dhbuf, eb1, eb2, eb4, eb8, rot_h, rot_d, acarry, a_scr, dy_ref):
        i = pl.program_id(0)
        chunk = nch - 1 - i
        dy_ref[...] = _dot_nt(dres_ref[...], wo_ref[...])

        @pl.when(i == 0)
        def _():
            for ref in (vec_ref, small_ref, dpblk_ref, dwglu_ref, dbblk_ref, dcblk_ref, dlam_ref, acarry):
                ref[...] = jnp.zeros_like(ref)
            for ref in (dcbuf, dhbuf, eb1, eb2, eb4, eb8):
                ref[T:T + HALO, :] = jnp.zeros((HALO, GW), F32)

        zh, zb, zc = z_ref[:, 0:GW], z_ref[:, GW:2 * GW], z_ref[:, 2 * GW:3 * GW]
        zcv, zcg, zs = z_ref[:, 4 * GW:5 * GW], z_ref[:, 5 * GW:6 * GW], z_ref[:, 6 * GW:7 * GW]
        conv, r, hc, yy = (aux_ref[:, 0:GW], aux_ref[:, GW:2 * GW], aux_ref[:, 2 * GW:3 * GW],
                           aux_ref[:, 3 * GW:4 * GW])
        has_prev = (chunk > 0).astype(F32)
        qbuf[0:HALO, :] = zprev_ref[:, 2 * GW:3 * GW] * zprev_ref[:, 0:GW] * has_prev
        hbuf[0:HALO, :] = zprev_ref[:, 4 * GW:5 * GW] * _sigmoid(zprev_ref[:, 5 * GW:6 * GW]) * has_prev

        dya = dy_ref[:, 0:GW]
        qbuf[HALO:HALO + T, :] = zc * zh
        dconv = dya * zb
        dcbuf[0:T, :] = dconv
        dq = None
        for k in range(SC_TAPS):
            small_ref[k:k + 1, :] += _colsum(dconv * qbuf[pl.ds(HALO - (SC_TAPS - 1) + k, T), :])
            v = scw_ref[k:k + 1, :] * dcbuf[pl.ds(SC_TAPS - 1 - k, T), :]
            dq = v if dq is None else dq + v
        dzh, dzb, dzc = dq * zc, dya * conv, dq * zh

        dyb = dy_ref[:, GW:2 * GW]
        o = _dot(r, pblk_ref[...])
        small_ref[3:4, :] += _colsum(dyb * o)
        do = dyb * ps_ref[...]
        dpblk_ref[...] += _dot_tn(r, do)
        dr = _dot_nt(do, pblk_ref[...])
        dzp = _window_sums(dr * _pool_inv_count(chunk), (eb1, eb2, eb4, eb8), 0, 1) - dr

        dyc = dy_ref[:, 2 * GW:3 * GW]
        sgc = _sigmoid(zcg)
        hbuf[HALO:HALO + T, :] = zcv * sgc
        mu = jnp.mean(hc, axis=-1, keepdims=True)
        xc = hc - mu
        var = jnp.mean(xc * xc, axis=-1, keepdims=True)
        rstd = lax.rsqrt(var + LN_EPS)
        nrm = xc * rstd
        ln = nrm * clg_ref[...] + clb_ref[...]
        sl = _sigmoid(ln)
        dln = dyc * (sl * (1.0 + ln * (1.0 - sl)))
        small_ref[5:6, :] += _colsum(dln * nrm)
        small_ref[6:7, :] += _colsum(dln)
        dn = dln * clg_ref[...]
        dhc = rstd * (dn - jnp.mean(dn, axis=-1, keepdims=True) - nrm * jnp.mean(dn * nrm, axis=-1, keepdims=True))
        small_ref[4:5, :] += _colsum(dhc)
        dhbuf[0:T, :] = dhc
        _phase_copies(hbuf, rot_h)
        _phase_copies(dhbuf, rot_d)
        dhg = None
        for k in range(CF_TAPS):
            small_ref[16 + k:17 + k, :] += _colsum(dhc * _tap(hbuf, rot_h, HALO - (CF_TAPS - 1) + k))
            v = dww_ref[k:k + 1, :] * _tap(dhbuf, rot_d, CF_TAPS - 1 - k)
            dhg = v if dhg is None else dhg + v
        dzcv = dhg * sgc
        dzcg = dhg * zcv * sgc * (1.0 - sgc)

        dyd = dy_ref[:, 3 * GW:4 * GW]
        yg = _gelu(yy)
        v = _dot(yg, wglu_ref[...]) + bglu_ref[...]
        sg = _sigmoid(v)
        dv = dyd * yg * sg * (1.0 - sg)
        small_ref[8:9, :] += _colsum(dv)
        dwglu_ref[...] += _dot_tn(yg, dv)
        g = (dyd * sg + _dot_nt(dv, wglu_ref[...])) * _gelu_grad(yy)
        small_ref[7:8, :] += _colsum(g * zs)
        dcblk_ref[...] += _dot_tn(xs_ref[...], g)
        a_scr[...] = _dot_nt(g, cblk_ref[...])
        _scan(a_scr, steps_ref, chain_ref, acarry[...], reverse=True)
        acarry[...] = a_scr[0:1, :]
        dzs = _dot_nt(a_scr[...], bblk_ref[...]) + ssd_ref[...] * g
        dbblk_ref[...] += _dot_tn(zs, a_scr[...])
        row = lax.broadcasted_iota(jnp.int32, (T, 128), 0)
        for j in range(XH // 128):
            lo, hi = j * 128, XH + j * 128
            first_r = xprev_ref[SUB - 1:SUB, lo:lo + 128] * has_prev
            first_i = xprev_ref[SUB - 1:SUB, hi:hi + 128] * has_prev
            pr = jnp.where(row == 0, first_r, pltpu.roll(xs_ref[:, lo:lo + 128], 1, 0))
            pi = jnp.where(row == 0, first_i, pltpu.roll(xs_ref[:, hi:hi + 128], 1, 0))
            ar, ai = a_scr[:, lo:lo + 128], a_scr[:, hi:hi + 128]
            dlam_ref[0:1, lo:lo + 128] += _colsum(ar * pr + ai * pi)
            dlam_ref[0:1, hi:hi + 128] += _colsum(ai * pr - ar * pi)

        parts = (dzh, dzb, dzc, dzp, dzcv, dzcg, dzs)
        for k, part in enumerate(parts):
            dz_ref[:, k * GW:(k + 1) * GW] = part.astype(dz_ref.dtype)
            vec_ref[0:1, k * GW:(k + 1) * GW] += _colsum(part)
        for ref in (dcbuf, dhbuf, eb1, eb2, eb4, eb8):
            ref[T:T + HALO, :] = ref[0:HALO, :]

    rev = lambda i: (nch - 1 - i, 0)

    def before(rows):
        return lambda i: (jnp.maximum((nch - 1 - i) * (T // rows) - 1, 0), 0)

    outs = [((s, INW), _MXU), ((8, INW), F32), ((48, GW), F32), ((GW, GW), F32), ((GW, GW), F32),
            ((GW, XW), F32), ((XW, GW), F32), ((8, XW), F32)]
    return _call(
        body, name=name, grid=(nch,),
        out_shape=[jax.ShapeDtypeStruct(sh, dt) for sh, dt in outs],
        in_specs=[pl.BlockSpec((T, INW), rev), pl.BlockSpec((HALO, INW), before(HALO)), pl.BlockSpec((T, D), rev),
                  _resident(w_o.shape), pl.BlockSpec((T, XW), rev), pl.BlockSpec((SUB, XW), before(SUB)),
                  pl.BlockSpec((T, 4 * GW), rev)]
        + [_sel(p, l) for p in params],
        out_specs=[pl.BlockSpec((T, INW), rev)] + [_full(sh) for sh, _ in outs[1:]],
        scratch=[pltpu.VMEM((HALO + T, GW), F32)] * 8
        + [pltpu.VMEM((SUB - 1, HALO + T, GW), F32)] * 2
        + [pltpu.VMEM((1, XW), F32), pltpu.VMEM((T, XW), F32), pltpu.VMEM((T, D), F32)],
        args=(z, z, dres, w_o, xs, xs, aux, *params))


def _blockdiag(m):
    *lead, g, h, p = m.shape
    eye = jnp.eye(g, dtype=m.dtype)
    return (m[..., :, :, None, :] * eye[:, None, :, None]).reshape(*lead, g * h, g * p)


def _blockdiag_extract(f, h, p):
    *lead, r, _ = f.shape
    g = r // h
    eye = jnp.eye(g, dtype=f.dtype)
    return jnp.sum(f.reshape(*lead, g, h, g, p) * eye[:, None, :, None], axis=-2)


def _t(a):
    return jnp.swapaxes(a, -1, -2)


PACK_ROWS = 512


def _pack(arrs):
    flat = jnp.concatenate([a.reshape(-1).astype(F32) for a in arrs])
    quantum = PACK_ROWS * 128
    padded = -(-flat.shape[0] // quantum) * quantum
    return jnp.pad(flat, (0, padded - flat.shape[0])).reshape(-1, 128)


def _unpack(buf, shapes, lead=()):
    flat = buf.reshape(lead + (-1,))
    out, off = [], 0
    for shp in shapes:
        n = math.prod(shp)
        out.append(flat[..., off:off + n].reshape(lead + tuple(shp)))
        off += n
    return out


def _cols_to_full(g):
    n, l, r, c = g.shape
    return g.transpose(1, 2, 0, 3).reshape(l, r, n * c)


def _rows_to_full(g):
    n, l, r, c = g.shape
    return g.transpose(1, 0, 2, 3).reshape(l, n * r, c)


REPLICATED = ['b_ada', 'b_in', 'pool_w', 'pool_scale', 'cf_dw_b', 'cf_ln_g', 'cf_ln_b', 'ssm_lam_re', 'ssm_lam_im',
              'ssm_log_dt', 'ssm_b_re', 'ssm_b_im', 'ssm_c_re', 'ssm_c_im', 'ssm_d', 'ssm_b_glu', 'ln1_g', 'ln1_b',
              'ln2_g', 'ln2_b']
SMALL_SHARDED = ['sc_w', 'cf_dw_w', 'ssm_w_glu']
COL_SHARDED = ['w_in', 'w_gate', 'w_up']
ROW_SHARDED = ['w_o', 'w_down']
BIG = COL_SHARDED + ROW_SHARDED
WEIGHTS = ['w_ada', 'b_ada', 'w_in', 'b_in', 'sc_w', 'pool_w', 'pool_scale', 'cf_dw_w', 'cf_dw_b', 'cf_ln_g',
           'cf_ln_b', 'ssm_lam_re', 'ssm_lam_im', 'ssm_log_dt', 'ssm_b_re', 'ssm_b_im', 'ssm_c_re', 'ssm_c_im',
           'ssm_d', 'ssm_w_glu', 'ssm_b_glu', 'w_o', 'ln1_g', 'ln1_b', 'w_gate', 'w_up', 'w_down', 'ln2_g', 'ln2_b']
SH1, SC1, G1, SH2, SC2, G2 = range(6)


def _train_step(p):
    xi, yi, ci = _me()
    me = _flat((xi, yi, ci))
    s = p['x'].shape[1]
    xs = p['x'].reshape(s, D)
    target = p['loss_target'].reshape(s, D)
    vec3 = lambda a: a.reshape(DEPTH, 1, -1)

    def shard(n, l):
        w = p[n][l]
        return (w.T if n in COL_SHARDED else w).astype(_MXU)

    def whole(g):
        return g.reshape(-1, g.shape[-1])

    small_shapes = [p[n].shape for n in SMALL_SHARDED] + [p['c'].shape]
    first = ('w_in', 'w_o')
    gathered = _allgather([_pack([p[n] for n in SMALL_SHARDED] + [p['c']])] + [shard(n, 0) for n in first],
                          "gather_first")
    scw_g, dww_g, wglu_g, c_g = _unpack(gathered[0], small_shapes, lead=(NDEV,))
    full = {'sc_w': _cols_to_full(scw_g), 'cf_dw_w': _cols_to_full(dww_g), 'ssm_w_glu': _rows_to_full(wglu_g)}
    wts = [dict(zip(first, (whole(g) for g in gathered[1:])))] + [dict() for _ in range(DEPTH - 1)]
    c_all = c_g.reshape(NDEV, D)

    ncol = p['w_ada'].shape[2]
    b_cols = lax.dynamic_slice_in_dim(p['b_ada'], me * ncol, ncol, axis=1).reshape(DEPTH, 1, ncol)
    cond, mod_cols = _ada_fwd(c_all, p['w_ada'], b_cols, "ada_fwd")
    (mod_x,) = _alltoall([mod_cols.transpose(1, 0, 2)], "mod_exchange")
    mod = mod_x.transpose(1, 0, 2).reshape(DEPTH, 6, 1, D)

    ssm_in = (p['ssm_lam_re'], p['ssm_lam_im'], p['ssm_log_dt'].reshape(DEPTH, NGRP, 1),
              _t(p['ssm_b_re']), _t(p['ssm_b_im']))
    pwr, pwi, bbr, bbi = _ssm_prep(*ssm_in, "ssm_prep")
    pw = jnp.concatenate([pwr.reshape(DEPTH, SUB, XH), pwi.reshape(DEPTH, SUB, XH)], axis=2)
    fwd_steps, fwd_chain, rev_steps, rev_chain = _scan_tables(pw)
    mp = dict(
        fwd_steps=fwd_steps, fwd_chain=fwd_chain, rev_steps=rev_steps, rev_chain=rev_chain,
        scw=full['sc_w'], pblk=_blockdiag(p['pool_w']).astype(_MXU), pscale=vec3(p['pool_scale']),
        dww=full['cf_dw_w'], dwb=vec3(p['cf_dw_b']), clg=vec3(p['cf_ln_g']), clb=vec3(p['cf_ln_b']),
        bblk=jnp.concatenate([_blockdiag(bbr), _blockdiag(bbi)], axis=2).astype(_MXU),
        cblk=jnp.concatenate([_blockdiag(_t(p['ssm_c_re'])), -_blockdiag(_t(p['ssm_c_im']))], axis=1).astype(_MXU),
        ssd=vec3(p['ssm_d']), wglu=full['ssm_w_glu'].astype(_MXU),
        bglu=vec3(p['ssm_b_glu']))
    b_in, ln1_g, ln1_b, ln2_g, ln2_b = (vec3(p[n]) for n in ('b_in', 'ln1_g', 'ln1_b', 'ln2_g', 'ln2_b'))

    saved = []
    for l in range(DEPTH):
        w = wts[l]

        def gather(at, *names):
            return None if at >= DEPTH or not names else (GATHER, [shard(n, at) for n in names])

        def take(res, n_own, at, *names):
            if at < DEPTH:
                for n, g in zip(names, res[n_own:]):
                    wts[at][n] = whole(g)
            return res[:n_own]

        ycat, states, aux, z = take(_mixer_fwd(xs, mod, l, w['w_in'], b_in, mp, "mixer_fwd",
                                               comm=gather(l, 'w_gate', 'w_up')), 4, l, 'w_gate', 'w_up')
        early = ('w_down',) if l == 0 else ()
        y, u1, x1 = take(_proj_residual_ln(ycat, w['w_o'], l, xs, mod, G1, ln1_g, ln1_b, "o_proj_ln",
                                           comm=gather(l, *early)), 3, l, *early)
        gate, up, act = take(_gate_up(x1, mod, l, w['w_gate'], w['w_up'], "gate_up",
                                      comm=gather(l + 1, 'w_in', 'w_o')), 3, l + 1, 'w_in', 'w_o')
        f, u2, x2 = take(_proj_residual_ln(act, w['w_down'], l, x1, mod, G2, ln2_g, ln2_b, "down_proj_ln",
                                           comm=gather(l + 1, 'w_down')), 3, l + 1, 'w_down')
        saved.append(dict(x=xs, z=z, ycat=ycat, states=states, aux=aux, y=y, u1=u1, x1=x1, gate=gate, up=up,
                          act=act, f=f, u2=u2))
        xs = x2

    recv = {n: [None] * DEPTH for n in BIG}
    small_g = {n: [None] * DEPTH for n in ('ln1_g', 'ln1_b', 'ln2_g', 'ln2_b')}
    dmod = [None] * DEPTH
    mixer_out = [None] * DEPTH
    pending = None
    top = DEPTH - 1
    du2, df, acc_top = _loss_ln_bwd(xs, target, saved[top]['u2'], saved[top]['f'], ln2_g, top, mod, G2,
                                    "loss_ln2_bwd")
    small_g['ln2_g'][top], small_g['ln2_b'][top], dg2 = acc_top[0], acc_top[1], acc_top[2]
    loss = lax.psum(acc_top[3, 0] * (0.5 / D), AXES)
    for l in reversed(range(DEPTH)):
        sv, w = saved[l], wts[l]
        (g_down,) = _weight_grad(sv['act'], df, "dw_down")
        lp, late_names, late = pending or (None, [], [])
        dgate, dup, du1, dyb, acc_a, recv['w_down'][l], *landed = _input_grad_ln_bwd(
            None, [w['w_gate'], w['w_up']], du2, (mod, l, SC2), sv['u1'], sv['y'], (ln1_g, l), (ln1_b, l),
            (mod, l, G1), TM_WIDE, "mlp_bwd", swiglu=(df, w['w_down'], sv['gate'], sv['up']),
            comm=(EXCHANGE, [g_down] + late))
        for n, r in zip(late_names, landed):
            recv[n][lp] = r
        (g_gate,) = _weight_grad(dgate, sv['x1'], "dw_gate", mod=(mod, l, SC2, SH2))
        (g_up,) = _weight_grad(dup, sv['x1'], "dw_up", mod=(mod, l, SC2, SH2))
        (g_o,) = _weight_grad(sv['ycat'], dyb, "dw_o")
        dz, *mixer_out[l] = _mixer_bwd(sv['z'], dyb, w['w_o'], sv['states'], sv['aux'], mp, l, "mixer_bwd")
        if l > 0:
            du2, df, acc_b, recv['w_gate'][l] = _input_grad_ln_bwd(
                [dz], [w['w_in']], du1, (mod, l, SC1), saved[l - 1]['u2'], saved[l - 1]['f'], (ln2_g, l - 1),
                (ln2_b, l - 1), (mod, l - 1, G2), TM_WIDE, "dh1_ln2_bwd", comm=(EXCHANGE, [g_gate]))
            (g_in,) = _weight_grad(dz, sv['x'], "dw_in", mod=(mod, l, SC1, SH1))
            pending = (l, ['w_up', 'w_o', 'w_in'], [g_up, g_o, g_in])
        else:
            dx, acc_b, recv['w_gate'][l] = _input_grad([dz], [w['w_in']], l, du1, sv['x'], mod, SC1, TM, "dh1",
                                                       comm=(EXCHANGE, [g_gate]))
            g_in, recv['w_up'][l] = _weight_grad(dz, sv['x'], "dw_in", mod=(mod, l, SC1, SH1),
                                                 comm=(EXCHANGE, [g_up]))
            pending = (l, ['w_o', 'w_in'], [g_o, g_in])
        dmod[l] = jnp.concatenate([acc_b[0], acc_b[1], acc_a[4], acc_a[0], acc_a[1], dg2])
        small_g['ln1_g'][l], small_g['ln1_b'][l] = acc_a[2], acc_a[3]
        if l > 0:
            small_g['ln2_g'][l - 1], small_g['ln2_b'][l - 1], dg2 = acc_b[2], acc_b[3], acc_b[4]
    lp, late_names, late = pending
    for n, r in zip(late_names, _alltoall(late, "exchange_last")):
        recv[n][lp] = r

    partial = {n: jnp.stack(v) for n, v in small_g.items()}
    partial['b_ada'] = jnp.stack(dmod)
    vec, small, dpblk, dwglu, dbblk, dcblk, dlam = (jnp.stack(t) for t in zip(*mixer_out))
    partial.update(
        b_in=vec[:, 0], sc_w=small[:, 0:SC_TAPS], pool_scale=small[:, 3], cf_dw_b=small[:, 4], cf_ln_g=small[:, 5],
        cf_ln_b=small[:, 6], ssm_d=small[:, 7], ssm_b_glu=small[:, 8], cf_dw_w=small[:, 16:16 + CF_TAPS],
        pool_w=_blockdiag_extract(dpblk, 64, 64), ssm_w_glu=dwglu,
        ssm_c_re=_t(_blockdiag_extract(dcblk[:, :XH], NSTATE, NCH)),
        ssm_c_im=-_t(_blockdiag_extract(dcblk[:, XH:], NSTATE, NCH)))
    dlr, dli, dlog, dbr, dbi = _ssm_param_bwd(
        *ssm_in, dlam[:, 0, :XH].reshape(DEPTH, NGRP, NSTATE), dlam[:, 0, XH:].reshape(DEPTH, NGRP, NSTATE),
        _blockdiag_extract(dbblk[:, :, :XH], NCH, NSTATE), _blockdiag_extract(dbblk[:, :, XH:], NCH, NSTATE),
        "ssm_param_bwd")
    partial.update(ssm_lam_re=dlr, ssm_lam_im=dli, ssm_log_dt=dlog.reshape(DEPTH, NGRP), ssm_b_re=_t(dbr),
                   ssm_b_im=_t(dbi))

    out = {'loss': loss, 'grad_x': dx.reshape(1, s, D)}

    def emit(n, g, delta, nm, nv):
        shp = p[n].shape
        out['grad_' + n], out['delta_' + n] = g.reshape(shp), delta.reshape(shp)
        out['new_m_' + n], out['new_v_' + n] = nm.reshape(shp), nv.reshape(shp)

    names_a = REPLICATED
    shapes_a = [partial[n].shape for n in names_a]
    (parts_a,) = _allgather([_pack([partial[n] for n in names_a])], "gather_small_grads")
    parts = dict(zip(names_a, _unpack(parts_a, shapes_a, lead=(NDEV,))))
    flat2 = lambda a: a.reshape(-1, a.shape[-1])

    def update_small(n, prt):
        prt = prt.reshape(NDEV, -1, prt.shape[-1])
        rows = prt.shape[1]
        emit(n, *_adamw(prt, flat2(p[n]), flat2(p['m_' + n]), flat2(p['v_' + n]), min(rows, PACK_ROWS),
                        "adamw_" + n))

    for n in REPLICATED:
        update_small(n, parts[n])
    def blocks_of(n):
        g = partial[n]
        if n == 'ssm_w_glu':
            nl, r, c = g.shape
            return g.reshape(nl, NDEV, r // NDEV, c).transpose(1, 0, 2, 3)
        nl, k, c = g.shape
        return g.reshape(nl, k, NDEV, c // NDEV).transpose(2, 0, 1, 3)

    blocks = [blocks_of(n) for n in SMALL_SHARDED]
    flat = jnp.concatenate([b.reshape(NDEV, -1) for b in blocks], axis=1)
    padded = -(-flat.shape[1] // 1024) * 1024
    send = jnp.pad(flat, ((0, 0), (0, padded - flat.shape[1]))).reshape(NDEV, -1, 128)
    (got,) = _alltoall([send], "exchange_small_grads")
    for n, prt in zip(SMALL_SHARDED, _unpack(got, [b.shape[1:] for b in blocks], lead=(NDEV,))):
        update_small(n, prt)

    dmod_cols = lax.dynamic_slice_in_dim(parts['b_ada'], me * ncol, ncol, axis=2).transpose(1, 0, 2)
    g_ada = _ada_bwd(cond, dmod_cols, "ada_bwd")
    emit('w_ada', *_adamw(flat2(g_ada)[None], flat2(p['w_ada']), flat2(p['m_w_ada']), flat2(p['v_w_ada']),
                          512, "adamw_w_ada"))

    for n in BIG:
        tr = recv[n][0].shape[1] // 2
        if n in ROW_SHARDED:
            emit(n, *_adamw_layers(recv[n], flat2(p[n]), flat2(p['m_' + n]), flat2(p['v_' + n]), tr, "adamw_" + n))
        else:
            g = flat2(_t(_reduce_layers(recv[n], tr, "sum_" + n)))
            emit(n, *_adamw(g[None], flat2(p[n]), flat2(p['m_' + n]), flat2(p['v_' + n]), 512, "adamw_" + n))
    return out


def kernel(x, c, w_ada, b_ada, w_in, b_in, sc_w, pool_w, pool_scale, cf_dw_w, cf_dw_b, cf_ln_g, cf_ln_b, ssm_lam_re, ssm_lam_im, ssm_log_dt, ssm_b_re, ssm_b_im, ssm_c_re, ssm_c_im, ssm_d, ssm_w_glu, ssm_b_glu, w_o, ln1_g, ln1_b, w_gate, w_up, w_down, ln2_g, ln2_b, loss_target, m_w_ada, m_b_ada, m_w_in, m_b_in, m_sc_w, m_pool_w, m_pool_scale, m_cf_dw_w, m_cf_dw_b, m_cf_ln_g, m_cf_ln_b, m_ssm_lam_re, m_ssm_lam_im, m_ssm_log_dt, m_ssm_b_re, m_ssm_b_im, m_ssm_c_re, m_ssm_c_im, m_ssm_d, m_ssm_w_glu, m_ssm_b_glu, m_w_o, m_ln1_g, m_ln1_b, m_w_gate, m_w_up, m_w_down, m_ln2_g, m_ln2_b, v_w_ada, v_b_ada, v_w_in, v_b_in, v_sc_w, v_pool_w, v_pool_scale, v_cf_dw_w, v_cf_dw_b, v_cf_ln_g, v_cf_ln_b, v_ssm_lam_re, v_ssm_lam_im, v_ssm_log_dt, v_ssm_b_re, v_ssm_b_im, v_ssm_c_re, v_ssm_c_im, v_ssm_d, v_ssm_w_glu, v_ssm_b_glu, v_w_o, v_ln1_g, v_ln1_b, v_w_gate, v_w_up, v_w_down, v_ln2_g, v_ln2_b):
    out = _train_step(dict(locals()))
    return (out['loss'], out['grad_x'], *[out[pre + n] for pre in ('grad_', 'delta_', 'new_m_', 'new_v_')
                                          for n in WEIGHTS])
```
